```python
import math, functools
import jax, jax.numpy as jnp
from jax import lax
import numpy as np

D_MODEL = 2048
BATCH = 4
SEQ = 2048
DEPTH = 1
DEC_BATCH = 8
DEC_SEQ = 1
PAST_LEN = 16384
PAGE_SIZE = 128

N_HEADS = 8
HEAD_DIM = 64
V_DIM = 2 * HEAD_DIM
ATTN_QK_WIDTH = N_HEADS * 2 * HEAD_DIM
ATTN_V_WIDTH = N_HEADS * V_DIM
SSM_WIDTH = D_MODEL // 2
SSM_GROUP = 16
N_GROUPS = SSM_WIDTH // SSM_GROUP
STATE_DIM = 64
D_FF = 5504
CONV_W = 3
NUM_BUCKETS = 32
MAX_EXACT = NUM_BUCKETS // 2
MAX_DISTANCE = 128
Q_BLOCK = 128
LN_EPS = 1e-5
NEG_INF = -1e30
Q_END = ATTN_QK_WIDTH
K_END = 2 * ATTN_QK_WIDTH
V_END = K_END + ATTN_V_WIDTH
U_END = V_END + SSM_WIDTH
GA_END = U_END + D_MODEL
IN_WIDTH = GA_END + D_MODEL

kernel_name = 'diffattn_s5_gated_convffn_deepnorm_step'


def layer_norm(x, g, b):
    xf = x.astype(jnp.float32)
    mu = jnp.mean(xf, axis=-1, keepdims=True)
    var = jnp.mean(jnp.square(xf - mu), axis=-1, keepdims=True)
    return ((xf - mu) * lax.rsqrt(var + LN_EPS) * g.astype(jnp.float32) + b.astype(jnp.float32)).astype(x.dtype)


def rms_norm(x, g):
    xf = x.astype(jnp.float32)
    return (xf * lax.rsqrt(jnp.mean(jnp.square(xf), axis=-1, keepdims=True) + LN_EPS) * g.astype(jnp.float32)).astype(x.dtype)


def rel_bucket(q_pos, k_pos):
    n = jnp.maximum(q_pos[:, None] - k_pos[None, :], 0)
    nf = jnp.maximum(n, 1).astype(jnp.float32)
    large = MAX_EXACT + (jnp.log(nf / MAX_EXACT) / math.log(MAX_DISTANCE / MAX_EXACT) * (NUM_BUCKETS - MAX_EXACT)).astype(jnp.int32)
    large = jnp.minimum(large, NUM_BUCKETS - 1)
    return jnp.where(n < MAX_EXACT, n, large)


def diff_attention(q, k, v, q_pos, k_pos, rel_bias, lam):
    s = jnp.einsum('bqhjd,bkhjd->bhjqk', q, k).astype(jnp.float32) * (HEAD_DIM ** -0.5)
    bias = jnp.transpose(rel_bias[rel_bucket(q_pos, k_pos)], (2, 0, 1)).astype(jnp.float32)
    causal = k_pos[None, :] <= q_pos[:, None]
    s = jnp.where(causal, s + bias[None, :, None], NEG_INF)
    p = jax.nn.softmax(s, axis=-1)
    p_diff = p[:, :, 0] - lam * p[:, :, 1]
    return jnp.einsum('bhqk,bkhe->bqhe', p_diff.astype(v.dtype), v)


def prompt_attention(q, k, v, rel_bias, lam):
    b, seq = q.shape[0], q.shape[1]
    n_blocks = seq // Q_BLOCK
    q_blocks = jnp.moveaxis(q.reshape(b, n_blocks, Q_BLOCK, N_HEADS, 2, HEAD_DIM), 1, 0)
    k_pos = jnp.arange(seq, dtype=jnp.int32)

    def one_block(args):
        idx, q_blk = args
        q_pos = idx * Q_BLOCK + jnp.arange(Q_BLOCK, dtype=jnp.int32)
        return diff_attention(q_blk, k, v, q_pos, k_pos, rel_bias, lam)

    out = lax.map(one_block, (jnp.arange(n_blocks, dtype=jnp.int32), q_blocks))
    return jnp.moveaxis(out, 0, 1).reshape(b, seq, N_HEADS, V_DIM)


def sample_attention(k_past, v_past, q, k, v, rel_bias, lam):
    past = k_past.shape[1]
    n_new = q.shape[1]
    k_all = jnp.concatenate([k_past.astype(k.dtype), k], axis=1)
    v_all = jnp.concatenate([v_past.astype(v.dtype), v], axis=1)
    q_pos = past + jnp.arange(n_new, dtype=jnp.int32)
    k_pos = jnp.arange(past + n_new, dtype=jnp.int32)
    return diff_attention(q, k_all, v_all, q_pos, k_pos, rel_bias, lam)


def s5_branch(u, x0_re, x0_im, a_re, a_im, log_dt, b_re, b_im, c_re, c_im, d, w_glu):
    b, seq = u.shape[0], u.shape[1]
    ug = u.reshape(b, seq, N_GROUPS, SSM_GROUP)
    dt = jnp.exp(log_dt)[:, None]
    mag = jnp.exp(a_re * dt)
    ang = a_im * dt
    abar_re = mag * jnp.cos(ang)
    abar_im = mag * jnp.sin(ang)
    den = a_re * a_re + a_im * a_im
    f_re = ((abar_re - 1.0) * a_re + abar_im * a_im) / den
    f_im = (abar_im * a_re - (abar_re - 1.0) * a_im) / den
    bb_re = f_re[..., None] * b_re - f_im[..., None] * b_im
    bb_im = f_re[..., None] * b_im + f_im[..., None] * b_re
    bu_re = jnp.einsum('blgc,gpc->blgp', ug, bb_re)
    bu_im = jnp.einsum('blgc,gpc->blgp', ug, bb_im)
    a_seq_re = jnp.broadcast_to(abar_re, bu_re.shape)
    a_seq_im = jnp.broadcast_to(abar_im, bu_im.shape)

    def combine(e1, e2):
        a1r, a1i, b1r, b1i = e1
        a2r, a2i, b2r, b2i = e2
        return (a2r * a1r - a2i * a1i,
                a2r * a1i + a2i * a1r,
                a2r * b1r - a2i * b1i + b2r,
                a2r * b1i + a2i * b1r + b2i)

    acum_re, acum_im, s_re, s_im = lax.associative_scan(combine, (a_seq_re, a_seq_im, bu_re, bu_im), axis=1)
    x_re = s_re + acum_re * x0_re[:, None] - acum_im * x0_im[:, None]
    x_im = s_im + acum_re * x0_im[:, None] + acum_im * x0_re[:, None]
    y = jnp.einsum('blgp,gcp->blgc', x_re, c_re) - jnp.einsum('blgp,gcp->blgc', x_im, c_im)
    y = y.reshape(b, seq, SSM_WIDTH) + d * u
    g = jax.nn.gelu(y)
    return g * jax.nn.sigmoid(g @ w_glu), x_re[:, -1], x_im[:, -1]


def conv_ffn(h, conv0, w_up, conv_w, conv_b, w_down):
    seq = h.shape[1]
    up = h @ w_up
    padded = jnp.concatenate([conv0.astype(up.dtype), up], axis=1)
    c = conv_b
    for j in range(CONV_W):
        c = c + conv_w[j] * padded[:, j:j + seq]
    gate, val = jnp.split(c, 2, axis=-1)
    return (jax.nn.silu(gate) * val) @ w_down, padded[:, -(CONV_W - 1):]


def decoder_layer(x, attend, ssm_re0, ssm_im0, conv0, layer_idx, rel_bias, layer_weights):
    (w_in, lambda_q1, lambda_k1, lambda_q2, lambda_k2, subln_g, ssm_a_re, ssm_a_im, ssm_log_dt,
     ssm_b_re, ssm_b_im, ssm_c_re, ssm_c_im, ssm_d, w_glu, w_proj_attn, w_proj_ssm, w_out,
     ln1_g, ln1_b, w_up, conv_w, conv_b, w_down, ln2_g, ln2_b) = layer_weights
    b, seq = x.shape[0], x.shape[1]
    alpha = (2.0 * DEPTH) ** 0.25
    lam_init = 0.8 - 0.6 * math.exp(-0.3 * layer_idx)
    z = x @ w_in
    q = z[..., :Q_END].reshape(b, seq, N_HEADS, 2, HEAD_DIM)
    k = z[..., Q_END:K_END].reshape(b, seq, N_HEADS, 2, HEAD_DIM)
    v = z[..., K_END:V_END].reshape(b, seq, N_HEADS, V_DIM)
    u = z[..., V_END:U_END]
    gate_a = jax.nn.sigmoid(z[..., U_END:GA_END])
    gate_s = jax.nn.sigmoid(z[..., GA_END:])
    lam = (jnp.exp(jnp.sum(lambda_q1.astype(jnp.float32) * lambda_k1.astype(jnp.float32)))
           - jnp.exp(jnp.sum(lambda_q2.astype(jnp.float32) * lambda_k2.astype(jnp.float32))) + lam_init)
    attn = rms_norm(attend(q, k, v, rel_bias, lam), subln_g) * (1.0 - lam_init)
    attn = attn.reshape(b, seq, ATTN_V_WIDTH)
    ssm, s_re, s_im = s5_branch(u, ssm_re0, ssm_im0, ssm_a_re, ssm_a_im, ssm_log_dt,
                                ssm_b_re, ssm_b_im, ssm_c_re, ssm_c_im, ssm_d, w_glu)
    merged = gate_a * (attn @ w_proj_attn) + gate_s * (ssm @ w_proj_ssm)
    h = layer_norm(alpha * x + merged @ w_out, ln1_g, ln1_b)
    ffn, conv_new = conv_ffn(h, conv0, w_up, conv_w, conv_b, w_down)
    y = layer_norm(alpha * h + ffn, ln2_g, ln2_b)
    return y, k, v, s_re, s_im, conv_new


def setup_inputs(seed: int = 0) -> dict:
    key = jax.random.key(seed)
    ks = jax.random.split(key, 40)
    f32 = jnp.float32
    beta = (8.0 * DEPTH) ** -0.25
    n_pages = PAST_LEN // PAGE_SIZE
    n_used = DEC_BATCH * n_pages
    n_pool = (n_used * 5) // 4

    def nrm(k, shape, scale=1.0):
        return jax.random.normal(k, shape, f32) * scale

    col_scale = jnp.concatenate([jnp.ones((K_END,), f32),
                                 jnp.full((ATTN_V_WIDTH + SSM_WIDTH,), beta, f32),
                                 jnp.ones((2 * D_MODEL,), f32)])
    page_table = jax.random.permutation(ks[5], n_pool)[:n_used].reshape(DEC_BATCH, n_pages).astype(jnp.int32)
    a_im_init = jnp.pi * jnp.arange(STATE_DIM, dtype=f32)
    return {
        'x_prompt': nrm(ks[0], (BATCH, SEQ, D_MODEL)),
        'x_sample': nrm(ks[1], (DEC_BATCH, DEC_SEQ, D_MODEL)),
        'cache_k': nrm(ks[2], (DEPTH, n_pool, PAGE_SIZE, N_HEADS, 2, HEAD_DIM)),
        'cache_v': nrm(ks[3], (DEPTH, n_pool, PAGE_SIZE, N_HEADS, V_DIM), beta),
        'state_ssm_re': nrm(ks[4], (DEPTH, DEC_BATCH, N_GROUPS, STATE_DIM), 0.3),
        'state_ssm_im': nrm(ks[6], (DEPTH, DEC_BATCH, N_GROUPS, STATE_DIM), 0.3),
        'state_conv': nrm(ks[7], (DEPTH, DEC_BATCH, CONV_W - 1, 2 * D_FF), beta),
        'page_table': page_table,
        'rel_bias': nrm(ks[8], (NUM_BUCKETS, N_HEADS), 0.5),
        'w_in': nrm(ks[9], (DEPTH, D_MODEL, IN_WIDTH), D_MODEL ** -0.5) * col_scale,
        'lambda_q1': nrm(ks[10], (DEPTH, HEAD_DIM), 0.1),
        'lambda_k1': nrm(ks[11], (DEPTH, HEAD_DIM), 0.1),
        'lambda_q2': nrm(ks[12], (DEPTH, HEAD_DIM), 0.1),
        'lambda_k2': nrm(ks[13], (DEPTH, HEAD_DIM), 0.1),
        'subln_g': 1.0 + nrm(ks[14], (DEPTH, V_DIM), 0.02),
        'ssm_a_re': -0.5 + nrm(ks[15], (DEPTH, N_GROUPS, STATE_DIM), 0.02),
        'ssm_a_im': a_im_init + nrm(ks[16], (DEPTH, N_GROUPS, STATE_DIM), 0.02),
        'ssm_log_dt': jax.random.uniform(ks[17], (DEPTH, N_GROUPS), f32, minval=math.log(1e-3), maxval=math.log(1e-1)),
        'ssm_b_re': nrm(ks[18], (DEPTH, N_GROUPS, STATE_DIM, SSM_GROUP), (2 * SSM_GROUP) ** -0.5),
        'ssm_b_im': nrm(ks[19], (DEPTH, N_GROUPS, STATE_DIM, SSM_GROUP), (2 * SSM_GROUP) ** -0.5),
        'ssm_c_re': nrm(ks[20], (DEPTH, N_GROUPS, SSM_GROUP, STATE_DIM), (2 * STATE_DIM) ** -0.5),
        'ssm_c_im': nrm(ks[21], (DEPTH, N_GROUPS, SSM_GROUP, STATE_DIM), (2 * STATE_DIM) ** -0.5),
        'ssm_d': nrm(ks[22], (DEPTH, SSM_WIDTH)),
        'w_glu': nrm(ks[23], (DEPTH, SSM_WIDTH, SSM_WIDTH), SSM_WIDTH ** -0.5),
        'w_proj_attn': nrm(ks[24], (DEPTH, ATTN_V_WIDTH, D_MODEL), beta * ATTN_V_WIDTH ** -0.5),
        'w_proj_ssm': nrm(ks[25], (DEPTH, SSM_WIDTH, D_MODEL), beta * SSM_WIDTH ** -0.5),
        'w_out': nrm(ks[26], (DEPTH, D_MODEL, D_MODEL), beta * D_MODEL ** -0.5),
        'ln1_g': 1.0 + nrm(ks[27], (DEPTH, D_MODEL), 0.02),
        'ln1_b': nrm(ks[28], (DEPTH, D_MODEL), 0.02),
        'w_up': nrm(ks[29], (DEPTH, D_MODEL, 2 * D_FF), beta * D_MODEL ** -0.5),
        'conv_w': nrm(ks[30], (DEPTH, CONV_W, 2 * D_FF), CONV_W ** -0.5),
        'conv_b': nrm(ks[31], (DEPTH, 2 * D_FF), 0.02),
        'w_down': nrm(ks[32], (DEPTH, D_FF, D_MODEL), beta * D_FF ** -0.5),
        'ln2_g': 1.0 + nrm(ks[33], (DEPTH, D_MODEL), 0.02),
        'ln2_b': nrm(ks[34], (DEPTH, D_MODEL), 0.02),
    }


def reference(x_prompt, x_sample, cache_k, cache_v, state_ssm_re, state_ssm_im, state_conv, page_table,
              rel_bias, w_in, lambda_q1, lambda_k1, lambda_q2, lambda_k2, subln_g, ssm_a_re, ssm_a_im,
              ssm_log_dt, ssm_b_re, ssm_b_im, ssm_c_re, ssm_c_im, ssm_d, w_glu, w_proj_attn, w_proj_ssm,
              w_out, ln1_g, ln1_b, w_up, conv_w, conv_b, w_down, ln2_g, ln2_b):
    bp = x_prompt.shape[0]
    bs = x_sample.shape[0]
    n_pages = page_table.shape[1]
    h_p = x_prompt
    h_s = x_sample
    kp_l, vp_l, rep_l, imp_l, cp_l = [], [], [], [], []
    ks_l, vs_l, res_l, ims_l, cs_l = [], [], [], [], []
    for l in range(DEPTH):
        lw = (w_in[l], lambda_q1[l], lambda_k1[l], lambda_q2[l], lambda_k2[l], subln_g[l], ssm_a_re[l],
              ssm_a_im[l], ssm_log_dt[l], ssm_b_re[l], ssm_b_im[l], ssm_c_re[l], ssm_c_im[l], ssm_d[l],
              w_glu[l], w_proj_attn[l], w_proj_ssm[l], w_out[l], ln1_g[l], ln1_b[l], w_up[l], conv_w[l],
              conv_b[l], w_down[l], ln2_g[l], ln2_b[l])
        zero_ssm = jnp.zeros((bp, N_GROUPS, STATE_DIM), x_prompt.dtype)
        zero_conv = jnp.zeros((bp, CONV_W - 1, 2 * D_FF), x_prompt.dtype)
        h_p, k_p, v_p, re_p, im_p, c_p = decoder_layer(h_p, prompt_attention, zero_ssm, zero_ssm, zero_conv,
                                                       l, rel_bias, lw)
        k_past = cache_k[l][page_table].reshape(bs, n_pages * PAGE_SIZE, N_HEADS, 2, HEAD_DIM)
        v_past = cache_v[l][page_table].reshape(bs, n_pages * PAGE_SIZE, N_HEADS, V_DIM)
        h_s, k_s, v_s, re_s, im_s, c_s = decoder_layer(h_s, functools.partial(sample_attention, k_past, v_past),
                                                       state_ssm_re[l], state_ssm_im[l], state_conv[l],
                                                       l, rel_bias, lw)
        kp_l.append(k_p); vp_l.append(v_p); rep_l.append(re_p); imp_l.append(im_p); cp_l.append(c_p)
        ks_l.append(k_s); vs_l.append(v_s); res_l.append(re_s); ims_l.append(im_s); cs_l.append(c_s)
    new_k_prompt = jnp.stack(kp_l, axis=0)
    new_v_prompt = jnp.stack(vp_l, axis=0)
    new_ssm_re_prompt = jnp.stack(rep_l, axis=0)
    new_ssm_im_prompt = jnp.stack(imp_l, axis=0)
    new_conv_prompt = jnp.stack(cp_l, axis=0)
    new_k_sample = jnp.stack(ks_l, axis=0)
    new_v_sample = jnp.stack(vs_l, axis=0)
    new_ssm_re_sample = jnp.stack(res_l, axis=0)
    new_ssm_im_sample = jnp.stack(ims_l, axis=0)
    new_conv_sample = jnp.stack(cs_l, axis=0)
    return (h_p, h_s, new_k_prompt, new_v_prompt, new_ssm_re_prompt, new_ssm_im_prompt, new_conv_prompt,
            new_k_sample, new_v_sample, new_ssm_re_sample, new_ssm_im_sample, new_conv_sample)
```

```python
import functools
import math

import jax
import jax.numpy as jnp
from jax import lax
from jax.experimental import pallas as pl
from jax.experimental.pallas import tpu as pltpu

F32 = jnp.float32
BF16 = jnp.bfloat16

N_HEADS = 8
HEAD_DIM = 64
V_DIM = 2 * HEAD_DIM
SSM_GROUP = 16
STATE_DIM = 64
CONV_W = 3
NUM_BUCKETS = 32
MAX_EXACT = NUM_BUCKETS // 2
MAX_DISTANCE = 128
LN_EPS = 1e-5
NEG_INF = -1e30

VMEM_LIMIT_BYTES = 56 * 1024 * 1024
LANES = 128
SLAB_GROUPS = 8
N_SLABS = 8
SLAB_IN = SLAB_GROUPS * SSM_GROUP
SLAB_STATE = SLAB_GROUPS * STATE_DIM


def _params(sem):
    return pltpu.CompilerParams(dimension_semantics=sem, vmem_limit_bytes=VMEM_LIMIT_BYTES)


def _mm_body(*refs, n_pairs, n_extra, n_out, epilogue):
    x_refs = refs[0:2 * n_pairs:2]
    w_refs = refs[1:2 * n_pairs:2]
    extra = refs[2 * n_pairs:2 * n_pairs + n_extra]
    outs = refs[2 * n_pairs + n_extra:2 * n_pairs + n_extra + n_out]
    wbf = refs[2 * n_pairs + n_extra + n_out:]

    @pl.when(pl.program_id(1) == 0)
    def _():
        for w, s in zip(w_refs, wbf):
            s[...] = w[...].astype(BF16)

    accs = [jnp.dot(x[...].astype(BF16), s[...], preferred_element_type=F32)
            for x, s in zip(x_refs, wbf)]
    res = epilogue(*accs, *[e[...] for e in extra])
    for o, r in zip(outs, res):
        o[...] = r.astype(o.dtype)


def _matmul(pairs, extras, out_dtypes, epilogue, *, n_cols, tm, tn, name):
    m = pairs[0][0].shape[0]
    tm = min(tm, m)
    grid = (n_cols // tn, m // tm)
    in_specs, args, scratch = [], [], []
    for x, w, c0 in pairs:
        k = x.shape[1]
        off = c0 // tn
        in_specs.append(pl.BlockSpec((tm, k), lambda j, i: (i, 0)))
        in_specs.append(pl.BlockSpec((k, tn), lambda j, i, off=off: (0, j + off)))
        args += [x, w]
        scratch.append(pltpu.VMEM((k, tn), BF16))
    for e in extras:
        if e.shape[0] == 1:
            in_specs.append(pl.BlockSpec((1, tn), lambda j, i: (0, j)))
        else:
            in_specs.append(pl.BlockSpec((tm, tn), lambda j, i: (i, j)))
        args.append(e)
    out_shape = [jax.ShapeDtypeStruct((m, n_cols), d) for d in out_dtypes]
    out_specs = [pl.BlockSpec((tm, tn), lambda j, i: (i, j)) for _ in out_dtypes]
    body = functools.partial(_mm_body, n_pairs=len(pairs), n_extra=len(extras),
                             n_out=len(out_dtypes), epilogue=epilogue)
    return pl.pallas_call(
        body, grid=grid, in_specs=in_specs, out_specs=out_specs, out_shape=out_shape,
        scratch_shapes=scratch, compiler_params=_params(("arbitrary", "arbitrary")),
        name=name)(*args)


def _ln_body(x_ref, g_ref, b_ref, o_ref, ob_ref):
    x = x_ref[...]
    mu = jnp.mean(x, axis=-1, keepdims=True)
    xc = x - mu
    var = jnp.mean(xc * xc, axis=-1, keepdims=True)
    y = xc * lax.rsqrt(var + LN_EPS) * g_ref[...] + b_ref[...]
    o_ref[...] = y
    ob_ref[...] = y.astype(BF16)


def _layer_norm(x, g, b, *, tm, name):
    m, d = x.shape
    tm = min(tm, m)
    return pl.pallas_call(
        _ln_body, grid=(m // tm,),
        in_specs=[pl.BlockSpec((tm, d), lambda i: (i, 0)),
                  pl.BlockSpec((1, d), lambda i: (0, 0)),
                  pl.BlockSpec((1, d), lambda i: (0, 0))],
        out_specs=[pl.BlockSpec((tm, d), lambda i: (i, 0)),
                   pl.BlockSpec((tm, d), lambda i: (i, 0))],
        out_shape=[jax.ShapeDtypeStruct((m, d), F32), jax.ShapeDtypeStruct((m, d), BF16)],
        compiler_params=_params(("arbitrary",)), name=name)(x, g.reshape(1, d), b.reshape(1, d))


def _rel_bucket(n):
    n = jnp.maximum(n, 0)
    nf = jnp.maximum(n, 1).astype(F32)
    large = MAX_EXACT + (jnp.log(nf / MAX_EXACT) / math.log(MAX_DISTANCE / MAX_EXACT)
                         * (NUM_BUCKETS - MAX_EXACT)).astype(jnp.int32)
    large = jnp.minimum(large, NUM_BUCKETS - 1)
    return jnp.where(n < MAX_EXACT, n, large)


def _bucket_lookup(bucket, table_fn):
    out = jnp.zeros(jnp.broadcast_shapes(bucket.shape, table_fn(0).shape), F32)
    for b in range(NUM_BUCKETS):
        out = out + jnp.where(bucket == b, table_fn(b), 0.0)
    return out


def _attn_body(lam_ref, rb_ref, q_ref, k_ref, v_ref, bucket_ref, g_ref, o_ref,
               bias_sc, m_sc, l_sc, acc_sc, *, t, out_scale):
    h = pl.program_id(0)
    b = pl.program_id(1)
    i = pl.program_id(2)

    @pl.when(jnp.logical_and(b == 0, i == 0))
    def _():
        for w in range(2):
            bias_sc[w] = _bucket_lookup(bucket_ref[w], lambda bk: rb_ref[bk, h])

    bias_far = rb_ref[NUM_BUCKETS - 1, h]

    q = q_ref[...] * (HEAD_DIM ** -0.5)
    lane = lax.broadcasted_iota(jnp.int32, q.shape, 1)
    zero = jnp.zeros_like(q)
    q_maps = (jnp.where(lane < HEAD_DIM, q, zero), jnp.where(lane >= HEAD_DIM, q, zero))

    m_sc[...] = jnp.full(m_sc.shape, NEG_INF, F32)
    l_sc[...] = jnp.zeros(l_sc.shape, F32)
    acc_sc[...] = jnp.zeros(acc_sc.shape, F32)

    def process(kb, bias, masked):
        start = pl.multiple_of(kb * t, t)
        kblk = k_ref[pl.ds(start, t), :]
        vblk = v_ref[pl.ds(start, t), :]
        for j in range(2):
            s = lax.dot_general(q_maps[j], kblk, (((1,), (1,)), ((), ())),
                                preferred_element_type=F32) + bias
            if masked:
                row = lax.broadcasted_iota(jnp.int32, s.shape, 0)
                col = lax.broadcasted_iota(jnp.int32, s.shape, 1)
                s = jnp.where(col <= row, s, NEG_INF)
            m_old = m_sc[j]
            m_new = jnp.maximum(m_old, jnp.max(s, axis=1, keepdims=True))
            a = jnp.exp(m_old - m_new)
            p = jnp.exp(s - m_new)
            l_sc[j] = a * l_sc[j] + jnp.sum(p, axis=1, keepdims=True)
            acc_sc[j] = a * acc_sc[j] + jnp.dot(p.astype(BF16), vblk, preferred_element_type=F32)
            m_sc[j] = m_new

    def far_body(kb, carry):
        process(kb, bias_far, False)
        return carry

    lax.fori_loop(0, jnp.maximum(i - 1, 0), far_body, 0)

    @pl.when(i >= 1)
    def _():
        process(i - 1, bias_sc[1], False)

    process(i, bias_sc[0], True)

    lam = lam_ref[0]
    o = acc_sc[0] / l_sc[0] - lam * (acc_sc[1] / l_sc[1])
    ms = jnp.mean(o * o, axis=-1, keepdims=True)
    o_ref[...] = (o * lax.rsqrt(ms + LN_EPS) * g_ref[...] * out_scale).astype(o_ref.dtype)


def _prompt_attention(q, k, v, rel_bias, lam, subln_g, *, batch, seq, t, out_scale):
    nq = seq // t
    r = jnp.arange(t, dtype=jnp.int32)
    diag = r[:, None] - r[None, :]
    buckets = jnp.stack([_rel_bucket(diag), _rel_bucket(diag + t)])
    smem = pl.BlockSpec(memory_space=pltpu.SMEM)
    body = functools.partial(_attn_body, t=t, out_scale=out_scale)
    return pl.pallas_call(
        body, grid=(N_HEADS, batch, nq),
        in_specs=[smem, smem,
                  pl.BlockSpec((t, V_DIM), lambda h, b, i: (b * nq + i, h)),
                  pl.BlockSpec((seq, V_DIM), lambda h, b, i: (b, h)),
                  pl.BlockSpec((seq, V_DIM), lambda h, b, i: (b, h)),
                  pl.BlockSpec((2, t, t), lambda h, b, i: (0, 0, 0)),
                  pl.BlockSpec((1, V_DIM), lambda h, b, i: (0, 0))],
        out_specs=pl.BlockSpec((t, V_DIM), lambda h, b, i: (b * nq + i, h)),
        out_shape=jax.ShapeDtypeStruct((batch * seq, N_HEADS * V_DIM), BF16),
        scratch_shapes=[pltpu.VMEM((2, t, t), F32), pltpu.VMEM((2, t, 1), F32),
                        pltpu.VMEM((2, t, 1), F32), pltpu.VMEM((2, t, V_DIM), F32)],
        compiler_params=_params(("arbitrary", "arbitrary", "arbitrary")),
        name="prompt_attention")(lam, rel_bias, q, k, v, buckets, subln_g.reshape(1, V_DIM))


def _decode_body(pt_ref, lam_ref, q_ref, kn_ref, vn_ref, rbt_ref, bucket_ref, g_ref, *rest,
                 pages, n_steps, out_scale):
    k_refs = rest[:pages]
    v_refs = rest[pages:2 * pages]
    o_ref = rest[2 * pages]
    qexp_sc, bias_sc, m_sc, l_sc, acc_sc = rest[2 * pages + 1:]
    step = pl.program_id(1)
    rows = 2 * N_HEADS
    width = N_HEADS * V_DIM

    row = lax.broadcasted_iota(jnp.int32, (rows, width), 0)
    col = lax.broadcasted_iota(jnp.int32, (rows, width), 1)
    own_qk = (col // HEAD_DIM) == (row % N_HEADS) * 2 + row // N_HEADS
    own_v = (col // V_DIM) == (row % N_HEADS)

    @pl.when(step == 0)
    def _():
        q = (q_ref[...] * (HEAD_DIM ** -0.5)).astype(BF16).astype(F32)
        qexp = jnp.where(own_qk, jnp.broadcast_to(q, (rows, width)), 0.0)
        qexp_sc[...] = qexp.astype(BF16)
        bias_sc[...] = _bucket_lookup(bucket_ref[...], lambda bk: rbt_ref[:, bk:bk + 1])
        kn = kn_ref[...].astype(BF16).astype(F32)
        s_self = jnp.sum(qexp * kn, axis=1, keepdims=True) + rbt_ref[:, 0:1]
        m_sc[...] = s_self
        l_sc[...] = jnp.ones(l_sc.shape, F32)
        vn = vn_ref[...].astype(BF16).astype(F32)
        acc_sc[...] = jnp.where(own_v, jnp.broadcast_to(vn, (rows, width)), 0.0)

    qexp = qexp_sc[...]
    bias_far = rbt_ref[:, NUM_BUCKETS - 1:NUM_BUCKETS]
    is_last = step == n_steps - 1
    s_parts = []
    for p in range(pages):
        s = lax.dot_general(qexp, k_refs[p][...].astype(BF16), (((1,), (1,)), ((), ())),
                            preferred_element_type=F32)
        if p == pages - 1:
            s = s + jnp.where(is_last, bias_sc[...], bias_far)
        else:
            s = s + bias_far
        s_parts.append(s)
    m_old = m_sc[...]
    m_new = m_old
    for s in s_parts:
        m_new = jnp.maximum(m_new, jnp.max(s, axis=1, keepdims=True))
    a = jnp.exp(m_old - m_new)
    l_new = a * l_sc[...]
    acc = a * acc_sc[...]
    for p in range(pages):
        pr = jnp.exp(s_parts[p] - m_new)
        l_new = l_new + jnp.sum(pr, axis=1, keepdims=True)
        acc = acc + jnp.dot(pr.astype(BF16), v_refs[p][...].astype(BF16),
                            preferred_element_type=F32)
    m_sc[...] = m_new
    l_sc[...] = l_new
    acc_sc[...] = acc

    @pl.when(is_last)
    def _():
        nrm = acc / l_new
        d = nrm[:N_HEADS] - lam_ref[0] * nrm[N_HEADS:]
        d = jnp.where(own_v[:N_HEADS], d, 0.0)
        ms = jnp.sum(d * d, axis=1, keepdims=True) * (1.0 / V_DIM)
        d = d * lax.rsqrt(ms + LN_EPS) * g_ref[...] * out_scale
        o_ref[...] = jnp.sum(d, axis=0, keepdims=True).astype(o_ref.dtype)


def _sample_attention(q, k_new, v_new, cache_k, cache_v, page_table, rel_bias, lam, subln_g,
                      *, pages, out_scale):
    bs, width = q.shape
    page = cache_k.shape[1]
    n_pages = page_table.shape[1]
    n_steps = n_pages // pages
    past = n_pages * page
    kpos = past - page + jnp.arange(page, dtype=jnp.int32)
    bucket_last = _rel_bucket(past - kpos).reshape(1, page)
    rbt = jnp.tile(rel_bias.T, (2, 1))
    g_tiled = jnp.tile(subln_g.reshape(1, V_DIM), (1, N_HEADS))

    def tok_spec():
        return pl.BlockSpec((None, 1, width), lambda b, s, pt: (b, 0, 0))

    def page_spec(p):
        return pl.BlockSpec((None, page, width), lambda b, s, pt, p=p: (pt[b, s * pages + p], 0, 0))

    full = lambda shape: pl.BlockSpec(shape, lambda b, s, pt: tuple(0 for _ in shape))
    in_specs = ([pl.BlockSpec(memory_space=pltpu.SMEM), tok_spec(), tok_spec(), tok_spec(),
                 full((2 * N_HEADS, NUM_BUCKETS)), full((1, page)), full((1, width))]
                + [page_spec(p) for p in range(pages)] + [page_spec(p) for p in range(pages)])
    rows = 2 * N_HEADS
    body = functools.partial(_decode_body, pages=pages, n_steps=n_steps, out_scale=out_scale)
    out = pl.pallas_call(
        body,
        grid_spec=pltpu.PrefetchScalarGridSpec(
            num_scalar_prefetch=1, grid=(bs, n_steps), in_specs=in_specs,
            out_specs=pl.BlockSpec((None, 1, width), lambda b, s, pt: (b, 0, 0)),
            scratch_shapes=[pltpu.VMEM((rows, width), BF16), pltpu.VMEM((rows, page), F32),
                            pltpu.VMEM((rows, 1), F32), pltpu.VMEM((rows, 1), F32),
                            pltpu.VMEM((rows, width), F32)]),
        out_shape=jax.ShapeDtypeStruct((bs, 1, width), BF16),
        compiler_params=_params(("arbitrary", "arbitrary")),
        name="sample_attention")(
            page_table, lam, q.reshape(bs, 1, width), k_new.reshape(bs, 1, width),
            v_new.reshape(bs, 1, width), rbt, bucket_last, g_tiled,
            *([cache_k] * pages), *([cache_v] * pages))
    return out.reshape(bs, width)


def _s5_params(a_re, a_im, log_dt, b_re, b_im, c_re, c_im):
    dt = jnp.exp(log_dt)[:, None]
    mag = jnp.exp(a_re * dt)
    ang = a_im * dt
    abar_re = mag * jnp.cos(ang)
    abar_im = mag * jnp.sin(ang)
    den = a_re * a_re + a_im * a_im
    f_re = ((abar_re - 1.0) * a_re + abar_im * a_im) / den
    f_im = (abar_im * a_re - (abar_re - 1.0) * a_im) / den
    bb_re = f_re[..., None] * b_re - f_im[..., None] * b_im
    bb_im = f_re[..., None] * b_im + f_im[..., None] * b_re
    eye = jnp.eye(SLAB_GROUPS, dtype=F32)

    def in_slabs(bb):
        tt = bb.reshape(N_SLABS, SLAB_GROUPS, STATE_DIM, SSM_GROUP).transpose(0, 1, 3, 2)
        full = tt[:, :, :, None, :] * eye[None, :, None, :, None]
        return full.reshape(N_SLABS, SLAB_IN, SLAB_STATE).astype(BF16)

    def out_slabs(cc):
        tt = cc.reshape(N_SLABS, SLAB_GROUPS, SSM_GROUP, STATE_DIM).transpose(0, 1, 3, 2)
        full = tt[:, :, :, None, :] * eye[None, :, None, :, None]
        return full.reshape(N_SLABS, SLAB_STATE, SLAB_IN).astype(BF16)

    return (abar_re.reshape(N_SLABS, SLAB_STATE), abar_im.reshape(N_SLABS, SLAB_STATE),
            in_slabs(bb_re), in_slabs(bb_im), out_slabs(c_re), out_slabs(c_im))


def _glu_out(y, u, d_ref, wglu_ref, o_ref):
    g = jax.nn.gelu(y + d_ref[...] * u)
    gate = jnp.dot(g.astype(BF16), wglu_ref[...].astype(BF16), preferred_element_type=F32)
    o_ref[...] = (g * jax.nn.sigmoid(gate)).astype(o_ref.dtype)


def _s5_prompt_body(u_ref, ar_ref, ai_ref, bbr_ref, bbi_ref, cr_ref, ci_ref, d_ref, wglu_ref,
                    o_ref, sre_ref, sim_ref, xr_sc, xi_sc, cr_sc, ci_sc, *, tc):
    c = pl.program_id(1)

    @pl.when(c == 0)
    def _():
        cr_sc[...] = jnp.zeros(cr_sc.shape, F32)
        ci_sc[...] = jnp.zeros(ci_sc.shape, F32)

    u = u_ref[...]
    ub = u.astype(BF16)
    nq = SLAB_STATE // LANES
    for j in range(N_SLABS):
        uj = ub[:, j * SLAB_IN:(j + 1) * SLAB_IN]
        br = jnp.dot(uj, bbr_ref[j], preferred_element_type=F32)
        bi = jnp.dot(uj, bbi_ref[j], preferred_element_type=F32)
        for qq in range(nq):
            xr_sc[qq, j * tc:(j + 1) * tc, :] = br[:, qq * LANES:(qq + 1) * LANES]
            xi_sc[qq, j * tc:(j + 1) * tc, :] = bi[:, qq * LANES:(qq + 1) * LANES]

    ar = [ar_ref[:, qq * LANES:(qq + 1) * LANES] for qq in range(nq)]
    ai = [ai_ref[:, qq * LANES:(qq + 1) * LANES] for qq in range(nq)]

    def step(tt, carry):
        rows = pl.ds(tt, N_SLABS, stride=tc)
        out = []
        for qq in range(nq):
            xr, xi = carry[2 * qq], carry[2 * qq + 1]
            nr = ar[qq] * xr - ai[qq] * xi + xr_sc[qq, rows, :]
            ni = ar[qq] * xi + ai[qq] * xr + xi_sc[qq, rows, :]
            xr_sc[qq, rows, :] = nr
            xi_sc[qq, rows, :] = ni
            out += [nr, ni]
        return tuple(out)

    init = []
    for qq in range(nq):
        init += [cr_sc[:, qq * LANES:(qq + 1) * LANES], ci_sc[:, qq * LANES:(qq + 1) * LANES]]
    fin = lax.fori_loop(0, tc, step, tuple(init))
    xr = jnp.concatenate(fin[0::2], axis=1)
    xi = jnp.concatenate(fin[1::2], axis=1)
    cr_sc[...] = xr
    ci_sc[...] = xi
    sre_ref[...] = xr
    sim_ref[...] = xi

    ys = []
    for j in range(N_SLABS):
        sr = jnp.concatenate([xr_sc[qq, j * tc:(j + 1) * tc, :] for qq in range(nq)], axis=1)
        si = jnp.concatenate([xi_sc[qq, j * tc:(j + 1) * tc, :] for qq in range(nq)], axis=1)
        ys.append(jnp.dot(sr.astype(BF16), cr_ref[j], preferred_element_type=F32)
                  - jnp.dot(si.astype(BF16), ci_ref[j], preferred_element_type=F32))
    _glu_out(jnp.concatenate(ys, axis=1), u, d_ref, wglu_ref, o_ref)


def _s5_prompt(u, sp, d, w_glu, *, batch, seq, tc):
    ar, ai, bbr, bbi, cr, ci = sp
    width = u.shape[1]
    nc = seq // tc
    const = lambda shape: pl.BlockSpec(shape, lambda b, c: tuple(0 for _ in shape))
    state_spec = pl.BlockSpec((None, N_SLABS, SLAB_STATE), lambda b, c: (b, 0, 0))
    state_shape = jax.ShapeDtypeStruct((batch, N_SLABS, SLAB_STATE), F32)
    return pl.pallas_call(
        functools.partial(_s5_prompt_body, tc=tc), grid=(batch, nc),
        in_specs=[pl.BlockSpec((tc, width), lambda b, c: (b * nc + c, 0)),
                  const(ar.shape), const(ai.shape), const(bbr.shape), const(bbi.shape),
                  const(cr.shape), const(ci.shape), const((1, width)), const(w_glu.shape)],
        out_specs=[pl.BlockSpec((tc, width), lambda b, c: (b * nc + c, 0)), state_spec, state_spec],
        out_shape=[jax.ShapeDtypeStruct((batch * seq, width), BF16), state_shape, state_shape],
        scratch_shapes=[pltpu.VMEM((SLAB_STATE // LANES, N_SLABS * tc, LANES), F32),
                        pltpu.VMEM((SLAB_STATE // LANES, N_SLABS * tc, LANES), F32),
                        pltpu.VMEM((N_SLABS, SLAB_STATE), F32),
                        pltpu.VMEM((N_SLABS, SLAB_STATE), F32)],
        compiler_params=_params(("arbitrary", "arbitrary")),
        name="s5_prompt")(u, ar, ai, bbr, bbi, cr, ci, d.reshape(1, width), w_glu)


def _s5_sample_body(u_ref, x0r_ref, x0i_ref, ar_ref, ai_ref, bbr_ref, bbi_ref, cr_ref, ci_ref,
                    d_ref, wglu_ref, o_ref, sre_ref, sim_ref):
    u = u_ref[...]
    ub = u.astype(BF16)
    ys = []
    for j in range(N_SLABS):
        uj = ub[:, j * SLAB_IN:(j + 1) * SLAB_IN]
        ar = ar_ref[j:j + 1, :]
        ai = ai_ref[j:j + 1, :]
        x0r = x0r_ref[j]
        x0i = x0i_ref[j]
        xr = ar * x0r - ai * x0i + jnp.dot(uj, bbr_ref[j], preferred_element_type=F32)
        xi = ar * x0i + ai * x0r + jnp.dot(uj, bbi_ref[j], preferred_element_type=F32)
        sre_ref[j] = xr
        sim_ref[j] = xi
        ys.append(jnp.dot(xr.astype(BF16), cr_ref[j], preferred_element_type=F32)
                  - jnp.dot(xi.astype(BF16), ci_ref[j], preferred_element_type=F32))
    _glu_out(jnp.concatenate(ys, axis=1), u, d_ref, wglu_ref, o_ref)


def _s5_sample(u, x0_re, x0_im, sp, d, w_glu):
    ar, ai, bbr, bbi, cr, ci = sp
    bs, width = u.shape
    state_shape = jax.ShapeDtypeStruct((N_SLABS, bs, SLAB_STATE), F32)
    return pl.pallas_call(
        _s5_sample_body,
        out_shape=[jax.ShapeDtypeStruct((bs, width), BF16), state_shape, state_shape],
        compiler_params=pltpu.CompilerParams(vmem_limit_bytes=VMEM_LIMIT_BYTES),
        name="s5_sample")(u, x0_re, x0_im, ar, ai, bbr, bbi, cr, ci, d.reshape(1, width), w_glu)


def _gated(c):
    gate = c[:, :LANES]
    return gate * jax.nn.sigmoid(gate) * c[:, LANES:]


def _up_prompt_body(h_ref, wg_ref, wv_ref, cwg_ref, cwv_ref, cbg_ref, cbv_ref,
                    act_ref, cg_ref, cv_ref):
    w = jnp.concatenate([wg_ref[...].astype(BF16), wv_ref[...].astype(BF16)], axis=1)
    up = jnp.dot(h_ref[...], w, preferred_element_type=F32)
    cw = jnp.concatenate([cwg_ref[...], cwv_ref[...]], axis=1)
    cb = jnp.concatenate([cbg_ref[...], cbv_ref[...]], axis=1)
    seq = up.shape[0]
    row = lax.broadcasted_iota(jnp.int32, up.shape, 0)
    c = cb + cw[CONV_W - 1:CONV_W] * up
    for back in range(1, CONV_W):
        shifted = jnp.where(row >= back, pltpu.roll(up, back, 0), 0.0)
        c = c + cw[CONV_W - 1 - back:CONV_W - back] * shifted
    act_ref[...] = _gated(c).astype(act_ref.dtype)
    tail = up[seq - 8:, :][8 - (CONV_W - 1):, :]
    cg_ref[...] = tail[:, :LANES]
    cv_ref[...] = tail[:, LANES:]


def _up_prompt(hb, w_up, conv_w, conv_b, *, batch, seq):
    d_model = hb.shape[1]
    d_ff = w_up.shape[1] // 2
    nb = d_ff // LANES
    cb = conv_b.reshape(1, 2 * d_ff)
    col = lambda off: (lambda b, j: (0, j + off))
    tail_spec = pl.BlockSpec((None, CONV_W - 1, LANES), lambda b, j: (b, 0, j))
    tail_shape = jax.ShapeDtypeStruct((batch, CONV_W - 1, d_ff), F32)
    return pl.pallas_call(
        _up_prompt_body, grid=(batch, nb),
        in_specs=[pl.BlockSpec((seq, d_model), lambda b, j: (b, 0)),
                  pl.BlockSpec((d_model, LANES), col(0)), pl.BlockSpec((d_model, LANES), col(nb)),
                  pl.BlockSpec((CONV_W, LANES), col(0)), pl.BlockSpec((CONV_W, LANES), col(nb)),
                  pl.BlockSpec((1, LANES), col(0)), pl.BlockSpec((1, LANES), col(nb))],
        out_specs=[pl.BlockSpec((seq, LANES), lambda b, j: (b, j)), tail_spec, tail_spec],
        out_shape=[jax.ShapeDtypeStruct((batch * seq, d_ff), BF16), tail_shape, tail_shape],
        compiler_params=_params(("arbitrary", "arbitrary")),
        name="up_prompt")(hb, w_up, w_up, conv_w, conv_w, cb, cb)


def _up_sample_body(h_ref, wg_ref, wv_ref, cwg_ref, cwv_ref, cbg_ref, cbv_ref, sg_ref, sv_ref,
                    act_ref, cg_ref, cv_ref):
    w = jnp.concatenate([wg_ref[...].astype(BF16), wv_ref[...].astype(BF16)], axis=1)
    up = jnp.dot(h_ref[...], w, preferred_element_type=F32)
    cw = jnp.concatenate([cwg_ref[...], cwv_ref[...]], axis=1)
    cb = jnp.concatenate([cbg_ref[...], cbv_ref[...]], axis=1)
    c = cb + cw[CONV_W - 1:CONV_W] * up
    for tap in range(CONV_W - 1):
        st = jnp.concatenate([sg_ref[tap], sv_ref[tap]], axis=1)
        c = c + cw[tap:tap + 1] * st
    act_ref[...] = _gated(c).astype(act_ref.dtype)
    for tap in range(1, CONV_W - 1):
        cg_ref[tap - 1] = sg_ref[tap]
        cv_ref[tap - 1] = sv_ref[tap]
    cg_ref[CONV_W - 2] = up[:, :LANES]
    cv_ref[CONV_W - 2] = up[:, LANES:]


def _up_sample(hb, w_up, conv_w, conv_b, state):
    bs, d_model = hb.shape
    d_ff = w_up.shape[1] // 2
    nb = d_ff // LANES
    cb = conv_b.reshape(1, 2 * d_ff)
    col = lambda off: (lambda j: (0, j + off))
    st = lambda off: pl.BlockSpec((CONV_W - 1, bs, LANES), lambda j, off=off: (0, 0, j + off))
    tail_spec = pl.BlockSpec((CONV_W - 1, bs, LANES), lambda j: (0, 0, j))
    tail_shape = jax.ShapeDtypeStruct((CONV_W - 1, bs, d_ff), F32)
    return pl.pallas_call(
        _up_sample_body, grid=(nb,),
        in_specs=[pl.BlockSpec((bs, d_model), lambda j: (0, 0)),
                  pl.BlockSpec((d_model, LANES), col(0)), pl.BlockSpec((d_model, LANES), col(nb)),
                  pl.BlockSpec((CONV_W, LANES), col(0)), pl.BlockSpec((CONV_W, LANES), col(nb)),
                  pl.BlockSpec((1, LANES), col(0)), pl.BlockSpec((1, LANES), col(nb)),
                  st(0), st(nb)],
        out_specs=[pl.BlockSpec((bs, LANES), lambda j: (0, j)), tail_spec, tail_spec],
        out_shape=[jax.ShapeDtypeStruct((bs, d_ff), BF16), tail_shape, tail_shape],
        compiler_params=_params(("arbitrary",)),
        name="up_sample")(hb, w_up, w_up, conv_w, conv_w, cb, cb, state, state)


def _one(x):
    return (x,)


def _sigmoid_out(acc):
    return (jax.nn.sigmoid(acc),)


def _both(acc):
    return (acc, acc)


def _merge(pa, ps, ga, gs):
    return (ga * pa + gs * ps,)


def _layer(x, w, *, batch, seq, alpha, lam, out_scale, attend, ssm, up, tm):
    d_model = x.shape[1]
    qk_w = N_HEADS * 2 * HEAD_DIM
    v_w = N_HEADS * V_DIM
    ssm_w = d_model // 2
    xb = x.astype(BF16)
    w_in = w["w_in"]
    c = 0
    (q,) = _matmul([(xb, w_in, c)], [], [BF16], _one, n_cols=qk_w, tm=tm, tn=1024, name="proj_q")
    c += qk_w
    k, kb = _matmul([(xb, w_in, c)], [], [F32, BF16], _both, n_cols=qk_w, tm=tm, tn=1024, name="proj_k")
    c += qk_w
    v, vb = _matmul([(xb, w_in, c)], [], [F32, BF16], _both, n_cols=v_w, tm=tm, tn=1024, name="proj_v")
    c += v_w
    (u,) = _matmul([(xb, w_in, c)], [], [F32], _one, n_cols=ssm_w, tm=tm, tn=1024, name="proj_u")
    c += ssm_w
    (ga,) = _matmul([(xb, w_in, c)], [], [F32], _sigmoid_out, n_cols=d_model, tm=tm, tn=1024, name="gate_a")
    c += d_model
    (gs,) = _matmul([(xb, w_in, c)], [], [F32], _sigmoid_out, n_cols=d_model, tm=tm, tn=1024, name="gate_s")

    attn = attend(q, k, v, kb, vb)
    ssm_out, s_re, s_im = ssm(u)

    (merged,) = _matmul([(attn, w["w_proj_attn"], 0), (ssm_out, w["w_proj_ssm"], 0)], [ga, gs], [BF16],
                        _merge, n_cols=d_model, tm=tm, tn=1024, name="merge")
    (r1,) = _matmul([(merged, w["w_out"], 0)], [x], [F32], lambda acc, res: (alpha * res + acc,),
                    n_cols=d_model, tm=tm, tn=1024, name="out_proj")
    h, hb = _layer_norm(r1, w["ln1_g"], w["ln1_b"], tm=tm, name="ln1")
    act, conv_new = up(hb)
    (r2,) = _matmul([(act, w["w_down"], 0)], [h], [F32], lambda acc, res: (alpha * res + acc,),
                    n_cols=d_model, tm=tm, tn=512, name="down_proj")
    y, _ = _layer_norm(r2, w["ln2_g"], w["ln2_b"], tm=tm, name="ln2")
    return y, k, v, s_re, s_im, conv_new


def kernel(x_prompt, x_sample, cache_k, cache_v, state_ssm_re, state_ssm_im, state_conv, page_table, rel_bias, w_in, lambda_q1, lambda_k1, lambda_q2, lambda_k2, subln_g, ssm_a_re, ssm_a_im, ssm_log_dt, ssm_b_re, ssm_b_im, ssm_c_re, ssm_c_im, ssm_d, w_glu, w_proj_attn, w_proj_ssm, w_out, ln1_g, ln1_b, w_up, conv_w, conv_b, w_down, ln2_g, ln2_b):
    depth = w_in.shape[0]
    assert depth == 1, "single-layer trunk"
    bp, seq, d_model = x_prompt.shape
    bs, dec_seq, _ = x_sample.shape
    assert dec_seq == 1
    n_pool, page = cache_k.shape[1], cache_k.shape[2]
    d_ff = w_down.shape[1]
    n_groups = ssm_a_re.shape[1]
    assert n_groups == N_SLABS * SLAB_GROUPS and d_ff % LANES == 0
    alpha = (2.0 * depth) ** 0.25
    width = N_HEADS * V_DIM

    hp = x_prompt.reshape(bp * seq, d_model)
    hs = x_sample.reshape(bs, d_model)
    outs = {}
    for l in range(depth):
        lam_init = 0.8 - 0.6 * math.exp(-0.3 * l)
        out_scale = 1.0 - lam_init
        lam = (jnp.exp(jnp.sum(lambda_q1[l] * lambda_k1[l]))
               - jnp.exp(jnp.sum(lambda_q2[l] * lambda_k2[l])) + lam_init).reshape(1)
        w = dict(w_in=w_in[l], w_proj_attn=w_proj_attn[l], w_proj_ssm=w_proj_ssm[l], w_out=w_out[l],
                 ln1_g=ln1_g[l], ln1_b=ln1_b[l], w_down=w_down[l], ln2_g=ln2_g[l], ln2_b=ln2_b[l])
        sp = _s5_params(ssm_a_re[l], ssm_a_im[l], ssm_log_dt[l], ssm_b_re[l], ssm_b_im[l],
                        ssm_c_re[l], ssm_c_im[l])

        def attend_p(q, k, v, kb, vb):
            return _prompt_attention(q, kb, vb, rel_bias, lam, subln_g[l], batch=bp, seq=seq,
                                     t=256, out_scale=out_scale)

        def ssm_p(u):
            return _s5_prompt(u, sp, ssm_d[l], w_glu[l], batch=bp, seq=seq, tc=256)

        def up_p(hb):
            act, cg, cv = _up_prompt(hb, w_up[l], conv_w[l], conv_b[l], batch=bp, seq=seq)
            return act, jnp.concatenate([cg, cv], axis=-1)

        hp, k_p, v_p, re_p, im_p, c_p = _layer(hp, w, batch=bp, seq=seq, alpha=alpha, lam=lam,
                                               out_scale=out_scale, attend=attend_p, ssm=ssm_p,
                                               up=up_p, tm=512)

        ck = cache_k[l].reshape(n_pool, page, width)
        cv_ = cache_v[l].reshape(n_pool, page, width)

        def attend_s(q, k, v, kb, vb):
            return _sample_attention(q.astype(F32), k, v, ck, cv_, page_table, rel_bias, lam,
                                     subln_g[l], pages=8, out_scale=out_scale)

        def ssm_s(u):
            x0r = state_ssm_re[l].reshape(bs, N_SLABS, SLAB_STATE).transpose(1, 0, 2)
            x0i = state_ssm_im[l].reshape(bs, N_SLABS, SLAB_STATE).transpose(1, 0, 2)
            o, sr, si = _s5_sample(u, x0r, x0i, sp, ssm_d[l], w_glu[l])
            return o, sr.transpose(1, 0, 2), si.transpose(1, 0, 2)

        def up_s(hb):
            act, cg, cv = _up_sample(hb, w_up[l], conv_w[l], conv_b[l],
                                     state_conv[l].transpose(1, 0, 2))
            return act, jnp.concatenate([cg, cv], axis=-1).transpose(1, 0, 2)

        hs, k_s, v_s, re_s, im_s, c_s = _layer(hs, w, batch=bs, seq=1, alpha=alpha, lam=lam,
                                               out_scale=out_scale, attend=attend_s, ssm=ssm_s,
                                               up=up_s, tm=512)

        for name, val in (("kp", k_p.reshape(bp, seq, N_HEADS, 2, HEAD_DIM)),
                          ("vp", v_p.reshape(bp, seq, N_HEADS, V_DIM)),
                          ("rep", re_p.reshape(bp, n_groups, STATE_DIM)),
                          ("imp", im_p.reshape(bp, n_groups, STATE_DIM)),
                          ("cp", c_p),
                          ("ks", k_s.reshape(bs, 1, N_HEADS, 2, HEAD_DIM)),
                          ("vs", v_s.reshape(bs, 1, N_HEADS, V_DIM)),
                          ("res", re_s.reshape(bs, n_groups, STATE_DIM)),
                          ("ims", im_s.reshape(bs, n_groups, STATE_DIM)),
                          ("cs", c_s)):
            outs.setdefault(name, []).append(val)

    st = {k: jnp.stack(v, axis=0) for k, v in outs.items()}
    return (hp.reshape(bp, seq, d_model), hs.reshape(bs, 1, d_model), st["kp"], st["vp"], st["rep"],
            st["imp"], st["cp"], st["ks"], st["vs"], st["res"], st["ims"], st["cs"])
```

```python
import functools
import math

import jax
import jax.numpy as jnp
from jax import lax
from jax.experimental import pallas as pl
from jax.experimental.pallas import tpu as pltpu

F32 = jnp.float32
BF16 = jnp.bfloat16

N_HEADS = 8
HEAD_DIM = 64
V_DIM = 2 * HEAD_DIM
SSM_GROUP = 16
STATE_DIM = 64
CONV_W = 3
NUM_BUCKETS = 32
MAX_EXACT = NUM_BUCKETS // 2
MAX_DISTANCE = 128
LN_EPS = 1e-5
NEG_INF = -1e30

VMEM_LIMIT_BYTES = 56 * 1024 * 1024
LANES = 128
SLAB_GROUPS = 8
N_SLABS = 8
SLAB_IN = SLAB_GROUPS * SSM_GROUP
SLAB_STATE = SLAB_GROUPS * STATE_DIM
SUBLANES = 8
SCAN_LEVELS = 3
SCAN_UNROLL = 2


def _params(sem):
    return pltpu.CompilerParams(dimension_semantics=sem, vmem_limit_bytes=VMEM_LIMIT_BYTES)


def _mm_body(*refs, n_pairs, n_extra, n_out, epilogue):
    x_refs = refs[0:2 * n_pairs:2]
    w_refs = refs[1:2 * n_pairs:2]
    extra = refs[2 * n_pairs:2 * n_pairs + n_extra]
    outs = refs[2 * n_pairs + n_extra:2 * n_pairs + n_extra + n_out]
    wbf = refs[2 * n_pairs + n_extra + n_out:]

    @pl.when(pl.program_id(1) == 0)
    def _():
        for w, s in zip(w_refs, wbf):
            s[...] = w[...].astype(BF16)

    accs = [jnp.dot(x[...].astype(BF16), s[...], preferred_element_type=F32)
            for x, s in zip(x_refs, wbf)]
    res = epilogue(*accs, *[e[...] for e in extra])
    for o, r in zip(outs, res):
        o[...] = r.astype(o.dtype)


def _matmul(pairs, extras, out_dtypes, epilogue, *, n_cols, tm, tn, name):
    m = pairs[0][0].shape[0]
    tm = min(tm, m)
    grid = (n_cols // tn, m // tm)
    in_specs, args, scratch = [], [], []
    for x, w, c0 in pairs:
        k = x.shape[1]
        off = c0 // tn
        in_specs.append(pl.BlockSpec((tm, k), lambda j, i: (i, 0)))
        in_specs.append(pl.BlockSpec((k, tn), lambda j, i, off=off: (0, j + off)))
        args += [x, w]
        scratch.append(pltpu.VMEM((k, tn), BF16))
    for e in extras:
        if e.shape[0] == 1:
            in_specs.append(pl.BlockSpec((1, tn), lambda j, i: (0, j)))
        else:
            in_specs.append(pl.BlockSpec((tm, tn), lambda j, i: (i, j)))
        args.append(e)
    out_shape = [jax.ShapeDtypeStruct((m, n_cols), d) for d in out_dtypes]
    out_specs = [pl.BlockSpec((tm, tn), lambda j, i: (i, j)) for _ in out_dtypes]
    body = functools.partial(_mm_body, n_pairs=len(pairs), n_extra=len(extras),
                             n_out=len(out_dtypes), epilogue=epilogue)
    return pl.pallas_call(
        body, grid=grid, in_specs=in_specs, out_specs=out_specs, out_shape=out_shape,
        scratch_shapes=scratch, compiler_params=_params(("arbitrary", "arbitrary")),
        name=name)(*args)


def _proj_t_body(x_ref, w_ref, o_ref, ob_ref, wt_sc):
    @pl.when(pl.program_id(0) == 0)
    def _():
        for c in range(w_ref.shape[1] // LANES):
            cols = slice(c * LANES, (c + 1) * LANES)
            wt_sc[cols, :] = w_ref[:, cols].T.astype(BF16)

    kt = lax.dot_general(wt_sc[...], x_ref[...], (((1,), (1,)), ((), ())),
                         preferred_element_type=F32)
    o_ref[...] = kt
    ob_ref[...] = kt.astype(BF16)


def _proj_transposed(x, w, c0, *, n_cols, batch, seq, tm, name):
    k = x.shape[1]
    per_b = seq // tm
    out_spec = pl.BlockSpec((None, n_cols, tm), lambda i: (i // per_b, 0, i % per_b))
    return pl.pallas_call(
        _proj_t_body, grid=(batch * per_b,),
        in_specs=[pl.BlockSpec((tm, k), lambda i: (i, 0)),
                  pl.BlockSpec((k, n_cols), lambda i, off=c0 // n_cols: (0, off))],
        out_specs=[out_spec, out_spec],
        out_shape=[jax.ShapeDtypeStruct((batch, n_cols, seq), F32),
                   jax.ShapeDtypeStruct((batch, n_cols, seq), BF16)],
        scratch_shapes=[pltpu.VMEM((n_cols, k), BF16)],
        compiler_params=_params(("arbitrary",)), name=name)(x, w)


def _ln_body(x_ref, g_ref, b_ref, o_ref, ob_ref):
    x = x_ref[...]
    mu = jnp.mean(x, axis=-1, keepdims=True)
    xc = x - mu
    var = jnp.mean(xc * xc, axis=-1, keepdims=True)
    y = xc * lax.rsqrt(var + LN_EPS) * g_ref[...] + b_ref[...]
    o_ref[...] = y
    ob_ref[...] = y.astype(BF16)


def _layer_norm(x, g, b, *, tm, name):
    m, d = x.shape
    tm = min(tm, m)
    return pl.pallas_call(
        _ln_body, grid=(m // tm,),
        in_specs=[pl.BlockSpec((tm, d), lambda i: (i, 0)),
                  pl.BlockSpec((1, d), lambda i: (0, 0)),
                  pl.BlockSpec((1, d), lambda i: (0, 0))],
        out_specs=[pl.BlockSpec((tm, d), lambda i: (i, 0)),
                   pl.BlockSpec((tm, d), lambda i: (i, 0))],
        out_shape=[jax.ShapeDtypeStruct((m, d), F32), jax.ShapeDtypeStruct((m, d), BF16)],
        compiler_params=_params(("arbitrary",)), name=name)(x, g.reshape(1, d), b.reshape(1, d))


def _rel_bucket(n):
    n = jnp.maximum(n, 0)
    nf = jnp.maximum(n, 1).astype(F32)
    large = MAX_EXACT + (jnp.log(nf / MAX_EXACT) / math.log(MAX_DISTANCE / MAX_EXACT)
                         * (NUM_BUCKETS - MAX_EXACT)).astype(jnp.int32)
    large = jnp.minimum(large, NUM_BUCKETS - 1)
    return jnp.where(n < MAX_EXACT, n, large)


def _bucket_lookup(bucket, table_fn):
    out = jnp.zeros(jnp.broadcast_shapes(bucket.shape, table_fn(0).shape), F32)
    for b in range(NUM_BUCKETS):
        out = out + jnp.where(bucket == b, table_fn(b), 0.0)
    return out


def _attn_body(lam_ref, rb_ref, q_ref, k_ref, v_ref, bucket_ref, g_ref, o_ref,
               bias_sc, m_sc, l_sc, acc_sc, *, t, out_scale):
    h = pl.program_id(0)
    b = pl.program_id(1)
    i = pl.program_id(2)

    @pl.when(jnp.logical_and(b == 0, i == 0))
    def _():
        for w in range(2):
            bias_sc[w] = _bucket_lookup(bucket_ref[w], lambda bk: rb_ref[bk, h])

    bias_far = rb_ref[NUM_BUCKETS - 1, h]

    q = q_ref[...] * (HEAD_DIM ** -0.5)
    lane = lax.broadcasted_iota(jnp.int32, q.shape, 1)
    zero = jnp.zeros_like(q)
    q_maps = (jnp.where(lane < HEAD_DIM, q, zero), jnp.where(lane >= HEAD_DIM, q, zero))

    m_sc[...] = jnp.full(m_sc.shape, NEG_INF, F32)
    l_sc[...] = jnp.zeros(l_sc.shape, F32)
    acc_sc[...] = jnp.zeros(acc_sc.shape, F32)

    def process(kb, bias, masked):
        start = pl.multiple_of(kb * t, t)
        kblk = k_ref[:, pl.ds(start, t)]
        vblk = v_ref[pl.ds(start, t), :]
        for j in range(2):
            s = jnp.dot(q_maps[j], kblk, preferred_element_type=F32) + bias
            if masked:
                row = lax.broadcasted_iota(jnp.int32, s.shape, 0)
                col = lax.broadcasted_iota(jnp.int32, s.shape, 1)
                s = jnp.where(col <= row, s, NEG_INF)
            m_old = m_sc[j]
            m_new = jnp.maximum(m_old, jnp.max(s, axis=1, keepdims=True))
            a = jnp.exp(m_old - m_new)
            p = jnp.exp(s - m_new)
            l_sc[j] = a * l_sc[j] + jnp.sum(p, axis=1, keepdims=True)
            acc_sc[j] = a * acc_sc[j] + jnp.dot(p.astype(BF16), vblk, preferred_element_type=F32)
            m_sc[j] = m_new

    def far_body(kb, carry):
        process(kb, bias_far, False)
        return carry

    lax.fori_loop(0, jnp.maximum(i - 1, 0), far_body, 0)

    @pl.when(i >= 1)
    def _():
        process(i - 1, bias_sc[1], False)

    process(i, bias_sc[0], True)

    lam = lam_ref[0]
    o = acc_sc[0] / l_sc[0] - lam * (acc_sc[1] / l_sc[1])
    ms = jnp.mean(o * o, axis=-1, keepdims=True)
    o_ref[...] = (o * lax.rsqrt(ms + LN_EPS) * g_ref[...] * out_scale).astype(o_ref.dtype)


def _prompt_attention(q, k, v, rel_bias, lam, subln_g, *, batch, seq, t, out_scale):
    nq = seq // t
    r = jnp.arange(t, dtype=jnp.int32)
    diag = r[:, None] - r[None, :]
    buckets = jnp.stack([_rel_bucket(diag), _rel_bucket(diag + t)])
    smem = pl.BlockSpec(memory_space=pltpu.SMEM)
    body = functools.partial(_attn_body, t=t, out_scale=out_scale)
    return pl.pallas_call(
        body, grid=(N_HEADS, batch, nq),
        in_specs=[smem, smem,
                  pl.BlockSpec((t, V_DIM), lambda h, b, i: (b * nq + i, h)),
                  pl.BlockSpec((None, V_DIM, seq), lambda h, b, i: (b, h, 0)),
                  pl.BlockSpec((seq, V_DIM), lambda h, b, i: (b, h)),
                  pl.BlockSpec((2, t, t), lambda h, b, i: (0, 0, 0)),
                  pl.BlockSpec((1, V_DIM), lambda h, b, i: (0, 0))],
        out_specs=pl.BlockSpec((t, V_DIM), lambda h, b, i: (b * nq + i, h)),
        out_shape=jax.ShapeDtypeStruct((batch * seq, N_HEADS * V_DIM), BF16),
        scratch_shapes=[pltpu.VMEM((2, t, t), F32), pltpu.VMEM((2, t, 1), F32),
                        pltpu.VMEM((2, t, 1), F32), pltpu.VMEM((2, t, V_DIM), F32)],
        compiler_params=_params(("arbitrary", "arbitrary", "arbitrary")),
        name="prompt_attention")(lam, rel_bias, q, k, v, buckets, subln_g.reshape(1, V_DIM))


def _decode_body(pt_ref, lam_ref, q_ref, kn_ref, vn_ref, rbt_ref, bucket_ref, g_ref, *rest,
                 pages, page, n_steps, out_scale):
    k_refs = rest[:pages]
    v_refs = rest[pages:2 * pages]
    o_ref = rest[2 * pages]
    qexp_sc, bias_sc, m_sc, l_sc, acc_sc = rest[2 * pages + 1:]
    step = pl.program_id(1)
    rows = 2 * N_HEADS
    width = N_HEADS * V_DIM

    @pl.when(step == 0)
    def _():
        row = lax.broadcasted_iota(jnp.int32, (rows, width), 0)
        col = lax.broadcasted_iota(jnp.int32, (rows, width), 1)
        own_qk = (col // HEAD_DIM) == (row % N_HEADS) * 2 + row // N_HEADS
        q = (q_ref[...] * (HEAD_DIM ** -0.5)).astype(BF16).astype(F32)
        qexp = jnp.where(own_qk, jnp.broadcast_to(q, (rows, width)), 0.0)
        qexp_sc[...] = qexp.astype(BF16)
        bias_sc[...] = _bucket_lookup(bucket_ref[...], lambda bk: rbt_ref[:, bk:bk + 1])
        kn = kn_ref[...].astype(BF16).astype(F32)
        s_self = jnp.sum(qexp * kn, axis=1, keepdims=True) + rbt_ref[:, 0:1]
        m_sc[...] = s_self
        l_sc[...] = jnp.ones(l_sc.shape, F32)
        vn = vn_ref[...].astype(BF16).astype(F32)
        acc_sc[...] = jnp.broadcast_to(vn[:, None, :], acc_sc.shape)

    qexp = qexp_sc[...]
    bias_far = rbt_ref[:, NUM_BUCKETS - 1:NUM_BUCKETS]
    is_last = step == n_steps - 1
    s_parts = []
    for p in range(pages):
        s = jnp.dot(qexp, k_refs[p][...].astype(BF16), preferred_element_type=F32)
        if p == pages - 1:
            s = s + jnp.where(is_last, bias_sc[...], bias_far)
        else:
            s = s + bias_far
        s_parts.append(s)
    m_old = m_sc[...]
    m_new = m_old
    for s in s_parts:
        m_new = jnp.maximum(m_new, jnp.max(s, axis=1, keepdims=True))
    a = jnp.exp(m_old - m_new)
    l_new = a * l_sc[...]
    probs = []
    for p in range(pages):
        pr = jnp.exp(s_parts[p] - m_new)
        l_new = l_new + jnp.sum(pr, axis=1, keepdims=True)
        probs.append(pr.astype(BF16))
    probs = jnp.concatenate(probs, axis=1)
    m_sc[...] = m_new
    l_sc[...] = l_new
    for h in range(N_HEADS):
        vh = jnp.concatenate([v_refs[p][pl.ds(h, page, stride=N_HEADS), :].astype(BF16)
                              for p in range(pages)], axis=0)
        acc_sc[h] = a * acc_sc[h] + jnp.dot(probs, vh, preferred_element_type=F32)

    @pl.when(is_last)
    def _():
        nrm = acc_sc[...] / l_new[None]
        r = lax.broadcasted_iota(jnp.int32, nrm.shape, 1)
        hh = lax.broadcasted_iota(jnp.int32, nrm.shape, 0)
        coef = jnp.where(r == hh, 1.0, jnp.where(r == hh + N_HEADS, -lam_ref[0], 0.0))
        d = jnp.sum(coef * nrm, axis=1)
        ms = jnp.mean(d * d, axis=-1, keepdims=True)
        o_ref[...] = (d * lax.rsqrt(ms + LN_EPS) * g_ref[...] * out_scale).astype(o_ref.dtype)


def _sample_attention(q, k_new, v_new, cache_k, cache_v, page_table, rel_bias, lam, subln_g,
                      *, pages, out_scale):
    bs, width = q.shape
    page = cache_k.shape[2]
    n_pages = page_table.shape[1]
    n_steps = n_pages // pages
    past = n_pages * page
    kpos = past - page + jnp.arange(page, dtype=jnp.int32)
    bucket_last = _rel_bucket(past - kpos).reshape(1, page)
    rbt = jnp.tile(rel_bias.T, (2, 1))

    def tok_spec():
        return pl.BlockSpec((None, 1, width), lambda b, s, pt: (b, 0, 0))

    def page_spec(shape, p):
        return pl.BlockSpec((None,) + shape, lambda b, s, pt, p=p: (pt[b, s * pages + p], 0, 0))

    full = lambda shape: pl.BlockSpec(shape, lambda b, s, pt: tuple(0 for _ in shape))
    head_spec = pl.BlockSpec((None, N_HEADS, V_DIM), lambda b, s, pt: (b, 0, 0))
    in_specs = ([pl.BlockSpec(memory_space=pltpu.SMEM), tok_spec(), tok_spec(), head_spec,
                 full((2 * N_HEADS, NUM_BUCKETS)), full((1, page)), full((1, V_DIM))]
                + [page_spec((width, page), p) for p in range(pages)]
                + [page_spec((page * N_HEADS, V_DIM), p) for p in range(pages)])
    rows = 2 * N_HEADS
    body = functools.partial(_decode_body, pages=pages, page=page, n_steps=n_steps,
                             out_scale=out_scale)
    out = pl.pallas_call(
        body,
        grid_spec=pltpu.PrefetchScalarGridSpec(
            num_scalar_prefetch=1, grid=(bs, n_steps), in_specs=in_specs,
            out_specs=head_spec,
            scratch_shapes=[pltpu.VMEM((rows, width), BF16), pltpu.VMEM((rows, page), F32),
                            pltpu.VMEM((rows, 1), F32), pltpu.VMEM((rows, 1), F32),
                            pltpu.VMEM((N_HEADS, rows, V_DIM), F32)]),
        out_shape=jax.ShapeDtypeStruct((bs, N_HEADS, V_DIM), BF16),
        compiler_params=_params(("arbitrary", "arbitrary")),
        name="sample_attention")(
            page_table, lam, q.reshape(bs, 1, width), k_new.reshape(bs, 1, width),
            v_new.reshape(bs, N_HEADS, V_DIM), rbt, bucket_last, subln_g.reshape(1, V_DIM),
            *([cache_k] * pages), *([cache_v] * pages))
    return out.reshape(bs, width)


def _s5_params(a_re, a_im, log_dt, b_re, b_im, c_re, c_im):
    dt = jnp.exp(log_dt)[:, None]
    mag = jnp.exp(a_re * dt)
    ang = a_im * dt
    abar_re = mag * jnp.cos(ang)
    abar_im = mag * jnp.sin(ang)
    den = a_re * a_re + a_im * a_im
    f_re = ((abar_re - 1.0) * a_re + abar_im * a_im) / den
    f_im = (abar_im * a_re - (abar_re - 1.0) * a_im) / den
    bb_re = f_re[..., None] * b_re - f_im[..., None] * b_im
    bb_im = f_re[..., None] * b_im + f_im[..., None] * b_re
    eye = jnp.eye(SLAB_GROUPS, dtype=F32)

    def in_slabs(bb):
        tt = bb.reshape(N_SLABS, SLAB_GROUPS, STATE_DIM, SSM_GROUP).transpose(0, 1, 3, 2)
        full = tt[:, :, :, None, :] * eye[None, :, None, :, None]
        return full.reshape(N_SLABS, SLAB_IN, SLAB_STATE).astype(BF16)

    def out_slabs(cc):
        tt = cc.reshape(N_SLABS, SLAB_GROUPS, SSM_GROUP, STATE_DIM).transpose(0, 1, 3, 2)
        full = tt[:, :, :, None, :] * eye[None, :, None, :, None]
        return full.reshape(N_SLABS, SLAB_STATE, SLAB_IN).astype(BF16)

    return (abar_re.reshape(N_SLABS, SLAB_STATE), abar_im.reshape(N_SLABS, SLAB_STATE),
            in_slabs(bb_re), in_slabs(bb_im), out_slabs(c_re), out_slabs(c_im))


def _glu_out(y, u, d_ref, wglu_ref, o_ref):
    g = jax.nn.gelu(y + d_ref[...] * u)
    gate = jnp.dot(g.astype(BF16), wglu_ref[...].astype(BF16), preferred_element_type=F32)
    o_ref[...] = (g * jax.nn.sigmoid(gate)).astype(o_ref.dtype)


def _s5_prompt_body(u_ref, ar_ref, ai_ref, bbr_ref, bbi_ref, cr_ref, ci_ref, d_ref, wglu_ref,
                    o_ref, sre_ref, sim_ref, xr_sc, xi_sc, pw_sc, cr_sc, ci_sc, *, tc):
    b = pl.program_id(0)
    c = pl.program_id(1)
    sub = SUBLANES

    def cmul(pr, pi, qr, qi):
        return pr * qr - pi * qi, pr * qi + pi * qr

    @pl.when(jnp.logical_and(b == 0, c == 0))
    def _():
        t = lax.broadcasted_iota(jnp.int32, (sub, SLAB_STATE), 0)
        for j in range(N_SLABS):
            a = (jnp.broadcast_to(ar_ref[j:j + 1, :], t.shape), jnp.broadcast_to(ai_ref[j:j + 1, :], t.shape))
            powers = [a]
            for _ in range(sub - 1):
                powers.append(cmul(*powers[-1], *a))
            for lvl in range(SCAN_LEVELS):
                k = 1 << lvl
                for part in range(2):
                    pw_sc[j, 2 * lvl + part] = jnp.where(t >= k, powers[k - 1][part], 0.0)
            for part in range(2):
                acc = powers[sub - 1][part]
                for row in range(sub - 1):
                    acc = jnp.where(t == row, powers[row][part], acc)
                pw_sc[j, 2 * SCAN_LEVELS + part] = acc

    @pl.when(c == 0)
    def _():
        cr_sc[...] = jnp.zeros(cr_sc.shape, F32)
        ci_sc[...] = jnp.zeros(ci_sc.shape, F32)

    u = u_ref[...]
    ub = u.astype(BF16)
    for j in range(N_SLABS):
        uj = ub[:, j * SLAB_IN:(j + 1) * SLAB_IN]
        xr_sc[j] = jnp.dot(uj, bbr_ref[j], preferred_element_type=F32)
        xi_sc[j] = jnp.dot(uj, bbi_ref[j], preferred_element_type=F32)

    def slab(j, _):
        def tile(v, carry):
            cr, ci = carry
            rows = pl.ds(pl.multiple_of(v * sub, sub), sub)
            xr = xr_sc[j, rows, :]
            xi = xi_sc[j, rows, :]
            for lvl in range(SCAN_LEVELS):
                k = 1 << lvl
                dr, di = cmul(pw_sc[j, 2 * lvl], pw_sc[j, 2 * lvl + 1],
                              pltpu.roll(xr, k, 0), pltpu.roll(xi, k, 0))
                xr, xi = xr + dr, xi + di
            dr, di = cmul(pw_sc[j, 2 * SCAN_LEVELS], pw_sc[j, 2 * SCAN_LEVELS + 1], cr, ci)
            xr, xi = xr + dr, xi + di
            xr_sc[j, rows, :] = xr
            xi_sc[j, rows, :] = xi
            return (jnp.broadcast_to(xr[sub - 1:sub, :], xr.shape),
                    jnp.broadcast_to(xi[sub - 1:sub, :], xi.shape))

        row = pl.ds(j, 1)
        init = (jnp.broadcast_to(cr_sc[row, :], (sub, SLAB_STATE)),
                jnp.broadcast_to(ci_sc[row, :], (sub, SLAB_STATE)))
        cr, ci = lax.fori_loop(0, tc // sub, tile, init, unroll=SCAN_UNROLL)
        cr_sc[row, :] = cr[0:1, :]
        ci_sc[row, :] = ci[0:1, :]
        return 0

    lax.fori_loop(0, N_SLABS, slab, 0)
    sre_ref[...] = cr_sc[...]
    sim_ref[...] = ci_sc[...]

    ys = []
    for j in range(N_SLABS):
        ys.append(jnp.dot(xr_sc[j].astype(BF16), cr_ref[j], preferred_element_type=F32)
                  - jnp.dot(xi_sc[j].astype(BF16), ci_ref[j], preferred_element_type=F32))
    _glu_out(jnp.concatenate(ys, axis=1), u, d_ref, wglu_ref, o_ref)


def _s5_prompt(u, sp, d, w_glu, *, batch, seq, tc):
    ar, ai, bbr, bbi, cr, ci = sp
    width = u.shape[1]
    nc = seq // tc
    const = lambda shape: pl.BlockSpec(shape, lambda b, c: tuple(0 for _ in shape))
    state_spec = pl.BlockSpec((None, N_SLABS, SLAB_STATE), lambda b, c: (b, 0, 0))
    state_shape = jax.ShapeDtypeStruct((batch, N_SLABS, SLAB_STATE), F32)
    return pl.pallas_call(
        functools.partial(_s5_prompt_body, tc=tc), grid=(batch, nc),
        in_specs=[pl.BlockSpec((tc, width), lambda b, c: (b * nc + c, 0)),
                  const(ar.shape), const(ai.shape), const(bbr.shape), const(bbi.shape),
                  const(cr.shape), const(ci.shape), const((1, width)), const(w_glu.shape)],
        out_specs=[pl.BlockSpec((tc, width), lambda b, c: (b * nc + c, 0)), state_spec, state_spec],
        out_shape=[jax.ShapeDtypeStruct((batch * seq, width), BF16), state_shape, state_shape],
        scratch_shapes=[pltpu.VMEM((N_SLABS, tc, SLAB_STATE), F32),
                        pltpu.VMEM((N_SLABS, tc, SLAB_STATE), F32),
                        pltpu.VMEM((N_SLABS, 2 * SCAN_LEVELS + 2, SUBLANES, SLAB_STATE), F32),
                        pltpu.VMEM((N_SLABS, SLAB_STATE), F32),
                        pltpu.VMEM((N_SLABS, SLAB_STATE), F32)],
        compiler_params=_params(("arbitrary", "arbitrary")),
        name="s5_prompt")(u, ar, ai, bbr, bbi, cr, ci, d.reshape(1, width), w_glu)


def _s5_sample_body(u_ref, x0r_ref, x0i_ref, ar_ref, ai_ref, bbr_ref, bbi_ref, cr_ref, ci_ref,
                    d_ref, wglu_ref, o_ref, sre_ref, sim_ref):
    u = u_ref[...]
    ub = u.astype(BF16)
    ys = []
    for j in range(N_SLABS):
        uj = ub[:, j * SLAB_IN:(j + 1) * SLAB_IN]
        ar = ar_ref[j:j + 1, :]
        ai = ai_ref[j:j + 1, :]
        x0r = x0r_ref[j]
        x0i = x0i_ref[j]
        xr = ar * x0r - ai * x0i + jnp.dot(uj, bbr_ref[j], preferred_element_type=F32)
        xi = ar * x0i + ai * x0r + jnp.dot(uj, bbi_ref[j], preferred_element_type=F32)
        sre_ref[j] = xr
        sim_ref[j] = xi
        ys.append(jnp.dot(xr.astype(BF16), cr_ref[j], preferred_element_type=F32)
                  - jnp.dot(xi.astype(BF16), ci_ref[j], preferred_element_type=F32))
    _glu_out(jnp.concatenate(ys, axis=1), u, d_ref, wglu_ref, o_ref)


def _s5_sample(u, x0_re, x0_im, sp, d, w_glu):
    ar, ai, bbr, bbi, cr, ci = sp
    bs, width = u.shape
    state_shape = jax.ShapeDtypeStruct((N_SLABS, bs, SLAB_STATE), F32)
    return pl.pallas_call(
        _s5_sample_body,
        out_shape=[jax.ShapeDtypeStruct((bs, width), BF16), state_shape, state_shape],
        compiler_params=pltpu.CompilerParams(vmem_limit_bytes=VMEM_LIMIT_BYTES),
        name="s5_sample")(u, x0_re, x0_im, ar, ai, bbr, bbi, cr, ci, d.reshape(1, width), w_glu)


def _gated(c):
    gate = c[:, :LANES]
    return gate * jax.nn.sigmoid(gate) * c[:, LANES:]


def _up_prompt_body(h_ref, wg_ref, wv_ref, cwg_ref, cwv_ref, cbg_ref, cbv_ref,
                    act_ref, cg_ref, cv_ref):
    w = jnp.concatenate([wg_ref[...].astype(BF16), wv_ref[...].astype(BF16)], axis=1)
    up = jnp.dot(h_ref[...], w, preferred_element_type=F32)
    cw = jnp.concatenate([cwg_ref[...], cwv_ref[...]], axis=1)
    cb = jnp.concatenate([cbg_ref[...], cbv_ref[...]], axis=1)
    seq = up.shape[0]
    row = lax.broadcasted_iota(jnp.int32, up.shape, 0)
    c = cb + cw[CONV_W - 1:CONV_W] * up
    for back in range(1, CONV_W):
        shifted = jnp.where(row >= back, pltpu.roll(up, back, 0), 0.0)
        c = c + cw[CONV_W - 1 - back:CONV_W - back] * shifted
    act_ref[...] = _gated(c).astype(act_ref.dtype)
    tail = up[seq - 8:, :][8 - (CONV_W - 1):, :]
    cg_ref[...] = tail[:, :LANES]
    cv_ref[...] = tail[:, LANES:]


def _up_prompt(hb, w_up, conv_w, conv_b, *, batch, seq):
    d_model = hb.shape[1]
    d_ff = w_up.shape[1] // 2
    nb = d_ff // LANES
    cb = conv_b.reshape(1, 2 * d_ff)
    col = lambda off: (lambda b, j: (0, j + off))
    tail_spec = pl.BlockSpec((None, CONV_W - 1, LANES), lambda b, j: (b, 0, j))
    tail_shape = jax.ShapeDtypeStruct((batch, CONV_W - 1, d_ff), F32)
    return pl.pallas_call(
        _up_prompt_body, grid=(batch, nb),
        in_specs=[pl.BlockSpec((seq, d_model), lambda b, j: (b, 0)),
                  pl.BlockSpec((d_model, LANES), col(0)), pl.BlockSpec((d_model, LANES), col(nb)),
                  pl.BlockSpec((CONV_W, LANES), col(0)), pl.BlockSpec((CONV_W, LANES), col(nb)),
                  pl.BlockSpec((1, LANES), col(0)), pl.BlockSpec((1, LANES), col(nb))],
        out_specs=[pl.BlockSpec((seq, LANES), lambda b, j: (b, j)), tail_spec, tail_spec],
        out_shape=[jax.ShapeDtypeStruct((batch * seq, d_ff), BF16), tail_shape, tail_shape],
        compiler_params=_params(("arbitrary", "arbitrary")),
        name="up_prompt")(hb, w_up, w_up, conv_w, conv_w, cb, cb)


def _up_sample_body(h_ref, wg_ref, wv_ref, cwg_ref, cwv_ref, cbg_ref, cbv_ref, sg_ref, sv_ref,
                    act_ref, cg_ref, cv_ref):
    w = jnp.concatenate([wg_ref[...].astype(BF16), wv_ref[...].astype(BF16)], axis=1)
    up = jnp.dot(h_ref[...], w, preferred_element_type=F32)
    cw = jnp.concatenate([cwg_ref[...], cwv_ref[...]], axis=1)
    cb = jnp.concatenate([cbg_ref[...], cbv_ref[...]], axis=1)
    c = cb + cw[CONV_W - 1:CONV_W] * up
    for tap in range(CONV_W - 1):
        st = jnp.concatenate([sg_ref[tap], sv_ref[tap]], axis=1)
        c = c + cw[tap:tap + 1] * st
    act_ref[...] = _gated(c).astype(act_ref.dtype)
    for tap in range(1, CONV_W - 1):
        cg_ref[tap - 1] = sg_ref[tap]
        cv_ref[tap - 1] = sv_ref[tap]
    cg_ref[CONV_W - 2] = up[:, :LANES]
    cv_ref[CONV_W - 2] = up[:, LANES:]


def _up_sample(hb, w_up, conv_w, conv_b, state):
    bs, d_model = hb.shape
    d_ff = w_up.shape[1] // 2
    nb = d_ff // LANES
    cb = conv_b.reshape(1, 2 * d_ff)
    col = lambda off: (lambda j: (0, j + off))
    st = lambda off: pl.BlockSpec((CONV_W - 1, bs, LANES), lambda j, off=off: (0, 0, j + off))
    tail_spec = pl.BlockSpec((CONV_W - 1, bs, LANES), lambda j: (0, 0, j))
    tail_shape = jax.ShapeDtypeStruct((CONV_W - 1, bs, d_ff), F32)
    return pl.pallas_call(
        _up_sample_body, grid=(nb,),
        in_specs=[pl.BlockSpec((bs, d_model), lambda j: (0, 0)),
                  pl.BlockSpec((d_model, LANES), col(0)), pl.BlockSpec((d_model, LANES), col(nb)),
                  pl.BlockSpec((CONV_W, LANES), col(0)), pl.BlockSpec((CONV_W, LANES), col(nb)),
                  pl.BlockSpec((1, LANES), col(0)), pl.BlockSpec((1, LANES), col(nb)),
                  st(0), st(nb)],
        out_specs=[pl.BlockSpec((bs, LANES), lambda j: (0, j)), tail_spec, tail_spec],
        out_shape=[jax.ShapeDtypeStruct((bs, d_ff), BF16), tail_shape, tail_shape],
        compiler_params=_params(("arbitrary",)),
        name="up_sample")(hb, w_up, w_up, conv_w, conv_w, cb, cb, state, state)


def _one(x):
    return (x,)


def _sigmoid_out(acc):
    return (jax.nn.sigmoid(acc),)


def _both(acc):
    return (acc, acc)


def _merge(pa, ps, ga, gs):
    return (ga * pa + gs * ps,)


def _layer(x, w, *, alpha, attend, ssm, up, tm, k_transposed=None):
    d_model = x.shape[1]
    qk_w = N_HEADS * 2 * HEAD_DIM
    v_w = N_HEADS * V_DIM
    ssm_w = d_model // 2
    xb = x.astype(BF16)
    w_in = w["w_in"]
    c = 0
    (q,) = _matmul([(xb, w_in, c)], [], [BF16], _one, n_cols=qk_w, tm=tm, tn=1024, name="proj_q")
    c += qk_w
    if k_transposed is None:
        k, kb = _matmul([(xb, w_in, c)], [], [F32, BF16], _both, n_cols=qk_w, tm=tm, tn=1024,
                        name="proj_k")
    else:
        k, kb = _proj_transposed(xb, w_in, c, n_cols=qk_w, batch=k_transposed[0],
                                 seq=k_transposed[1], tm=tm, name="proj_kt")
    c += qk_w
    v, vb = _matmul([(xb, w_in, c)], [], [F32, BF16], _both, n_cols=v_w, tm=tm, tn=1024, name="proj_v")
    c += v_w
    (u,) = _matmul([(xb, w_in, c)], [], [F32], _one, n_cols=ssm_w, tm=tm, tn=1024, name="proj_u")
    c += ssm_w
    (ga,) = _matmul([(xb, w_in, c)], [], [F32], _sigmoid_out, n_cols=d_model, tm=tm, tn=1024, name="gate_a")
    c += d_model
    (gs,) = _matmul([(xb, w_in, c)], [], [F32], _sigmoid_out, n_cols=d_model, tm=tm, tn=1024, name="gate_s")

    attn = attend(q, k, v, kb, vb)
    ssm_out, s_re, s_im = ssm(u)

    (merged,) = _matmul([(attn, w["w_proj_attn"], 0), (ssm_out, w["w_proj_ssm"], 0)], [ga, gs], [BF16],
                        _merge, n_cols=d_model, tm=tm, tn=1024, name="merge")
    (r1,) = _matmul([(merged, w["w_out"], 0)], [x], [F32], lambda acc, res: (alpha * res + acc,),
                    n_cols=d_model, tm=tm, tn=1024, name="out_proj")
    h, hb = _layer_norm(r1, w["ln1_g"], w["ln1_b"], tm=tm, name="ln1")
    act, conv_new = up(hb)
    (r2,) = _matmul([(act, w["w_down"], 0)], [h], [F32], lambda acc, res: (alpha * res + acc,),
                    n_cols=d_model, tm=tm, tn=512, name="down_proj")
    y, _ = _layer_norm(r2, w["ln2_g"], w["ln2_b"], tm=tm, name="ln2")
    return y, k, v, s_re, s_im, conv_new


def kernel(x_prompt, x_sample, cache_k, cache_v, state_ssm_re, state_ssm_im, state_conv, page_table, rel_bias, w_in, lambda_q1, lambda_k1, lambda_q2, lambda_k2, subln_g, ssm_a_re, ssm_a_im, ssm_log_dt, ssm_b_re, ssm_b_im, ssm_c_re, ssm_c_im, ssm_d, w_glu, w_proj_attn, w_proj_ssm, w_out, ln1_g, ln1_b, w_up, conv_w, conv_b, w_down, ln2_g, ln2_b):
    depth = w_in.shape[0]
    assert depth == 1, "single-layer trunk"
    bp, seq, d_model = x_prompt.shape
    bs, dec_seq, _ = x_sample.shape
    assert dec_seq == 1
    n_pool, page = cache_k.shape[1], cache_k.shape[2]
    d_ff = w_down.shape[1]
    n_groups = ssm_a_re.shape[1]
    assert n_groups == N_SLABS * SLAB_GROUPS and d_ff % LANES == 0
    alpha = (2.0 * depth) ** 0.25
    width = N_HEADS * V_DIM

    hp = x_prompt.reshape(bp * seq, d_model)
    hs = x_sample.reshape(bs, d_model)
    outs = {}
    for l in range(depth):
        lam_init = 0.8 - 0.6 * math.exp(-0.3 * l)
        out_scale = 1.0 - lam_init
        lam = (jnp.exp(jnp.sum(lambda_q1[l] * lambda_k1[l]))
               - jnp.exp(jnp.sum(lambda_q2[l] * lambda_k2[l])) + lam_init).reshape(1)
        w = dict(w_in=w_in[l], w_proj_attn=w_proj_attn[l], w_proj_ssm=w_proj_ssm[l], w_out=w_out[l],
                 ln1_g=ln1_g[l], ln1_b=ln1_b[l], w_down=w_down[l], ln2_g=ln2_g[l], ln2_b=ln2_b[l])
        sp = _s5_params(ssm_a_re[l], ssm_a_im[l], ssm_log_dt[l], ssm_b_re[l], ssm_b_im[l],
                        ssm_c_re[l], ssm_c_im[l])

        def attend_p(q, k, v, kb, vb):
            return _prompt_attention(q, kb, vb, rel_bias, lam, subln_g[l], batch=bp, seq=seq,
                                     t=256, out_scale=out_scale)

        def ssm_p(u):
            return _s5_prompt(u, sp, ssm_d[l], w_glu[l], batch=bp, seq=seq, tc=256)

        def up_p(hb):
            act, cg, cv = _up_prompt(hb, w_up[l], conv_w[l], conv_b[l], batch=bp, seq=seq)
            return act, jnp.concatenate([cg, cv], axis=-1)

        hp, kt_p, v_p, re_p, im_p, c_p = _layer(hp, w, alpha=alpha, attend=attend_p, ssm=ssm_p,
                                                up=up_p, tm=512, k_transposed=(bp, seq))
        k_p = kt_p.reshape(bp, N_HEADS, 2, HEAD_DIM, seq).transpose(0, 4, 1, 2, 3)

        ck = cache_k[l].transpose(0, 2, 3, 4, 1).reshape(n_pool, width, page)
        cv_ = cache_v[l].reshape(n_pool, page * N_HEADS, V_DIM)

        def attend_s(q, k, v, kb, vb):
            return _sample_attention(q.astype(F32), k, v, ck, cv_, page_table, rel_bias, lam,
                                     subln_g[l], pages=8, out_scale=out_scale)

        def ssm_s(u):
            x0r = state_ssm_re[l].reshape(bs, N_SLABS, SLAB_STATE).transpose(1, 0, 2)
            x0i = state_ssm_im[l].reshape(bs, N_SLABS, SLAB_STATE).transpose(1, 0, 2)
            o, sr, si = _s5_sample(u, x0r, x0i, sp, ssm_d[l], w_glu[l])
            return o, sr.transpose(1, 0, 2), si.transpose(1, 0, 2)

        def up_s(hb):
            act, cg, cv = _up_sample(hb, w_up[l], conv_w[l], conv_b[l],
                                     state_conv[l].transpose(1, 0, 2))
            return act, jnp.concatenate([cg, cv], axis=-1).transpose(1, 0, 2)

        hs, k_s, v_s, re_s, im_s, c_s = _layer(hs, w, alpha=alpha, attend=attend_s, ssm=ssm_s,
                                               up=up_s, tm=512)

        for name, val in (("kp", k_p.reshape(bp, seq, N_HEADS, 2, HEAD_DIM)),
                          ("vp", v_p.reshape(bp, seq, N_HEADS, V_DIM)),
                          ("rep", re_p.reshape(bp, n_groups, STATE_DIM)),
                          ("imp", im_p.reshape(bp, n_groups, STATE_DIM)),
                          ("cp", c_p),
                          ("ks", k_s.reshape(bs, 1, N_HEADS, 2, HEAD_DIM)),
                          ("vs", v_s.reshape(bs, 1, N_HEADS, V_DIM)),
                          ("res", re_s.reshape(bs, n_groups, STATE_DIM)),
                          ("ims", im_s.reshape(bs, n_groups, STATE_DIM)),
                          ("cs", c_s)):
            outs.setdefault(name, []).append(val)

    st = {k: jnp.stack(v, axis=0) for k, v in outs.items()}
    return (hp.reshape(bp, seq, d_model), hs.reshape(bs, 1, d_model), st["kp"], st["vp"], st["rep"],
            st["imp"], st["cp"], st["ks"], st["vs"], st["res"], st["ims"], st["cs"])
```

```python
import functools
import math

import jax
import jax.numpy as jnp
from jax import lax
from jax.experimental import pallas as pl
from jax.experimental.pallas import tpu as pltpu

F32 = jnp.float32
BF16 = jnp.bfloat16

N_HEADS = 8
HEAD_DIM = 64
V_DIM = 2 * HEAD_DIM
SSM_GROUP = 16
STATE_DIM = 64
CONV_W = 3
NUM_BUCKETS = 32
MAX_EXACT = NUM_BUCKETS // 2
MAX_DISTANCE = 128
LN_EPS = 1e-5
NEG_INF = -1e30

VMEM_LIMIT_BYTES = 56 * 1024 * 1024
LANES = 128
SLAB_GROUPS = 8
N_SLABS = 8
SLAB_IN = SLAB_GROUPS * SSM_GROUP
SLAB_STATE = SLAB_GROUPS * STATE_DIM
SUBLANES = 8
UP_ROW_CHUNK = 512
SCAN_LEVELS = 3
SCAN_UNROLL = 2


def _params(sem):
    return pltpu.CompilerParams(dimension_semantics=sem, vmem_limit_bytes=VMEM_LIMIT_BYTES)


def _mm_body(*refs, n_pairs, n_extra, n_out, epilogue):
    x_refs = refs[0:2 * n_pairs:2]
    w_refs = refs[1:2 * n_pairs:2]
    extra = refs[2 * n_pairs:2 * n_pairs + n_extra]
    outs = refs[2 * n_pairs + n_extra:2 * n_pairs + n_extra + n_out]
    wbf = refs[2 * n_pairs + n_extra + n_out:]

    @pl.when(pl.program_id(1) == 0)
    def _():
        for w, s in zip(w_refs, wbf):
            s[...] = w[...].astype(BF16)

    accs = [jnp.dot(x[...].astype(BF16), s[...], preferred_element_type=F32)
            for x, s in zip(x_refs, wbf)]
    res = epilogue(*accs, *[e[...] for e in extra])
    for o, r in zip(outs, res):
        o[...] = r.astype(o.dtype)


def _matmul(pairs, extras, out_dtypes, epilogue, *, n_cols, tm, tn, name):
    m = pairs[0][0].shape[0]
    tm = min(tm, m)
    grid = (n_cols // tn, m // tm)
    in_specs, args, scratch = [], [], []
    for x, w, c0 in pairs:
        k = x.shape[1]
        off = c0 // tn
        in_specs.append(pl.BlockSpec((tm, k), lambda j, i: (i, 0)))
        in_specs.append(pl.BlockSpec((k, tn), lambda j, i, off=off: (0, j + off)))
        args += [x, w]
        scratch.append(pltpu.VMEM((k, tn), BF16))
    for e in extras:
        if e.shape[0] == 1:
            in_specs.append(pl.BlockSpec((1, tn), lambda j, i: (0, j)))
        else:
            in_specs.append(pl.BlockSpec((tm, tn), lambda j, i: (i, j)))
        args.append(e)
    out_shape = [jax.ShapeDtypeStruct((m, n_cols), d) for d in out_dtypes]
    out_specs = [pl.BlockSpec((tm, tn), lambda j, i: (i, j)) for _ in out_dtypes]
    body = functools.partial(_mm_body, n_pairs=len(pairs), n_extra=len(extras),
                             n_out=len(out_dtypes), epilogue=epilogue)
    return pl.pallas_call(
        body, grid=grid, in_specs=in_specs, out_specs=out_specs, out_shape=out_shape,
        scratch_shapes=scratch, compiler_params=_params(("arbitrary", "arbitrary")),
        name=name)(*args)


def _proj_t_body(x_ref, w_ref, o_ref, ob_ref, wt_sc):
    @pl.when(pl.program_id(0) == 0)
    def _():
        for c in range(w_ref.shape[1] // LANES):
            cols = slice(c * LANES, (c + 1) * LANES)
            wt_sc[cols, :] = w_ref[:, cols].T.astype(BF16)

    kt = lax.dot_general(wt_sc[...], x_ref[...], (((1,), (1,)), ((), ())),
                         preferred_element_type=F32)
    o_ref[...] = kt
    ob_ref[...] = kt.astype(BF16)


def _proj_transposed(x, w, c0, *, n_cols, batch, seq, tm, name):
    k = x.shape[1]
    per_b = seq // tm
    out_spec = pl.BlockSpec((None, n_cols, tm), lambda i: (i // per_b, 0, i % per_b))
    return pl.pallas_call(
        _proj_t_body, grid=(batch * per_b,),
        in_specs=[pl.BlockSpec((tm, k), lambda i: (i, 0)),
                  pl.BlockSpec((k, n_cols), lambda i, off=c0 // n_cols: (0, off))],
        out_specs=[out_spec, out_spec],
        out_shape=[jax.ShapeDtypeStruct((batch, n_cols, seq), F32),
                   jax.ShapeDtypeStruct((batch, n_cols, seq), BF16)],
        scratch_shapes=[pltpu.VMEM((n_cols, k), BF16)],
        compiler_params=_params(("arbitrary",)), name=name)(x, w)


def _ln_body(x_ref, g_ref, b_ref, o_ref, ob_ref):
    x = x_ref[...]
    mu = jnp.mean(x, axis=-1, keepdims=True)
    xc = x - mu
    var = jnp.mean(xc * xc, axis=-1, keepdims=True)
    y = xc * lax.rsqrt(var + LN_EPS) * g_ref[...] + b_ref[...]
    o_ref[...] = y
    ob_ref[...] = y.astype(BF16)


def _layer_norm(x, g, b, *, tm, name):
    m, d = x.shape
    tm = min(tm, m)
    return pl.pallas_call(
        _ln_body, grid=(m // tm,),
        in_specs=[pl.BlockSpec((tm, d), lambda i: (i, 0)),
                  pl.BlockSpec((1, d), lambda i: (0, 0)),
                  pl.BlockSpec((1, d), lambda i: (0, 0))],
        out_specs=[pl.BlockSpec((tm, d), lambda i: (i, 0)),
                   pl.BlockSpec((tm, d), lambda i: (i, 0))],
        out_shape=[jax.ShapeDtypeStruct((m, d), F32), jax.ShapeDtypeStruct((m, d), BF16)],
        compiler_params=_params(("arbitrary",)), name=name)(x, g.reshape(1, d), b.reshape(1, d))


def _rel_bucket(n):
    n = jnp.maximum(n, 0)
    nf = jnp.maximum(n, 1).astype(F32)
    large = MAX_EXACT + (jnp.log(nf / MAX_EXACT) / math.log(MAX_DISTANCE / MAX_EXACT)
                         * (NUM_BUCKETS - MAX_EXACT)).astype(jnp.int32)
    large = jnp.minimum(large, NUM_BUCKETS - 1)
    return jnp.where(n < MAX_EXACT, n, large)


def _bucket_lookup(bucket, table_fn):
    out = jnp.zeros(jnp.broadcast_shapes(bucket.shape, table_fn(0).shape), F32)
    for b in range(NUM_BUCKETS):
        out = out + jnp.where(bucket == b, table_fn(b), 0.0)
    return out


def _bias_tile_body(rb_ref, bucket_ref, o_ref):
    h = pl.program_id(0)
    o_ref[...] = _bucket_lookup(bucket_ref[...], lambda bk: rb_ref[bk, h])


def _bias_tiles(rel_bias, t):
    r = jnp.arange(t, dtype=jnp.int32)
    c = jnp.arange(2 * t, dtype=jnp.int32)
    buckets = _rel_bucket(r[:, None] + t - c[None, :])
    return pl.pallas_call(
        _bias_tile_body, grid=(N_HEADS,),
        in_specs=[pl.BlockSpec(memory_space=pltpu.SMEM), pl.BlockSpec((t, 2 * t), lambda h: (0, 0))],
        out_specs=pl.BlockSpec((None, t, 2 * t), lambda h: (h, 0, 0)),
        out_shape=jax.ShapeDtypeStruct((N_HEADS, t, 2 * t), F32),
        compiler_params=_params(("arbitrary",)), name="bias_tiles")(rel_bias, buckets)


def _attn_body(lam_ref, rb_ref, q_ref, k_ref, v_ref, bias_ref, g_ref, o_ref, *, t, n_far, near, out_scale):
    h = pl.program_id(0)
    far = n_far * t
    q = q_ref[...] * (HEAD_DIM ** -0.5)
    lane = lax.broadcasted_iota(jnp.int32, q.shape, 1)
    zero = jnp.zeros_like(q)
    q2 = jnp.concatenate([jnp.where(lane < HEAD_DIM, q, zero), jnp.where(lane >= HEAD_DIM, q, zero)],
                         axis=0)

    bias = bias_ref[:, 2 * t - near:]
    row = lax.broadcasted_iota(jnp.int32, (t, near), 0)
    col = lax.broadcasted_iota(jnp.int32, (t, near), 1)
    keep = col - (near - t) <= row
    s_near = jnp.dot(q2, k_ref[:, far:far + near], preferred_element_type=F32)
    s_near = jnp.where(jnp.concatenate([keep, keep], axis=0),
                       s_near + jnp.concatenate([bias, bias], axis=0), NEG_INF)
    m = jnp.max(s_near, axis=1, keepdims=True)
    if n_far:
        s_far = jnp.dot(q2, k_ref[:, 0:far], preferred_element_type=F32) + rb_ref[NUM_BUCKETS - 1, h]
        m = jnp.maximum(m, jnp.max(s_far, axis=1, keepdims=True))
    p = jnp.exp(s_near - m)
    l = jnp.sum(p, axis=1, keepdims=True)
    acc = jnp.dot(p.astype(BF16), v_ref[far:far + near, :], preferred_element_type=F32)
    if n_far:
        p = jnp.exp(s_far - m)
        l = l + jnp.sum(p, axis=1, keepdims=True)
        acc = acc + jnp.dot(p.astype(BF16), v_ref[0:far, :], preferred_element_type=F32)

    nrm = acc / l
    o = nrm[0:t] - lam_ref[0] * nrm[t:2 * t]
    ms = jnp.mean(o * o, axis=-1, keepdims=True)
    o_ref[...] = (o * lax.rsqrt(ms + LN_EPS) * g_ref[...] * out_scale).astype(o_ref.dtype)


def _prompt_attention(q, k, v, rel_bias, lam, subln_g, *, batch, seq, t, out_scale):
    assert t >= MAX_DISTANCE
    nq = seq // t
    width = N_HEADS * V_DIM
    q3 = q.reshape(batch, seq, width)
    v3 = v.reshape(batch, seq, width)
    bias = _bias_tiles(rel_bias, t)
    smem = pl.BlockSpec(memory_space=pltpu.SMEM)
    pieces = []
    for i in range(nq):
        n_far = max(i - 1, 0)
        near = min(i + 1, 2) * t
        keys = n_far * t + near
        body = functools.partial(_attn_body, t=t, n_far=n_far, near=near, out_scale=out_scale)
        pieces.append(pl.pallas_call(
            body, grid=(N_HEADS, batch),
            in_specs=[smem, smem,
                      pl.BlockSpec((None, t, V_DIM), lambda h, b, i=i: (b, i, h)),
                      pl.BlockSpec((None, V_DIM, keys), lambda h, b: (b, h, 0)),
                      pl.BlockSpec((None, keys, V_DIM), lambda h, b: (b, 0, h)),
                      pl.BlockSpec((None, t, 2 * t), lambda h, b: (h, 0, 0)),
                      pl.BlockSpec((1, V_DIM), lambda h, b: (0, 0))],
            out_specs=pl.BlockSpec((None, t, V_DIM), lambda h, b: (b, 0, h)),
            out_shape=jax.ShapeDtypeStruct((batch, t, width), BF16),
            compiler_params=_params(("arbitrary", "arbitrary")),
            name=f"prompt_attention_q{i}")(lam, rel_bias, q3, k, v3, bias, subln_g.reshape(1, V_DIM)))
    return jnp.stack(pieces, axis=1).reshape(batch * seq, width)


def _decode_body(pt_ref, lam_ref, q_ref, kn_ref, vn_ref, rbt_ref, bucket_ref, g_ref, *rest,
                 pages, page, n_steps, out_scale):
    k_refs = rest[:pages]
    v_refs = rest[pages:2 * pages]
    o_ref = rest[2 * pages]
    qexp_sc, bias_sc, m_sc, l_sc, acc_sc = rest[2 * pages + 1:]
    step = pl.program_id(1)
    rows = 2 * N_HEADS
    width = N_HEADS * V_DIM

    @pl.when(step == 0)
    def _():
        row = lax.broadcasted_iota(jnp.int32, (rows, width), 0)
        col = lax.broadcasted_iota(jnp.int32, (rows, width), 1)
        own_qk = (col // HEAD_DIM) == (row % N_HEADS) * 2 + row // N_HEADS
        q = (q_ref[...] * (HEAD_DIM ** -0.5)).astype(BF16).astype(F32)
        qexp = jnp.where(own_qk, jnp.broadcast_to(q, (rows, width)), 0.0)
        qexp_sc[...] = qexp.astype(BF16)
        bias_sc[...] = _bucket_lookup(bucket_ref[...], lambda bk: rbt_ref[:, bk:bk + 1])
        kn = kn_ref[...].astype(BF16).astype(F32)
        s_self = jnp.sum(qexp * kn, axis=1, keepdims=True) + rbt_ref[:, 0:1]
        m_sc[...] = s_self
        l_sc[...] = jnp.ones(l_sc.shape, F32)
        vn = vn_ref[...].astype(BF16).astype(F32)
        acc_sc[...] = jnp.broadcast_to(vn[:, None, :], acc_sc.shape)

    qexp = qexp_sc[...]
    bias_far = rbt_ref[:, NUM_BUCKETS - 1:NUM_BUCKETS]
    is_last = step == n_steps - 1
    s_parts = []
    for p in range(pages):
        s = jnp.dot(qexp, k_refs[p][...].astype(BF16), preferred_element_type=F32)
        if p == pages - 1:
            s = s + jnp.where(is_last, bias_sc[...], bias_far)
        else:
            s = s + bias_far
        s_parts.append(s)
    m_old = m_sc[...]
    m_new = m_old
    for s in s_parts:
        m_new = jnp.maximum(m_new, jnp.max(s, axis=1, keepdims=True))
    a = jnp.exp(m_old - m_new)
    l_new = a * l_sc[...]
    probs = []
    for p in range(pages):
        pr = jnp.exp(s_parts[p] - m_new)
        l_new = l_new + jnp.sum(pr, axis=1, keepdims=True)
        probs.append(pr.astype(BF16))
    probs = jnp.concatenate(probs, axis=1)
    m_sc[...] = m_new
    l_sc[...] = l_new
    for h in range(N_HEADS):
        vh = jnp.concatenate([v_refs[p][pl.ds(h, page, stride=N_HEADS), :].astype(BF16)
                              for p in range(pages)], axis=0)
        acc_sc[h] = a * acc_sc[h] + jnp.dot(probs, vh, preferred_element_type=F32)

    @pl.when(is_last)
    def _():
        nrm = acc_sc[...] / l_new[None]
        r = lax.broadcasted_iota(jnp.int32, nrm.shape, 1)
        hh = lax.broadcasted_iota(jnp.int32, nrm.shape, 0)
        coef = jnp.where(r == hh, 1.0, jnp.where(r == hh + N_HEADS, -lam_ref[0], 0.0))
        d = jnp.sum(coef * nrm, axis=1)
        ms = jnp.mean(d * d, axis=-1, keepdims=True)
        o_ref[...] = (d * lax.rsqrt(ms + LN_EPS) * g_ref[...] * out_scale).astype(o_ref.dtype)


def _sample_attention(q, k_new, v_new, cache_k, cache_v, page_table, rel_bias, lam, subln_g,
                      *, pages, out_scale):
    bs, width = q.shape
    page = cache_k.shape[2]
    n_pages = page_table.shape[1]
    n_steps = n_pages // pages
    past = n_pages * page
    kpos = past - page + jnp.arange(page, dtype=jnp.int32)
    bucket_last = _rel_bucket(past - kpos).reshape(1, page)
    rbt = jnp.tile(rel_bias.T, (2, 1))

    def tok_spec():
        return pl.BlockSpec((None, 1, width), lambda b, s, pt: (b, 0, 0))

    def page_spec(shape, p):
        return pl.BlockSpec((None,) + shape, lambda b, s, pt, p=p: (pt[b, s * pages + p], 0, 0))

    full = lambda shape: pl.BlockSpec(shape, lambda b, s, pt: tuple(0 for _ in shape))
    head_spec = pl.BlockSpec((None, N_HEADS, V_DIM), lambda b, s, pt: (b, 0, 0))
    in_specs = ([pl.BlockSpec(memory_space=pltpu.SMEM), tok_spec(), tok_spec(), head_spec,
                 full((2 * N_HEADS, NUM_BUCKETS)), full((1, page)), full((1, V_DIM))]
                + [page_spec((width, page), p) for p in range(pages)]
                + [page_spec((page * N_HEADS, V_DIM), p) for p in range(pages)])
    rows = 2 * N_HEADS
    body = functools.partial(_decode_body, pages=pages, page=page, n_steps=n_steps,
                             out_scale=out_scale)
    out = pl.pallas_call(
        body,
        grid_spec=pltpu.PrefetchScalarGridSpec(
            num_scalar_prefetch=1, grid=(bs, n_steps), in_specs=in_specs,
            out_specs=head_spec,
            scratch_shapes=[pltpu.VMEM((rows, width), BF16), pltpu.VMEM((rows, page), F32),
                            pltpu.VMEM((rows, 1), F32), pltpu.VMEM((rows, 1), F32),
                            pltpu.VMEM((N_HEADS, rows, V_DIM), F32)]),
        out_shape=jax.ShapeDtypeStruct((bs, N_HEADS, V_DIM), BF16),
        compiler_params=_params(("arbitrary", "arbitrary")),
        name="sample_attention")(
            page_table, lam, q.reshape(bs, 1, width), k_new.reshape(bs, 1, width),
            v_new.reshape(bs, N_HEADS, V_DIM), rbt, bucket_last, subln_g.reshape(1, V_DIM),
            *([cache_k] * pages), *([cache_v] * pages))
    return out.reshape(bs, width)


def _s5_params(a_re, a_im, log_dt, b_re, b_im, c_re, c_im):
    dt = jnp.exp(log_dt)[:, None]
    mag = jnp.exp(a_re * dt)
    ang = a_im * dt
    abar_re = mag * jnp.cos(ang)
    abar_im = mag * jnp.sin(ang)
    den = a_re * a_re + a_im * a_im
    f_re = ((abar_re - 1.0) * a_re + abar_im * a_im) / den
    f_im = (abar_im * a_re - (abar_re - 1.0) * a_im) / den
    bb_re = f_re[..., None] * b_re - f_im[..., None] * b_im
    bb_im = f_re[..., None] * b_im + f_im[..., None] * b_re
    eye = jnp.eye(SLAB_GROUPS, dtype=F32)

    def in_slabs(bb):
        tt = bb.reshape(N_SLABS, SLAB_GROUPS, STATE_DIM, SSM_GROUP).transpose(0, 1, 3, 2)
        full = tt[:, :, :, None, :] * eye[None, :, None, :, None]
        return full.reshape(N_SLABS, SLAB_IN, SLAB_STATE).astype(BF16)

    def out_slabs(cc):
        tt = cc.reshape(N_SLABS, SLAB_GROUPS, SSM_GROUP, STATE_DIM).transpose(0, 1, 3, 2)
        full = tt[:, :, :, None, :] * eye[None, :, None, :, None]
        return full.reshape(N_SLABS, SLAB_STATE, SLAB_IN).astype(BF16)

    return (abar_re.reshape(N_SLABS, SLAB_STATE), abar_im.reshape(N_SLABS, SLAB_STATE),
            in_slabs(bb_re), in_slabs(bb_im), out_slabs(c_re), out_slabs(c_im))


def _glu_out(y, u, d_ref, wglu_ref, o_ref):
    g = jax.nn.gelu(y + d_ref[...] * u)
    gate = jnp.dot(g.astype(BF16), wglu_ref[...].astype(BF16), preferred_element_type=F32)
    o_ref[...] = (g * jax.nn.sigmoid(gate)).astype(o_ref.dtype)


def _s5_prompt_body(u_ref, ar_ref, ai_ref, bbr_ref, bbi_ref, cr_ref, ci_ref, d_ref, wglu_ref,
                    o_ref, sre_ref, sim_ref, xr_sc, xi_sc, pw_sc, cr_sc, ci_sc, *, tc):
    b = pl.program_id(0)
    c = pl.program_id(1)
    sub = SUBLANES

    def cmul(pr, pi, qr, qi):
        return pr * qr - pi * qi, pr * qi + pi * qr

    @pl.when(jnp.logical_and(b == 0, c == 0))
    def _():
        t = lax.broadcasted_iota(jnp.int32, (sub, SLAB_STATE), 0)
        for j in range(N_SLABS):
            a = (jnp.broadcast_to(ar_ref[j:j + 1, :], t.shape), jnp.broadcast_to(ai_ref[j:j + 1, :], t.shape))
            powers = [a]
            for _ in range(sub - 1):
                powers.append(cmul(*powers[-1], *a))
            for lvl in range(SCAN_LEVELS):
                k = 1 << lvl
                for part in range(2):
                    pw_sc[j, 2 * lvl + part] = jnp.where(t >= k, powers[k - 1][part], 0.0)
            for part in range(2):
                acc = powers[sub - 1][part]
                for row in range(sub - 1):
                    acc = jnp.where(t == row, powers[row][part], acc)
                pw_sc[j, 2 * SCAN_LEVELS + part] = acc

    @pl.when(c == 0)
    def _():
        cr_sc[...] = jnp.zeros(cr_sc.shape, F32)
        ci_sc[...] = jnp.zeros(ci_sc.shape, F32)

    u = u_ref[...]
    ub = u.astype(BF16)
    for j in range(N_SLABS):
        uj = ub[:, j * SLAB_IN:(j + 1) * SLAB_IN]
        xr_sc[j] = jnp.dot(uj, bbr_ref[j], preferred_element_type=F32)
        xi_sc[j] = jnp.dot(uj, bbi_ref[j], preferred_element_type=F32)

    def slab(j, _):
        def tile(v, carry):
            cr, ci = carry
            rows = pl.ds(pl.multiple_of(v * sub, sub), sub)
            xr = xr_sc[j, rows, :]
            xi = xi_sc[j, rows, :]
            for lvl in range(SCAN_LEVELS):
                k = 1 << lvl
                dr, di = cmul(pw_sc[j, 2 * lvl], pw_sc[j, 2 * lvl + 1],
                              pltpu.roll(xr, k, 0), pltpu.roll(xi, k, 0))
                xr, xi = xr + dr, xi + di
            dr, di = cmul(pw_sc[j, 2 * SCAN_LEVELS], pw_sc[j, 2 * SCAN_LEVELS + 1], cr, ci)
            xr, xi = xr + dr, xi + di
            xr_sc[j, rows, :] = xr
            xi_sc[j, rows, :] = xi
            return (jnp.broadcast_to(xr[sub - 1:sub, :], xr.shape),
                    jnp.broadcast_to(xi[sub - 1:sub, :], xi.shape))

        row = pl.ds(j, 1)
        init = (jnp.broadcast_to(cr_sc[row, :], (sub, SLAB_STATE)),
                jnp.broadcast_to(ci_sc[row, :], (sub, SLAB_STATE)))
        cr, ci = lax.fori_loop(0, tc // sub, tile, init, unroll=SCAN_UNROLL)
        cr_sc[row, :] = cr[0:1, :]
        ci_sc[row, :] = ci[0:1, :]
        return 0

    lax.fori_loop(0, N_SLABS, slab, 0)
    sre_ref[...] = cr_sc[...]
    sim_ref[...] = ci_sc[...]

    ys = []
    for j in range(N_SLABS):
        ys.append(jnp.dot(xr_sc[j].astype(BF16), cr_ref[j], preferred_element_type=F32)
                  - jnp.dot(xi_sc[j].astype(BF16), ci_ref[j], preferred_element_type=F32))
    _glu_out(jnp.concatenate(ys, axis=1), u, d_ref, wglu_ref, o_ref)


def _s5_prompt(u, sp, d, w_glu, *, batch, seq, tc):
    ar, ai, bbr, bbi, cr, ci = sp
    width = u.shape[1]
    nc = seq // tc
    const = lambda shape: pl.BlockSpec(shape, lambda b, c: tuple(0 for _ in shape))
    state_spec = pl.BlockSpec((None, N_SLABS, SLAB_STATE), lambda b, c: (b, 0, 0))
    state_shape = jax.ShapeDtypeStruct((batch, N_SLABS, SLAB_STATE), F32)
    return pl.pallas_call(
        functools.partial(_s5_prompt_body, tc=tc), grid=(batch, nc),
        in_specs=[pl.BlockSpec((tc, width), lambda b, c: (b * nc + c, 0)),
                  const(ar.shape), const(ai.shape), const(bbr.shape), const(bbi.shape),
                  const(cr.shape), const(ci.shape), const((1, width)), const(w_glu.shape)],
        out_specs=[pl.BlockSpec((tc, width), lambda b, c: (b * nc + c, 0)), state_spec, state_spec],
        out_shape=[jax.ShapeDtypeStruct((batch * seq, width), BF16), state_shape, state_shape],
        scratch_shapes=[pltpu.VMEM((N_SLABS, tc, SLAB_STATE), F32),
                        pltpu.VMEM((N_SLABS, tc, SLAB_STATE), F32),
                        pltpu.VMEM((N_SLABS, 2 * SCAN_LEVELS + 2, SUBLANES, SLAB_STATE), F32),
                        pltpu.VMEM((N_SLABS, SLAB_STATE), F32),
                        pltpu.VMEM((N_SLABS, SLAB_STATE), F32)],
        compiler_params=_params(("arbitrary", "arbitrary")),
        name="s5_prompt")(u, ar, ai, bbr, bbi, cr, ci, d.reshape(1, width), w_glu)


def _s5_sample_body(u_ref, x0r_ref, x0i_ref, ar_ref, ai_ref, bbr_ref, bbi_ref, cr_ref, ci_ref,
                    d_ref, wglu_ref, o_ref, sre_ref, sim_ref):
    u = u_ref[...]
    ub = u.astype(BF16)
    ys = []
    for j in range(N_SLABS):
        uj = ub[:, j * SLAB_IN:(j + 1) * SLAB_IN]
        ar = ar_ref[j:j + 1, :]
        ai = ai_ref[j:j + 1, :]
        x0r = x0r_ref[j]
        x0i = x0i_ref[j]
        xr = ar * x0r - ai * x0i + jnp.dot(uj, bbr_ref[j], preferred_element_type=F32)
        xi = ar * x0i + ai * x0r + jnp.dot(uj, bbi_ref[j], preferred_element_type=F32)
        sre_ref[j] = xr
        sim_ref[j] = xi
        ys.append(jnp.dot(xr.astype(BF16), cr_ref[j], preferred_element_type=F32)
                  - jnp.dot(xi.astype(BF16), ci_ref[j], preferred_element_type=F32))
    _glu_out(jnp.concatenate(ys, axis=1), u, d_ref, wglu_ref, o_ref)


def _s5_sample(u, x0_re, x0_im, sp, d, w_glu):
    ar, ai, bbr, bbi, cr, ci = sp
    bs, width = u.shape
    state_shape = jax.ShapeDtypeStruct((N_SLABS, bs, SLAB_STATE), F32)
    return pl.pallas_call(
        _s5_sample_body,
        out_shape=[jax.ShapeDtypeStruct((bs, width), BF16), state_shape, state_shape],
        compiler_params=pltpu.CompilerParams(vmem_limit_bytes=VMEM_LIMIT_BYTES),
        name="s5_sample")(u, x0_re, x0_im, ar, ai, bbr, bbi, cr, ci, d.reshape(1, width), w_glu)


def _gated(c):
    gate = c[:, :LANES]
    return gate * jax.nn.sigmoid(gate) * c[:, LANES:]


def _up_prompt_body(h_ref, wg_ref, wv_ref, cwg_ref, cwv_ref, cbg_ref, cbv_ref,
                    act_ref, cg_ref, cv_ref, w_sc, *, rows):
    w_sc[:, 0:LANES] = wg_ref[...].astype(BF16)
    w_sc[:, LANES:2 * LANES] = wv_ref[...].astype(BF16)
    cw = jnp.concatenate([cwg_ref[...], cwv_ref[...]], axis=1)
    cb = jnp.concatenate([cbg_ref[...], cbv_ref[...]], axis=1)
    seq = h_ref.shape[0]
    sub = SUBLANES
    row8 = lax.broadcasted_iota(jnp.int32, (sub, 2 * LANES), 0)
    prev = jnp.zeros((sub, 2 * LANES), F32)
    for c in range(seq // rows):
        up = jnp.dot(h_ref[c * rows:(c + 1) * rows, :], w_sc[...], preferred_element_type=F32)
        cur = cb + cw[CONV_W - 1:CONV_W] * up
        for back in range(1, CONV_W):
            rolled = pltpu.roll(up, back, 0)
            head = jnp.where(row8 < back, pltpu.roll(prev, back, 0), rolled[0:sub])
            shifted = jnp.concatenate([head, rolled[sub:]], axis=0)
            cur = cur + cw[CONV_W - 1 - back:CONV_W - back] * shifted
        act_ref[c * rows:(c + 1) * rows, :] = _gated(cur).astype(act_ref.dtype)
        prev = up[rows - sub:, :]
    tail = prev[sub - (CONV_W - 1):, :]
    cg_ref[...] = tail[:, :LANES]
    cv_ref[...] = tail[:, LANES:]


def _up_prompt(hb, w_up, conv_w, conv_b, *, batch, seq):
    d_model = hb.shape[1]
    d_ff = w_up.shape[1] // 2
    nb = d_ff // LANES
    cb = conv_b.reshape(1, 2 * d_ff)
    col = lambda off: (lambda b, j: (0, j + off))
    tail_spec = pl.BlockSpec((None, CONV_W - 1, LANES), lambda b, j: (b, 0, j))
    tail_shape = jax.ShapeDtypeStruct((batch, CONV_W - 1, d_ff), F32)
    return pl.pallas_call(
        functools.partial(_up_prompt_body, rows=UP_ROW_CHUNK), grid=(batch, nb),
        scratch_shapes=[pltpu.VMEM((d_model, 2 * LANES), BF16)],
        in_specs=[pl.BlockSpec((seq, d_model), lambda b, j: (b, 0)),
                  pl.BlockSpec((d_model, LANES), col(0)), pl.BlockSpec((d_model, LANES), col(nb)),
                  pl.BlockSpec((CONV_W, LANES), col(0)), pl.BlockSpec((CONV_W, LANES), col(nb)),
                  pl.BlockSpec((1, LANES), col(0)), pl.BlockSpec((1, LANES), col(nb))],
        out_specs=[pl.BlockSpec((seq, LANES), lambda b, j: (b, j)), tail_spec, tail_spec],
        out_shape=[jax.ShapeDtypeStruct((batch * seq, d_ff), BF16), tail_shape, tail_shape],
        compiler_params=_params(("arbitrary", "arbitrary")),
        name="up_prompt")(hb, w_up, w_up, conv_w, conv_w, cb, cb)


def _up_sample_body(h_ref, wg_ref, wv_ref, cwg_ref, cwv_ref, cbg_ref, cbv_ref, sg_ref, sv_ref,
                    act_ref, cg_ref, cv_ref):
    w = jnp.concatenate([wg_ref[...].astype(BF16), wv_ref[...].astype(BF16)], axis=1)
    up = jnp.dot(h_ref[...], w, preferred_element_type=F32)
    cw = jnp.concatenate([cwg_ref[...], cwv_ref[...]], axis=1)
    cb = jnp.concatenate([cbg_ref[...], cbv_ref[...]], axis=1)
    c = cb + cw[CONV_W - 1:CONV_W] * up
    for tap in range(CONV_W - 1):
        st = jnp.concatenate([sg_ref[tap], sv_ref[tap]], axis=1)
        c = c + cw[tap:tap + 1] * st
    act_ref[...] = _gated(c).astype(act_ref.dtype)
    for tap in range(1, CONV_W - 1):
        cg_ref[tap - 1] = sg_ref[tap]
        cv_ref[tap - 1] = sv_ref[tap]
    cg_ref[CONV_W - 2] = up[:, :LANES]
    cv_ref[CONV_W - 2] = up[:, LANES:]


def _up_sample(hb, w_up, conv_w, conv_b, state):
    bs, d_model = hb.shape
    d_ff = w_up.shape[1] // 2
    nb = d_ff // LANES
    cb = conv_b.reshape(1, 2 * d_ff)
    col = lambda off: (lambda j: (0, j + off))
    st = lambda off: pl.BlockSpec((CONV_W - 1, bs, LANES), lambda j, off=off: (0, 0, j + off))
    tail_spec = pl.BlockSpec((CONV_W - 1, bs, LANES), lambda j: (0, 0, j))
    tail_shape = jax.ShapeDtypeStruct((CONV_W - 1, bs, d_ff), F32)
    return pl.pallas_call(
        _up_sample_body, grid=(nb,),
        in_specs=[pl.BlockSpec((bs, d_model), lambda j: (0, 0)),
                  pl.BlockSpec((d_model, LANES), col(0)), pl.BlockSpec((d_model, LANES), col(nb)),
                  pl.BlockSpec((CONV_W, LANES), col(0)), pl.BlockSpec((CONV_W, LANES), col(nb)),
                  pl.BlockSpec((1, LANES), col(0)), pl.BlockSpec((1, LANES), col(nb)),
                  st(0), st(nb)],
        out_specs=[pl.BlockSpec((bs, LANES), lambda j: (0, j)), tail_spec, tail_spec],
        out_shape=[jax.ShapeDtypeStruct((bs, d_ff), BF16), tail_shape, tail_shape],
        compiler_params=_params(("arbitrary",)),
        name="up_sample")(hb, w_up, w_up, conv_w, conv_w, cb, cb, state, state)


def _one(x):
    return (x,)


def _sigmoid_out(acc):
    return (jax.nn.sigmoid(acc),)


def _both(acc):
    return (acc, acc)


def _merge(pa, ps, ga, gs):
    return (ga * pa + gs * ps,)


def _layer(x, w, *, alpha, attend, ssm, up, tm, k_transposed=None):
    d_model = x.shape[1]
    qk_w = N_HEADS * 2 * HEAD_DIM
    v_w = N_HEADS * V_DIM
    ssm_w = d_model // 2
    xb = x.astype(BF16)
    w_in = w["w_in"]
    c = 0
    (q,) = _matmul([(xb, w_in, c)], [], [BF16], _one, n_cols=qk_w, tm=tm, tn=1024, name="proj_q")
    c += qk_w
    if k_transposed is None:
        k, kb = _matmul([(xb, w_in, c)], [], [F32, BF16], _both, n_cols=qk_w, tm=tm, tn=1024,
                        name="proj_k")
    else:
        k, kb = _proj_transposed(xb, w_in, c, n_cols=qk_w, batch=k_transposed[0],
                                 seq=k_transposed[1], tm=tm, name="proj_kt")
    c += qk_w
    v, vb = _matmul([(xb, w_in, c)], [], [F32, BF16], _both, n_cols=v_w, tm=tm, tn=1024, name="proj_v")
    c += v_w
    (u,) = _matmul([(xb, w_in, c)], [], [F32], _one, n_cols=ssm_w, tm=tm, tn=1024, name="proj_u")
    c += ssm_w
    (ga,) = _matmul([(xb, w_in, c)], [], [F32], _sigmoid_out, n_cols=d_model, tm=tm, tn=1024, name="gate_a")
    c += d_model
    (gs,) = _matmul([(xb, w_in, c)], [], [F32], _sigmoid_out, n_cols=d_model, tm=tm, tn=1024, name="gate_s")

    attn = attend(q, k, v, kb, vb)
    ssm_out, s_re, s_im = ssm(u)

    (merged,) = _matmul([(attn, w["w_proj_attn"], 0), (ssm_out, w["w_proj_ssm"], 0)], [ga, gs], [BF16],
                        _merge, n_cols=d_model, tm=tm, tn=1024, name="merge")
    (r1,) = _matmul([(merged, w["w_out"], 0)], [x], [F32], lambda acc, res: (alpha * res + acc,),
                    n_cols=d_model, tm=tm, tn=1024, name="out_proj")
    h, hb = _layer_norm(r1, w["ln1_g"], w["ln1_b"], tm=tm, name="ln1")
    act, conv_new = up(hb)
    (r2,) = _matmul([(act, w["w_down"], 0)], [h], [F32], lambda acc, res: (alpha * res + acc,),
                    n_cols=d_model, tm=tm, tn=512, name="down_proj")
    y, _ = _layer_norm(r2, w["ln2_g"], w["ln2_b"], tm=tm, name="ln2")
    return y, k, v, s_re, s_im, conv_new


def kernel(x_prompt, x_sample, cache_k, cache_v, state_ssm_re, state_ssm_im, state_conv, page_table, rel_bias, w_in, lambda_q1, lambda_k1, lambda_q2, lambda_k2, subln_g, ssm_a_re, ssm_a_im, ssm_log_dt, ssm_b_re, ssm_b_im, ssm_c_re, ssm_c_im, ssm_d, w_glu, w_proj_attn, w_proj_ssm, w_out, ln1_g, ln1_b, w_up, conv_w, conv_b, w_down, ln2_g, ln2_b):
    depth = w_in.shape[0]
    assert depth == 1, "single-layer trunk"
    bp, seq, d_model = x_prompt.shape
    bs, dec_seq, _ = x_sample.shape
    assert dec_seq == 1
    n_pool, page = cache_k.shape[1], cache_k.shape[2]
    d_ff = w_down.shape[1]
    n_groups = ssm_a_re.shape[1]
    assert n_groups == N_SLABS * SLAB_GROUPS and d_ff % LANES == 0
    alpha = (2.0 * depth) ** 0.25
    width = N_HEADS * V_DIM

    hp = x_prompt.reshape(bp * seq, d_model)
    hs = x_sample.reshape(bs, d_model)
    outs = {}
    for l in range(depth):
        lam_init = 0.8 - 0.6 * math.exp(-0.3 * l)
        out_scale = 1.0 - lam_init
        lam = (jnp.exp(jnp.sum(lambda_q1[l] * lambda_k1[l]))
               - jnp.exp(jnp.sum(lambda_q2[l] * lambda_k2[l])) + lam_init).reshape(1)
        w = dict(w_in=w_in[l], w_proj_attn=w_proj_attn[l], w_proj_ssm=w_proj_ssm[l], w_out=w_out[l],
                 ln1_g=ln1_g[l], ln1_b=ln1_b[l], w_down=w_down[l], ln2_g=ln2_g[l], ln2_b=ln2_b[l])
        sp = _s5_params(ssm_a_re[l], ssm_a_im[l], ssm_log_dt[l], ssm_b_re[l], ssm_b_im[l],
                        ssm_c_re[l], ssm_c_im[l])

        def attend_p(q, k, v, kb, vb):
            return _prompt_attention(q, kb, vb, rel_bias, lam, subln_g[l], batch=bp, seq=seq,
                                     t=256, out_scale=out_scale)

        def ssm_p(u):
            return _s5_prompt(u, sp, ssm_d[l], w_glu[l], batch=bp, seq=seq, tc=256)

        def up_p(hb):
            act, cg, cv = _up_prompt(hb, w_up[l], conv_w[l], conv_b[l], batch=bp, seq=seq)
            return act, jnp.concatenate([cg, cv], axis=-1)

        hp, kt_p, v_p, re_p, im_p, c_p = _layer(hp, w, alpha=alpha, attend=attend_p, ssm=ssm_p,
                                                up=up_p, tm=512, k_transposed=(bp, seq))
        k_p = kt_p.reshape(bp, N_HEADS, 2, HEAD_DIM, seq).transpose(0, 4, 1, 2, 3)

        ck = cache_k[l].transpose(0, 2, 3, 4, 1).reshape(n_pool, width, page)
        cv_ = cache_v[l].reshape(n_pool, page * N_HEADS, V_DIM)

        def attend_s(q, k, v, kb, vb):
            return _sample_attention(q.astype(F32), k, v, ck, cv_, page_table, rel_bias, lam,
                                     subln_g[l], pages=8, out_scale=out_scale)

        def ssm_s(u):
            x0r = state_ssm_re[l].reshape(bs, N_SLABS, SLAB_STATE).transpose(1, 0, 2)
            x0i = state_ssm_im[l].reshape(bs, N_SLABS, SLAB_STATE).transpose(1, 0, 2)
            o, sr, si = _s5_sample(u, x0r, x0i, sp, ssm_d[l], w_glu[l])
            return o, sr.transpose(1, 0, 2), si.transpose(1, 0, 2)

        def up_s(hb):
            act, cg, cv = _up_sample(hb, w_up[l], conv_w[l], conv_b[l],
                                     state_conv[l].transpose(1, 0, 2))
            return act, jnp.concatenate([cg, cv], axis=-1).transpose(1, 0, 2)

        hs, k_s, v_s, re_s, im_s, c_s = _layer(hs, w, alpha=alpha, attend=attend_s, ssm=ssm_s,
                                               up=up_s, tm=512)

        for name, val in (("kp", k_p.reshape(bp, seq, N_HEADS, 2, HEAD_DIM)),
                          ("vp", v_p.reshape(bp, seq, N_HEADS, V_DIM)),
                          ("rep", re_p.reshape(bp, n_groups, STATE_DIM)),
                          ("imp", im_p.reshape(bp, n_groups, STATE_DIM)),
                          ("cp", c_p),
                          ("ks", k_s.reshape(bs, 1, N_HEADS, 2, HEAD_DIM)),
                          ("vs", v_s.reshape(bs, 1, N_HEADS, V_DIM)),
                          ("res", re_s.reshape(bs, n_groups, STATE_DIM)),
                          ("ims", im_s.reshape(bs, n_groups, STATE_DIM)),
                          ("cs", c_s)):
            outs.setdefault(name, []).append(val)

    st = {k: jnp.stack(v, axis=0) for k, v in outs.items()}
    return (hp.reshape(bp, seq, d_model), hs.reshape(bs, 1, d_model), st["kp"], st["vp"], st["rep"],
            st["imp"], st["cp"], st["ks"], st["vs"], st["res"], st["ims"], st["cs"])
```

```python
import functools
import math

import jax
import jax.numpy as jnp
from jax import lax
from jax.experimental import pallas as pl
from jax.experimental.pallas import tpu as pltpu

F32 = jnp.float32
BF16 = jnp.bfloat16

N_HEADS = 8
HEAD_DIM = 64
V_DIM = 2 * HEAD_DIM
SSM_GROUP = 16
STATE_DIM = 64
CONV_W = 3
NUM_BUCKETS = 32
MAX_EXACT = NUM_BUCKETS // 2
MAX_DISTANCE = 128
LN_EPS = 1e-5
NEG_INF = -1e30

VMEM_LIMIT_BYTES = 56 * 1024 * 1024
LANES = 128
SLAB_GROUPS = 8
N_SLABS = 8
SLAB_IN = SLAB_GROUPS * SSM_GROUP
SLAB_STATE = SLAB_GROUPS * STATE_DIM
SUBLANES = 8
MM_ROWS, MM_COLS = 1024, 1024
LN_MM_ROWS = 512
DOWN_ROWS, DOWN_COLS = 512, 512
LN_ROWS = 512
ATTN_BLOCK = 256
S5_CHUNK = 256
DECODE_PAGES = 8
UP_ROW_CHUNK = 256
SCAN_LEVELS = 3
SCAN_UNROLL = 2


def _params(sem):
    return pltpu.CompilerParams(dimension_semantics=sem, vmem_limit_bytes=VMEM_LIMIT_BYTES)


def _mm_body(*refs, n_pairs, n_extra, n_out, epilogue):
    it = iter(refs)
    take = lambda n: [next(it) for _ in range(n)]
    x_refs, xs_refs, w_refs = take(n_pairs), take(n_pairs), take(n_pairs)
    extra, extra_s = take(n_extra), take(n_extra)
    outs, outs_s = take(n_out), take(n_out)
    wbf = take(n_pairs)

    def apply(lhs_refs, extra_refs, out_refs):
        accs = [jnp.dot(x[...].astype(BF16), s[...], preferred_element_type=F32)
                for x, s in zip(lhs_refs, wbf)]
        for o, r in zip(out_refs, epilogue(*accs, *[e[...] for e in extra_refs])):
            o[...] = r.astype(o.dtype)

    @pl.when(pl.program_id(1) == 0)
    def _():
        for w, s in zip(w_refs, wbf):
            s[...] = w[...].astype(BF16)
        apply(xs_refs, extra_s, outs_s)

    apply(x_refs, extra, outs)


def _matmul(pairs, extras, out_dtypes, epilogue, *, n_cols, tm, tn, name):
    m = pairs[0][0].shape[0]
    ms = pairs[0][1].shape[0]
    grid = (n_cols // tn, m // tm)
    w_mode = dict(pipeline_mode=pl.Buffered(1)) if grid[0] == 1 else {}
    specs_x, specs_xs, specs_w, scratch = [], [], [], []
    for x, xs, w, c0 in pairs:
        k = x.shape[1]
        specs_x.append(pl.BlockSpec((tm, k), lambda j, i: (i, 0)))
        specs_xs.append(pl.BlockSpec((ms, k), lambda j, i: (0, 0)))
        specs_w.append(pl.BlockSpec((k, tn), lambda j, i, off=c0 // tn: (0, j + off), **w_mode))
        scratch.append(pltpu.VMEM((k, tn), BF16))
    specs_e, specs_es, args_e, args_es = [], [], [], []
    for e in extras:
        if isinstance(e, tuple):
            specs_e.append(pl.BlockSpec((tm, tn), lambda j, i: (i, j)))
            specs_es.append(pl.BlockSpec((ms, tn), lambda j, i: (0, j)))
            args_e.append(e[0])
            args_es.append(e[1])
        else:
            specs_e.append(pl.BlockSpec((1, tn), lambda j, i: (0, j)))
            specs_es.append(pl.BlockSpec((1, tn), lambda j, i: (0, j)))
            args_e.append(e)
            args_es.append(e)
    out_shape = ([jax.ShapeDtypeStruct((m, n_cols), d) for d in out_dtypes]
                 + [jax.ShapeDtypeStruct((ms, n_cols), d) for d in out_dtypes])
    out_specs = ([pl.BlockSpec((tm, tn), lambda j, i: (i, j)) for _ in out_dtypes]
                 + [pl.BlockSpec((ms, tn), lambda j, i: (0, j)) for _ in out_dtypes])
    body = functools.partial(_mm_body, n_pairs=len(pairs), n_extra=len(extras),
                             n_out=len(out_dtypes), epilogue=epilogue)
    outs = pl.pallas_call(
        body, grid=grid, in_specs=specs_x + specs_xs + specs_w + specs_e + specs_es,
        out_specs=out_specs, out_shape=out_shape, scratch_shapes=scratch,
        compiler_params=_params(("arbitrary", "arbitrary")), name=name)(
            *[p[0] for p in pairs], *[p[1] for p in pairs], *[p[2] for p in pairs],
            *args_e, *args_es)
    n = len(out_dtypes)
    return outs[:n], outs[n:]


def _proj_t_body(x_ref, xs_ref, w_ref, o_ref, ob_ref, os_ref, wt_sc):
    @pl.when(pl.program_id(0) == 0)
    def _():
        for c in range(w_ref.shape[1] // LANES):
            cols = slice(c * LANES, (c + 1) * LANES)
            wt_sc[cols, :] = w_ref[:, cols].T.astype(BF16)
        os_ref[...] = lax.dot_general(xs_ref[...].astype(BF16), wt_sc[...], (((1,), (1,)), ((), ())),
                                      preferred_element_type=F32)

    kt = lax.dot_general(wt_sc[...], x_ref[...], (((1,), (1,)), ((), ())),
                         preferred_element_type=F32)
    o_ref[...] = kt
    ob_ref[...] = kt.astype(BF16)


def _proj_transposed(x, xs, w, c0, *, n_cols, batch, seq, tm, name):
    k = x.shape[1]
    ms = xs.shape[0]
    per_b = seq // tm
    out_spec = pl.BlockSpec((None, n_cols, tm), lambda i: (i // per_b, 0, i % per_b))
    return pl.pallas_call(
        _proj_t_body, grid=(batch * per_b,),
        in_specs=[pl.BlockSpec((tm, k), lambda i: (i, 0)),
                  pl.BlockSpec((ms, k), lambda i: (0, 0)),
                  pl.BlockSpec((k, n_cols), lambda i, off=c0 // n_cols: (0, off),
                               pipeline_mode=pl.Buffered(1))],
        out_specs=[out_spec, out_spec, pl.BlockSpec((ms, n_cols), lambda i: (0, 0))],
        out_shape=[jax.ShapeDtypeStruct((batch, n_cols, seq), F32),
                   jax.ShapeDtypeStruct((batch, n_cols, seq), BF16),
                   jax.ShapeDtypeStruct((ms, n_cols), F32)],
        scratch_shapes=[pltpu.VMEM((n_cols, k), BF16)],
        compiler_params=_params(("arbitrary",)), name=name)(x, xs, w)


def _ln(x, g, b):
    mu = jnp.mean(x, axis=-1, keepdims=True)
    xc = x - mu
    var = jnp.mean(xc * xc, axis=-1, keepdims=True)
    return xc * lax.rsqrt(var + LN_EPS) * g + b


def _ln_body(x_ref, g_ref, b_ref, o_ref):
    o_ref[...] = _ln(x_ref[...], g_ref[...], b_ref[...])


def _layer_norm(x, g, b, *, tm, name):
    m, d = x.shape
    tm = min(tm, m)
    return pl.pallas_call(
        _ln_body, grid=(m // tm,),
        in_specs=[pl.BlockSpec((tm, d), lambda i: (i, 0)),
                  pl.BlockSpec((1, d), lambda i: (0, 0)),
                  pl.BlockSpec((1, d), lambda i: (0, 0))],
        out_specs=pl.BlockSpec((tm, d), lambda i: (i, 0)),
        out_shape=jax.ShapeDtypeStruct((m, d), F32),
        compiler_params=_params(("arbitrary",)), name=name)(x, g.reshape(1, d), b.reshape(1, d))


def _rel_bucket(n):
    n = jnp.maximum(n, 0)
    nf = jnp.maximum(n, 1).astype(F32)
    large = MAX_EXACT + (jnp.log(nf / MAX_EXACT) / math.log(MAX_DISTANCE / MAX_EXACT)
                         * (NUM_BUCKETS - MAX_EXACT)).astype(jnp.int32)
    large = jnp.minimum(large, NUM_BUCKETS - 1)
    return jnp.where(n < MAX_EXACT, n, large)


def _bucket_lookup(bucket, table_fn):
    out = jnp.zeros(jnp.broadcast_shapes(bucket.shape, table_fn(0).shape), F32)
    for b in range(NUM_BUCKETS):
        out = out + jnp.where(bucket == b, table_fn(b), 0.0)
    return out


def _bias_tile_body(rb_ref, bucket_ref, o_ref):
    h = pl.program_id(0)
    o_ref[...] = _bucket_lookup(bucket_ref[...], lambda bk: rb_ref[bk, h])


def _bias_tiles(rel_bias, t):
    r = jnp.arange(t, dtype=jnp.int32)
    c = jnp.arange(2 * t, dtype=jnp.int32)
    buckets = _rel_bucket(r[:, None] + t - c[None, :])
    return pl.pallas_call(
        _bias_tile_body, grid=(N_HEADS,),
        in_specs=[pl.BlockSpec(memory_space=pltpu.SMEM), pl.BlockSpec((t, 2 * t), lambda h: (0, 0))],
        out_specs=pl.BlockSpec((None, t, 2 * t), lambda h: (h, 0, 0)),
        out_shape=jax.ShapeDtypeStruct((N_HEADS, t, 2 * t), F32),
        compiler_params=_params(("arbitrary",)), name="bias_tiles")(rel_bias, buckets)


def _attn_body(lam_ref, rb_ref, q_ref, k_ref, v_ref, bias_ref, g_ref, o_ref, *, t, n_far, near, out_scale):
    h = pl.program_id(0)
    far = n_far * t
    q = q_ref[...] * (HEAD_DIM ** -0.5)
    lane = lax.broadcasted_iota(jnp.int32, q.shape, 1)
    zero = jnp.zeros_like(q)
    q2 = jnp.concatenate([jnp.where(lane < HEAD_DIM, q, zero), jnp.where(lane >= HEAD_DIM, q, zero)],
                         axis=0)

    bias = bias_ref[:, 2 * t - near:]
    row = lax.broadcasted_iota(jnp.int32, (t, near), 0)
    col = lax.broadcasted_iota(jnp.int32, (t, near), 1)
    keep = col - (near - t) <= row
    s_near = jnp.dot(q2, k_ref[:, far:far + near], preferred_element_type=F32)
    s_near = jnp.where(jnp.concatenate([keep, keep], axis=0),
                       s_near + jnp.concatenate([bias, bias], axis=0), NEG_INF)
    m = jnp.max(s_near, axis=1, keepdims=True)
    if n_far:
        s_far = jnp.dot(q2, k_ref[:, 0:far], preferred_element_type=F32) + rb_ref[NUM_BUCKETS - 1, h]
        m = jnp.maximum(m, jnp.max(s_far, axis=1, keepdims=True))
    p = jnp.exp(s_near - m)
    l = jnp.sum(p, axis=1, keepdims=True)
    acc = jnp.dot(p.astype(BF16), v_ref[far:far + near, :], preferred_element_type=F32)
    if n_far:
        p = jnp.exp(s_far - m)
        l = l + jnp.sum(p, axis=1, keepdims=True)
        acc = acc + jnp.dot(p.astype(BF16), v_ref[0:far, :], preferred_element_type=F32)

    nrm = acc / l
    o = nrm[0:t] - lam_ref[0] * nrm[t:2 * t]
    ms = jnp.mean(o * o, axis=-1, keepdims=True)
    o_ref[...] = (o * lax.rsqrt(ms + LN_EPS) * g_ref[...] * out_scale).astype(o_ref.dtype)


def _prompt_attention(q, k, v, rel_bias, lam, subln_g, *, batch, seq, t, out_scale):
    assert t >= MAX_DISTANCE
    nq = seq // t
    width = N_HEADS * V_DIM
    q3 = q.reshape(batch, seq, width)
    v3 = v.reshape(batch, seq, width)
    bias = _bias_tiles(rel_bias, t)
    smem = pl.BlockSpec(memory_space=pltpu.SMEM)
    pieces = []
    for i in range(nq):
        n_far = max(i - 1, 0)
        near = min(i + 1, 2) * t
        keys = n_far * t + near
        body = functools.partial(_attn_body, t=t, n_far=n_far, near=near, out_scale=out_scale)
        pieces.append(pl.pallas_call(
            body, grid=(N_HEADS, batch),
            in_specs=[smem, smem,
                      pl.BlockSpec((None, t, V_DIM), lambda h, b, i=i: (b, i, h)),
                      pl.BlockSpec((None, V_DIM, keys), lambda h, b: (b, h, 0)),
                      pl.BlockSpec((None, keys, V_DIM), lambda h, b: (b, 0, h)),
                      pl.BlockSpec((None, t, 2 * t), lambda h, b: (h, 0, 0)),
                      pl.BlockSpec((1, V_DIM), lambda h, b: (0, 0))],
            out_specs=pl.BlockSpec((None, t, V_DIM), lambda h, b: (b, 0, h)),
            out_shape=jax.ShapeDtypeStruct((batch, t, width), BF16),
            compiler_params=_params(("arbitrary", "arbitrary")),
            name=f"prompt_attention_q{i}")(lam, rel_bias, q3, k, v3, bias, subln_g.reshape(1, V_DIM)))
    return jnp.stack(pieces, axis=1).reshape(batch * seq, width)


def _decode_body(pt_ref, lam_ref, q_ref, kn_ref, vn_ref, rbt_ref, bucket_ref, g_ref, *rest,
                 pages, page, n_steps, out_scale):
    k_refs = rest[:pages]
    v_refs = rest[pages:2 * pages]
    o_ref = rest[2 * pages]
    qexp_sc, bias_sc, m_sc, l_sc, acc_sc = rest[2 * pages + 1:]
    step = pl.program_id(1)
    rows = 2 * N_HEADS
    width = N_HEADS * V_DIM

    @pl.when(step == 0)
    def _():
        row = lax.broadcasted_iota(jnp.int32, (rows, width), 0)
        col = lax.broadcasted_iota(jnp.int32, (rows, width), 1)
        own_qk = (col // HEAD_DIM) == (row % N_HEADS) * 2 + row // N_HEADS
        q = (q_ref[...] * (HEAD_DIM ** -0.5)).astype(BF16).astype(F32)
        qexp = jnp.where(own_qk, jnp.broadcast_to(q, (rows, width)), 0.0)
        qexp_sc[...] = qexp.astype(BF16)
        bias_sc[...] = _bucket_lookup(bucket_ref[...], lambda bk: rbt_ref[:, bk:bk + 1])
        kn = kn_ref[...].astype(BF16).astype(F32)
        s_self = jnp.sum(qexp * kn, axis=1, keepdims=True) + rbt_ref[:, 0:1]
        m_sc[...] = s_self
        l_sc[...] = jnp.ones(l_sc.shape, F32)
        vn = vn_ref[...].astype(BF16).astype(F32)
        acc_sc[...] = jnp.broadcast_to(vn[:, None, :], acc_sc.shape)

    qexp = qexp_sc[...]
    bias_far = rbt_ref[:, NUM_BUCKETS - 1:NUM_BUCKETS]
    is_last = step == n_steps - 1
    s_parts = []
    for p in range(pages):
        s = jnp.dot(qexp, k_refs[p][...].astype(BF16), preferred_element_type=F32)
        if p == pages - 1:
            s = s + jnp.where(is_last, bias_sc[...], bias_far)
        else:
            s = s + bias_far
        s_parts.append(s)
    m_old = m_sc[...]
    m_new = m_old
    for s in s_parts:
        m_new = jnp.maximum(m_new, jnp.max(s, axis=1, keepdims=True))
    a = jnp.exp(m_old - m_new)
    l_new = a * l_sc[...]
    probs = []
    for p in range(pages):
        pr = jnp.exp(s_parts[p] - m_new)
        l_new = l_new + jnp.sum(pr, axis=1, keepdims=True)
        probs.append(pr.astype(BF16))
    probs = jnp.concatenate(probs, axis=1)
    m_sc[...] = m_new
    l_sc[...] = l_new
    for h in range(N_HEADS):
        vh = jnp.concatenate([v_refs[p][pl.ds(h, page, stride=N_HEADS), :].astype(BF16)
                              for p in range(pages)], axis=0)
        acc_sc[h] = a * acc_sc[h] + jnp.dot(probs, vh, preferred_element_type=F32)

    @pl.when(is_last)
    def _():
        nrm = acc_sc[...] / l_new[None]
        r = lax.broadcasted_iota(jnp.int32, nrm.shape, 1)
        hh = lax.broadcasted_iota(jnp.int32, nrm.shape, 0)
        coef = jnp.where(r == hh, 1.0, jnp.where(r == hh + N_HEADS, -lam_ref[0], 0.0))
        d = jnp.sum(coef * nrm, axis=1)
        ms = jnp.mean(d * d, axis=-1, keepdims=True)
        o_ref[...] = (d * lax.rsqrt(ms + LN_EPS) * g_ref[...] * out_scale).astype(o_ref.dtype)


def _sample_attention(q, k_new, v_new, cache_k, cache_v, page_table, rel_bias, lam, subln_g,
                      *, pages, out_scale):
    bs, width = q.shape
    page = cache_k.shape[2]
    n_pages = page_table.shape[1]
    n_steps = n_pages // pages
    past = n_pages * page
    kpos = past - page + jnp.arange(page, dtype=jnp.int32)
    bucket_last = _rel_bucket(past - kpos).reshape(1, page)
    rbt = jnp.tile(rel_bias.T, (2, 1))

    def tok_spec():
        return pl.BlockSpec((None, 1, width), lambda b, s, pt: (b, 0, 0))

    def page_spec(shape, p):
        return pl.BlockSpec((None,) + shape, lambda b, s, pt, p=p: (pt[b, s * pages + p], 0, 0))

    full = lambda shape: pl.BlockSpec(shape, lambda b, s, pt: tuple(0 for _ in shape))
    head_spec = pl.BlockSpec((None, N_HEADS, V_DIM), lambda b, s, pt: (b, 0, 0))
    in_specs = ([pl.BlockSpec(memory_space=pltpu.SMEM), tok_spec(), tok_spec(), head_spec,
                 full((2 * N_HEADS, NUM_BUCKETS)), full((1, page)), full((1, V_DIM))]
                + [page_spec((width, page), p) for p in range(pages)]
                + [page_spec((page * N_HEADS, V_DIM), p) for p in range(pages)])
    rows = 2 * N_HEADS
    body = functools.partial(_decode_body, pages=pages, page=page, n_steps=n_steps,
                             out_scale=out_scale)
    out = pl.pallas_call(
        body,
        grid_spec=pltpu.PrefetchScalarGridSpec(
            num_scalar_prefetch=1, grid=(bs, n_steps), in_specs=in_specs,
            out_specs=head_spec,
            scratch_shapes=[pltpu.VMEM((rows, width), BF16), pltpu.VMEM((rows, page), F32),
                            pltpu.VMEM((rows, 1), F32), pltpu.VMEM((rows, 1), F32),
                            pltpu.VMEM((N_HEADS, rows, V_DIM), F32)]),
        out_shape=jax.ShapeDtypeStruct((bs, N_HEADS, V_DIM), BF16),
        compiler_params=_params(("arbitrary", "arbitrary")),
        name="sample_attention")(
            page_table, lam, q.reshape(bs, 1, width), k_new.reshape(bs, 1, width),
            v_new.reshape(bs, N_HEADS, V_DIM), rbt, bucket_last, subln_g.reshape(1, V_DIM),
            *([cache_k] * pages), *([cache_v] * pages))
    return out.reshape(bs, width)


def _s5_params(a_re, a_im, log_dt, b_re, b_im, c_re, c_im):
    dt = jnp.exp(log_dt)[:, None]
    mag = jnp.exp(a_re * dt)
    ang = a_im * dt
    abar_re = mag * jnp.cos(ang)
    abar_im = mag * jnp.sin(ang)
    den = a_re * a_re + a_im * a_im
    f_re = ((abar_re - 1.0) * a_re + abar_im * a_im) / den
    f_im = (abar_im * a_re - (abar_re - 1.0) * a_im) / den
    bb_re = f_re[..., None] * b_re - f_im[..., None] * b_im
    bb_im = f_re[..., None] * b_im + f_im[..., None] * b_re
    eye = jnp.eye(SLAB_GROUPS, dtype=F32)

    def in_slabs(bb):
        tt = bb.reshape(N_SLABS, SLAB_GROUPS, STATE_DIM, SSM_GROUP).transpose(0, 1, 3, 2)
        full = tt[:, :, :, None, :] * eye[None, :, None, :, None]
        return full.reshape(N_SLABS, SLAB_IN, SLAB_STATE).astype(BF16)

    def out_slabs(cc):
        tt = cc.reshape(N_SLABS, SLAB_GROUPS, SSM_GROUP, STATE_DIM).transpose(0, 1, 3, 2)
        full = tt[:, :, :, None, :] * eye[None, :, None, :, None]
        return full.reshape(N_SLABS, SLAB_STATE, SLAB_IN).astype(BF16)

    return (abar_re.reshape(N_SLABS, SLAB_STATE), abar_im.reshape(N_SLABS, SLAB_STATE),
            in_slabs(bb_re), in_slabs(bb_im), out_slabs(c_re), out_slabs(c_im))


def _glu_out(y, u, d_ref, wglu_ref, o_ref):
    g = jax.nn.gelu(y + d_ref[...] * u)
    gate = jnp.dot(g.astype(BF16), wglu_ref[...].astype(BF16), preferred_element_type=F32)
    o_ref[...] = (g * jax.nn.sigmoid(gate)).astype(o_ref.dtype)


def _s5_prompt_body(u_ref, ar_ref, ai_ref, bbr_ref, bbi_ref, cr_ref, ci_ref, d_ref, wglu_ref,
                    o_ref, sre_ref, sim_ref, xr_sc, xi_sc, pw_sc, cr_sc, ci_sc, *, tc):
    b = pl.program_id(0)
    c = pl.program_id(1)
    sub = SUBLANES

    def cmul(pr, pi, qr, qi):
        return pr * qr - pi * qi, pr * qi + pi * qr

    @pl.when(jnp.logical_and(b == 0, c == 0))
    def _():
        t = lax.broadcasted_iota(jnp.int32, (sub, SLAB_STATE), 0)
        for j in range(N_SLABS):
            a = (jnp.broadcast_to(ar_ref[j:j + 1, :], t.shape), jnp.broadcast_to(ai_ref[j:j + 1, :], t.shape))
            powers = [a]
            for _ in range(sub - 1):
                powers.append(cmul(*powers[-1], *a))
            for lvl in range(SCAN_LEVELS):
                k = 1 << lvl
                for part in range(2):
                    pw_sc[j, 2 * lvl + part] = jnp.where(t >= k, powers[k - 1][part], 0.0)
            for part in range(2):
                acc = powers[sub - 1][part]
                for row in range(sub - 1):
                    acc = jnp.where(t == row, powers[row][part], acc)
                pw_sc[j, 2 * SCAN_LEVELS + part] = acc

    @pl.when(c == 0)
    def _():
        cr_sc[...] = jnp.zeros(cr_sc.shape, F32)
        ci_sc[...] = jnp.zeros(ci_sc.shape, F32)

    u = u_ref[...]
    ub = u.astype(BF16)
    for j in range(N_SLABS):
        uj = ub[:, j * SLAB_IN:(j + 1) * SLAB_IN]
        xr_sc[j] = jnp.dot(uj, bbr_ref[j], preferred_element_type=F32)
        xi_sc[j] = jnp.dot(uj, bbi_ref[j], preferred_element_type=F32)

    for j in range(N_SLABS):
        cr = jnp.broadcast_to(cr_sc[j:j + 1, :], (sub, SLAB_STATE))
        ci = jnp.broadcast_to(ci_sc[j:j + 1, :], (sub, SLAB_STATE))
        for v in range(tc // sub):
            rows = slice(v * sub, (v + 1) * sub)
            xr = xr_sc[j, rows, :]
            xi = xi_sc[j, rows, :]
            for lvl in range(SCAN_LEVELS):
                k = 1 << lvl
                dr, di = cmul(pw_sc[j, 2 * lvl], pw_sc[j, 2 * lvl + 1],
                              pltpu.roll(xr, k, 0), pltpu.roll(xi, k, 0))
                xr, xi = xr + dr, xi + di
            dr, di = cmul(pw_sc[j, 2 * SCAN_LEVELS], pw_sc[j, 2 * SCAN_LEVELS + 1], cr, ci)
            xr, xi = xr + dr, xi + di
            xr_sc[j, rows, :] = xr
            xi_sc[j, rows, :] = xi
            cr = jnp.broadcast_to(xr[sub - 1:sub, :], xr.shape)
            ci = jnp.broadcast_to(xi[sub - 1:sub, :], xi.shape)
        cr_sc[j:j + 1, :] = cr[0:1, :]
        ci_sc[j:j + 1, :] = ci[0:1, :]
    sre_ref[...] = cr_sc[...]
    sim_ref[...] = ci_sc[...]

    ys = []
    for j in range(N_SLABS):
        ys.append(jnp.dot(xr_sc[j].astype(BF16), cr_ref[j], preferred_element_type=F32)
                  - jnp.dot(xi_sc[j].astype(BF16), ci_ref[j], preferred_element_type=F32))
    _glu_out(jnp.concatenate(ys, axis=1), u, d_ref, wglu_ref, o_ref)


def _s5_prompt(u, sp, d, w_glu, *, batch, seq, tc):
    ar, ai, bbr, bbi, cr, ci = sp
    width = u.shape[1]
    nc = seq // tc
    const = lambda shape: pl.BlockSpec(shape, lambda b, c: tuple(0 for _ in shape))
    state_spec = pl.BlockSpec((None, N_SLABS, SLAB_STATE), lambda b, c: (b, 0, 0))
    state_shape = jax.ShapeDtypeStruct((batch, N_SLABS, SLAB_STATE), F32)
    return pl.pallas_call(
        functools.partial(_s5_prompt_body, tc=tc), grid=(batch, nc),
        in_specs=[pl.BlockSpec((tc, width), lambda b, c: (b * nc + c, 0)),
                  const(ar.shape), const(ai.shape), const(bbr.shape), const(bbi.shape),
                  const(cr.shape), const(ci.shape), const((1, width)), const(w_glu.shape)],
        out_specs=[pl.BlockSpec((tc, width), lambda b, c: (b * nc + c, 0)), state_spec, state_spec],
        out_shape=[jax.ShapeDtypeStruct((batch * seq, width), BF16), state_shape, state_shape],
        scratch_shapes=[pltpu.VMEM((N_SLABS, tc, SLAB_STATE), F32),
                        pltpu.VMEM((N_SLABS, tc, SLAB_STATE), F32),
                        pltpu.VMEM((N_SLABS, 2 * SCAN_LEVELS + 2, SUBLANES, SLAB_STATE), F32),
                        pltpu.VMEM((N_SLABS, SLAB_STATE), F32),
                        pltpu.VMEM((N_SLABS, SLAB_STATE), F32)],
        compiler_params=_params(("arbitrary", "arbitrary")),
        name="s5_prompt")(u, ar, ai, bbr, bbi, cr, ci, d.reshape(1, width), w_glu)


def _s5_sample_body(u_ref, x0r_ref, x0i_ref, ar_ref, ai_ref, bbr_ref, bbi_ref, cr_ref, ci_ref,
                    d_ref, wglu_ref, o_ref, sre_ref, sim_ref):
    u = u_ref[...]
    ub = u.astype(BF16)
    ys = []
    for j in range(N_SLABS):
        uj = ub[:, j * SLAB_IN:(j + 1) * SLAB_IN]
        ar = ar_ref[j:j + 1, :]
        ai = ai_ref[j:j + 1, :]
        x0r = x0r_ref[j]
        x0i = x0i_ref[j]
        xr = ar * x0r - ai * x0i + jnp.dot(uj, bbr_ref[j], preferred_element_type=F32)
        xi = ar * x0i + ai * x0r + jnp.dot(uj, bbi_ref[j], preferred_element_type=F32)
        sre_ref[j] = xr
        sim_ref[j] = xi
        ys.append(jnp.dot(xr.astype(BF16), cr_ref[j], preferred_element_type=F32)
                  - jnp.dot(xi.astype(BF16), ci_ref[j], preferred_element_type=F32))
    _glu_out(jnp.concatenate(ys, axis=1), u, d_ref, wglu_ref, o_ref)


def _s5_sample(u, x0_re, x0_im, sp, d, w_glu):
    ar, ai, bbr, bbi, cr, ci = sp
    bs, width = u.shape
    state_shape = jax.ShapeDtypeStruct((N_SLABS, bs, SLAB_STATE), F32)
    return pl.pallas_call(
        _s5_sample_body,
        out_shape=[jax.ShapeDtypeStruct((bs, width), BF16), state_shape, state_shape],
        compiler_params=pltpu.CompilerParams(vmem_limit_bytes=VMEM_LIMIT_BYTES),
        name="s5_sample")(u, x0_re, x0_im, ar, ai, bbr, bbi, cr, ci, d.reshape(1, width), w_glu)


def _gated(c):
    gate = c[:, :LANES]
    return gate * jax.nn.sigmoid(gate) * c[:, LANES:]


def _up_prompt_body(h_ref, wg_ref, wv_ref, cwg_ref, cwv_ref, cbg_ref, cbv_ref,
                    act_ref, cg_ref, cv_ref, w_sc, *, rows):
    w_sc[:, 0:LANES] = wg_ref[...].astype(BF16)
    w_sc[:, LANES:2 * LANES] = wv_ref[...].astype(BF16)
    cw = jnp.concatenate([cwg_ref[...], cwv_ref[...]], axis=1)
    cb = jnp.concatenate([cbg_ref[...], cbv_ref[...]], axis=1)
    seq = h_ref.shape[0]
    sub = SUBLANES
    row8 = lax.broadcasted_iota(jnp.int32, (sub, 2 * LANES), 0)
    prev = jnp.zeros((sub, 2 * LANES), F32)
    for c in range(seq // rows):
        up = jnp.dot(h_ref[c * rows:(c + 1) * rows, :], w_sc[...], preferred_element_type=F32)
        cur = cb + cw[CONV_W - 1:CONV_W] * up
        for back in range(1, CONV_W):
            rolled = pltpu.roll(up, back, 0)
            head = jnp.where(row8 < back, pltpu.roll(prev, back, 0), rolled[0:sub])
            shifted = jnp.concatenate([head, rolled[sub:]], axis=0)
            cur = cur + cw[CONV_W - 1 - back:CONV_W - back] * shifted
        act_ref[c * rows:(c + 1) * rows, :] = _gated(cur).astype(act_ref.dtype)
        prev = up[rows - sub:, :]
    tail = prev[sub - (CONV_W - 1):, :]
    cg_ref[...] = tail[:, :LANES]
    cv_ref[...] = tail[:, LANES:]


def _up_prompt(hb, w_up, conv_w, conv_b, *, batch, seq):
    d_model = hb.shape[1]
    d_ff = w_up.shape[1] // 2
    nb = d_ff // LANES
    cb = conv_b.reshape(1, 2 * d_ff)
    col = lambda off: (lambda b, j: (0, j + off))
    tail_spec = pl.BlockSpec((None, CONV_W - 1, LANES), lambda b, j: (b, 0, j))
    tail_shape = jax.ShapeDtypeStruct((batch, CONV_W - 1, d_ff), F32)
    return pl.pallas_call(
        functools.partial(_up_prompt_body, rows=UP_ROW_CHUNK), grid=(batch, nb),
        scratch_shapes=[pltpu.VMEM((d_model, 2 * LANES), BF16)],
        in_specs=[pl.BlockSpec((seq, d_model), lambda b, j: (b, 0)),
                  pl.BlockSpec((d_model, LANES), col(0)), pl.BlockSpec((d_model, LANES), col(nb)),
                  pl.BlockSpec((CONV_W, LANES), col(0)), pl.BlockSpec((CONV_W, LANES), col(nb)),
                  pl.BlockSpec((1, LANES), col(0)), pl.BlockSpec((1, LANES), col(nb))],
        out_specs=[pl.BlockSpec((seq, LANES), lambda b, j: (b, j)), tail_spec, tail_spec],
        out_shape=[jax.ShapeDtypeStruct((batch * seq, d_ff), BF16), tail_shape, tail_shape],
        compiler_params=_params(("arbitrary", "arbitrary")),
        name="up_prompt")(hb, w_up, w_up, conv_w, conv_w, cb, cb)


def _up_sample_body(h_ref, wg_ref, wv_ref, cwg_ref, cwv_ref, cbg_ref, cbv_ref, sg_ref, sv_ref,
                    act_ref, cg_ref, cv_ref):
    w = jnp.concatenate([wg_ref[...].astype(BF16), wv_ref[...].astype(BF16)], axis=1)
    up = jnp.dot(h_ref[...], w, preferred_element_type=F32)
    cw = jnp.concatenate([cwg_ref[...], cwv_ref[...]], axis=1)
    cb = jnp.concatenate([cbg_ref[...], cbv_ref[...]], axis=1)
    c = cb + cw[CONV_W - 1:CONV_W] * up
    for tap in range(CONV_W - 1):
        st = jnp.concatenate([sg_ref[tap], sv_ref[tap]], axis=1)
        c = c + cw[tap:tap + 1] * st
    act_ref[...] = _gated(c).astype(act_ref.dtype)
    for tap in range(1, CONV_W - 1):
        cg_ref[tap - 1] = sg_ref[tap]
        cv_ref[tap - 1] = sv_ref[tap]
    cg_ref[CONV_W - 2] = up[:, :LANES]
    cv_ref[CONV_W - 2] = up[:, LANES:]


def _up_sample(hb, w_up, conv_w, conv_b, state):
    bs, d_model = hb.shape
    d_ff = w_up.shape[1] // 2
    nb = d_ff // LANES
    cb = conv_b.reshape(1, 2 * d_ff)
    col = lambda off: (lambda j: (0, j + off))
    st = lambda off: pl.BlockSpec((CONV_W - 1, bs, LANES), lambda j, off=off: (0, 0, j + off))
    tail_spec = pl.BlockSpec((CONV_W - 1, bs, LANES), lambda j: (0, 0, j))
    tail_shape = jax.ShapeDtypeStruct((CONV_W - 1, bs, d_ff), F32)
    return pl.pallas_call(
        _up_sample_body, grid=(nb,),
        in_specs=[pl.BlockSpec((bs, d_model), lambda j: (0, 0)),
                  pl.BlockSpec((d_model, LANES), col(0)), pl.BlockSpec((d_model, LANES), col(nb)),
                  pl.BlockSpec((CONV_W, LANES), col(0)), pl.BlockSpec((CONV_W, LANES), col(nb)),
                  pl.BlockSpec((1, LANES), col(0)), pl.BlockSpec((1, LANES), col(nb)),
                  st(0), st(nb)],
        out_specs=[pl.BlockSpec((bs, LANES), lambda j: (0, j)), tail_spec, tail_spec],
        out_shape=[jax.ShapeDtypeStruct((bs, d_ff), BF16), tail_shape, tail_shape],
        compiler_params=_params(("arbitrary",)),
        name="up_sample")(hb, w_up, w_up, conv_w, conv_w, cb, cb, state, state)


def _one(x):
    return (x,)


def _sigmoid_out(acc):
    return (jax.nn.sigmoid(acc),)


def _both(acc):
    return (acc, acc)


def _merge(pa, ps, ga, gs):
    return (ga * pa + gs * ps,)


def _layer(xp, xs, w, *, batch, seq, alpha, attend_p, attend_s, ssm_p, ssm_s, up_p, up_s):
    d_model = xp.shape[1]
    qk_w = N_HEADS * 2 * HEAD_DIM
    v_w = N_HEADS * V_DIM
    ssm_w = d_model // 2
    xb = xp.astype(BF16)
    w_in = w["w_in"]
    g1, b1 = w["ln1_g"].reshape(1, d_model), w["ln1_b"].reshape(1, d_model)
    big = dict(tm=MM_ROWS, tn=MM_COLS)
    c = 0
    (q,), (q_s,) = _matmul([(xb, xs, w_in, c)], [], [BF16], _one, n_cols=qk_w, name="proj_q", **big)
    c += qk_w
    kt, ktb, k_s = _proj_transposed(xb, xs, w_in, c, n_cols=qk_w, batch=batch, seq=seq, tm=MM_ROWS,
                                    name="proj_kt")
    c += qk_w
    (v, vb), (v_s, _) = _matmul([(xb, xs, w_in, c)], [], [F32, BF16], _both, n_cols=v_w, name="proj_v", **big)
    c += v_w
    (u,), (u_s,) = _matmul([(xb, xs, w_in, c)], [], [F32], _one, n_cols=ssm_w, name="proj_u", **big)
    c += ssm_w
    (ga,), (ga_s,) = _matmul([(xb, xs, w_in, c)], [], [F32], _sigmoid_out, n_cols=d_model, name="gate_a", **big)
    c += d_model
    (gs,), (gs_s,) = _matmul([(xb, xs, w_in, c)], [], [F32], _sigmoid_out, n_cols=d_model, name="gate_s", **big)

    attn = attend_p(q, ktb, vb)
    attn_s = attend_s(q_s, k_s, v_s)
    ssm_out, re_p, im_p = ssm_p(u)
    ssm_out_s, re_s, im_s = ssm_s(u_s)

    (merged,), (merged_s,) = _matmul(
        [(attn, attn_s, w["w_proj_attn"], 0), (ssm_out, ssm_out_s, w["w_proj_ssm"], 0)],
        [(ga, ga_s), (gs, gs_s)], [BF16], _merge, n_cols=d_model, name="merge", **big)

    def post_ln1(acc, res, g, b):
        h = _ln(alpha * res + acc, g, b)
        return h, h

    (h, hb), (h_s, hb_s) = _matmul([(merged, merged_s, w["w_out"], 0)], [(xp, xs), g1, b1], [F32, BF16],
                                   post_ln1, n_cols=d_model, tm=LN_MM_ROWS, tn=d_model, name="out_proj_ln1")
    act, conv_p = up_p(hb)
    act_s, conv_s = up_s(hb_s)
    (r2,), (r2_s,) = _matmul([(act, act_s, w["w_down"], 0)], [(h, h_s)], [F32],
                             lambda acc, res: (alpha * res + acc,), n_cols=d_model,
                             tm=DOWN_ROWS, tn=DOWN_COLS, name="down_proj")
    y = _layer_norm(r2, w["ln2_g"], w["ln2_b"], tm=LN_ROWS, name="ln2")
    y_s = _layer_norm(r2_s, w["ln2_g"], w["ln2_b"], tm=LN_ROWS, name="ln2_sample")
    return (y, kt, v, re_p, im_p, conv_p), (y_s, k_s, v_s, re_s, im_s, conv_s)


def kernel(x_prompt, x_sample, cache_k, cache_v, state_ssm_re, state_ssm_im, state_conv, page_table, rel_bias, w_in, lambda_q1, lambda_k1, lambda_q2, lambda_k2, subln_g, ssm_a_re, ssm_a_im, ssm_log_dt, ssm_b_re, ssm_b_im, ssm_c_re, ssm_c_im, ssm_d, w_glu, w_proj_attn, w_proj_ssm, w_out, ln1_g, ln1_b, w_up, conv_w, conv_b, w_down, ln2_g, ln2_b):
    depth = w_in.shape[0]
    assert depth == 1, "single-layer trunk"
    bp, seq, d_model = x_prompt.shape
    bs, dec_seq, _ = x_sample.shape
    assert dec_seq == 1
    n_pool, page = cache_k.shape[1], cache_k.shape[2]
    d_ff = w_down.shape[1]
    n_groups = ssm_a_re.shape[1]
    assert n_groups == N_SLABS * SLAB_GROUPS and d_ff % LANES == 0
    alpha = (2.0 * depth) ** 0.25
    width = N_HEADS * V_DIM

    hp = x_prompt.reshape(bp * seq, d_model)
    hs = x_sample.reshape(bs, d_model)
    outs = {}
    for l in range(depth):
        lam_init = 0.8 - 0.6 * math.exp(-0.3 * l)
        out_scale = 1.0 - lam_init
        lam = (jnp.exp(jnp.sum(lambda_q1[l] * lambda_k1[l]))
               - jnp.exp(jnp.sum(lambda_q2[l] * lambda_k2[l])) + lam_init).reshape(1)
        w = dict(w_in=w_in[l], w_proj_attn=w_proj_attn[l], w_proj_ssm=w_proj_ssm[l], w_out=w_out[l],
                 ln1_g=ln1_g[l], ln1_b=ln1_b[l], w_down=w_down[l], ln2_g=ln2_g[l], ln2_b=ln2_b[l])
        sp = _s5_params(ssm_a_re[l], ssm_a_im[l], ssm_log_dt[l], ssm_b_re[l], ssm_b_im[l],
                        ssm_c_re[l], ssm_c_im[l])

        def attend_p(q, kt, v):
            return _prompt_attention(q, kt, v, rel_bias, lam, subln_g[l], batch=bp, seq=seq,
                                     t=ATTN_BLOCK, out_scale=out_scale)

        def ssm_p(u):
            return _s5_prompt(u, sp, ssm_d[l], w_glu[l], batch=bp, seq=seq, tc=S5_CHUNK)

        def up_p(hb):
            act, cg, cv = _up_prompt(hb, w_up[l], conv_w[l], conv_b[l], batch=bp, seq=seq)
            return act, jnp.concatenate([cg, cv], axis=-1)

        ck = cache_k[l].transpose(0, 2, 3, 4, 1).reshape(n_pool, width, page)
        cv_ = cache_v[l].reshape(n_pool, page * N_HEADS, V_DIM)

        def attend_s(q, k, v):
            return _sample_attention(q.astype(F32), k, v, ck, cv_, page_table, rel_bias, lam,
                                     subln_g[l], pages=DECODE_PAGES, out_scale=out_scale)

        def ssm_s(u):
            x0r = state_ssm_re[l].reshape(bs, N_SLABS, SLAB_STATE).transpose(1, 0, 2)
            x0i = state_ssm_im[l].reshape(bs, N_SLABS, SLAB_STATE).transpose(1, 0, 2)
            o, sr, si = _s5_sample(u, x0r, x0i, sp, ssm_d[l], w_glu[l])
            return o, sr.transpose(1, 0, 2), si.transpose(1, 0, 2)

        def up_s(hb):
            act, cg, cv = _up_sample(hb, w_up[l], conv_w[l], conv_b[l],
                                     state_conv[l].transpose(1, 0, 2))
            return act, jnp.concatenate([cg, cv], axis=-1).transpose(1, 0, 2)

        (hp, kt_p, v_p, re_p, im_p, c_p), (hs, k_s, v_s, re_s, im_s, c_s) = _layer(
            hp, hs, w, batch=bp, seq=seq, alpha=alpha, attend_p=attend_p, attend_s=attend_s,
            ssm_p=ssm_p, ssm_s=ssm_s, up_p=up_p, up_s=up_s)
        k_p = kt_p.reshape(bp, N_HEADS, 2, HEAD_DIM, seq).transpose(0, 4, 1, 2, 3)

        for name, val in (("kp", k_p.reshape(bp, seq, N_HEADS, 2, HEAD_DIM)),
                          ("vp", v_p.reshape(bp, seq, N_HEADS, V_DIM)),
                          ("rep", re_p.reshape(bp, n_groups, STATE_DIM)),
                          ("imp", im_p.reshape(bp, n_groups, STATE_DIM)),
                          ("cp", c_p),
                          ("ks", k_s.reshape(bs, 1, N_HEADS, 2, HEAD_DIM)),
                          ("vs", v_s.reshape(bs, 1, N_HEADS, V_DIM)),
                          ("res", re_s.reshape(bs, n_groups, STATE_DIM)),
                          ("ims", im_s.reshape(bs, n_groups, STATE_DIM)),
                          ("cs", c_s)):
            outs.setdefault(name, []).append(val)

    st = {k: jnp.stack(v, axis=0) for k, v in outs.items()}
    return (hp.reshape(bp, seq, d_model), hs.reshape(bs, 1, d_model), st["kp"], st["vp"], st["rep"],
            st["imp"], st["cp"], st["ks"], st["vs"], st["res"], st["ims"], st["cs"])
```

```python
import functools
import math

import jax
import jax.numpy as jnp
from jax import lax
from jax.experimental import pallas as pl
from jax.experimental.pallas import tpu as pltpu

F32 = jnp.float32
BF16 = jnp.bfloat16

N_HEADS = 8
HEAD_DIM = 64
V_DIM = 2 * HEAD_DIM
SSM_GROUP = 16
STATE_DIM = 64
CONV_W = 3
NUM_BUCKETS = 32
MAX_EXACT = NUM_BUCKETS // 2
MAX_DISTANCE = 128
LN_EPS = 1e-5
NEG_INF = -1e30

VMEM_LIMIT_BYTES = 56 * 1024 * 1024
LANES = 128
SLAB_GROUPS = 8
N_SLABS = 8
SLAB_IN = SLAB_GROUPS * SSM_GROUP
SLAB_STATE = SLAB_GROUPS * STATE_DIM
SUBLANES = 8
MM_ROWS, MM_COLS = 1024, 1024
LN_MM_ROWS = 512
DOWN_ROWS, DOWN_COLS = 512, 512
LN_ROWS = 512
ATTN_BLOCK = 256
ATTN_KEY_BLOCK = 256
ATTN_HEADS_PER_STEP = 4
S5_CHUNK = 256
DECODE_PAGES = 16
UP_ROW_CHUNK = 256
UP_PIECE = 32
SCAN_LEVELS = 3
SCAN_UNROLL = 2


def _params(sem):
    return pltpu.CompilerParams(dimension_semantics=sem, vmem_limit_bytes=VMEM_LIMIT_BYTES)


def _mm_body(*refs, n_pairs, n_extra, n_out, epilogue):
    it = iter(refs)
    take = lambda n: [next(it) for _ in range(n)]
    x_refs, xs_refs, w_refs = take(n_pairs), take(n_pairs), take(n_pairs)
    extra, extra_s = take(n_extra), take(n_extra)
    outs, outs_s = take(n_out), take(n_out)
    wbf = take(n_pairs)

    def apply(lhs_refs, extra_refs, out_refs):
        accs = [jnp.dot(x[...].astype(BF16), s[...], preferred_element_type=F32)
                for x, s in zip(lhs_refs, wbf)]
        for o, r in zip(out_refs, epilogue(*accs, *[e[...] for e in extra_refs])):
            o[...] = r.astype(o.dtype)

    @pl.when(pl.program_id(1) == 0)
    def _():
        for w, s in zip(w_refs, wbf):
            s[...] = w[...].astype(BF16)
        apply(xs_refs, extra_s, outs_s)

    apply(x_refs, extra, outs)


def _matmul(pairs, extras, out_dtypes, epilogue, *, n_cols, tm, tn, name):
    m = pairs[0][0].shape[0]
    ms = pairs[0][1].shape[0]
    grid = (n_cols // tn, m // tm)
    w_mode = dict(pipeline_mode=pl.Buffered(1)) if grid[0] == 1 else {}
    specs_x, specs_xs, specs_w, scratch = [], [], [], []
    for x, xs, w, c0 in pairs:
        k = x.shape[1]
        specs_x.append(pl.BlockSpec((tm, k), lambda j, i: (i, 0)))
        specs_xs.append(pl.BlockSpec((ms, k), lambda j, i: (0, 0)))
        specs_w.append(pl.BlockSpec((k, tn), lambda j, i, off=c0 // tn: (0, j + off), **w_mode))
        scratch.append(pltpu.VMEM((k, tn), BF16))
    specs_e, specs_es, args_e, args_es = [], [], [], []
    for e in extras:
        if isinstance(e, tuple):
            specs_e.append(pl.BlockSpec((tm, tn), lambda j, i: (i, j)))
            specs_es.append(pl.BlockSpec((ms, tn), lambda j, i: (0, j)))
            args_e.append(e[0])
            args_es.append(e[1])
        else:
            specs_e.append(pl.BlockSpec((1, tn), lambda j, i: (0, j)))
            specs_es.append(pl.BlockSpec((1, tn), lambda j, i: (0, j)))
            args_e.append(e)
            args_es.append(e)
    out_shape = ([jax.ShapeDtypeStruct((m, n_cols), d) for d in out_dtypes]
                 + [jax.ShapeDtypeStruct((ms, n_cols), d) for d in out_dtypes])
    out_specs = ([pl.BlockSpec((tm, tn), lambda j, i: (i, j)) for _ in out_dtypes]
                 + [pl.BlockSpec((ms, tn), lambda j, i: (0, j)) for _ in out_dtypes])
    body = functools.partial(_mm_body, n_pairs=len(pairs), n_extra=len(extras),
                             n_out=len(out_dtypes), epilogue=epilogue)
    outs = pl.pallas_call(
        body, grid=grid, in_specs=specs_x + specs_xs + specs_w + specs_e + specs_es,
        out_specs=out_specs, out_shape=out_shape, scratch_shapes=scratch,
        compiler_params=_params(("arbitrary", "arbitrary")), name=name)(
            *[p[0] for p in pairs], *[p[1] for p in pairs], *[p[2] for p in pairs],
            *args_e, *args_es)
    n = len(out_dtypes)
    return outs[:n], outs[n:]


def _proj_t_body(x_ref, xs_ref, w_ref, o_ref, ob_ref, os_ref, wt_sc):
    @pl.when(pl.program_id(0) == 0)
    def _():
        for c in range(w_ref.shape[1] // LANES):
            cols = slice(c * LANES, (c + 1) * LANES)
            wt_sc[cols, :] = w_ref[:, cols].T.astype(BF16)
        os_ref[...] = lax.dot_general(xs_ref[...].astype(BF16), wt_sc[...], (((1,), (1,)), ((), ())),
                                      preferred_element_type=F32)

    kt = lax.dot_general(wt_sc[...], x_ref[...], (((1,), (1,)), ((), ())),
                         preferred_element_type=F32)
    o_ref[...] = kt
    ob_ref[...] = kt.astype(BF16)


def _proj_transposed(x, xs, w, c0, *, n_cols, batch, seq, tm, name):
    k = x.shape[1]
    ms = xs.shape[0]
    per_b = seq // tm
    out_spec = pl.BlockSpec((None, n_cols, tm), lambda i: (i // per_b, 0, i % per_b))
    return pl.pallas_call(
        _proj_t_body, grid=(batch * per_b,),
        in_specs=[pl.BlockSpec((tm, k), lambda i: (i, 0)),
                  pl.BlockSpec((ms, k), lambda i: (0, 0)),
                  pl.BlockSpec((k, n_cols), lambda i, off=c0 // n_cols: (0, off),
                               pipeline_mode=pl.Buffered(1))],
        out_specs=[out_spec, out_spec, pl.BlockSpec((ms, n_cols), lambda i: (0, 0))],
        out_shape=[jax.ShapeDtypeStruct((batch, n_cols, seq), F32),
                   jax.ShapeDtypeStruct((batch, n_cols, seq), BF16),
                   jax.ShapeDtypeStruct((ms, n_cols), F32)],
        scratch_shapes=[pltpu.VMEM((n_cols, k), BF16)],
        compiler_params=_params(("arbitrary",)), name=name)(x, xs, w)


def _ln(x, g, b):
    mu = jnp.mean(x, axis=-1, keepdims=True)
    xc = x - mu
    var = jnp.mean(xc * xc, axis=-1, keepdims=True)
    return xc * lax.rsqrt(var + LN_EPS) * g + b


def _ln_body(x_ref, g_ref, b_ref, o_ref):
    o_ref[...] = _ln(x_ref[...], g_ref[...], b_ref[...])


def _layer_norm(x, g, b, *, tm, name):
    m, d = x.shape
    tm = min(tm, m)
    return pl.pallas_call(
        _ln_body, grid=(m // tm,),
        in_specs=[pl.BlockSpec((tm, d), lambda i: (i, 0)),
                  pl.BlockSpec((1, d), lambda i: (0, 0)),
                  pl.BlockSpec((1, d), lambda i: (0, 0))],
        out_specs=pl.BlockSpec((tm, d), lambda i: (i, 0)),
        out_shape=jax.ShapeDtypeStruct((m, d), F32),
        compiler_params=_params(("arbitrary",)), name=name)(x, g.reshape(1, d), b.reshape(1, d))


def _rel_bucket(n):
    n = jnp.maximum(n, 0)
    nf = jnp.maximum(n, 1).astype(F32)
    large = MAX_EXACT + (jnp.log(nf / MAX_EXACT) / math.log(MAX_DISTANCE / MAX_EXACT)
                         * (NUM_BUCKETS - MAX_EXACT)).astype(jnp.int32)
    large = jnp.minimum(large, NUM_BUCKETS - 1)
    return jnp.where(n < MAX_EXACT, n, large)


def _bucket_lookup(bucket, table_fn):
    out = jnp.zeros(jnp.broadcast_shapes(bucket.shape, table_fn(0).shape), F32)
    for b in range(NUM_BUCKETS):
        out = out + jnp.where(bucket == b, table_fn(b), 0.0)
    return out


def _bias_tile_body(rb_ref, bucket_ref, o_ref):
    h = pl.program_id(0)
    o_ref[...] = _bucket_lookup(bucket_ref[...], lambda bk: rb_ref[bk, h])


def _bias_tiles(rel_bias, t):
    r = jnp.arange(t, dtype=jnp.int32)
    c = jnp.arange(2 * t, dtype=jnp.int32)
    buckets = _rel_bucket(r[:, None] + t - c[None, :])
    return pl.pallas_call(
        _bias_tile_body, grid=(N_HEADS,),
        in_specs=[pl.BlockSpec(memory_space=pltpu.SMEM), pl.BlockSpec((t, 2 * t), lambda h: (0, 0))],
        out_specs=pl.BlockSpec((None, t, 2 * t), lambda h: (h, 0, 0)),
        out_shape=jax.ShapeDtypeStruct((N_HEADS, t, 2 * t), F32),
        compiler_params=_params(("arbitrary",)), name="bias_tiles")(rel_bias, buckets)


def _attn_body(lam_ref, rb_ref, q_ref, k_ref, v_ref, bias_ref, g_ref, o_ref, *scratch,
               t, n_far, near, heads, out_scale):
    s_bufs, p_bufs, m_bufs, mf_bufs, l_bufs, a_bufs = (scratch[0:2], scratch[2:4], scratch[4:6],
                                                       scratch[6:8], scratch[8:10], scratch[10:12])
    hg = pl.program_id(0)
    far = n_far * t
    kb_w = ATTN_KEY_BLOCK
    n_kb = (far + near) // kb_w
    n_tiles = kb_w // LANES

    def fold(x, op):
        out = x[:, 0:LANES]
        for c in range(1, n_tiles):
            out = op(out, x[:, c * LANES:(c + 1) * LANES])
        return out

    def head_cols(hh):
        return slice(hh * V_DIM, (hh + 1) * V_DIM)

    def score_block(hh, kb):
        par = hh % 2
        q = q_ref[:, head_cols(hh)] * (HEAD_DIM ** -0.5)
        lane = lax.broadcasted_iota(jnp.int32, q.shape, 1)
        zero = jnp.zeros_like(q)
        q2 = jnp.concatenate([jnp.where(lane < HEAD_DIM, q, zero), jnp.where(lane >= HEAD_DIM, q, zero)],
                             axis=0)
        cols = slice(kb * kb_w, (kb + 1) * kb_w)
        s = jnp.dot(q2, k_ref[head_cols(hh), cols], preferred_element_type=F32)
        bias_far = rb_ref[NUM_BUCKETS - 1, hg * heads + hh]
        if kb * kb_w < far:
            top = fold(s, jnp.maximum) + bias_far
        else:
            off = kb * kb_w - far
            lo = 2 * t - near + off
            bias = bias_ref[hh, :, lo:lo + kb_w]
            row = lax.broadcasted_iota(jnp.int32, (t, kb_w), 0)
            col = lax.broadcasted_iota(jnp.int32, (t, kb_w), 1)
            keep = col + (off - (near - t)) <= row
            s = jnp.where(jnp.concatenate([keep, keep], axis=0),
                          s + jnp.concatenate([bias, bias], axis=0), NEG_INF)
            top = fold(s, jnp.maximum)
        s_bufs[par][:, cols] = s
        if kb == 0:
            m_bufs[par][...] = top
        else:
            m_bufs[par][...] = jnp.maximum(m_bufs[par][...], top)
        if kb == n_kb - 1:
            m = jnp.broadcast_to(jnp.max(m_bufs[par][...], axis=1, keepdims=True), m_bufs[par].shape)
            m_bufs[par][...] = m
            mf_bufs[par][...] = m - bias_far

    def exp_block(hh, kb):
        par = hh % 2
        cols = slice(kb * kb_w, (kb + 1) * kb_w)
        m = (mf_bufs if kb * kb_w < far else m_bufs)[par][...]
        p = jnp.exp(s_bufs[par][:, cols] - jnp.concatenate([m] * n_tiles, axis=1))
        if kb == 0:
            l_bufs[par][...] = fold(p, jnp.add)
        else:
            l_bufs[par][...] += fold(p, jnp.add)
        p_bufs[par][:, cols] = p.astype(BF16)

    def value_block(hh, kb):
        par = hh % 2
        rows = slice(kb * kb_w, (kb + 1) * kb_w)
        pv = jnp.dot(p_bufs[par][:, rows], v_ref[rows, head_cols(hh)], preferred_element_type=F32)
        if kb == 0:
            a_bufs[par][...] = pv
        else:
            a_bufs[par][...] += pv
        if kb == n_kb - 1:
            nrm = a_bufs[par][...] / jnp.sum(l_bufs[par][...], axis=1, keepdims=True)
            o = nrm[0:t] - lam_ref[0] * nrm[t:2 * t]
            ms = jnp.mean(o * o, axis=-1, keepdims=True)
            o_ref[:, head_cols(hh)] = (o * lax.rsqrt(ms + LN_EPS) * g_ref[...] * out_scale
                                       ).astype(o_ref.dtype)

    for stage in range(heads + 2):
        for kb in range(n_kb):
            if stage < heads:
                score_block(stage, kb)
            if 0 <= stage - 1 < heads:
                exp_block(stage - 1, kb)
            if 0 <= stage - 2 < heads:
                value_block(stage - 2, kb)


def _prompt_attention(q, k, v, rel_bias, lam, subln_g, *, batch, seq, t, out_scale):
    assert t >= MAX_DISTANCE
    nq = seq // t
    width = N_HEADS * V_DIM
    q3 = q.reshape(batch, seq, width)
    v3 = v.reshape(batch, seq, width)
    bias = _bias_tiles(rel_bias, t)
    smem = pl.BlockSpec(memory_space=pltpu.SMEM)
    pieces = []
    for i in range(nq):
        n_far = max(i - 1, 0)
        near = min(i + 1, 2) * t
        keys = n_far * t + near
        hp = ATTN_HEADS_PER_STEP
        body = functools.partial(_attn_body, t=t, n_far=n_far, near=near, heads=hp, out_scale=out_scale)
        pair = lambda shape, dtype: [pltpu.VMEM(shape, dtype)] * 2
        pieces.append(pl.pallas_call(
            body, grid=(N_HEADS // hp, batch),
            in_specs=[smem, smem,
                      pl.BlockSpec((None, t, hp * V_DIM), lambda h, b, i=i: (b, i, h)),
                      pl.BlockSpec((None, hp * V_DIM, keys), lambda h, b: (b, h, 0)),
                      pl.BlockSpec((None, keys, hp * V_DIM), lambda h, b: (b, 0, h)),
                      pl.BlockSpec((hp, t, 2 * t), lambda h, b: (h, 0, 0)),
                      pl.BlockSpec((1, V_DIM), lambda h, b: (0, 0))],
            out_specs=pl.BlockSpec((None, t, hp * V_DIM), lambda h, b: (b, 0, h)),
            out_shape=jax.ShapeDtypeStruct((batch, t, width), BF16),
            scratch_shapes=(pair((2 * t, keys), F32) + pair((2 * t, keys), BF16)
                            + pair((2 * t, LANES), F32) + pair((2 * t, LANES), F32)
                            + pair((2 * t, LANES), F32) + pair((2 * t, V_DIM), F32)),
            compiler_params=_params(("arbitrary", "arbitrary")),
            name=f"prompt_attention_q{i}")(lam, rel_bias, q3, k, v3, bias, subln_g.reshape(1, V_DIM)))
    return jnp.stack(pieces, axis=1).reshape(batch * seq, width)


def _decode_body(pt_ref, lam_ref, q_ref, kn_ref, vn_ref, rbt_ref, bucket_ref, g_ref, *rest,
                 pages, page, n_steps, out_scale):
    k_refs = rest[:pages]
    v_refs = rest[pages:2 * pages]
    o_ref = rest[2 * pages]
    qexp_sc, bias_sc, m_sc, l_sc, acc_sc = rest[2 * pages + 1:]
    step = pl.program_id(1)
    rows = 2 * N_HEADS
    width = N_HEADS * V_DIM

    @pl.when(step == 0)
    def _():
        row = lax.broadcasted_iota(jnp.int32, (rows, width), 0)
        col = lax.broadcasted_iota(jnp.int32, (rows, width), 1)
        own_qk = (col // HEAD_DIM) == (row % N_HEADS) * 2 + row // N_HEADS
        q = (q_ref[...] * (HEAD_DIM ** -0.5)).astype(BF16).astype(F32)
        qexp = jnp.where(own_qk, jnp.broadcast_to(q, (rows, width)), 0.0)
        qexp_sc[...] = qexp.astype(BF16)
        bias_sc[...] = _bucket_lookup(bucket_ref[...], lambda bk: rbt_ref[:, bk:bk + 1])
        kn = kn_ref[...].astype(BF16).astype(F32)
        s_self = jnp.sum(qexp * kn, axis=1, keepdims=True) + rbt_ref[:, 0:1]
        m_sc[...] = s_self
        l_sc[...] = jnp.ones(l_sc.shape, F32)
        vn = vn_ref[...].astype(BF16).astype(F32)
        acc_sc[...] = jnp.broadcast_to(vn[:, None, :], acc_sc.shape)

    qexp = qexp_sc[...]
    bias_far = rbt_ref[:, NUM_BUCKETS - 1:NUM_BUCKETS]
    is_last = step == n_steps - 1
    s_parts = []
    for p in range(pages):
        s = jnp.dot(qexp, k_refs[p][...].astype(BF16), preferred_element_type=F32)
        if p == pages - 1:
            s = s + jnp.where(is_last, bias_sc[...], bias_far)
        else:
            s = s + bias_far
        s_parts.append(s)
    m_old = m_sc[...]
    m_new = m_old
    for s in s_parts:
        m_new = jnp.maximum(m_new, jnp.max(s, axis=1, keepdims=True))
    a = jnp.exp(m_old - m_new)
    l_new = a * l_sc[...]
    probs = []
    for p in range(pages):
        pr = jnp.exp(s_parts[p] - m_new)
        l_new = l_new + jnp.sum(pr, axis=1, keepdims=True)
        probs.append(pr.astype(BF16))
    probs = jnp.concatenate(probs, axis=1)
    m_sc[...] = m_new
    l_sc[...] = l_new
    for h in range(N_HEADS):
        vh = jnp.concatenate([v_refs[p][pl.ds(h, page, stride=N_HEADS), :].astype(BF16)
                              for p in range(pages)], axis=0)
        acc_sc[h] = a * acc_sc[h] + jnp.dot(probs, vh, preferred_element_type=F32)

    @pl.when(is_last)
    def _():
        nrm = acc_sc[...] / l_new[None]
        r = lax.broadcasted_iota(jnp.int32, nrm.shape, 1)
        hh = lax.broadcasted_iota(jnp.int32, nrm.shape, 0)
        coef = jnp.where(r == hh, 1.0, jnp.where(r == hh + N_HEADS, -lam_ref[0], 0.0))
        d = jnp.sum(coef * nrm, axis=1)
        ms = jnp.mean(d * d, axis=-1, keepdims=True)
        o_ref[...] = (d * lax.rsqrt(ms + LN_EPS) * g_ref[...] * out_scale).astype(o_ref.dtype)


def _sample_attention(q, k_new, v_new, cache_k, cache_v, page_table, rel_bias, lam, subln_g,
                      *, pages, out_scale):
    bs, width = q.shape
    page = cache_k.shape[2]
    n_pages = page_table.shape[1]
    n_steps = n_pages // pages
    past = n_pages * page
    kpos = past - page + jnp.arange(page, dtype=jnp.int32)
    bucket_last = _rel_bucket(past - kpos).reshape(1, page)
    rbt = jnp.tile(rel_bias.T, (2, 1))

    def tok_spec():
        return pl.BlockSpec((None, 1, width), lambda b, s, pt: (b, 0, 0))

    def page_spec(shape, p):
        return pl.BlockSpec((None,) + shape, lambda b, s, pt, p=p: (pt[b, s * pages + p], 0, 0))

    full = lambda shape: pl.BlockSpec(shape, lambda b, s, pt: tuple(0 for _ in shape))
    head_spec = pl.BlockSpec((None, N_HEADS, V_DIM), lambda b, s, pt: (b, 0, 0))
    in_specs = ([pl.BlockSpec(memory_space=pltpu.SMEM), tok_spec(), tok_spec(), head_spec,
                 full((2 * N_HEADS, NUM_BUCKETS)), full((1, page)), full((1, V_DIM))]
                + [page_spec((width, page), p) for p in range(pages)]
                + [page_spec((page * N_HEADS, V_DIM), p) for p in range(pages)])
    rows = 2 * N_HEADS
    body = functools.partial(_decode_body, pages=pages, page=page, n_steps=n_steps,
                             out_scale=out_scale)
    out = pl.pallas_call(
        body,
        grid_spec=pltpu.PrefetchScalarGridSpec(
            num_scalar_prefetch=1, grid=(bs, n_steps), in_specs=in_specs,
            out_specs=head_spec,
            scratch_shapes=[pltpu.VMEM((rows, width), BF16), pltpu.VMEM((rows, page), F32),
                            pltpu.VMEM((rows, 1), F32), pltpu.VMEM((rows, 1), F32),
                            pltpu.VMEM((N_HEADS, rows, V_DIM), F32)]),
        out_shape=jax.ShapeDtypeStruct((bs, N_HEADS, V_DIM), BF16),
        compiler_params=_params(("arbitrary", "arbitrary")),
        name="sample_attention")(
            page_table, lam, q.reshape(bs, 1, width), k_new.reshape(bs, 1, width),
            v_new.reshape(bs, N_HEADS, V_DIM), rbt, bucket_last, subln_g.reshape(1, V_DIM),
            *([cache_k] * pages), *([cache_v] * pages))
    return out.reshape(bs, width)


def _s5_params(a_re, a_im, log_dt, b_re, b_im, c_re, c_im):
    dt = jnp.exp(log_dt)[:, None]
    mag = jnp.exp(a_re * dt)
    ang = a_im * dt
    abar_re = mag * jnp.cos(ang)
    abar_im = mag * jnp.sin(ang)
    den = a_re * a_re + a_im * a_im
    f_re = ((abar_re - 1.0) * a_re + abar_im * a_im) / den
    f_im = (abar_im * a_re - (abar_re - 1.0) * a_im) / den
    bb_re = f_re[..., None] * b_re - f_im[..., None] * b_im
    bb_im = f_re[..., None] * b_im + f_im[..., None] * b_re
    eye = jnp.eye(SLAB_GROUPS, dtype=F32)

    def in_slabs(bb):
        tt = bb.reshape(N_SLABS, SLAB_GROUPS, STATE_DIM, SSM_GROUP).transpose(0, 1, 3, 2)
        full = tt[:, :, :, None, :] * eye[None, :, None, :, None]
        return full.reshape(N_SLABS, SLAB_IN, SLAB_STATE).astype(BF16)

    def out_slabs(cc):
        tt = cc.reshape(N_SLABS, SLAB_GROUPS, SSM_GROUP, STATE_DIM).transpose(0, 1, 3, 2)
        full = tt[:, :, :, None, :] * eye[None, :, None, :, None]
        return full.reshape(N_SLABS, SLAB_STATE, SLAB_IN).astype(BF16)

    return (abar_re.reshape(N_SLABS, SLAB_STATE), abar_im.reshape(N_SLABS, SLAB_STATE),
            in_slabs(bb_re), in_slabs(bb_im), out_slabs(c_re), out_slabs(c_im))


def _glu_out(y, u, d_ref, wglu_ref, o_ref):
    g = jax.nn.gelu(y + d_ref[...] * u)
    gate = jnp.dot(g.astype(BF16), wglu_ref[...].astype(BF16), preferred_element_type=F32)
    o_ref[...] = (g * jax.nn.sigmoid(gate)).astype(o_ref.dtype)


def _s5_prompt_body(u_ref, ar_ref, ai_ref, bbr_ref, bbi_ref, cr_ref, ci_ref, d_ref, wglu_ref,
                    o_ref, sre_ref, sim_ref, xr_sc, xi_sc, pw_sc, cr_sc, ci_sc, *, tc):
    b = pl.program_id(0)
    c = pl.program_id(1)
    sub = SUBLANES

    def cmul(pr, pi, qr, qi):
        return pr * qr - pi * qi, pr * qi + pi * qr

    @pl.when(jnp.logical_and(b == 0, c == 0))
    def _():
        t = lax.broadcasted_iota(jnp.int32, (sub, SLAB_STATE), 0)
        for j in range(N_SLABS):
            a = (jnp.broadcast_to(ar_ref[j:j + 1, :], t.shape), jnp.broadcast_to(ai_ref[j:j + 1, :], t.shape))
            powers = [a]
            for _ in range(sub - 1):
                powers.append(cmul(*powers[-1], *a))
            for lvl in range(SCAN_LEVELS):
                k = 1 << lvl
                for part in range(2):
                    pw_sc[j, 2 * lvl + part] = jnp.where(t >= k, powers[k - 1][part], 0.0)
            for part in range(2):
                acc = powers[sub - 1][part]
                for row in range(sub - 1):
                    acc = jnp.where(t == row, powers[row][part], acc)
                pw_sc[j, 2 * SCAN_LEVELS + part] = acc

    @pl.when(c == 0)
    def _():
        cr_sc[...] = jnp.zeros(cr_sc.shape, F32)
        ci_sc[...] = jnp.zeros(ci_sc.shape, F32)

    u = u_ref[...]
    ub = u.astype(BF16)
    for j in range(N_SLABS):
        uj = ub[:, j * SLAB_IN:(j + 1) * SLAB_IN]
        xr_sc[j] = jnp.dot(uj, bbr_ref[j], preferred_element_type=F32)
        xi_sc[j] = jnp.dot(uj, bbi_ref[j], preferred_element_type=F32)

    for j in range(N_SLABS):
        cr = jnp.broadcast_to(cr_sc[j:j + 1, :], (sub, SLAB_STATE))
        ci = jnp.broadcast_to(ci_sc[j:j + 1, :], (sub, SLAB_STATE))
        for v in range(tc // sub):
            rows = slice(v * sub, (v + 1) * sub)
            xr = xr_sc[j, rows, :]
            xi = xi_sc[j, rows, :]
            for lvl in range(SCAN_LEVELS):
                k = 1 << lvl
                dr, di = cmul(pw_sc[j, 2 * lvl], pw_sc[j, 2 * lvl + 1],
                              pltpu.roll(xr, k, 0), pltpu.roll(xi, k, 0))
                xr, xi = xr + dr, xi + di
            dr, di = cmul(pw_sc[j, 2 * SCAN_LEVELS], pw_sc[j, 2 * SCAN_LEVELS + 1], cr, ci)
            xr, xi = xr + dr, xi + di
            xr_sc[j, rows, :] = xr
            xi_sc[j, rows, :] = xi
            cr = jnp.broadcast_to(xr[sub - 1:sub, :], xr.shape)
            ci = jnp.broadcast_to(xi[sub - 1:sub, :], xi.shape)
        cr_sc[j:j + 1, :] = cr[0:1, :]
        ci_sc[j:j + 1, :] = ci[0:1, :]
    sre_ref[...] = cr_sc[...]
    sim_ref[...] = ci_sc[...]

    ys = []
    for j in range(N_SLABS):
        ys.append(jnp.dot(xr_sc[j].astype(BF16), cr_ref[j], preferred_element_type=F32)
                  - jnp.dot(xi_sc[j].astype(BF16), ci_ref[j], preferred_element_type=F32))
    _glu_out(jnp.concatenate(ys, axis=1), u, d_ref, wglu_ref, o_ref)


def _s5_prompt(u, sp, d, w_glu, *, batch, seq, tc):
    ar, ai, bbr, bbi, cr, ci = sp
    width = u.shape[1]
    nc = seq // tc
    const = lambda shape: pl.BlockSpec(shape, lambda b, c: tuple(0 for _ in shape))
    state_spec = pl.BlockSpec((None, N_SLABS, SLAB_STATE), lambda b, c: (b, 0, 0))
    state_shape = jax.ShapeDtypeStruct((batch, N_SLABS, SLAB_STATE), F32)
    return pl.pallas_call(
        functools.partial(_s5_prompt_body, tc=tc), grid=(batch, nc),
        in_specs=[pl.BlockSpec((tc, width), lambda b, c: (b * nc + c, 0)),
                  const(ar.shape), const(ai.shape), const(bbr.shape), const(bbi.shape),
                  const(cr.shape), const(ci.shape), const((1, width)), const(w_glu.shape)],
        out_specs=[pl.BlockSpec((tc, width), lambda b, c: (b * nc + c, 0)), state_spec, state_spec],
        out_shape=[jax.ShapeDtypeStruct((batch * seq, width), BF16), state_shape, state_shape],
        scratch_shapes=[pltpu.VMEM((N_SLABS, tc, SLAB_STATE), F32),
                        pltpu.VMEM((N_SLABS, tc, SLAB_STATE), F32),
                        pltpu.VMEM((N_SLABS, 2 * SCAN_LEVELS + 2, SUBLANES, SLAB_STATE), F32),
                        pltpu.VMEM((N_SLABS, SLAB_STATE), F32),
                        pltpu.VMEM((N_SLABS, SLAB_STATE), F32)],
        compiler_params=_params(("arbitrary", "arbitrary")),
        name="s5_prompt")(u, ar, ai, bbr, bbi, cr, ci, d.reshape(1, width), w_glu)


def _s5_sample_body(u_ref, x0r_ref, x0i_ref, ar_ref, ai_ref, bbr_ref, bbi_ref, cr_ref, ci_ref,
                    d_ref, wglu_ref, o_ref, sre_ref, sim_ref):
    u = u_ref[...]
    ub = u.astype(BF16)
    ys = []
    for j in range(N_SLABS):
        uj = ub[:, j * SLAB_IN:(j + 1) * SLAB_IN]
        ar = ar_ref[j:j + 1, :]
        ai = ai_ref[j:j + 1, :]
        x0r = x0r_ref[j]
        x0i = x0i_ref[j]
        xr = ar * x0r - ai * x0i + jnp.dot(uj, bbr_ref[j], preferred_element_type=F32)
        xi = ar * x0i + ai * x0r + jnp.dot(uj, bbi_ref[j], preferred_element_type=F32)
        sre_ref[j] = xr
        sim_ref[j] = xi
        ys.append(jnp.dot(xr.astype(BF16), cr_ref[j], preferred_element_type=F32)
                  - jnp.dot(xi.astype(BF16), ci_ref[j], preferred_element_type=F32))
    _glu_out(jnp.concatenate(ys, axis=1), u, d_ref, wglu_ref, o_ref)


def _s5_sample(u, x0_re, x0_im, sp, d, w_glu):
    ar, ai, bbr, bbi, cr, ci = sp
    bs, width = u.shape
    state_shape = jax.ShapeDtypeStruct((N_SLABS, bs, SLAB_STATE), F32)
    return pl.pallas_call(
        _s5_sample_body,
        out_shape=[jax.ShapeDtypeStruct((bs, width), BF16), state_shape, state_shape],
        compiler_params=pltpu.CompilerParams(vmem_limit_bytes=VMEM_LIMIT_BYTES),
        name="s5_sample")(u, x0_re, x0_im, ar, ai, bbr, bbi, cr, ci, d.reshape(1, width), w_glu)


def _gated(c):
    gate = c[:, :LANES]
    return gate * jax.nn.sigmoid(gate) * c[:, LANES:]


def _up_prompt_body(h_ref, wg_ref, wv_ref, cwg_ref, cwv_ref, cbg_ref, cbv_ref,
                    act_ref, cg_ref, cv_ref, w_sc, up_sc, *, rows):
    w_sc[:, 0:LANES] = wg_ref[...].astype(BF16)
    w_sc[:, LANES:2 * LANES] = wv_ref[...].astype(BF16)
    cw = jnp.concatenate([cwg_ref[...], cwv_ref[...]], axis=1)
    cb = jnp.concatenate([cbg_ref[...], cbv_ref[...]], axis=1)
    seq = h_ref.shape[0]
    sub = SUBLANES
    n_chunks = seq // rows
    row8 = lax.broadcasted_iota(jnp.int32, (sub, 2 * LANES), 0)

    prev = jnp.zeros((sub, 2 * LANES), F32)
    for c in range(n_chunks):
        up_sc[...] = jnp.dot(h_ref[c * rows:(c + 1) * rows, :], w_sc[...], preferred_element_type=F32)
        for r in range(0, rows, UP_PIECE):
            up = up_sc[r:r + UP_PIECE, :]
            cur = cb + cw[CONV_W - 1:CONV_W] * up
            for back in range(1, CONV_W):
                rolled = pltpu.roll(up, back, 0)
                head = jnp.where(row8 < back, pltpu.roll(prev, back, 0), rolled[0:sub])
                shifted = jnp.concatenate([head, rolled[sub:]], axis=0)
                cur = cur + cw[CONV_W - 1 - back:CONV_W - back] * shifted
            act_ref[c * rows + r:c * rows + r + UP_PIECE, :] = _gated(cur).astype(act_ref.dtype)
            prev = up[UP_PIECE - sub:, :]
    tail = prev[sub - (CONV_W - 1):, :]
    cg_ref[...] = tail[:, :LANES]
    cv_ref[...] = tail[:, LANES:]


def _up_prompt(hb, w_up, conv_w, conv_b, *, batch, seq):
    d_model = hb.shape[1]
    d_ff = w_up.shape[1] // 2
    nb = d_ff // LANES
    cb = conv_b.reshape(1, 2 * d_ff)
    col = lambda off: (lambda b, j: (0, j + off))
    tail_spec = pl.BlockSpec((None, CONV_W - 1, LANES), lambda b, j: (b, 0, j))
    tail_shape = jax.ShapeDtypeStruct((batch, CONV_W - 1, d_ff), F32)
    return pl.pallas_call(
        functools.partial(_up_prompt_body, rows=UP_ROW_CHUNK), grid=(batch, nb),
        scratch_shapes=[pltpu.VMEM((d_model, 2 * LANES), BF16),
                        pltpu.VMEM((UP_ROW_CHUNK, 2 * LANES), F32)],
        in_specs=[pl.BlockSpec((seq, d_model), lambda b, j: (b, 0)),
                  pl.BlockSpec((d_model, LANES), col(0)), pl.BlockSpec((d_model, LANES), col(nb)),
                  pl.BlockSpec((CONV_W, LANES), col(0)), pl.BlockSpec((CONV_W, LANES), col(nb)),
                  pl.BlockSpec((1, LANES), col(0)), pl.BlockSpec((1, LANES), col(nb))],
        out_specs=[pl.BlockSpec((seq, LANES), lambda b, j: (b, j)), tail_spec, tail_spec],
        out_shape=[jax.ShapeDtypeStruct((batch * seq, d_ff), BF16), tail_shape, tail_shape],
        compiler_params=_params(("arbitrary", "arbitrary")),
        name="up_prompt")(hb, w_up, w_up, conv_w, conv_w, cb, cb)


def _up_sample_body(h_ref, wg_ref, wv_ref, cwg_ref, cwv_ref, cbg_ref, cbv_ref, sg_ref, sv_ref,
                    act_ref, cg_ref, cv_ref):
    w = jnp.concatenate([wg_ref[...].astype(BF16), wv_ref[...].astype(BF16)], axis=1)
    up = jnp.dot(h_ref[...], w, preferred_element_type=F32)
    cw = jnp.concatenate([cwg_ref[...], cwv_ref[...]], axis=1)
    cb = jnp.concatenate([cbg_ref[...], cbv_ref[...]], axis=1)
    c = cb + cw[CONV_W - 1:CONV_W] * up
    for tap in range(CONV_W - 1):
        st = jnp.concatenate([sg_ref[tap], sv_ref[tap]], axis=1)
        c = c + cw[tap:tap + 1] * st
    act_ref[...] = _gated(c).astype(act_ref.dtype)
    for tap in range(1, CONV_W - 1):
        cg_ref[tap - 1] = sg_ref[tap]
        cv_ref[tap - 1] = sv_ref[tap]
    cg_ref[CONV_W - 2] = up[:, :LANES]
    cv_ref[CONV_W - 2] = up[:, LANES:]


def _up_sample(hb, w_up, conv_w, conv_b, state):
    bs, d_model = hb.shape
    d_ff = w_up.shape[1] // 2
    nb = d_ff // LANES
    cb = conv_b.reshape(1, 2 * d_ff)
    col = lambda off: (lambda j: (0, j + off))
    st = lambda off: pl.BlockSpec((CONV_W - 1, bs, LANES), lambda j, off=off: (0, 0, j + off))
    tail_spec = pl.BlockSpec((CONV_W - 1, bs, LANES), lambda j: (0, 0, j))
    tail_shape = jax.ShapeDtypeStruct((CONV_W - 1, bs, d_ff), F32)
    return pl.pallas_call(
        _up_sample_body, grid=(nb,),
        in_specs=[pl.BlockSpec((bs, d_model), lambda j: (0, 0)),
                  pl.BlockSpec((d_model, LANES), col(0)), pl.BlockSpec((d_model, LANES), col(nb)),
                  pl.BlockSpec((CONV_W, LANES), col(0)), pl.BlockSpec((CONV_W, LANES), col(nb)),
                  pl.BlockSpec((1, LANES), col(0)), pl.BlockSpec((1, LANES), col(nb)),
                  st(0), st(nb)],
        out_specs=[pl.BlockSpec((bs, LANES), lambda j: (0, j)), tail_spec, tail_spec],
        out_shape=[jax.ShapeDtypeStruct((bs, d_ff), BF16), tail_shape, tail_shape],
        compiler_params=_params(("arbitrary",)),
        name="up_sample")(hb, w_up, w_up, conv_w, conv_w, cb, cb, state, state)


def _one(x):
    return (x,)


def _sigmoid_out(acc):
    return (jax.nn.sigmoid(acc),)


def _both(acc):
    return (acc, acc)


def _merge(pa, ps, ga, gs):
    return (ga * pa + gs * ps,)


def _layer(xp, xs, w, *, batch, seq, alpha, attend_p, attend_s, ssm_p, ssm_s, up_p, up_s):
    d_model = xp.shape[1]
    qk_w = N_HEADS * 2 * HEAD_DIM
    v_w = N_HEADS * V_DIM
    ssm_w = d_model // 2
    xb = xp.astype(BF16)
    w_in = w["w_in"]
    g1, b1 = w["ln1_g"].reshape(1, d_model), w["ln1_b"].reshape(1, d_model)
    big = dict(tm=MM_ROWS, tn=MM_COLS)
    c = 0
    (q,), (q_s,) = _matmul([(xb, xs, w_in, c)], [], [BF16], _one, n_cols=qk_w, name="proj_q", **big)
    c += qk_w
    kt, ktb, k_s = _proj_transposed(xb, xs, w_in, c, n_cols=qk_w, batch=batch, seq=seq, tm=MM_ROWS,
                                    name="proj_kt")
    c += qk_w
    (v, vb), (v_s, _) = _matmul([(xb, xs, w_in, c)], [], [F32, BF16], _both, n_cols=v_w, name="proj_v", **big)
    c += v_w
    (u,), (u_s,) = _matmul([(xb, xs, w_in, c)], [], [F32], _one, n_cols=ssm_w, name="proj_u", **big)
    c += ssm_w
    (ga,), (ga_s,) = _matmul([(xb, xs, w_in, c)], [], [F32], _sigmoid_out, n_cols=d_model, name="gate_a", **big)
    c += d_model
    (gs,), (gs_s,) = _matmul([(xb, xs, w_in, c)], [], [F32], _sigmoid_out, n_cols=d_model, name="gate_s", **big)

    attn = attend_p(q, ktb, vb)
    attn_s = attend_s(q_s, k_s, v_s)
    ssm_out, re_p, im_p = ssm_p(u)
    ssm_out_s, re_s, im_s = ssm_s(u_s)

    (merged,), (merged_s,) = _matmul(
        [(attn, attn_s, w["w_proj_attn"], 0), (ssm_out, ssm_out_s, w["w_proj_ssm"], 0)],
        [(ga, ga_s), (gs, gs_s)], [BF16], _merge, n_cols=d_model, name="merge", **big)

    def post_ln1(acc, res, g, b):
        h = _ln(alpha * res + acc, g, b)
        return h, h

    (h, hb), (h_s, hb_s) = _matmul([(merged, merged_s, w["w_out"], 0)], [(xp, xs), g1, b1], [F32, BF16],
                                   post_ln1, n_cols=d_model, tm=LN_MM_ROWS, tn=d_model, name="out_proj_ln1")
    act, conv_p = up_p(hb)
    act_s, conv_s = up_s(hb_s)
    (r2,), (r2_s,) = _matmul([(act, act_s, w["w_down"], 0)], [(h, h_s)], [F32],
                             lambda acc, res: (alpha * res + acc,), n_cols=d_model,
                             tm=DOWN_ROWS, tn=DOWN_COLS, name="down_proj")
    y = _layer_norm(r2, w["ln2_g"], w["ln2_b"], tm=LN_ROWS, name="ln2")
    y_s = _layer_norm(r2_s, w["ln2_g"], w["ln2_b"], tm=LN_ROWS, name="ln2_sample")
    return (y, kt, v, re_p, im_p, conv_p), (y_s, k_s, v_s, re_s, im_s, conv_s)


def kernel(x_prompt, x_sample, cache_k, cache_v, state_ssm_re, state_ssm_im, state_conv, page_table, rel_bias, w_in, lambda_q1, lambda_k1, lambda_q2, lambda_k2, subln_g, ssm_a_re, ssm_a_im, ssm_log_dt, ssm_b_re, ssm_b_im, ssm_c_re, ssm_c_im, ssm_d, w_glu, w_proj_attn, w_proj_ssm, w_out, ln1_g, ln1_b, w_up, conv_w, conv_b, w_down, ln2_g, ln2_b):
    depth = w_in.shape[0]
    assert depth == 1, "single-layer trunk"
    bp, seq, d_model = x_prompt.shape
    bs, dec_seq, _ = x_sample.shape
    assert dec_seq == 1
    n_pool, page = cache_k.shape[1], cache_k.shape[2]
    d_ff = w_down.shape[1]
    n_groups = ssm_a_re.shape[1]
    assert n_groups == N_SLABS * SLAB_GROUPS and d_ff % LANES == 0
    alpha = (2.0 * depth) ** 0.25
    width = N_HEADS * V_DIM

    hp = x_prompt.reshape(bp * seq, d_model)
    hs = x_sample.reshape(bs, d_model)
    outs = {}
    for l in range(depth):
        lam_init = 0.8 - 0.6 * math.exp(-0.3 * l)
        out_scale = 1.0 - lam_init
        lam = (jnp.exp(jnp.sum(lambda_q1[l] * lambda_k1[l]))
               - jnp.exp(jnp.sum(lambda_q2[l] * lambda_k2[l])) + lam_init).reshape(1)
        w = dict(w_in=w_in[l], w_proj_attn=w_proj_attn[l], w_proj_ssm=w_proj_ssm[l], w_out=w_out[l],
                 ln1_g=ln1_g[l], ln1_b=ln1_b[l], w_down=w_down[l], ln2_g=ln2_g[l], ln2_b=ln2_b[l])
        sp = _s5_params(ssm_a_re[l], ssm_a_im[l], ssm_log_dt[l], ssm_b_re[l], ssm_b_im[l],
                        ssm_c_re[l], ssm_c_im[l])

        def attend_p(q, kt, v):
            return _prompt_attention(q, kt, v, rel_bias, lam, subln_g[l], batch=bp, seq=seq,
                                     t=ATTN_BLOCK, out_scale=out_scale)

        def ssm_p(u):
            return _s5_prompt(u, sp, ssm_d[l], w_glu[l], batch=bp, seq=seq, tc=S5_CHUNK)

        def up_p(hb):
            act, cg, cv = _up_prompt(hb, w_up[l], conv_w[l], conv_b[l], batch=bp, seq=seq)
            return act, jnp.concatenate([cg, cv], axis=-1)

        ck = cache_k[l].transpose(0, 2, 3, 4, 1).reshape(n_pool, width, page)
        cv_ = cache_v[l].reshape(n_pool, page * N_HEADS, V_DIM)

        def attend_s(q, k, v):
            return _sample_attention(q.astype(F32), k, v, ck, cv_, page_table, rel_bias, lam,
                                     subln_g[l], pages=DECODE_PAGES, out_scale=out_scale)

        def ssm_s(u):
            x0r = state_ssm_re[l].reshape(bs, N_SLABS, SLAB_STATE).transpose(1, 0, 2)
            x0i = state_ssm_im[l].reshape(bs, N_SLABS, SLAB_STATE).transpose(1, 0, 2)
            o, sr, si = _s5_sample(u, x0r, x0i, sp, ssm_d[l], w_glu[l])
            return o, sr.transpose(1, 0, 2), si.transpose(1, 0, 2)

        def up_s(hb):
            act, cg, cv = _up_sample(hb, w_up[l], conv_w[l], conv_b[l],
                                     state_conv[l].transpose(1, 0, 2))
            return act, jnp.concatenate([cg, cv], axis=-1).transpose(1, 0, 2)

        (hp, kt_p, v_p, re_p, im_p, c_p), (hs, k_s, v_s, re_s, im_s, c_s) = _layer(
            hp, hs, w, batch=bp, seq=seq, alpha=alpha, attend_p=attend_p, attend_s=attend_s,
            ssm_p=ssm_p, ssm_s=ssm_s, up_p=up_p, up_s=up_s)
        k_p = kt_p.reshape(bp, N_HEADS, 2, HEAD_DIM, seq).transpose(0, 4, 1, 2, 3)

        for name, val in (("kp", k_p.reshape(bp, seq, N_HEADS, 2, HEAD_DIM)),
                          ("vp", v_p.reshape(bp, seq, N_HEADS, V_DIM)),
                          ("rep", re_p.reshape(bp, n_groups, STATE_DIM)),
                          ("imp", im_p.reshape(bp, n_groups, STATE_DIM)),
                          ("cp", c_p),
                          ("ks", k_s.reshape(bs, 1, N_HEADS, 2, HEAD_DIM)),
                          ("vs", v_s.reshape(bs, 1, N_HEADS, V_DIM)),
                          ("res", re_s.reshape(bs, n_groups, STATE_DIM)),
                          ("ims", im_s.reshape(bs, n_groups, STATE_DIM)),
                          ("cs", c_s)):
            outs.setdefault(name, []).append(val)

    st = {k: jnp.stack(v, axis=0) for k, v in outs.items()}
    return (hp.reshape(bp, seq, d_model), hs.reshape(bs, 1, d_model), st["kp"], st["vp"], st["rep"],
            st["imp"], st["cp"], st["ks"], st["vs"], st["res"], st["ims"], st["cs"])
```

```python
import functools
import math

import jax
import jax.numpy as jnp
from jax import lax
from jax.experimental import pallas as pl
from jax.experimental.pallas import tpu as pltpu

F32 = jnp.float32
BF16 = jnp.bfloat16

N_HEADS = 8
HEAD_DIM = 64
V_DIM = 2 * HEAD_DIM
SSM_GROUP = 16
STATE_DIM = 64
CONV_W = 3
NUM_BUCKETS = 32
MAX_EXACT = NUM_BUCKETS // 2
MAX_DISTANCE = 128
LN_EPS = 1e-5
NEG_INF = -1e30

VMEM_LIMIT_BYTES = 56 * 1024 * 1024
LANES = 128
SLAB_GROUPS = 8
N_SLABS = 8
SLAB_IN = SLAB_GROUPS * SSM_GROUP
SLAB_STATE = SLAB_GROUPS * STATE_DIM
SUBLANES = 8
MM_ROWS, MM_COLS = 1024, 1024
LN_MM_ROWS = 512
DOWN_ROWS, DOWN_COLS = 512, 512
LN_ROWS = 512
ATTN_BLOCK = 256
ATTN_KEY_BLOCK = 256
ATTN_HEADS_PER_STEP = 4
S5_ROWS = 512
DECODE_PAGES = 16
UP_ROW_CHUNK = 512


def _params(sem):
    return pltpu.CompilerParams(dimension_semantics=sem, vmem_limit_bytes=VMEM_LIMIT_BYTES)


def _mm_body(*refs, n_pairs, n_extra, n_out, epilogue):
    it = iter(refs)
    take = lambda n: [next(it) for _ in range(n)]
    x_refs, xs_refs, w_refs = take(n_pairs), take(n_pairs), take(n_pairs)
    extra, extra_s = take(n_extra), take(n_extra)
    outs, outs_s = take(n_out), take(n_out)
    wbf = take(n_pairs)

    def apply(lhs_refs, extra_refs, out_refs):
        accs = [jnp.dot(x[...].astype(BF16), s[...], preferred_element_type=F32)
                for x, s in zip(lhs_refs, wbf)]
        for o, r in zip(out_refs, epilogue(*accs, *[e[...] for e in extra_refs])):
            o[...] = r.astype(o.dtype)

    @pl.when(pl.program_id(1) == 0)
    def _():
        for w, s in zip(w_refs, wbf):
            s[...] = w[...].astype(BF16)
        apply(xs_refs, extra_s, outs_s)

    apply(x_refs, extra, outs)


def _matmul(pairs, extras, out_dtypes, epilogue, *, n_cols, tm, tn, name, m=None,
            position_major_lhs=(), position_major_out=None):
    m = pairs[0][0].shape[0] if m is None else m
    ms = pairs[0][1].shape[0]
    grid = (n_cols // tn, m // tm)
    w_mode = dict(pipeline_mode=pl.Buffered(1)) if grid[0] == 1 else {}
    specs_x, specs_xs, specs_w, scratch = [], [], [], []
    for idx, (x, xs, w, c0) in enumerate(pairs):
        if idx in position_major_lhs:
            per_b = x.shape[0] // tm
            k = w.shape[0]
            specs_x.append(pl.BlockSpec((tm, k), lambda j, i, per_b=per_b: (i % per_b, i // per_b)))
            specs_xs.append(pl.BlockSpec((ms, k), lambda j, i: (0, 0)))
            specs_w.append(pl.BlockSpec((k, tn), lambda j, i, off=c0 // tn: (0, j + off), **w_mode))
            scratch.append(pltpu.VMEM((k, tn), BF16))
            continue
        k = x.shape[1]
        specs_x.append(pl.BlockSpec((tm, k), lambda j, i: (i, 0)))
        specs_xs.append(pl.BlockSpec((ms, k), lambda j, i: (0, 0)))
        specs_w.append(pl.BlockSpec((k, tn), lambda j, i, off=c0 // tn: (0, j + off), **w_mode))
        scratch.append(pltpu.VMEM((k, tn), BF16))
    specs_e, specs_es, args_e, args_es = [], [], [], []
    for e in extras:
        if isinstance(e, tuple):
            specs_e.append(pl.BlockSpec((tm, tn), lambda j, i: (i, j)))
            specs_es.append(pl.BlockSpec((ms, tn), lambda j, i: (0, j)))
            args_e.append(e[0])
            args_es.append(e[1])
        else:
            specs_e.append(pl.BlockSpec((1, tn), lambda j, i: (0, j)))
            specs_es.append(pl.BlockSpec((1, tn), lambda j, i: (0, j)))
            args_e.append(e)
            args_es.append(e)
    if position_major_out is None:
        p_shape = (m, n_cols)
        p_spec = pl.BlockSpec((tm, tn), lambda j, i: (i, j))
    else:
        batch, seq = position_major_out
        per_b, nj = seq // tm, n_cols // tn
        p_shape = (seq, batch * n_cols)
        p_spec = pl.BlockSpec((tm, tn), lambda j, i: (i % per_b, (i // per_b) * nj + j))
    out_shape = ([jax.ShapeDtypeStruct(p_shape, d) for d in out_dtypes]
                 + [jax.ShapeDtypeStruct((ms, n_cols), d) for d in out_dtypes])
    out_specs = ([p_spec for _ in out_dtypes]
                 + [pl.BlockSpec((ms, tn), lambda j, i: (0, j)) for _ in out_dtypes])
    body = functools.partial(_mm_body, n_pairs=len(pairs), n_extra=len(extras),
                             n_out=len(out_dtypes), epilogue=epilogue)
    outs = pl.pallas_call(
        body, grid=grid, in_specs=specs_x + specs_xs + specs_w + specs_e + specs_es,
        out_specs=out_specs, out_shape=out_shape, scratch_shapes=scratch,
        compiler_params=_params(("arbitrary", "arbitrary")), name=name)(
            *[p[0] for p in pairs], *[p[1] for p in pairs], *[p[2] for p in pairs],
            *args_e, *args_es)
    n = len(out_dtypes)
    return outs[:n], outs[n:]


def _proj_t_body(x_ref, xs_ref, w_ref, o_ref, ob_ref, os_ref, wt_sc):
    @pl.when(pl.program_id(0) == 0)
    def _():
        for c in range(w_ref.shape[1] // LANES):
            cols = slice(c * LANES, (c + 1) * LANES)
            wt_sc[cols, :] = w_ref[:, cols].T.astype(BF16)
        os_ref[...] = lax.dot_general(xs_ref[...].astype(BF16), wt_sc[...], (((1,), (1,)), ((), ())),
                                      preferred_element_type=F32)

    kt = lax.dot_general(wt_sc[...], x_ref[...], (((1,), (1,)), ((), ())),
                         preferred_element_type=F32)
    o_ref[...] = kt
    ob_ref[...] = kt.astype(BF16)


def _proj_transposed(x, xs, w, c0, *, n_cols, batch, seq, tm, name):
    k = x.shape[1]
    ms = xs.shape[0]
    per_b = seq // tm
    out_spec = pl.BlockSpec((None, n_cols, tm), lambda i: (i // per_b, 0, i % per_b))
    return pl.pallas_call(
        _proj_t_body, grid=(batch * per_b,),
        in_specs=[pl.BlockSpec((tm, k), lambda i: (i, 0)),
                  pl.BlockSpec((ms, k), lambda i: (0, 0)),
                  pl.BlockSpec((k, n_cols), lambda i, off=c0 // n_cols: (0, off),
                               pipeline_mode=pl.Buffered(1))],
        out_specs=[out_spec, out_spec, pl.BlockSpec((ms, n_cols), lambda i: (0, 0))],
        out_shape=[jax.ShapeDtypeStruct((batch, n_cols, seq), F32),
                   jax.ShapeDtypeStruct((batch, n_cols, seq), BF16),
                   jax.ShapeDtypeStruct((ms, n_cols), F32)],
        scratch_shapes=[pltpu.VMEM((n_cols, k), BF16)],
        compiler_params=_params(("arbitrary",)), name=name)(x, xs, w)


def _ln(x, g, b):
    mu = jnp.mean(x, axis=-1, keepdims=True)
    xc = x - mu
    var = jnp.mean(xc * xc, axis=-1, keepdims=True)
    return xc * lax.rsqrt(var + LN_EPS) * g + b


def _ln_body(x_ref, g_ref, b_ref, o_ref):
    o_ref[...] = _ln(x_ref[...], g_ref[...], b_ref[...])


def _layer_norm(x, g, b, *, tm, name):
    m, d = x.shape
    tm = min(tm, m)
    return pl.pallas_call(
        _ln_body, grid=(m // tm,),
        in_specs=[pl.BlockSpec((tm, d), lambda i: (i, 0)),
                  pl.BlockSpec((1, d), lambda i: (0, 0)),
                  pl.BlockSpec((1, d), lambda i: (0, 0))],
        out_specs=pl.BlockSpec((tm, d), lambda i: (i, 0)),
        out_shape=jax.ShapeDtypeStruct((m, d), F32),
        compiler_params=_params(("arbitrary",)), name=name)(x, g.reshape(1, d), b.reshape(1, d))


def _rel_bucket(n):
    n = jnp.maximum(n, 0)
    nf = jnp.maximum(n, 1).astype(F32)
    large = MAX_EXACT + (jnp.log(nf / MAX_EXACT) / math.log(MAX_DISTANCE / MAX_EXACT)
                         * (NUM_BUCKETS - MAX_EXACT)).astype(jnp.int32)
    large = jnp.minimum(large, NUM_BUCKETS - 1)
    return jnp.where(n < MAX_EXACT, n, large)


def _bucket_lookup(bucket, table_fn):
    out = jnp.zeros(jnp.broadcast_shapes(bucket.shape, table_fn(0).shape), F32)
    for b in range(NUM_BUCKETS):
        out = out + jnp.where(bucket == b, table_fn(b), 0.0)
    return out


def _bias_tile_body(rb_ref, bucket_ref, o_ref):
    h = pl.program_id(0)
    o_ref[...] = _bucket_lookup(bucket_ref[...], lambda bk: rb_ref[bk, h])


def _bias_tiles(rel_bias, t):
    r = jnp.arange(t, dtype=jnp.int32)
    c = jnp.arange(2 * t, dtype=jnp.int32)
    buckets = _rel_bucket(r[:, None] + t - c[None, :])
    return pl.pallas_call(
        _bias_tile_body, grid=(N_HEADS,),
        in_specs=[pl.BlockSpec(memory_space=pltpu.SMEM), pl.BlockSpec((t, 2 * t), lambda h: (0, 0))],
        out_specs=pl.BlockSpec((None, t, 2 * t), lambda h: (h, 0, 0)),
        out_shape=jax.ShapeDtypeStruct((N_HEADS, t, 2 * t), F32),
        compiler_params=_params(("arbitrary",)), name="bias_tiles")(rel_bias, buckets)


def _attn_body(lam_ref, rb_ref, q_ref, k_ref, v_ref, bias_ref, g_ref, o_ref, *scratch,
               t, n_far, near, heads, out_scale):
    s_bufs, p_bufs, m_bufs, mf_bufs, l_bufs, a_bufs = (scratch[0:2], scratch[2:4], scratch[4:6],
                                                       scratch[6:8], scratch[8:10], scratch[10:12])
    hg = pl.program_id(0)
    far = n_far * t
    kb_w = ATTN_KEY_BLOCK
    n_kb = (far + near) // kb_w
    n_tiles = kb_w // LANES

    def fold(x, op):
        out = x[:, 0:LANES]
        for c in range(1, n_tiles):
            out = op(out, x[:, c * LANES:(c + 1) * LANES])
        return out

    def head_cols(hh):
        return slice(hh * V_DIM, (hh + 1) * V_DIM)

    def score_block(hh, kb):
        par = hh % 2
        q = q_ref[:, head_cols(hh)] * (HEAD_DIM ** -0.5)
        lane = lax.broadcasted_iota(jnp.int32, q.shape, 1)
        zero = jnp.zeros_like(q)
        q2 = jnp.concatenate([jnp.where(lane < HEAD_DIM, q, zero), jnp.where(lane >= HEAD_DIM, q, zero)],
                             axis=0)
        cols = slice(kb * kb_w, (kb + 1) * kb_w)
        s = jnp.dot(q2, k_ref[head_cols(hh), cols], preferred_element_type=F32)
        bias_far = rb_ref[NUM_BUCKETS - 1, hg * heads + hh]
        if kb * kb_w < far:
            top = fold(s, jnp.maximum) + bias_far
        else:
            off = kb * kb_w - far
            lo = 2 * t - near + off
            bias = bias_ref[hh, :, lo:lo + kb_w]
            row = lax.broadcasted_iota(jnp.int32, (t, kb_w), 0)
            col = lax.broadcasted_iota(jnp.int32, (t, kb_w), 1)
            keep = col + (off - (near - t)) <= row
            s = jnp.where(jnp.concatenate([keep, keep], axis=0),
                          s + jnp.concatenate([bias, bias], axis=0), NEG_INF)
            top = fold(s, jnp.maximum)
        s_bufs[par][:, cols] = s
        if kb == 0:
            m_bufs[par][...] = top
        else:
            m_bufs[par][...] = jnp.maximum(m_bufs[par][...], top)
        if kb == n_kb - 1:
            m = jnp.broadcast_to(jnp.max(m_bufs[par][...], axis=1, keepdims=True), m_bufs[par].shape)
            m_bufs[par][...] = m
            mf_bufs[par][...] = m - bias_far

    def exp_block(hh, kb):
        par = hh % 2
        cols = slice(kb * kb_w, (kb + 1) * kb_w)
        m = (mf_bufs if kb * kb_w < far else m_bufs)[par][...]
        p = jnp.exp(s_bufs[par][:, cols] - jnp.concatenate([m] * n_tiles, axis=1))
        if kb == 0:
            l_bufs[par][...] = fold(p, jnp.add)
        else:
            l_bufs[par][...] += fold(p, jnp.add)
        p_bufs[par][:, cols] = p.astype(BF16)

    def value_block(hh, kb):
        par = hh % 2
        rows = slice(kb * kb_w, (kb + 1) * kb_w)
        pv = jnp.dot(p_bufs[par][:, rows], v_ref[rows, head_cols(hh)], preferred_element_type=F32)
        if kb == 0:
            a_bufs[par][...] = pv
        else:
            a_bufs[par][...] += pv
        if kb == n_kb - 1:
            nrm = a_bufs[par][...] / jnp.sum(l_bufs[par][...], axis=1, keepdims=True)
            o = nrm[0:t] - lam_ref[0] * nrm[t:2 * t]
            ms = jnp.mean(o * o, axis=-1, keepdims=True)
            o_ref[:, head_cols(hh)] = (o * lax.rsqrt(ms + LN_EPS) * g_ref[...] * out_scale
                                       ).astype(o_ref.dtype)

    for stage in range(heads + 2):
        for kb in range(n_kb):
            if stage < heads:
                score_block(stage, kb)
            if 0 <= stage - 1 < heads:
                exp_block(stage - 1, kb)
            if 0 <= stage - 2 < heads:
                value_block(stage - 2, kb)


def _prompt_attention(q, k, v, rel_bias, lam, subln_g, *, batch, seq, t, out_scale):
    assert t >= MAX_DISTANCE
    nq = seq // t
    width = N_HEADS * V_DIM
    q3 = q.reshape(batch, seq, width)
    v3 = v.reshape(batch, seq, width)
    bias = _bias_tiles(rel_bias, t)
    smem = pl.BlockSpec(memory_space=pltpu.SMEM)
    pieces = []
    for i in range(nq):
        n_far = max(i - 1, 0)
        near = min(i + 1, 2) * t
        keys = n_far * t + near
        hp = ATTN_HEADS_PER_STEP
        body = functools.partial(_attn_body, t=t, n_far=n_far, near=near, heads=hp, out_scale=out_scale)
        pair = lambda shape, dtype: [pltpu.VMEM(shape, dtype)] * 2
        pieces.append(pl.pallas_call(
            body, grid=(N_HEADS // hp, batch),
            in_specs=[smem, smem,
                      pl.BlockSpec((None, t, hp * V_DIM), lambda h, b, i=i: (b, i, h)),
                      pl.BlockSpec((None, hp * V_DIM, keys), lambda h, b: (b, h, 0)),
                      pl.BlockSpec((None, keys, hp * V_DIM), lambda h, b: (b, 0, h)),
                      pl.BlockSpec((hp, t, 2 * t), lambda h, b: (h, 0, 0)),
                      pl.BlockSpec((1, V_DIM), lambda h, b: (0, 0))],
            out_specs=pl.BlockSpec((None, t, hp * V_DIM), lambda h, b: (b, 0, h)),
            out_shape=jax.ShapeDtypeStruct((batch, t, width), BF16),
            scratch_shapes=(pair((2 * t, keys), F32) + pair((2 * t, keys), BF16)
                            + pair((2 * t, LANES), F32) + pair((2 * t, LANES), F32)
                            + pair((2 * t, LANES), F32) + pair((2 * t, V_DIM), F32)),
            compiler_params=_params(("arbitrary", "arbitrary")),
            name=f"prompt_attention_q{i}")(lam, rel_bias, q3, k, v3, bias, subln_g.reshape(1, V_DIM)))
    return jnp.stack(pieces, axis=1).reshape(batch * seq, width)


def _decode_body(pt_ref, lam_ref, q_ref, kn_ref, vn_ref, rbt_ref, bucket_ref, g_ref, *rest,
                 pages, page, n_steps, out_scale):
    k_refs = rest[:pages]
    v_refs = rest[pages:2 * pages]
    o_ref = rest[2 * pages]
    qexp_sc, bias_sc, m_sc, l_sc, acc_sc = rest[2 * pages + 1:]
    step = pl.program_id(1)
    rows = 2 * N_HEADS
    width = N_HEADS * V_DIM

    @pl.when(step == 0)
    def _():
        row = lax.broadcasted_iota(jnp.int32, (rows, width), 0)
        col = lax.broadcasted_iota(jnp.int32, (rows, width), 1)
        own_qk = (col // HEAD_DIM) == (row % N_HEADS) * 2 + row // N_HEADS
        q = (q_ref[...] * (HEAD_DIM ** -0.5)).astype(BF16).astype(F32)
        qexp = jnp.where(own_qk, jnp.broadcast_to(q, (rows, width)), 0.0)
        qexp_sc[...] = qexp.astype(BF16)
        bias_sc[...] = _bucket_lookup(bucket_ref[...], lambda bk: rbt_ref[:, bk:bk + 1])
        kn = kn_ref[...].astype(BF16).astype(F32)
        s_self = jnp.sum(qexp * kn, axis=1, keepdims=True) + rbt_ref[:, 0:1]
        m_sc[...] = s_self
        l_sc[...] = jnp.ones(l_sc.shape, F32)
        vn = vn_ref[...].astype(BF16).astype(F32)
        acc_sc[...] = jnp.broadcast_to(vn[:, None, :], acc_sc.shape)

    qexp = qexp_sc[...]
    bias_far = rbt_ref[:, NUM_BUCKETS - 1:NUM_BUCKETS]
    is_last = step == n_steps - 1
    s_parts = []
    for p in range(pages):
        s = jnp.dot(qexp, k_refs[p][...].astype(BF16), preferred_element_type=F32)
        if p == pages - 1:
            s = s + jnp.where(is_last, bias_sc[...], bias_far)
        else:
            s = s + bias_far
        s_parts.append(s)
    m_old = m_sc[...]
    m_new = m_old
    for s in s_parts:
        m_new = jnp.maximum(m_new, jnp.max(s, axis=1, keepdims=True))
    a = jnp.exp(m_old - m_new)
    l_new = a * l_sc[...]
    probs = []
    for p in range(pages):
        pr = jnp.exp(s_parts[p] - m_new)
        l_new = l_new + jnp.sum(pr, axis=1, keepdims=True)
        probs.append(pr.astype(BF16))
    probs = jnp.concatenate(probs, axis=1)
    m_sc[...] = m_new
    l_sc[...] = l_new
    for h in range(N_HEADS):
        vh = jnp.concatenate([v_refs[p][pl.ds(h, page, stride=N_HEADS), :].astype(BF16)
                              for p in range(pages)], axis=0)
        acc_sc[h] = a * acc_sc[h] + jnp.dot(probs, vh, preferred_element_type=F32)

    @pl.when(is_last)
    def _():
        nrm = acc_sc[...] / l_new[None]
        r = lax.broadcasted_iota(jnp.int32, nrm.shape, 1)
        hh = lax.broadcasted_iota(jnp.int32, nrm.shape, 0)
        coef = jnp.where(r == hh, 1.0, jnp.where(r == hh + N_HEADS, -lam_ref[0], 0.0))
        d = jnp.sum(coef * nrm, axis=1)
        ms = jnp.mean(d * d, axis=-1, keepdims=True)
        o_ref[...] = (d * lax.rsqrt(ms + LN_EPS) * g_ref[...] * out_scale).astype(o_ref.dtype)


def _sample_attention(q, k_new, v_new, cache_k, cache_v, page_table, rel_bias, lam, subln_g,
                      *, pages, out_scale):
    bs, width = q.shape
    page = cache_k.shape[2]
    n_pages = page_table.shape[1]
    n_steps = n_pages // pages
    past = n_pages * page
    kpos = past - page + jnp.arange(page, dtype=jnp.int32)
    bucket_last = _rel_bucket(past - kpos).reshape(1, page)
    rbt = jnp.tile(rel_bias.T, (2, 1))

    def tok_spec():
        return pl.BlockSpec((None, 1, width), lambda b, s, pt: (b, 0, 0))

    def page_spec(shape, p):
        return pl.BlockSpec((None,) + shape, lambda b, s, pt, p=p: (pt[b, s * pages + p], 0, 0))

    full = lambda shape: pl.BlockSpec(shape, lambda b, s, pt: tuple(0 for _ in shape))
    head_spec = pl.BlockSpec((None, N_HEADS, V_DIM), lambda b, s, pt: (b, 0, 0))
    in_specs = ([pl.BlockSpec(memory_space=pltpu.SMEM), tok_spec(), tok_spec(), head_spec,
                 full((2 * N_HEADS, NUM_BUCKETS)), full((1, page)), full((1, V_DIM))]
                + [page_spec((width, page), p) for p in range(pages)]
                + [page_spec((page * N_HEADS, V_DIM), p) for p in range(pages)])
    rows = 2 * N_HEADS
    body = functools.partial(_decode_body, pages=pages, page=page, n_steps=n_steps,
                             out_scale=out_scale)
    out = pl.pallas_call(
        body,
        grid_spec=pltpu.PrefetchScalarGridSpec(
            num_scalar_prefetch=1, grid=(bs, n_steps), in_specs=in_specs,
            out_specs=head_spec,
            scratch_shapes=[pltpu.VMEM((rows, width), BF16), pltpu.VMEM((rows, page), F32),
                            pltpu.VMEM((rows, 1), F32), pltpu.VMEM((rows, 1), F32),
                            pltpu.VMEM((N_HEADS, rows, V_DIM), F32)]),
        out_shape=jax.ShapeDtypeStruct((bs, N_HEADS, V_DIM), BF16),
        compiler_params=_params(("arbitrary", "arbitrary")),
        name="sample_attention")(
            page_table, lam, q.reshape(bs, 1, width), k_new.reshape(bs, 1, width),
            v_new.reshape(bs, N_HEADS, V_DIM), rbt, bucket_last, subln_g.reshape(1, V_DIM),
            *([cache_k] * pages), *([cache_v] * pages))
    return out.reshape(bs, width)


def _s5_params(a_re, a_im, log_dt, b_re, b_im, c_re, c_im):
    dt = jnp.exp(log_dt)[:, None]
    mag = jnp.exp(a_re * dt)
    ang = a_im * dt
    abar_re = mag * jnp.cos(ang)
    abar_im = mag * jnp.sin(ang)
    den = a_re * a_re + a_im * a_im
    f_re = ((abar_re - 1.0) * a_re + abar_im * a_im) / den
    f_im = (abar_im * a_re - (abar_re - 1.0) * a_im) / den
    bb_re = f_re[..., None] * b_re - f_im[..., None] * b_im
    bb_im = f_re[..., None] * b_im + f_im[..., None] * b_re
    eye = jnp.eye(SLAB_GROUPS, dtype=F32)

    def in_slabs(bb):
        tt = bb.reshape(N_SLABS, SLAB_GROUPS, STATE_DIM, SSM_GROUP).transpose(0, 1, 3, 2)
        full = tt[:, :, :, None, :] * eye[None, :, None, :, None]
        return full.reshape(N_SLABS, SLAB_IN, SLAB_STATE).astype(BF16)

    def out_slabs(cc):
        tt = cc.reshape(N_SLABS, SLAB_GROUPS, SSM_GROUP, STATE_DIM).transpose(0, 1, 3, 2)
        full = tt[:, :, :, None, :] * eye[None, :, None, :, None]
        return full.reshape(N_SLABS, SLAB_STATE, SLAB_IN).astype(BF16)

    return (abar_re.reshape(N_SLABS, SLAB_STATE), abar_im.reshape(N_SLABS, SLAB_STATE),
            in_slabs(bb_re), in_slabs(bb_im), out_slabs(c_re), out_slabs(c_im))


def _glu_out(y, u, d_ref, wglu_ref, o_ref):
    g = jax.nn.gelu(y + d_ref[...] * u)
    gate = jnp.dot(g.astype(BF16), wglu_ref[...].astype(BF16), preferred_element_type=F32)
    o_ref[...] = (g * jax.nn.sigmoid(gate)).astype(o_ref.dtype)


def _s5_prompt_body(u_ref, ar_ref, ai_ref, bbr_ref, bbi_ref, cr_ref, ci_ref, d_ref, wglu_ref,
                    o_ref, sre_ref, sim_ref, xr_sc, xi_sc, pw_sc, cr_sc, ci_sc, *, batch):
    c = pl.program_id(0)
    sub = SUBLANES
    n_rows = u_ref.shape[0]
    t = lax.broadcasted_iota(jnp.int32, (sub, SLAB_STATE), 0)
    first = t < batch

    def cmul(pr, pi, qr, qi):
        return pr * qr - pi * qi, pr * qi + pi * qr

    @pl.when(c == 0)
    def _():
        for j in range(N_SLABS):
            a = (jnp.broadcast_to(ar_ref[j:j + 1, :], t.shape), jnp.broadcast_to(ai_ref[j:j + 1, :], t.shape))
            a2 = cmul(*a, *a)
            for part in range(2):
                pw_sc[j, part] = jnp.where(first, 0.0, a[part])
                pw_sc[j, 2 + part] = jnp.where(first, a[part], a2[part])
        cr_sc[...] = jnp.zeros(cr_sc.shape, F32)
        ci_sc[...] = jnp.zeros(ci_sc.shape, F32)

    u = u_ref[...]
    ub = u.astype(BF16)
    for j in range(N_SLABS):
        uj = ub[:, j * SLAB_IN:(j + 1) * SLAB_IN]
        xr_sc[j] = jnp.dot(uj, bbr_ref[j], preferred_element_type=F32)
        xi_sc[j] = jnp.dot(uj, bbi_ref[j], preferred_element_type=F32)

    for j in range(N_SLABS):
        pr, pi = cr_sc[j], ci_sc[j]
        for v in range(n_rows // sub):
            rows = slice(v * sub, (v + 1) * sub)
            br = xr_sc[j, rows, :]
            bi = xi_sc[j, rows, :]
            lr = jnp.where(first, pltpu.roll(pr, batch, 0), pr)
            li = jnp.where(first, pltpu.roll(pi, batch, 0), pi)
            dr, di = cmul(pw_sc[j, 0], pw_sc[j, 1], pltpu.roll(br, batch, 0), pltpu.roll(bi, batch, 0))
            er, ei = cmul(pw_sc[j, 2], pw_sc[j, 3], lr, li)
            pr, pi = br + dr + er, bi + di + ei
            xr_sc[j, rows, :] = pr
            xi_sc[j, rows, :] = pi
        cr_sc[j] = pr
        ci_sc[j] = pi
    sre_ref[...] = cr_sc[...]
    sim_ref[...] = ci_sc[...]

    ys = []
    for j in range(N_SLABS):
        ys.append(jnp.dot(xr_sc[j].astype(BF16), cr_ref[j], preferred_element_type=F32)
                  - jnp.dot(xi_sc[j].astype(BF16), ci_ref[j], preferred_element_type=F32))
    _glu_out(jnp.concatenate(ys, axis=1), u, d_ref, wglu_ref, o_ref)


def _s5_prompt(u, sp, d, w_glu, *, batch, seq, rows):
    assert SUBLANES == 2 * batch, "tile = two positions of every sequence"
    ar, ai, bbr, bbi, cr, ci = sp
    width = u.shape[1]
    const = lambda shape: pl.BlockSpec(shape, lambda c: tuple(0 for _ in shape))
    state_spec = const((N_SLABS, SUBLANES, SLAB_STATE))
    state_shape = jax.ShapeDtypeStruct((N_SLABS, SUBLANES, SLAB_STATE), F32)
    return pl.pallas_call(
        functools.partial(_s5_prompt_body, batch=batch), grid=(batch * seq // rows,),
        in_specs=[pl.BlockSpec((rows, width), lambda c: (c, 0)),
                  const(ar.shape), const(ai.shape), const(bbr.shape), const(bbi.shape),
                  const(cr.shape), const(ci.shape), const((1, width)), const(w_glu.shape)],
        out_specs=[pl.BlockSpec((rows, width), lambda c: (c, 0)), state_spec, state_spec],
        out_shape=[jax.ShapeDtypeStruct((batch * seq, width), BF16), state_shape, state_shape],
        scratch_shapes=[pltpu.VMEM((N_SLABS, rows, SLAB_STATE), F32),
                        pltpu.VMEM((N_SLABS, rows, SLAB_STATE), F32),
                        pltpu.VMEM((N_SLABS, 4, SUBLANES, SLAB_STATE), F32),
                        pltpu.VMEM((N_SLABS, SUBLANES, SLAB_STATE), F32),
                        pltpu.VMEM((N_SLABS, SUBLANES, SLAB_STATE), F32)],
        compiler_params=_params(("arbitrary",)),
        name="s5_prompt")(u, ar, ai, bbr, bbi, cr, ci, d.reshape(1, width), w_glu)


def _s5_sample_body(u_ref, x0r_ref, x0i_ref, ar_ref, ai_ref, bbr_ref, bbi_ref, cr_ref, ci_ref,
                    d_ref, wglu_ref, o_ref, sre_ref, sim_ref):
    u = u_ref[...]
    ub = u.astype(BF16)
    ys = []
    for j in range(N_SLABS):
        uj = ub[:, j * SLAB_IN:(j + 1) * SLAB_IN]
        ar = ar_ref[j:j + 1, :]
        ai = ai_ref[j:j + 1, :]
        x0r = x0r_ref[j]
        x0i = x0i_ref[j]
        xr = ar * x0r - ai * x0i + jnp.dot(uj, bbr_ref[j], preferred_element_type=F32)
        xi = ar * x0i + ai * x0r + jnp.dot(uj, bbi_ref[j], preferred_element_type=F32)
        sre_ref[j] = xr
        sim_ref[j] = xi
        ys.append(jnp.dot(xr.astype(BF16), cr_ref[j], preferred_element_type=F32)
                  - jnp.dot(xi.astype(BF16), ci_ref[j], preferred_element_type=F32))
    _glu_out(jnp.concatenate(ys, axis=1), u, d_ref, wglu_ref, o_ref)


def _s5_sample(u, x0_re, x0_im, sp, d, w_glu):
    ar, ai, bbr, bbi, cr, ci = sp
    bs, width = u.shape
    state_shape = jax.ShapeDtypeStruct((N_SLABS, bs, SLAB_STATE), F32)
    return pl.pallas_call(
        _s5_sample_body,
        out_shape=[jax.ShapeDtypeStruct((bs, width), BF16), state_shape, state_shape],
        compiler_params=pltpu.CompilerParams(vmem_limit_bytes=VMEM_LIMIT_BYTES),
        name="s5_sample")(u, x0_re, x0_im, ar, ai, bbr, bbi, cr, ci, d.reshape(1, width), w_glu)


def _gated(c):
    gate = c[:, :LANES]
    return gate * jax.nn.sigmoid(gate) * c[:, LANES:]


def _up_prompt_body(h_ref, wg_ref, wv_ref, cwg_ref, cwv_ref, cbg_ref, cbv_ref,
                    act_ref, cg_ref, cv_ref, w_sc, *, rows):
    w_sc[:, 0:LANES] = wg_ref[...].astype(BF16)
    w_sc[:, LANES:2 * LANES] = wv_ref[...].astype(BF16)
    cw = jnp.concatenate([cwg_ref[...], cwv_ref[...]], axis=1)
    cb = jnp.concatenate([cbg_ref[...], cbv_ref[...]], axis=1)
    seq = h_ref.shape[0]
    sub = SUBLANES
    row8 = lax.broadcasted_iota(jnp.int32, (sub, 2 * LANES), 0)

    prev = jnp.zeros((sub, 2 * LANES), F32)
    for c in range(seq // rows):
        up = jnp.dot(h_ref[c * rows:(c + 1) * rows, :], w_sc[...], preferred_element_type=F32)
        cur = cb + cw[CONV_W - 1:CONV_W] * up
        for back in range(1, CONV_W):
            rolled = pltpu.roll(up, back, 0)
            head = jnp.where(row8 < back, pltpu.roll(prev, back, 0), rolled[0:sub])
            shifted = jnp.concatenate([head, rolled[sub:]], axis=0)
            cur = cur + cw[CONV_W - 1 - back:CONV_W - back] * shifted
        act_ref[c * rows:(c + 1) * rows, :] = _gated(cur).astype(act_ref.dtype)
        prev = up[rows - sub:, :]
    tail = prev[sub - (CONV_W - 1):, :]
    cg_ref[...] = tail[:, :LANES]
    cv_ref[...] = tail[:, LANES:]


def _up_prompt(hb, w_up, conv_w, conv_b, *, batch, seq):
    d_model = hb.shape[1]
    d_ff = w_up.shape[1] // 2
    nb = d_ff // LANES
    cb = conv_b.reshape(1, 2 * d_ff)
    col = lambda off: (lambda b, j: (0, j + off))
    tail_spec = pl.BlockSpec((None, CONV_W - 1, LANES), lambda b, j: (b, 0, j))
    tail_shape = jax.ShapeDtypeStruct((batch, CONV_W - 1, d_ff), F32)
    return pl.pallas_call(
        functools.partial(_up_prompt_body, rows=UP_ROW_CHUNK), grid=(batch, nb),
        scratch_shapes=[pltpu.VMEM((d_model, 2 * LANES), BF16)],
        in_specs=[pl.BlockSpec((seq, d_model), lambda b, j: (b, 0)),
                  pl.BlockSpec((d_model, LANES), col(0)), pl.BlockSpec((d_model, LANES), col(nb)),
                  pl.BlockSpec((CONV_W, LANES), col(0)), pl.BlockSpec((CONV_W, LANES), col(nb)),
                  pl.BlockSpec((1, LANES), col(0)), pl.BlockSpec((1, LANES), col(nb))],
        out_specs=[pl.BlockSpec((seq, LANES), lambda b, j: (b, j)), tail_spec, tail_spec],
        out_shape=[jax.ShapeDtypeStruct((batch * seq, d_ff), BF16), tail_shape, tail_shape],
        compiler_params=_params(("arbitrary", "arbitrary")),
        name="up_prompt")(hb, w_up, w_up, conv_w, conv_w, cb, cb)


def _up_sample_body(h_ref, wg_ref, wv_ref, cwg_ref, cwv_ref, cbg_ref, cbv_ref, sg_ref, sv_ref,
                    act_ref, cg_ref, cv_ref):
    w = jnp.concatenate([wg_ref[...].astype(BF16), wv_ref[...].astype(BF16)], axis=1)
    up = jnp.dot(h_ref[...], w, preferred_element_type=F32)
    cw = jnp.concatenate([cwg_ref[...], cwv_ref[...]], axis=1)
    cb = jnp.concatenate([cbg_ref[...], cbv_ref[...]], axis=1)
    c = cb + cw[CONV_W - 1:CONV_W] * up
    for tap in range(CONV_W - 1):
        st = jnp.concatenate([sg_ref[tap], sv_ref[tap]], axis=1)
        c = c + cw[tap:tap + 1] * st
    act_ref[...] = _gated(c).astype(act_ref.dtype)
    for tap in range(1, CONV_W - 1):
        cg_ref[tap - 1] = sg_ref[tap]
        cv_ref[tap - 1] = sv_ref[tap]
    cg_ref[CONV_W - 2] = up[:, :LANES]
    cv_ref[CONV_W - 2] = up[:, LANES:]


def _up_sample(hb, w_up, conv_w, conv_b, state):
    bs, d_model = hb.shape
    d_ff = w_up.shape[1] // 2
    nb = d_ff // LANES
    cb = conv_b.reshape(1, 2 * d_ff)
    col = lambda off: (lambda j: (0, j + off))
    st = lambda off: pl.BlockSpec((CONV_W - 1, bs, LANES), lambda j, off=off: (0, 0, j + off))
    tail_spec = pl.BlockSpec((CONV_W - 1, bs, LANES), lambda j: (0, 0, j))
    tail_shape = jax.ShapeDtypeStruct((CONV_W - 1, bs, d_ff), F32)
    return pl.pallas_call(
        _up_sample_body, grid=(nb,),
        in_specs=[pl.BlockSpec((bs, d_model), lambda j: (0, 0)),
                  pl.BlockSpec((d_model, LANES), col(0)), pl.BlockSpec((d_model, LANES), col(nb)),
                  pl.BlockSpec((CONV_W, LANES), col(0)), pl.BlockSpec((CONV_W, LANES), col(nb)),
                  pl.BlockSpec((1, LANES), col(0)), pl.BlockSpec((1, LANES), col(nb)),
                  st(0), st(nb)],
        out_specs=[pl.BlockSpec((bs, LANES), lambda j: (0, j)), tail_spec, tail_spec],
        out_shape=[jax.ShapeDtypeStruct((bs, d_ff), BF16), tail_shape, tail_shape],
        compiler_params=_params(("arbitrary",)),
        name="up_sample")(hb, w_up, w_up, conv_w, conv_w, cb, cb, state, state)


def _one(x):
    return (x,)


def _sigmoid_out(acc):
    return (jax.nn.sigmoid(acc),)


def _both(acc):
    return (acc, acc)


def _merge(pa, ps, ga, gs):
    return (ga * pa + gs * ps,)


def _layer(xp, xs, w, *, batch, seq, alpha, attend_p, attend_s, ssm_p, ssm_s, up_p, up_s):
    d_model = xp.shape[1]
    qk_w = N_HEADS * 2 * HEAD_DIM
    v_w = N_HEADS * V_DIM
    ssm_w = d_model // 2
    xb = xp.astype(BF16)
    w_in = w["w_in"]
    g1, b1 = w["ln1_g"].reshape(1, d_model), w["ln1_b"].reshape(1, d_model)
    big = dict(tm=MM_ROWS, tn=MM_COLS)
    c = 0
    (q,), (q_s,) = _matmul([(xb, xs, w_in, c)], [], [BF16], _one, n_cols=qk_w, name="proj_q", **big)
    c += qk_w
    kt, ktb, k_s = _proj_transposed(xb, xs, w_in, c, n_cols=qk_w, batch=batch, seq=seq, tm=MM_ROWS,
                                    name="proj_kt")
    c += qk_w
    (v, vb), (v_s, _) = _matmul([(xb, xs, w_in, c)], [], [F32, BF16], _both, n_cols=v_w, name="proj_v", **big)
    c += v_w
    (u,), (u_s,) = _matmul([(xb, xs, w_in, c)], [], [F32], _one, n_cols=ssm_w, name="proj_u",
                           position_major_out=(batch, seq), **big)
    c += ssm_w
    (ga,), (ga_s,) = _matmul([(xb, xs, w_in, c)], [], [F32], _sigmoid_out, n_cols=d_model, name="gate_a", **big)
    c += d_model
    (gs,), (gs_s,) = _matmul([(xb, xs, w_in, c)], [], [F32], _sigmoid_out, n_cols=d_model, name="gate_s", **big)

    attn = attend_p(q, ktb, vb)
    attn_s = attend_s(q_s, k_s, v_s)
    ssm_out, re_p, im_p = ssm_p(u)
    ssm_out_s, re_s, im_s = ssm_s(u_s)

    (merged,), (merged_s,) = _matmul(
        [(attn, attn_s, w["w_proj_attn"], 0), (ssm_out, ssm_out_s, w["w_proj_ssm"], 0)],
        [(ga, ga_s), (gs, gs_s)], [BF16], _merge, n_cols=d_model, name="merge",
        position_major_lhs=(1,), **big)

    def post_ln1(acc, res, g, b):
        h = _ln(alpha * res + acc, g, b)
        return h, h

    (h, hb), (h_s, hb_s) = _matmul([(merged, merged_s, w["w_out"], 0)], [(xp, xs), g1, b1], [F32, BF16],
                                   post_ln1, n_cols=d_model, tm=LN_MM_ROWS, tn=d_model, name="out_proj_ln1")
    act, conv_p = up_p(hb)
    act_s, conv_s = up_s(hb_s)
    (r2,), (r2_s,) = _matmul([(act, act_s, w["w_down"], 0)], [(h, h_s)], [F32],
                             lambda acc, res: (alpha * res + acc,), n_cols=d_model,
                             tm=DOWN_ROWS, tn=DOWN_COLS, name="down_proj")
    y = _layer_norm(r2, w["ln2_g"], w["ln2_b"], tm=LN_ROWS, name="ln2")
    y_s = _layer_norm(r2_s, w["ln2_g"], w["ln2_b"], tm=LN_ROWS, name="ln2_sample")
    return (y, kt, v, re_p, im_p, conv_p), (y_s, k_s, v_s, re_s, im_s, conv_s)


def kernel(x_prompt, x_sample, cache_k, cache_v, state_ssm_re, state_ssm_im, state_conv, page_table, rel_bias, w_in, lambda_q1, lambda_k1, lambda_q2, lambda_k2, subln_g, ssm_a_re, ssm_a_im, ssm_log_dt, ssm_b_re, ssm_b_im, ssm_c_re, ssm_c_im, ssm_d, w_glu, w_proj_attn, w_proj_ssm, w_out, ln1_g, ln1_b, w_up, conv_w, conv_b, w_down, ln2_g, ln2_b):
    depth = w_in.shape[0]
    assert depth == 1, "single-layer trunk"
    bp, seq, d_model = x_prompt.shape
    bs, dec_seq, _ = x_sample.shape
    assert dec_seq == 1
    n_pool, page = cache_k.shape[1], cache_k.shape[2]
    d_ff = w_down.shape[1]
    n_groups = ssm_a_re.shape[1]
    assert n_groups == N_SLABS * SLAB_GROUPS and d_ff % LANES == 0
    alpha = (2.0 * depth) ** 0.25
    width = N_HEADS * V_DIM

    hp = x_prompt.reshape(bp * seq, d_model)
    hs = x_sample.reshape(bs, d_model)
    outs = {}
    for l in range(depth):
        lam_init = 0.8 - 0.6 * math.exp(-0.3 * l)
        out_scale = 1.0 - lam_init
        lam = (jnp.exp(jnp.sum(lambda_q1[l] * lambda_k1[l]))
               - jnp.exp(jnp.sum(lambda_q2[l] * lambda_k2[l])) + lam_init).reshape(1)
        w = dict(w_in=w_in[l], w_proj_attn=w_proj_attn[l], w_proj_ssm=w_proj_ssm[l], w_out=w_out[l],
                 ln1_g=ln1_g[l], ln1_b=ln1_b[l], w_down=w_down[l], ln2_g=ln2_g[l], ln2_b=ln2_b[l])
        sp = _s5_params(ssm_a_re[l], ssm_a_im[l], ssm_log_dt[l], ssm_b_re[l], ssm_b_im[l],
                        ssm_c_re[l], ssm_c_im[l])

        def attend_p(q, kt, v):
            return _prompt_attention(q, kt, v, rel_bias, lam, subln_g[l], batch=bp, seq=seq,
                                     t=ATTN_BLOCK, out_scale=out_scale)

        def ssm_p(u):
            ssm_w = u.shape[1] // bp
            o, sr, si = _s5_prompt(u.reshape(seq * bp, ssm_w), sp, ssm_d[l], w_glu[l], batch=bp,
                                   seq=seq, rows=S5_ROWS)
            last = lambda st: st[:, SUBLANES - bp:, :].transpose(1, 0, 2)
            return o.reshape(seq, bp * ssm_w), last(sr), last(si)

        def up_p(hb):
            act, cg, cv = _up_prompt(hb, w_up[l], conv_w[l], conv_b[l], batch=bp, seq=seq)
            return act, jnp.concatenate([cg, cv], axis=-1)

        ck = cache_k[l].transpose(0, 2, 3, 4, 1).reshape(n_pool, width, page)
        cv_ = cache_v[l].reshape(n_pool, page * N_HEADS, V_DIM)

        def attend_s(q, k, v):
            return _sample_attention(q.astype(F32), k, v, ck, cv_, page_table, rel_bias, lam,
                                     subln_g[l], pages=DECODE_PAGES, out_scale=out_scale)

        def ssm_s(u):
            x0r = state_ssm_re[l].reshape(bs, N_SLABS, SLAB_STATE).transpose(1, 0, 2)
            x0i = state_ssm_im[l].reshape(bs, N_SLABS, SLAB_STATE).transpose(1, 0, 2)
            o, sr, si = _s5_sample(u, x0r, x0i, sp, ssm_d[l], w_glu[l])
            return o, sr.transpose(1, 0, 2), si.transpose(1, 0, 2)

        def up_s(hb):
            act, cg, cv = _up_sample(hb, w_up[l], conv_w[l], conv_b[l],
                                     state_conv[l].transpose(1, 0, 2))
            return act, jnp.concatenate([cg, cv], axis=-1).transpose(1, 0, 2)

        (hp, kt_p, v_p, re_p, im_p, c_p), (hs, k_s, v_s, re_s, im_s, c_s) = _layer(
            hp, hs, w, batch=bp, seq=seq, alpha=alpha, attend_p=attend_p, attend_s=attend_s,
            ssm_p=ssm_p, ssm_s=ssm_s, up_p=up_p, up_s=up_s)
        k_p = kt_p.reshape(bp, N_HEADS, 2, HEAD_DIM, seq).transpose(0, 4, 1, 2, 3)

        for name, val in (("kp", k_p.reshape(bp, seq, N_HEADS, 2, HEAD_DIM)),
                          ("vp", v_p.reshape(bp, seq, N_HEADS, V_DIM)),
                          ("rep", re_p.reshape(bp, n_groups, STATE_DIM)),
                          ("imp", im_p.reshape(bp, n_groups, STATE_DIM)),
                          ("cp", c_p),
                          ("ks", k_s.reshape(bs, 1, N_HEADS, 2, HEAD_DIM)),
                          ("vs", v_s.reshape(bs, 1, N_HEADS, V_DIM)),
                          ("res", re_s.reshape(bs, n_groups, STATE_DIM)),
                          ("ims", im_s.reshape(bs, n_groups, STATE_DIM)),
                          ("cs", c_s)):
            outs.setdefault(name, []).append(val)

    st = {k: jnp.stack(v, axis=0) for k, v in outs.items()}
    return (hp.reshape(bp, seq, d_model), hs.reshape(bs, 1, d_model), st["kp"], st["vp"], st["rep"],
            st["imp"], st["cp"], st["ks"], st["vs"], st["res"], st["ims"], st["cs"])
```

```python
import functools
import math

import jax
import jax.numpy as jnp
from jax import lax
from jax.experimental import pallas as pl
from jax.experimental.pallas import tpu as pltpu

F32 = jnp.float32
BF16 = jnp.bfloat16

N_HEADS = 8
HEAD_DIM = 64
V_DIM = 2 * HEAD_DIM
SSM_GROUP = 16
STATE_DIM = 64
CONV_W = 3
NUM_BUCKETS = 32
MAX_EXACT = NUM_BUCKETS // 2
MAX_DISTANCE = 128
LN_EPS = 1e-5
NEG_INF = -1e30

VMEM_LIMIT_BYTES = 56 * 1024 * 1024
LANES = 128
SLAB_GROUPS = 8
N_SLABS = 8
SLAB_IN = SLAB_GROUPS * SSM_GROUP
SLAB_STATE = SLAB_GROUPS * STATE_DIM
SUBLANES = 8
MM_ROWS, MM_COLS = 1024, 1024
LN_MM_ROWS = 512
DOWN_ROWS, DOWN_COLS = 512, 512
LN_ROWS = 512
ATTN_BLOCK = 256
ATTN_KEY_BLOCK = 256
ATTN_HEADS_PER_STEP = 4
S5_POSITIONS = 64
DECODE_PAGES = 16
UP_ROW_CHUNK = 512


def _params(sem):
    return pltpu.CompilerParams(dimension_semantics=sem, vmem_limit_bytes=VMEM_LIMIT_BYTES)


def _mm_body(*refs, n_pairs, n_extra, n_out, epilogue):
    it = iter(refs)
    take = lambda n: [next(it) for _ in range(n)]
    x_refs, xs_refs, w_refs = take(n_pairs), take(n_pairs), take(n_pairs)
    extra, extra_s = take(n_extra), take(n_extra)
    outs, outs_s = take(n_out), take(n_out)
    wbf = take(n_pairs)

    def apply(lhs_refs, extra_refs, out_refs):
        accs = [jnp.dot(x[...].astype(BF16), s[...], preferred_element_type=F32)
                for x, s in zip(lhs_refs, wbf)]
        for o, r in zip(out_refs, epilogue(*accs, *[e[...] for e in extra_refs])):
            o[...] = r.astype(o.dtype)

    @pl.when(pl.program_id(1) == 0)
    def _():
        for w, s in zip(w_refs, wbf):
            s[...] = w[...].astype(BF16)
        apply(xs_refs, extra_s, outs_s)

    apply(x_refs, extra, outs)


def _matmul(pairs, extras, out_dtypes, epilogue, *, n_cols, tm, tn, name):
    m = pairs[0][0].shape[0]
    ms = pairs[0][1].shape[0]
    grid = (n_cols // tn, m // tm)
    w_mode = dict(pipeline_mode=pl.Buffered(1)) if grid[0] == 1 else {}
    specs_x, specs_xs, specs_w, scratch = [], [], [], []
    for x, xs, w, c0 in pairs:
        k = x.shape[1]
        specs_x.append(pl.BlockSpec((tm, k), lambda j, i: (i, 0)))
        specs_xs.append(pl.BlockSpec((ms, k), lambda j, i: (0, 0)))
        specs_w.append(pl.BlockSpec((k, tn), lambda j, i, off=c0 // tn: (0, j + off), **w_mode))
        scratch.append(pltpu.VMEM((k, tn), BF16))
    specs_e, specs_es, args_e, args_es = [], [], [], []
    for e in extras:
        if isinstance(e, tuple):
            specs_e.append(pl.BlockSpec((tm, tn), lambda j, i: (i, j)))
            specs_es.append(pl.BlockSpec((ms, tn), lambda j, i: (0, j)))
            args_e.append(e[0])
            args_es.append(e[1])
        else:
            specs_e.append(pl.BlockSpec((1, tn), lambda j, i: (0, j)))
            specs_es.append(pl.BlockSpec((1, tn), lambda j, i: (0, j)))
            args_e.append(e)
            args_es.append(e)
    out_shape = ([jax.ShapeDtypeStruct((m, n_cols), d) for d in out_dtypes]
                 + [jax.ShapeDtypeStruct((ms, n_cols), d) for d in out_dtypes])
    out_specs = ([pl.BlockSpec((tm, tn), lambda j, i: (i, j)) for _ in out_dtypes]
                 + [pl.BlockSpec((ms, tn), lambda j, i: (0, j)) for _ in out_dtypes])
    body = functools.partial(_mm_body, n_pairs=len(pairs), n_extra=len(extras),
                             n_out=len(out_dtypes), epilogue=epilogue)
    outs = pl.pallas_call(
        body, grid=grid, in_specs=specs_x + specs_xs + specs_w + specs_e + specs_es,
        out_specs=out_specs, out_shape=out_shape, scratch_shapes=scratch,
        compiler_params=_params(("arbitrary", "arbitrary")), name=name)(
            *[p[0] for p in pairs], *[p[1] for p in pairs], *[p[2] for p in pairs],
            *args_e, *args_es)
    n = len(out_dtypes)
    return outs[:n], outs[n:]


def _proj_t_body(x_ref, xs_ref, w_ref, o_ref, ob_ref, os_ref, wt_sc):
    @pl.when(pl.program_id(0) == 0)
    def _():
        for c in range(w_ref.shape[1] // LANES):
            cols = slice(c * LANES, (c + 1) * LANES)
            wt_sc[cols, :] = w_ref[:, cols].T.astype(BF16)
        os_ref[...] = lax.dot_general(xs_ref[...].astype(BF16), wt_sc[...], (((1,), (1,)), ((), ())),
                                      preferred_element_type=F32)

    kt = lax.dot_general(wt_sc[...], x_ref[...], (((1,), (1,)), ((), ())),
                         preferred_element_type=F32)
    o_ref[...] = kt
    ob_ref[...] = kt.astype(BF16)


def _proj_transposed(x, xs, w, c0, *, n_cols, batch, seq, tm, name):
    k = x.shape[1]
    ms = xs.shape[0]
    per_b = seq // tm
    out_spec = pl.BlockSpec((None, n_cols, tm), lambda i: (i // per_b, 0, i % per_b))
    return pl.pallas_call(
        _proj_t_body, grid=(batch * per_b,),
        in_specs=[pl.BlockSpec((tm, k), lambda i: (i, 0)),
                  pl.BlockSpec((ms, k), lambda i: (0, 0)),
                  pl.BlockSpec((k, n_cols), lambda i, off=c0 // n_cols: (0, off),
                               pipeline_mode=pl.Buffered(1))],
        out_specs=[out_spec, out_spec, pl.BlockSpec((ms, n_cols), lambda i: (0, 0))],
        out_shape=[jax.ShapeDtypeStruct((batch, n_cols, seq), F32),
                   jax.ShapeDtypeStruct((batch, n_cols, seq), BF16),
                   jax.ShapeDtypeStruct((ms, n_cols), F32)],
        scratch_shapes=[pltpu.VMEM((n_cols, k), BF16)],
        compiler_params=_params(("arbitrary",)), name=name)(x, xs, w)


def _ln(x, g, b):
    mu = jnp.mean(x, axis=-1, keepdims=True)
    xc = x - mu
    var = jnp.mean(xc * xc, axis=-1, keepdims=True)
    return xc * lax.rsqrt(var + LN_EPS) * g + b


def _ln_body(x_ref, g_ref, b_ref, o_ref):
    o_ref[...] = _ln(x_ref[...], g_ref[...], b_ref[...])


def _layer_norm(x, g, b, *, tm, name):
    m, d = x.shape
    tm = min(tm, m)
    return pl.pallas_call(
        _ln_body, grid=(m // tm,),
        in_specs=[pl.BlockSpec((tm, d), lambda i: (i, 0)),
                  pl.BlockSpec((1, d), lambda i: (0, 0)),
                  pl.BlockSpec((1, d), lambda i: (0, 0))],
        out_specs=pl.BlockSpec((tm, d), lambda i: (i, 0)),
        out_shape=jax.ShapeDtypeStruct((m, d), F32),
        compiler_params=_params(("arbitrary",)), name=name)(x, g.reshape(1, d), b.reshape(1, d))


def _rel_bucket(n):
    n = jnp.maximum(n, 0)
    nf = jnp.maximum(n, 1).astype(F32)
    large = MAX_EXACT + (jnp.log(nf / MAX_EXACT) / math.log(MAX_DISTANCE / MAX_EXACT)
                         * (NUM_BUCKETS - MAX_EXACT)).astype(jnp.int32)
    large = jnp.minimum(large, NUM_BUCKETS - 1)
    return jnp.where(n < MAX_EXACT, n, large)


def _bucket_lookup(bucket, table_fn):
    out = jnp.zeros(jnp.broadcast_shapes(bucket.shape, table_fn(0).shape), F32)
    for b in range(NUM_BUCKETS):
        out = out + jnp.where(bucket == b, table_fn(b), 0.0)
    return out


def _bias_tile_body(rb_ref, bucket_ref, o_ref):
    h = pl.program_id(0)
    o_ref[...] = _bucket_lookup(bucket_ref[...], lambda bk: rb_ref[bk, h])


def _bias_tiles(rel_bias, t):
    r = jnp.arange(t, dtype=jnp.int32)
    c = jnp.arange(2 * t, dtype=jnp.int32)
    buckets = _rel_bucket(r[:, None] + t - c[None, :])
    return pl.pallas_call(
        _bias_tile_body, grid=(N_HEADS,),
        in_specs=[pl.BlockSpec(memory_space=pltpu.SMEM), pl.BlockSpec((t, 2 * t), lambda h: (0, 0))],
        out_specs=pl.BlockSpec((None, t, 2 * t), lambda h: (h, 0, 0)),
        out_shape=jax.ShapeDtypeStruct((N_HEADS, t, 2 * t), F32),
        compiler_params=_params(("arbitrary",)), name="bias_tiles")(rel_bias, buckets)


def _attn_body(lam_ref, rb_ref, q_ref, k_ref, v_ref, bias_ref, g_ref, o_ref, *scratch,
               t, n_far, near, heads, out_scale):
    s_bufs, p_bufs, m_bufs, mf_bufs, l_bufs, a_bufs = (scratch[0:2], scratch[2:4], scratch[4:6],
                                                       scratch[6:8], scratch[8:10], scratch[10:12])
    hg = pl.program_id(0)
    far = n_far * t
    kb_w = ATTN_KEY_BLOCK
    n_kb = (far + near) // kb_w
    n_tiles = kb_w // LANES

    def fold(x, op):
        out = x[:, 0:LANES]
        for c in range(1, n_tiles):
            out = op(out, x[:, c * LANES:(c + 1) * LANES])
        return out

    def head_cols(hh):
        return slice(hh * V_DIM, (hh + 1) * V_DIM)

    def score_block(hh, kb):
        par = hh % 2
        q = q_ref[:, head_cols(hh)] * (HEAD_DIM ** -0.5)
        lane = lax.broadcasted_iota(jnp.int32, q.shape, 1)
        zero = jnp.zeros_like(q)
        q2 = jnp.concatenate([jnp.where(lane < HEAD_DIM, q, zero), jnp.where(lane >= HEAD_DIM, q, zero)],
                             axis=0)
        cols = slice(kb * kb_w, (kb + 1) * kb_w)
        s = jnp.dot(q2, k_ref[head_cols(hh), cols], preferred_element_type=F32)
        bias_far = rb_ref[NUM_BUCKETS - 1, hg * heads + hh]
        if kb * kb_w < far:
            top = fold(s, jnp.maximum) + bias_far
        else:
            off = kb * kb_w - far
            lo = 2 * t - near + off
            bias = bias_ref[hh, :, lo:lo + kb_w]
            row = lax.broadcasted_iota(jnp.int32, (t, kb_w), 0)
            col = lax.broadcasted_iota(jnp.int32, (t, kb_w), 1)
            keep = col + (off - (near - t)) <= row
            s = jnp.where(jnp.concatenate([keep, keep], axis=0),
                          s + jnp.concatenate([bias, bias], axis=0), NEG_INF)
            top = fold(s, jnp.maximum)
        s_bufs[par][:, cols] = s
        if kb == 0:
            m_bufs[par][...] = top
        else:
            m_bufs[par][...] = jnp.maximum(m_bufs[par][...], top)
        if kb == n_kb - 1:
            m = jnp.broadcast_to(jnp.max(m_bufs[par][...], axis=1, keepdims=True), m_bufs[par].shape)
            m_bufs[par][...] = m
            mf_bufs[par][...] = m - bias_far

    def exp_block(hh, kb):
        par = hh % 2
        cols = slice(kb * kb_w, (kb + 1) * kb_w)
        m = (mf_bufs if kb * kb_w < far else m_bufs)[par][...]
        p = jnp.exp(s_bufs[par][:, cols] - jnp.concatenate([m] * n_tiles, axis=1))
        if kb == 0:
            l_bufs[par][...] = fold(p, jnp.add)
        else:
            l_bufs[par][...] += fold(p, jnp.add)
        p_bufs[par][:, cols] = p.astype(BF16)

    def value_block(hh, kb):
        par = hh % 2
        rows = slice(kb * kb_w, (kb + 1) * kb_w)
        pv = jnp.dot(p_bufs[par][:, rows], v_ref[rows, head_cols(hh)], preferred_element_type=F32)
        if kb == 0:
            a_bufs[par][...] = pv
        else:
            a_bufs[par][...] += pv
        if kb == n_kb - 1:
            nrm = a_bufs[par][...] / jnp.sum(l_bufs[par][...], axis=1, keepdims=True)
            o = nrm[0:t] - lam_ref[0] * nrm[t:2 * t]
            ms = jnp.mean(o * o, axis=-1, keepdims=True)
            o_ref[:, head_cols(hh)] = (o * lax.rsqrt(ms + LN_EPS) * g_ref[...] * out_scale
                                       ).astype(o_ref.dtype)

    for stage in range(heads + 2):
        for kb in range(n_kb):
            if stage < heads:
                score_block(stage, kb)
            if 0 <= stage - 1 < heads:
                exp_block(stage - 1, kb)
            if 0 <= stage - 2 < heads:
                value_block(stage - 2, kb)


def _prompt_attention(q, k, v, rel_bias, lam, subln_g, *, batch, seq, t, out_scale):
    assert t >= MAX_DISTANCE
    nq = seq // t
    width = N_HEADS * V_DIM
    q3 = q.reshape(batch, seq, width)
    v3 = v.reshape(batch, seq, width)
    bias = _bias_tiles(rel_bias, t)
    smem = pl.BlockSpec(memory_space=pltpu.SMEM)
    pieces = []
    for i in range(nq):
        n_far = max(i - 1, 0)
        near = min(i + 1, 2) * t
        keys = n_far * t + near
        hp = ATTN_HEADS_PER_STEP
        body = functools.partial(_attn_body, t=t, n_far=n_far, near=near, heads=hp, out_scale=out_scale)
        pair = lambda shape, dtype: [pltpu.VMEM(shape, dtype)] * 2
        pieces.append(pl.pallas_call(
            body, grid=(N_HEADS // hp, batch),
            in_specs=[smem, smem,
                      pl.BlockSpec((None, t, hp * V_DIM), lambda h, b, i=i: (b, i, h)),
                      pl.BlockSpec((None, hp * V_DIM, keys), lambda h, b: (b, h, 0)),
                      pl.BlockSpec((None, keys, hp * V_DIM), lambda h, b: (b, 0, h)),
                      pl.BlockSpec((hp, t, 2 * t), lambda h, b: (h, 0, 0)),
                      pl.BlockSpec((1, V_DIM), lambda h, b: (0, 0))],
            out_specs=pl.BlockSpec((None, t, hp * V_DIM), lambda h, b: (b, 0, h)),
            out_shape=jax.ShapeDtypeStruct((batch, t, width), BF16),
            scratch_shapes=(pair((2 * t, keys), F32) + pair((2 * t, keys), BF16)
                            + pair((2 * t, LANES), F32) + pair((2 * t, LANES), F32)
                            + pair((2 * t, LANES), F32) + pair((2 * t, V_DIM), F32)),
            compiler_params=_params(("arbitrary", "arbitrary")),
            name=f"prompt_attention_q{i}")(lam, rel_bias, q3, k, v3, bias, subln_g.reshape(1, V_DIM)))
    return jnp.stack(pieces, axis=1).reshape(batch * seq, width)


def _decode_body(pt_ref, lam_ref, q_ref, kn_ref, vn_ref, rbt_ref, bucket_ref, g_ref, *rest,
                 pages, page, n_steps, out_scale):
    k_refs = rest[:pages]
    v_refs = rest[pages:2 * pages]
    o_ref = rest[2 * pages]
    qexp_sc, bias_sc, m_sc, l_sc, acc_sc = rest[2 * pages + 1:]
    step = pl.program_id(1)
    rows = 2 * N_HEADS
    width = N_HEADS * V_DIM

    @pl.when(step == 0)
    def _():
        row = lax.broadcasted_iota(jnp.int32, (rows, width), 0)
        col = lax.broadcasted_iota(jnp.int32, (rows, width), 1)
        own_qk = (col // HEAD_DIM) == (row % N_HEADS) * 2 + row // N_HEADS
        q = (q_ref[...] * (HEAD_DIM ** -0.5)).astype(BF16).astype(F32)
        qexp = jnp.where(own_qk, jnp.broadcast_to(q, (rows, width)), 0.0)
        qexp_sc[...] = qexp.astype(BF16)
        bias_sc[...] = _bucket_lookup(bucket_ref[...], lambda bk: rbt_ref[:, bk:bk + 1])
        kn = kn_ref[...].astype(BF16).astype(F32)
        s_self = jnp.sum(qexp * kn, axis=1, keepdims=True) + rbt_ref[:, 0:1]
        m_sc[...] = s_self
        l_sc[...] = jnp.ones(l_sc.shape, F32)
        vn = vn_ref[...].astype(BF16).astype(F32)
        acc_sc[...] = jnp.broadcast_to(vn[:, None, :], acc_sc.shape)

    qexp = qexp_sc[...]
    bias_far = rbt_ref[:, NUM_BUCKETS - 1:NUM_BUCKETS]
    is_last = step == n_steps - 1
    s_parts = []
    for p in range(pages):
        s = jnp.dot(qexp, k_refs[p][...].astype(BF16), preferred_element_type=F32)
        if p == pages - 1:
            s = s + jnp.where(is_last, bias_sc[...], bias_far)
        else:
            s = s + bias_far
        s_parts.append(s)
    m_old = m_sc[...]
    m_new = m_old
    for s in s_parts:
        m_new = jnp.maximum(m_new, jnp.max(s, axis=1, keepdims=True))
    a = jnp.exp(m_old - m_new)
    l_new = a * l_sc[...]
    probs = []
    for p in range(pages):
        pr = jnp.exp(s_parts[p] - m_new)
        l_new = l_new + jnp.sum(pr, axis=1, keepdims=True)
        probs.append(pr.astype(BF16))
    probs = jnp.concatenate(probs, axis=1)
    m_sc[...] = m_new
    l_sc[...] = l_new
    for h in range(N_HEADS):
        vh = jnp.concatenate([v_refs[p][pl.ds(h, page, stride=N_HEADS), :].astype(BF16)
                              for p in range(pages)], axis=0)
        acc_sc[h] = a * acc_sc[h] + jnp.dot(probs, vh, preferred_element_type=F32)

    @pl.when(is_last)
    def _():
        nrm = acc_sc[...] / l_new[None]
        r = lax.broadcasted_iota(jnp.int32, nrm.shape, 1)
        hh = lax.broadcasted_iota(jnp.int32, nrm.shape, 0)
        coef = jnp.where(r == hh, 1.0, jnp.where(r == hh + N_HEADS, -lam_ref[0], 0.0))
        d = jnp.sum(coef * nrm, axis=1)
        ms = jnp.mean(d * d, axis=-1, keepdims=True)
        o_ref[...] = (d * lax.rsqrt(ms + LN_EPS) * g_ref[...] * out_scale).astype(o_ref.dtype)


def _sample_attention(q, k_new, v_new, cache_k, cache_v, page_table, rel_bias, lam, subln_g,
                      *, pages, out_scale):
    bs, width = q.shape
    page = cache_k.shape[2]
    n_pages = page_table.shape[1]
    n_steps = n_pages // pages
    past = n_pages * page
    kpos = past - page + jnp.arange(page, dtype=jnp.int32)
    bucket_last = _rel_bucket(past - kpos).reshape(1, page)
    rbt = jnp.tile(rel_bias.T, (2, 1))

    def tok_spec():
        return pl.BlockSpec((None, 1, width), lambda b, s, pt: (b, 0, 0))

    def page_spec(shape, p):
        return pl.BlockSpec((None,) + shape, lambda b, s, pt, p=p: (pt[b, s * pages + p], 0, 0))

    full = lambda shape: pl.BlockSpec(shape, lambda b, s, pt: tuple(0 for _ in shape))
    head_spec = pl.BlockSpec((None, N_HEADS, V_DIM), lambda b, s, pt: (b, 0, 0))
    in_specs = ([pl.BlockSpec(memory_space=pltpu.SMEM), tok_spec(), tok_spec(), head_spec,
                 full((2 * N_HEADS, NUM_BUCKETS)), full((1, page)), full((1, V_DIM))]
                + [page_spec((width, page), p) for p in range(pages)]
                + [page_spec((page * N_HEADS, V_DIM), p) for p in range(pages)])
    rows = 2 * N_HEADS
    body = functools.partial(_decode_body, pages=pages, page=page, n_steps=n_steps,
                             out_scale=out_scale)
    out = pl.pallas_call(
        body,
        grid_spec=pltpu.PrefetchScalarGridSpec(
            num_scalar_prefetch=1, grid=(bs, n_steps), in_specs=in_specs,
            out_specs=head_spec,
            scratch_shapes=[pltpu.VMEM((rows, width), BF16), pltpu.VMEM((rows, page), F32),
                            pltpu.VMEM((rows, 1), F32), pltpu.VMEM((rows, 1), F32),
                            pltpu.VMEM((N_HEADS, rows, V_DIM), F32)]),
        out_shape=jax.ShapeDtypeStruct((bs, N_HEADS, V_DIM), BF16),
        compiler_params=_params(("arbitrary", "arbitrary")),
        name="sample_attention")(
            page_table, lam, q.reshape(bs, 1, width), k_new.reshape(bs, 1, width),
            v_new.reshape(bs, N_HEADS, V_DIM), rbt, bucket_last, subln_g.reshape(1, V_DIM),
            *([cache_k] * pages), *([cache_v] * pages))
    return out.reshape(bs, width)


def _s5_params(a_re, a_im, log_dt, b_re, b_im, c_re, c_im):
    dt = jnp.exp(log_dt)[:, None]
    mag = jnp.exp(a_re * dt)
    ang = a_im * dt
    abar_re = mag * jnp.cos(ang)
    abar_im = mag * jnp.sin(ang)
    den = a_re * a_re + a_im * a_im
    f_re = ((abar_re - 1.0) * a_re + abar_im * a_im) / den
    f_im = (abar_im * a_re - (abar_re - 1.0) * a_im) / den
    bb_re = f_re[..., None] * b_re - f_im[..., None] * b_im
    bb_im = f_re[..., None] * b_im + f_im[..., None] * b_re
    eye = jnp.eye(SLAB_GROUPS, dtype=F32)

    def in_slabs(bb):
        tt = bb.reshape(N_SLABS, SLAB_GROUPS, STATE_DIM, SSM_GROUP).transpose(0, 1, 3, 2)
        full = tt[:, :, :, None, :] * eye[None, :, None, :, None]
        return full.reshape(N_SLABS, SLAB_IN, SLAB_STATE).astype(BF16)

    def out_slabs(cc):
        tt = cc.reshape(N_SLABS, SLAB_GROUPS, SSM_GROUP, STATE_DIM).transpose(0, 1, 3, 2)
        full = tt[:, :, :, None, :] * eye[None, :, None, :, None]
        return full.reshape(N_SLABS, SLAB_STATE, SLAB_IN).astype(BF16)

    return (abar_re.reshape(N_SLABS, SLAB_STATE), abar_im.reshape(N_SLABS, SLAB_STATE),
            in_slabs(bb_re), in_slabs(bb_im), out_slabs(c_re), out_slabs(c_im))


def _glu_out(y, u, d_ref, wglu_ref):
    g = jax.nn.gelu(y + d_ref[...] * u)
    gate = jnp.dot(g.astype(BF16), wglu_ref[...].astype(BF16), preferred_element_type=F32)
    return g * jax.nn.sigmoid(gate)


def _s5_prompt_body(u_ref, ar_ref, ai_ref, bbr_ref, bbi_ref, cr_ref, ci_ref, d_ref, wglu_ref,
                    o_ref, sre_ref, sim_ref, xr_sc, xi_sc, pw_sc, cr_sc, ci_sc, perm_sc, *, batch):
    c = pl.program_id(0)
    sub = SUBLANES
    _, tc, width = u_ref.shape
    n_rows = batch * tc
    t = lax.broadcasted_iota(jnp.int32, (sub, SLAB_STATE), 0)
    first = t < batch

    def cmul(pr, pi, qr, qi):
        return pr * qr - pi * qi, pr * qi + pi * qr

    @pl.when(c == 0)
    def _():
        for j in range(N_SLABS):
            a = (jnp.broadcast_to(ar_ref[j:j + 1, :], t.shape), jnp.broadcast_to(ai_ref[j:j + 1, :], t.shape))
            a2 = cmul(*a, *a)
            for part in range(2):
                pw_sc[j, part] = jnp.where(first, 0.0, a[part])
                pw_sc[j, 2 + part] = jnp.where(first, a[part], a2[part])
        cr_sc[...] = jnp.zeros(cr_sc.shape, F32)
        ci_sc[...] = jnp.zeros(ci_sc.shape, F32)
        r = lax.broadcasted_iota(jnp.int32, (n_rows, n_rows), 0)
        k = lax.broadcasted_iota(jnp.int32, (n_rows, n_rows), 1)
        perm_sc[0] = jnp.where(k == (r % batch) * tc + r // batch, 1.0, 0.0).astype(BF16)
        perm_sc[1] = jnp.where(r == (k % batch) * tc + k // batch, 1.0, 0.0).astype(BF16)

    def permute(which, x):
        return jnp.dot(perm_sc[which], x, preferred_element_type=F32)

    u_seq = u_ref[...].reshape(n_rows, width)
    hi = u_seq.astype(BF16)
    rest = u_seq - hi.astype(F32)
    mid = rest.astype(BF16)
    lo = (rest - mid.astype(F32)).astype(BF16)
    u_hi = permute(0, hi)
    ub = u_hi.astype(BF16)
    u = u_hi + permute(0, mid) + permute(0, lo)
    for j in range(N_SLABS):
        uj = ub[:, j * SLAB_IN:(j + 1) * SLAB_IN]
        xr_sc[j] = jnp.dot(uj, bbr_ref[j], preferred_element_type=F32)
        xi_sc[j] = jnp.dot(uj, bbi_ref[j], preferred_element_type=F32)

    for j in range(N_SLABS):
        pr, pi = cr_sc[j], ci_sc[j]
        for v in range(n_rows // sub):
            rows = slice(v * sub, (v + 1) * sub)
            br = xr_sc[j, rows, :]
            bi = xi_sc[j, rows, :]
            lr = jnp.where(first, pltpu.roll(pr, batch, 0), pr)
            li = jnp.where(first, pltpu.roll(pi, batch, 0), pi)
            dr, di = cmul(pw_sc[j, 0], pw_sc[j, 1], pltpu.roll(br, batch, 0), pltpu.roll(bi, batch, 0))
            er, ei = cmul(pw_sc[j, 2], pw_sc[j, 3], lr, li)
            pr, pi = br + dr + er, bi + di + ei
            xr_sc[j, rows, :] = pr
            xi_sc[j, rows, :] = pi
        cr_sc[j] = pr
        ci_sc[j] = pi
    sre_ref[...] = cr_sc[...]
    sim_ref[...] = ci_sc[...]

    ys = []
    for j in range(N_SLABS):
        ys.append(jnp.dot(xr_sc[j].astype(BF16), cr_ref[j], preferred_element_type=F32)
                  - jnp.dot(xi_sc[j].astype(BF16), ci_ref[j], preferred_element_type=F32))
    out = _glu_out(jnp.concatenate(ys, axis=1), u, d_ref, wglu_ref).astype(BF16)
    o_ref[...] = permute(1, out).astype(o_ref.dtype).reshape(batch, tc, width)


def _s5_prompt(u, sp, d, w_glu, *, batch, seq, tc):
    assert SUBLANES == 2 * batch, "tile = two positions of every sequence"
    ar, ai, bbr, bbi, cr, ci = sp
    width = u.shape[1]
    rows = batch * tc
    const = lambda shape: pl.BlockSpec(shape, lambda c: tuple(0 for _ in shape))
    block = pl.BlockSpec((batch, tc, width), lambda c: (0, c, 0))
    state_spec = const((N_SLABS, SUBLANES, SLAB_STATE))
    state_shape = jax.ShapeDtypeStruct((N_SLABS, SUBLANES, SLAB_STATE), F32)
    out, s_re, s_im = pl.pallas_call(
        functools.partial(_s5_prompt_body, batch=batch), grid=(seq // tc,),
        in_specs=[block, const(ar.shape), const(ai.shape), const(bbr.shape), const(bbi.shape),
                  const(cr.shape), const(ci.shape), const((1, width)), const(w_glu.shape)],
        out_specs=[block, state_spec, state_spec],
        out_shape=[jax.ShapeDtypeStruct((batch, seq, width), BF16), state_shape, state_shape],
        scratch_shapes=[pltpu.VMEM((N_SLABS, rows, SLAB_STATE), F32),
                        pltpu.VMEM((N_SLABS, rows, SLAB_STATE), F32),
                        pltpu.VMEM((N_SLABS, 4, SUBLANES, SLAB_STATE), F32),
                        pltpu.VMEM((N_SLABS, SUBLANES, SLAB_STATE), F32),
                        pltpu.VMEM((N_SLABS, SUBLANES, SLAB_STATE), F32),
                        pltpu.VMEM((2, rows, rows), BF16)],
        compiler_params=_params(("arbitrary",)),
        name="s5_prompt")(u.reshape(batch, seq, width), ar, ai, bbr, bbi, cr, ci,
                          d.reshape(1, width), w_glu)
    return out.reshape(batch * seq, width), s_re, s_im


def _s5_sample_body(u_ref, x0r_ref, x0i_ref, ar_ref, ai_ref, bbr_ref, bbi_ref, cr_ref, ci_ref,
                    d_ref, wglu_ref, o_ref, sre_ref, sim_ref):
    u = u_ref[...]
    ub = u.astype(BF16)
    ys = []
    for j in range(N_SLABS):
        uj = ub[:, j * SLAB_IN:(j + 1) * SLAB_IN]
        ar = ar_ref[j:j + 1, :]
        ai = ai_ref[j:j + 1, :]
        x0r = x0r_ref[j]
        x0i = x0i_ref[j]
        xr = ar * x0r - ai * x0i + jnp.dot(uj, bbr_ref[j], preferred_element_type=F32)
        xi = ar * x0i + ai * x0r + jnp.dot(uj, bbi_ref[j], preferred_element_type=F32)
        sre_ref[j] = xr
        sim_ref[j] = xi
        ys.append(jnp.dot(xr.astype(BF16), cr_ref[j], preferred_element_type=F32)
                  - jnp.dot(xi.astype(BF16), ci_ref[j], preferred_element_type=F32))
    o_ref[...] = _glu_out(jnp.concatenate(ys, axis=1), u, d_ref, wglu_ref).astype(o_ref.dtype)


def _s5_sample(u, x0_re, x0_im, sp, d, w_glu):
    ar, ai, bbr, bbi, cr, ci = sp
    bs, width = u.shape
    state_shape = jax.ShapeDtypeStruct((N_SLABS, bs, SLAB_STATE), F32)
    return pl.pallas_call(
        _s5_sample_body,
        out_shape=[jax.ShapeDtypeStruct((bs, width), BF16), state_shape, state_shape],
        compiler_params=pltpu.CompilerParams(vmem_limit_bytes=VMEM_LIMIT_BYTES),
        name="s5_sample")(u, x0_re, x0_im, ar, ai, bbr, bbi, cr, ci, d.reshape(1, width), w_glu)


def _gated(c):
    gate = c[:, :LANES]
    return gate * jax.nn.sigmoid(gate) * c[:, LANES:]


def _up_prompt_body(h_ref, wg_ref, wv_ref, cwg_ref, cwv_ref, cbg_ref, cbv_ref,
                    act_ref, cg_ref, cv_ref, w_sc, *, rows):
    w_sc[:, 0:LANES] = wg_ref[...].astype(BF16)
    w_sc[:, LANES:2 * LANES] = wv_ref[...].astype(BF16)
    cw = jnp.concatenate([cwg_ref[...], cwv_ref[...]], axis=1)
    cb = jnp.concatenate([cbg_ref[...], cbv_ref[...]], axis=1)
    seq = h_ref.shape[0]
    sub = SUBLANES
    row8 = lax.broadcasted_iota(jnp.int32, (sub, 2 * LANES), 0)

    prev = jnp.zeros((sub, 2 * LANES), F32)
    for c in range(seq // rows):
        up = jnp.dot(h_ref[c * rows:(c + 1) * rows, :], w_sc[...], preferred_element_type=F32)
        cur = cb + cw[CONV_W - 1:CONV_W] * up
        for back in range(1, CONV_W):
            rolled = pltpu.roll(up, back, 0)
            head = jnp.where(row8 < back, pltpu.roll(prev, back, 0), rolled[0:sub])
            shifted = jnp.concatenate([head, rolled[sub:]], axis=0)
            cur = cur + cw[CONV_W - 1 - back:CONV_W - back] * shifted
        act_ref[c * rows:(c + 1) * rows, :] = _gated(cur).astype(act_ref.dtype)
        prev = up[rows - sub:, :]
    tail = prev[sub - (CONV_W - 1):, :]
    cg_ref[...] = tail[:, :LANES]
    cv_ref[...] = tail[:, LANES:]


def _up_prompt(hb, w_up, conv_w, conv_b, *, batch, seq):
    d_model = hb.shape[1]
    d_ff = w_up.shape[1] // 2
    nb = d_ff // LANES
    cb = conv_b.reshape(1, 2 * d_ff)
    col = lambda off: (lambda b, j: (0, j + off))
    tail_spec = pl.BlockSpec((None, CONV_W - 1, LANES), lambda b, j: (b, 0, j))
    tail_shape = jax.ShapeDtypeStruct((batch, CONV_W - 1, d_ff), F32)
    return pl.pallas_call(
        functools.partial(_up_prompt_body, rows=UP_ROW_CHUNK), grid=(batch, nb),
        scratch_shapes=[pltpu.VMEM((d_model, 2 * LANES), BF16)],
        in_specs=[pl.BlockSpec((seq, d_model), lambda b, j: (b, 0)),
                  pl.BlockSpec((d_model, LANES), col(0)), pl.BlockSpec((d_model, LANES), col(nb)),
                  pl.BlockSpec((CONV_W, LANES), col(0)), pl.BlockSpec((CONV_W, LANES), col(nb)),
                  pl.BlockSpec((1, LANES), col(0)), pl.BlockSpec((1, LANES), col(nb))],
        out_specs=[pl.BlockSpec((seq, LANES), lambda b, j: (b, j)), tail_spec, tail_spec],
        out_shape=[jax.ShapeDtypeStruct((batch * seq, d_ff), BF16), tail_shape, tail_shape],
        compiler_params=_params(("arbitrary", "arbitrary")),
        name="up_prompt")(hb, w_up, w_up, conv_w, conv_w, cb, cb)


def _up_sample_body(h_ref, wg_ref, wv_ref, cwg_ref, cwv_ref, cbg_ref, cbv_ref, sg_ref, sv_ref,
                    act_ref, cg_ref, cv_ref):
    w = jnp.concatenate([wg_ref[...].astype(BF16), wv_ref[...].astype(BF16)], axis=1)
    up = jnp.dot(h_ref[...], w, preferred_element_type=F32)
    cw = jnp.concatenate([cwg_ref[...], cwv_ref[...]], axis=1)
    cb = jnp.concatenate([cbg_ref[...], cbv_ref[...]], axis=1)
    c = cb + cw[CONV_W - 1:CONV_W] * up
    for tap in range(CONV_W - 1):
        st = jnp.concatenate([sg_ref[tap], sv_ref[tap]], axis=1)
        c = c + cw[tap:tap + 1] * st
    act_ref[...] = _gated(c).astype(act_ref.dtype)
    for tap in range(1, CONV_W - 1):
        cg_ref[tap - 1] = sg_ref[tap]
        cv_ref[tap - 1] = sv_ref[tap]
    cg_ref[CONV_W - 2] = up[:, :LANES]
    cv_ref[CONV_W - 2] = up[:, LANES:]


def _up_sample(hb, w_up, conv_w, conv_b, state):
    bs, d_model = hb.shape
    d_ff = w_up.shape[1] // 2
    nb = d_ff // LANES
    cb = conv_b.reshape(1, 2 * d_ff)
    col = lambda off: (lambda j: (0, j + off))
    st = lambda off: pl.BlockSpec((CONV_W - 1, bs, LANES), lambda j, off=off: (0, 0, j + off))
    tail_spec = pl.BlockSpec((CONV_W - 1, bs, LANES), lambda j: (0, 0, j))
    tail_shape = jax.ShapeDtypeStruct((CONV_W - 1, bs, d_ff), F32)
    return pl.pallas_call(
        _up_sample_body, grid=(nb,),
        in_specs=[pl.BlockSpec((bs, d_model), lambda j: (0, 0)),
                  pl.BlockSpec((d_model, LANES), col(0)), pl.BlockSpec((d_model, LANES), col(nb)),
                  pl.BlockSpec((CONV_W, LANES), col(0)), pl.BlockSpec((CONV_W, LANES), col(nb)),
                  pl.BlockSpec((1, LANES), col(0)), pl.BlockSpec((1, LANES), col(nb)),
                  st(0), st(nb)],
        out_specs=[pl.BlockSpec((bs, LANES), lambda j: (0, j)), tail_spec, tail_spec],
        out_shape=[jax.ShapeDtypeStruct((bs, d_ff), BF16), tail_shape, tail_shape],
        compiler_params=_params(("arbitrary",)),
        name="up_sample")(hb, w_up, w_up, conv_w, conv_w, cb, cb, state, state)


def _one(x):
    return (x,)


def _sigmoid_out(acc):
    return (jax.nn.sigmoid(acc),)


def _both(acc):
    return (acc, acc)


def _merge(pa, ps, ga, gs):
    return (ga * pa + gs * ps,)


def _layer(xp, xs, w, *, batch, seq, alpha, attend_p, attend_s, ssm_p, ssm_s, up_p, up_s):
    d_model = xp.shape[1]
    qk_w = N_HEADS * 2 * HEAD_DIM
    v_w = N_HEADS * V_DIM
    ssm_w = d_model // 2
    xb = xp.astype(BF16)
    w_in = w["w_in"]
    g1, b1 = w["ln1_g"].reshape(1, d_model), w["ln1_b"].reshape(1, d_model)
    big = dict(tm=MM_ROWS, tn=MM_COLS)
    c = 0
    (q,), (q_s,) = _matmul([(xb, xs, w_in, c)], [], [BF16], _one, n_cols=qk_w, name="proj_q", **big)
    c += qk_w
    kt, ktb, k_s = _proj_transposed(xb, xs, w_in, c, n_cols=qk_w, batch=batch, seq=seq, tm=MM_ROWS,
                                    name="proj_kt")
    c += qk_w
    (v, vb), (v_s, _) = _matmul([(xb, xs, w_in, c)], [], [F32, BF16], _both, n_cols=v_w, name="proj_v", **big)
    c += v_w
    (u,), (u_s,) = _matmul([(xb, xs, w_in, c)], [], [F32], _one, n_cols=ssm_w, name="proj_u", **big)
    c += ssm_w
    (ga,), (ga_s,) = _matmul([(xb, xs, w_in, c)], [], [F32], _sigmoid_out, n_cols=d_model, name="gate_a", **big)
    c += d_model
    (gs,), (gs_s,) = _matmul([(xb, xs, w_in, c)], [], [F32], _sigmoid_out, n_cols=d_model, name="gate_s", **big)

    attn = attend_p(q, ktb, vb)
    attn_s = attend_s(q_s, k_s, v_s)
    ssm_out, re_p, im_p = ssm_p(u)
    ssm_out_s, re_s, im_s = ssm_s(u_s)

    (merged,), (merged_s,) = _matmul(
        [(attn, attn_s, w["w_proj_attn"], 0), (ssm_out, ssm_out_s, w["w_proj_ssm"], 0)],
        [(ga, ga_s), (gs, gs_s)], [BF16], _merge, n_cols=d_model, name="merge", **big)

    def post_ln1(acc, res, g, b):
        h = _ln(alpha * res + acc, g, b)
        return h, h

    (h, hb), (h_s, hb_s) = _matmul([(merged, merged_s, w["w_out"], 0)], [(xp, xs), g1, b1], [F32, BF16],
                                   post_ln1, n_cols=d_model, tm=LN_MM_ROWS, tn=d_model, name="out_proj_ln1")
    act, conv_p = up_p(hb)
    act_s, conv_s = up_s(hb_s)
    (r2,), (r2_s,) = _matmul([(act, act_s, w["w_down"], 0)], [(h, h_s)], [F32],
                             lambda acc, res: (alpha * res + acc,), n_cols=d_model,
                             tm=DOWN_ROWS, tn=DOWN_COLS, name="down_proj")
    y = _layer_norm(r2, w["ln2_g"], w["ln2_b"], tm=LN_ROWS, name="ln2")
    y_s = _layer_norm(r2_s, w["ln2_g"], w["ln2_b"], tm=LN_ROWS, name="ln2_sample")
    return (y, kt, v, re_p, im_p, conv_p), (y_s, k_s, v_s, re_s, im_s, conv_s)


def kernel(x_prompt, x_sample, cache_k, cache_v, state_ssm_re, state_ssm_im, state_conv, page_table, rel_bias, w_in, lambda_q1, lambda_k1, lambda_q2, lambda_k2, subln_g, ssm_a_re, ssm_a_im, ssm_log_dt, ssm_b_re, ssm_b_im, ssm_c_re, ssm_c_im, ssm_d, w_glu, w_proj_attn, w_proj_ssm, w_out, ln1_g, ln1_b, w_up, conv_w, conv_b, w_down, ln2_g, ln2_b):
    depth = w_in.shape[0]
    assert depth == 1, "single-layer trunk"
    bp, seq, d_model = x_prompt.shape
    bs, dec_seq, _ = x_sample.shape
    assert dec_seq == 1
    n_pool, page = cache_k.shape[1], cache_k.shape[2]
    d_ff = w_down.shape[1]
    n_groups = ssm_a_re.shape[1]
    assert n_groups == N_SLABS * SLAB_GROUPS and d_ff % LANES == 0
    alpha = (2.0 * depth) ** 0.25
    width = N_HEADS * V_DIM

    hp = x_prompt.reshape(bp * seq, d_model)
    hs = x_sample.reshape(bs, d_model)
    outs = {}
    for l in range(depth):
        lam_init = 0.8 - 0.6 * math.exp(-0.3 * l)
        out_scale = 1.0 - lam_init
        lam = (jnp.exp(jnp.sum(lambda_q1[l] * lambda_k1[l]))
               - jnp.exp(jnp.sum(lambda_q2[l] * lambda_k2[l])) + lam_init).reshape(1)
        w = dict(w_in=w_in[l], w_proj_attn=w_proj_attn[l], w_proj_ssm=w_proj_ssm[l], w_out=w_out[l],
                 ln1_g=ln1_g[l], ln1_b=ln1_b[l], w_down=w_down[l], ln2_g=ln2_g[l], ln2_b=ln2_b[l])
        sp = _s5_params(ssm_a_re[l], ssm_a_im[l], ssm_log_dt[l], ssm_b_re[l], ssm_b_im[l],
                        ssm_c_re[l], ssm_c_im[l])

        def attend_p(q, kt, v):
            return _prompt_attention(q, kt, v, rel_bias, lam, subln_g[l], batch=bp, seq=seq,
                                     t=ATTN_BLOCK, out_scale=out_scale)

        def ssm_p(u):
            o, sr, si = _s5_prompt(u, sp, ssm_d[l], w_glu[l], batch=bp, seq=seq, tc=S5_POSITIONS)
            last = lambda st: st[:, SUBLANES - bp:, :].transpose(1, 0, 2)
            return o, last(sr), last(si)

        def up_p(hb):
            act, cg, cv = _up_prompt(hb, w_up[l], conv_w[l], conv_b[l], batch=bp, seq=seq)
            return act, jnp.concatenate([cg, cv], axis=-1)

        ck = cache_k[l].transpose(0, 2, 3, 4, 1).reshape(n_pool, width, page)
        cv_ = cache_v[l].reshape(n_pool, page * N_HEADS, V_DIM)

        def attend_s(q, k, v):
            return _sample_attention(q.astype(F32), k, v, ck, cv_, page_table, rel_bias, lam,
                                     subln_g[l], pages=DECODE_PAGES, out_scale=out_scale)

        def ssm_s(u):
            x0r = state_ssm_re[l].reshape(bs, N_SLABS, SLAB_STATE).transpose(1, 0, 2)
            x0i = state_ssm_im[l].reshape(bs, N_SLABS, SLAB_STATE).transpose(1, 0, 2)
            o, sr, si = _s5_sample(u, x0r, x0i, sp, ssm_d[l], w_glu[l])
            return o, sr.transpose(1, 0, 2), si.transpose(1, 0, 2)

        def up_s(hb):
            act, cg, cv = _up_sample(hb, w_up[l], conv_w[l], conv_b[l],
                                     state_conv[l].transpose(1, 0, 2))
            return act, jnp.concatenate([cg, cv], axis=-1).transpose(1, 0, 2)

        (hp, kt_p, v_p, re_p, im_p, c_p), (hs, k_s, v_s, re_s, im_s, c_s) = _layer(
            hp, hs, w, batch=bp, seq=seq, alpha=alpha, attend_p=attend_p, attend_s=attend_s,
            ssm_p=ssm_p, ssm_s=ssm_s, up_p=up_p, up_s=up_s)
        k_p = kt_p.reshape(bp, N_HEADS, 2, HEAD_DIM, seq).transpose(0, 4, 1, 2, 3)

        for name, val in (("kp", k_p.reshape(bp, seq, N_HEADS, 2, HEAD_DIM)),
                          ("vp", v_p.reshape(bp, seq, N_HEADS, V_DIM)),
                          ("rep", re_p.reshape(bp, n_groups, STATE_DIM)),
                          ("imp", im_p.reshape(bp, n_groups, STATE_DIM)),
                          ("cp", c_p),
                          ("ks", k_s.reshape(bs, 1, N_HEADS, 2, HEAD_DIM)),
                          ("vs", v_s.reshape(bs, 1, N_HEADS, V_DIM)),
                          ("res", re_s.reshape(bs, n_groups, STATE_DIM)),
                          ("ims", im_s.reshape(bs, n_groups, STATE_DIM)),
                          ("cs", c_s)):
            outs.setdefault(name, []).append(val)

    st = {k: jnp.stack(v, axis=0) for k, v in outs.items()}
    return (hp.reshape(bp, seq, d_model), hs.reshape(bs, 1, d_model), st["kp"], st["vp"], st["rep"],
            st["imp"], st["cp"], st["ks"], st["vs"], st["res"], st["ims"], st["cs"])
```

```python
import functools
import math

import jax
import jax.numpy as jnp
from jax import lax
from jax.experimental import pallas as pl
from jax.experimental.pallas import tpu as pltpu

F32 = jnp.float32
BF16 = jnp.bfloat16

N_HEADS = 8
HEAD_DIM = 64
V_DIM = 2 * HEAD_DIM
SSM_GROUP = 16
STATE_DIM = 64
CONV_W = 3
NUM_BUCKETS = 32
MAX_EXACT = NUM_BUCKETS // 2
MAX_DISTANCE = 128
LN_EPS = 1e-5
NEG_INF = -1e30

VMEM_LIMIT_BYTES = 56 * 1024 * 1024
LANES = 128
SLAB_GROUPS = 8
N_SLABS = 8
SLAB_IN = SLAB_GROUPS * SSM_GROUP
SLAB_STATE = SLAB_GROUPS * STATE_DIM
SUBLANES = 8
MM_ROWS, MM_COLS = 1024, 1024
LN_MM_ROWS = 512
DOWN_ROWS, DOWN_COLS = 512, 512
LN_ROWS = 512
ATTN_BLOCK = 256
ATTN_KEY_BLOCK = 256
ATTN_HEADS_PER_STEP = 4
S5_POSITIONS = 64
DECODE_PAGES = 16
UP_ROW_CHUNK = 512


def _params(sem):
    return pltpu.CompilerParams(dimension_semantics=sem, vmem_limit_bytes=VMEM_LIMIT_BYTES)


def _mm_body(*refs, n_pairs, n_extra, n_out, epilogue, emit_lhs):
    it = iter(refs)
    take = lambda n: [next(it) for _ in range(n)]
    x_refs, xs_refs, w_refs = take(n_pairs), take(n_pairs), take(n_pairs)
    extra, extra_s = take(n_extra), take(n_extra)
    outs, outs_s = take(n_out), take(n_out)
    lhs_out = take(1) if emit_lhs else []
    wbf = take(n_pairs)

    def apply(lhs_refs, extra_refs, out_refs, keep):
        lhs = [x[...].astype(BF16) for x in lhs_refs]
        for ref in keep:
            ref[...] = lhs[0]
        accs = [jnp.dot(x, s[...], preferred_element_type=F32) for x, s in zip(lhs, wbf)]
        for o, r in zip(out_refs, epilogue(*accs, *[e[...] for e in extra_refs])):
            o[...] = r.astype(o.dtype)

    @pl.when(pl.program_id(1) == 0)
    def _():
        for w, s in zip(w_refs, wbf):
            s[...] = w[...].astype(BF16)
        apply(xs_refs, extra_s, outs_s, [])

    apply(x_refs, extra, outs, lhs_out)


def _matmul(pairs, extras, out_dtypes, epilogue, *, n_cols, tm, tn, name, emit_lhs=False):
    m = pairs[0][0].shape[0]
    ms = pairs[0][1].shape[0]
    grid = (n_cols // tn, m // tm)
    assert not emit_lhs or grid[0] == 1
    w_mode = dict(pipeline_mode=pl.Buffered(1)) if grid[0] == 1 else {}
    specs_x, specs_xs, specs_w, scratch = [], [], [], []
    for x, xs, w, c0 in pairs:
        k = x.shape[1]
        specs_x.append(pl.BlockSpec((tm, k), lambda j, i: (i, 0)))
        specs_xs.append(pl.BlockSpec((ms, k), lambda j, i: (0, 0)))
        specs_w.append(pl.BlockSpec((k, tn), lambda j, i, off=c0 // tn: (0, j + off), **w_mode))
        scratch.append(pltpu.VMEM((k, tn), BF16))
    specs_e, specs_es, args_e, args_es = [], [], [], []
    for e in extras:
        if isinstance(e, tuple):
            specs_e.append(pl.BlockSpec((tm, tn), lambda j, i: (i, j)))
            specs_es.append(pl.BlockSpec((ms, tn), lambda j, i: (0, j)))
            args_e.append(e[0])
            args_es.append(e[1])
        else:
            specs_e.append(pl.BlockSpec((1, tn), lambda j, i: (0, j)))
            specs_es.append(pl.BlockSpec((1, tn), lambda j, i: (0, j)))
            args_e.append(e)
            args_es.append(e)
    out_shape = ([jax.ShapeDtypeStruct((m, n_cols), d) for d in out_dtypes]
                 + [jax.ShapeDtypeStruct((ms, n_cols), d) for d in out_dtypes])
    out_specs = ([pl.BlockSpec((tm, tn), lambda j, i: (i, j)) for _ in out_dtypes]
                 + [pl.BlockSpec((ms, tn), lambda j, i: (0, j)) for _ in out_dtypes])
    if emit_lhs:
        k0 = pairs[0][0].shape[1]
        out_shape.append(jax.ShapeDtypeStruct((m, k0), BF16))
        out_specs.append(pl.BlockSpec((tm, k0), lambda j, i: (i, 0)))
    body = functools.partial(_mm_body, n_pairs=len(pairs), n_extra=len(extras),
                             n_out=len(out_dtypes), epilogue=epilogue, emit_lhs=emit_lhs)
    outs = pl.pallas_call(
        body, grid=grid, in_specs=specs_x + specs_xs + specs_w + specs_e + specs_es,
        out_specs=out_specs, out_shape=out_shape, scratch_shapes=scratch,
        compiler_params=_params(("arbitrary", "arbitrary")), name=name)(
            *[p[0] for p in pairs], *[p[1] for p in pairs], *[p[2] for p in pairs],
            *args_e, *args_es)
    n = len(out_dtypes)
    if emit_lhs:
        return outs[:n], outs[n:2 * n], outs[2 * n]
    return outs[:n], outs[n:]


def _proj_t_body(x_ref, xs_ref, w_ref, o_ref, ob_ref, os_ref, wt_sc):
    @pl.when(pl.program_id(0) == 0)
    def _():
        for c in range(w_ref.shape[1] // LANES):
            cols = slice(c * LANES, (c + 1) * LANES)
            wt_sc[cols, :] = w_ref[:, cols].T.astype(BF16)
        os_ref[...] = lax.dot_general(xs_ref[...].astype(BF16), wt_sc[...], (((1,), (1,)), ((), ())),
                                      preferred_element_type=F32)

    kt = lax.dot_general(wt_sc[...], x_ref[...], (((1,), (1,)), ((), ())),
                         preferred_element_type=F32)
    o_ref[...] = kt
    ob_ref[...] = kt.astype(BF16)


def _proj_transposed(x, xs, w, c0, *, n_cols, batch, seq, tm, name):
    k = x.shape[1]
    ms = xs.shape[0]
    per_b = seq // tm
    out_spec = pl.BlockSpec((None, n_cols, tm), lambda i: (i // per_b, 0, i % per_b))
    return pl.pallas_call(
        _proj_t_body, grid=(batch * per_b,),
        in_specs=[pl.BlockSpec((tm, k), lambda i: (i, 0)),
                  pl.BlockSpec((ms, k), lambda i: (0, 0)),
                  pl.BlockSpec((k, n_cols), lambda i, off=c0 // n_cols: (0, off),
                               pipeline_mode=pl.Buffered(1))],
        out_specs=[out_spec, out_spec, pl.BlockSpec((ms, n_cols), lambda i: (0, 0))],
        out_shape=[jax.ShapeDtypeStruct((batch, n_cols, seq), F32),
                   jax.ShapeDtypeStruct((batch, n_cols, seq), BF16),
                   jax.ShapeDtypeStruct((ms, n_cols), F32)],
        scratch_shapes=[pltpu.VMEM((n_cols, k), BF16)],
        compiler_params=_params(("arbitrary",)), name=name)(x, xs, w)


def _ln(x, g, b):
    mu = jnp.mean(x, axis=-1, keepdims=True)
    xc = x - mu
    var = jnp.mean(xc * xc, axis=-1, keepdims=True)
    return xc * lax.rsqrt(var + LN_EPS) * g + b


def _ln_body(x_ref, g_ref, b_ref, o_ref):
    o_ref[...] = _ln(x_ref[...], g_ref[...], b_ref[...])


def _layer_norm(x, g, b, *, tm, name):
    m, d = x.shape
    tm = min(tm, m)
    return pl.pallas_call(
        _ln_body, grid=(m // tm,),
        in_specs=[pl.BlockSpec((tm, d), lambda i: (i, 0)),
                  pl.BlockSpec((1, d), lambda i: (0, 0)),
                  pl.BlockSpec((1, d), lambda i: (0, 0))],
        out_specs=pl.BlockSpec((tm, d), lambda i: (i, 0)),
        out_shape=jax.ShapeDtypeStruct((m, d), F32),
        compiler_params=_params(("arbitrary",)), name=name)(x, g.reshape(1, d), b.reshape(1, d))


def _rel_bucket(n):
    n = jnp.maximum(n, 0)
    nf = jnp.maximum(n, 1).astype(F32)
    large = MAX_EXACT + (jnp.log(nf / MAX_EXACT) / math.log(MAX_DISTANCE / MAX_EXACT)
                         * (NUM_BUCKETS - MAX_EXACT)).astype(jnp.int32)
    large = jnp.minimum(large, NUM_BUCKETS - 1)
    return jnp.where(n < MAX_EXACT, n, large)


def _bucket_lookup(bucket, table_fn):
    out = jnp.zeros(jnp.broadcast_shapes(bucket.shape, table_fn(0).shape), F32)
    for b in range(NUM_BUCKETS):
        out = out + jnp.where(bucket == b, table_fn(b), 0.0)
    return out


def _bias_tile_body(rb_ref, bucket_ref, o_ref):
    h = pl.program_id(0)
    o_ref[...] = _bucket_lookup(bucket_ref[...], lambda bk: rb_ref[bk, h])


def _bias_tiles(rel_bias, t):
    r = jnp.arange(t, dtype=jnp.int32)
    c = jnp.arange(2 * t, dtype=jnp.int32)
    buckets = _rel_bucket(r[:, None] + t - c[None, :])
    return pl.pallas_call(
        _bias_tile_body, grid=(N_HEADS,),
        in_specs=[pl.BlockSpec(memory_space=pltpu.SMEM), pl.BlockSpec((t, 2 * t), lambda h: (0, 0))],
        out_specs=pl.BlockSpec((None, t, 2 * t), lambda h: (h, 0, 0)),
        out_shape=jax.ShapeDtypeStruct((N_HEADS, t, 2 * t), F32),
        compiler_params=_params(("arbitrary",)), name="bias_tiles")(rel_bias, buckets)


def _attn_body(lam_ref, rb_ref, q_ref, k_ref, v_ref, bias_ref, g_ref, o_ref, *scratch,
               t, n_far, near, heads, out_scale):
    s_bufs, p_bufs, m_bufs, mf_bufs, l_bufs, a_bufs = (scratch[0:2], scratch[2:4], scratch[4:6],
                                                       scratch[6:8], scratch[8:10], scratch[10:12])
    hg = pl.program_id(0)
    far = n_far * t
    kb_w = ATTN_KEY_BLOCK
    n_kb = (far + near) // kb_w
    n_tiles = kb_w // LANES

    def fold(x, op):
        out = x[:, 0:LANES]
        for c in range(1, n_tiles):
            out = op(out, x[:, c * LANES:(c + 1) * LANES])
        return out

    def head_cols(hh):
        return slice(hh * V_DIM, (hh + 1) * V_DIM)

    def score_block(hh, kb):
        par = hh % 2
        q = q_ref[:, head_cols(hh)] * (HEAD_DIM ** -0.5)
        lane = lax.broadcasted_iota(jnp.int32, q.shape, 1)
        zero = jnp.zeros_like(q)
        q2 = jnp.concatenate([jnp.where(lane < HEAD_DIM, q, zero), jnp.where(lane >= HEAD_DIM, q, zero)],
                             axis=0)
        cols = slice(kb * kb_w, (kb + 1) * kb_w)
        s = jnp.dot(q2, k_ref[head_cols(hh), cols], preferred_element_type=F32)
        bias_far = rb_ref[NUM_BUCKETS - 1, hg * heads + hh]
        if kb * kb_w < far:
            top = fold(s, jnp.maximum) + bias_far
        else:
            off = kb * kb_w - far
            lo = 2 * t - near + off
            bias = bias_ref[hh, :, lo:lo + kb_w]
            row = lax.broadcasted_iota(jnp.int32, (t, kb_w), 0)
            col = lax.broadcasted_iota(jnp.int32, (t, kb_w), 1)
            keep = col + (off - (near - t)) <= row
            s = jnp.where(jnp.concatenate([keep, keep], axis=0),
                          s + jnp.concatenate([bias, bias], axis=0), NEG_INF)
            top = fold(s, jnp.maximum)
        s_bufs[par][:, cols] = s
        if kb == 0:
            m_bufs[par][...] = top
        else:
            m_bufs[par][...] = jnp.maximum(m_bufs[par][...], top)
        if kb == n_kb - 1:
            m = jnp.broadcast_to(jnp.max(m_bufs[par][...], axis=1, keepdims=True), m_bufs[par].shape)
            m_bufs[par][...] = m
            mf_bufs[par][...] = m - bias_far

    def exp_block(hh, kb):
        par = hh % 2
        cols = slice(kb * kb_w, (kb + 1) * kb_w)
        m = (mf_bufs if kb * kb_w < far else m_bufs)[par][...]
        p = jnp.exp(s_bufs[par][:, cols] - jnp.concatenate([m] * n_tiles, axis=1))
        if kb == 0:
            l_bufs[par][...] = fold(p, jnp.add)
        else:
            l_bufs[par][...] += fold(p, jnp.add)
        p_bufs[par][:, cols] = p.astype(BF16)

    def value_block(hh, kb):
        par = hh % 2
        rows = slice(kb * kb_w, (kb + 1) * kb_w)
        pv = jnp.dot(p_bufs[par][:, rows], v_ref[rows, head_cols(hh)], preferred_element_type=F32)
        if kb == 0:
            a_bufs[par][...] = pv
        else:
            a_bufs[par][...] += pv
        if kb == n_kb - 1:
            nrm = a_bufs[par][...] / jnp.sum(l_bufs[par][...], axis=1, keepdims=True)
            o = nrm[0:t] - lam_ref[0] * nrm[t:2 * t]
            ms = jnp.mean(o * o, axis=-1, keepdims=True)
            o_ref[:, head_cols(hh)] = (o * lax.rsqrt(ms + LN_EPS) * g_ref[...] * out_scale
                                       ).astype(o_ref.dtype)

    for stage in range(heads + 2):
        for kb in range(n_kb):
            if stage < heads:
                score_block(stage, kb)
            if 0 <= stage - 1 < heads:
                exp_block(stage - 1, kb)
            if 0 <= stage - 2 < heads:
                value_block(stage - 2, kb)


def _prompt_attention(q, k, v, rel_bias, lam, subln_g, *, batch, seq, t, out_scale):
    assert t >= MAX_DISTANCE
    nq = seq // t
    width = N_HEADS * V_DIM
    q3 = q.reshape(batch, seq, width)
    v3 = v.reshape(batch, seq, width)
    bias = _bias_tiles(rel_bias, t)
    smem = pl.BlockSpec(memory_space=pltpu.SMEM)
    pieces = []
    for i in range(nq):
        n_far = max(i - 1, 0)
        near = min(i + 1, 2) * t
        keys = n_far * t + near
        hp = ATTN_HEADS_PER_STEP
        body = functools.partial(_attn_body, t=t, n_far=n_far, near=near, heads=hp, out_scale=out_scale)
        pair = lambda shape, dtype: [pltpu.VMEM(shape, dtype)] * 2
        pieces.append(pl.pallas_call(
            body, grid=(N_HEADS // hp, batch),
            in_specs=[smem, smem,
                      pl.BlockSpec((None, t, hp * V_DIM), lambda h, b, i=i: (b, i, h)),
                      pl.BlockSpec((None, hp * V_DIM, keys), lambda h, b: (b, h, 0)),
                      pl.BlockSpec((None, keys, hp * V_DIM), lambda h, b: (b, 0, h)),
                      pl.BlockSpec((hp, t, 2 * t), lambda h, b: (h, 0, 0)),
                      pl.BlockSpec((1, V_DIM), lambda h, b: (0, 0))],
            out_specs=pl.BlockSpec((None, t, hp * V_DIM), lambda h, b: (b, 0, h)),
            out_shape=jax.ShapeDtypeStruct((batch, t, width), BF16),
            scratch_shapes=(pair((2 * t, keys), F32) + pair((2 * t, keys), BF16)
                            + pair((2 * t, LANES), F32) + pair((2 * t, LANES), F32)
                            + pair((2 * t, LANES), F32) + pair((2 * t, V_DIM), F32)),
            compiler_params=_params(("arbitrary", "arbitrary")),
            name=f"prompt_attention_q{i}")(lam, rel_bias, q3, k, v3, bias, subln_g.reshape(1, V_DIM)))
    return jnp.stack(pieces, axis=1).reshape(batch * seq, width)


def _decode_body(pt_ref, lam_ref, q_ref, kn_ref, vn_ref, rbt_ref, bucket_ref, g_ref, *rest,
                 pages, page, n_steps, out_scale):
    k_refs = rest[:pages]
    v_refs = rest[pages:2 * pages]
    o_ref = rest[2 * pages]
    qexp_sc, bias_sc, m_sc, l_sc, acc_sc = rest[2 * pages + 1:]
    step = pl.program_id(1)
    rows = 2 * N_HEADS
    width = N_HEADS * V_DIM

    @pl.when(step == 0)
    def _():
        row = lax.broadcasted_iota(jnp.int32, (rows, width), 0)
        col = lax.broadcasted_iota(jnp.int32, (rows, width), 1)
        own_qk = (col // HEAD_DIM) == (row % N_HEADS) * 2 + row // N_HEADS
        q = (q_ref[...] * (HEAD_DIM ** -0.5)).astype(BF16).astype(F32)
        qexp = jnp.where(own_qk, jnp.broadcast_to(q, (rows, width)), 0.0)
        qexp_sc[...] = qexp.astype(BF16)
        bias_sc[...] = _bucket_lookup(bucket_ref[...], lambda bk: rbt_ref[:, bk:bk + 1])
        kn = kn_ref[...].astype(BF16).astype(F32)
        s_self = jnp.sum(qexp * kn, axis=1, keepdims=True) + rbt_ref[:, 0:1]
        m_sc[...] = s_self
        l_sc[...] = jnp.ones(l_sc.shape, F32)
        vn = vn_ref[...].astype(BF16).astype(F32)
        acc_sc[...] = jnp.broadcast_to(vn[:, None, :], acc_sc.shape)

    qexp = qexp_sc[...]
    bias_far = rbt_ref[:, NUM_BUCKETS - 1:NUM_BUCKETS]
    is_last = step == n_steps - 1
    s_parts = []
    for p in range(pages):
        s = jnp.dot(qexp, k_refs[p][...].astype(BF16), preferred_element_type=F32)
        if p == pages - 1:
            s = s + jnp.where(is_last, bias_sc[...], bias_far)
        else:
            s = s + bias_far
        s_parts.append(s)
    m_old = m_sc[...]
    m_new = m_old
    for s in s_parts:
        m_new = jnp.maximum(m_new, jnp.max(s, axis=1, keepdims=True))
    a = jnp.exp(m_old - m_new)
    l_new = a * l_sc[...]
    probs = []
    for p in range(pages):
        pr = jnp.exp(s_parts[p] - m_new)
        l_new = l_new + jnp.sum(pr, axis=1, keepdims=True)
        probs.append(pr.astype(BF16))
    probs = jnp.concatenate(probs, axis=1)
    m_sc[...] = m_new
    l_sc[...] = l_new
    for h in range(N_HEADS):
        vh = jnp.concatenate([v_refs[p][pl.ds(h, page, stride=N_HEADS), :].astype(BF16)
                              for p in range(pages)], axis=0)
        acc_sc[h] = a * acc_sc[h] + jnp.dot(probs, vh, preferred_element_type=F32)

    @pl.when(is_last)
    def _():
        nrm = acc_sc[...] / l_new[None]
        r = lax.broadcasted_iota(jnp.int32, nrm.shape, 1)
        hh = lax.broadcasted_iota(jnp.int32, nrm.shape, 0)
        coef = jnp.where(r == hh, 1.0, jnp.where(r == hh + N_HEADS, -lam_ref[0], 0.0))
        d = jnp.sum(coef * nrm, axis=1)
        ms = jnp.mean(d * d, axis=-1, keepdims=True)
        o_ref[...] = (d * lax.rsqrt(ms + LN_EPS) * g_ref[...] * out_scale).astype(o_ref.dtype)


def _sample_attention(q, k_new, v_new, cache_k, cache_v, page_table, rel_bias, lam, subln_g,
                      *, pages, out_scale):
    bs, width = q.shape
    page = cache_k.shape[2]
    n_pages = page_table.shape[1]
    n_steps = n_pages // pages
    past = n_pages * page
    kpos = past - page + jnp.arange(page, dtype=jnp.int32)
    bucket_last = _rel_bucket(past - kpos).reshape(1, page)
    rbt = jnp.tile(rel_bias.T, (2, 1))

    def tok_spec():
        return pl.BlockSpec((None, 1, width), lambda b, s, pt: (b, 0, 0))

    def page_spec(shape, p):
        return pl.BlockSpec((None,) + shape, lambda b, s, pt, p=p: (pt[b, s * pages + p], 0, 0))

    full = lambda shape: pl.BlockSpec(shape, lambda b, s, pt: tuple(0 for _ in shape))
    head_spec = pl.BlockSpec((None, N_HEADS, V_DIM), lambda b, s, pt: (b, 0, 0))
    in_specs = ([pl.BlockSpec(memory_space=pltpu.SMEM), tok_spec(), tok_spec(), head_spec,
                 full((2 * N_HEADS, NUM_BUCKETS)), full((1, page)), full((1, V_DIM))]
                + [page_spec((width, page), p) for p in range(pages)]
                + [page_spec((page * N_HEADS, V_DIM), p) for p in range(pages)])
    rows = 2 * N_HEADS
    body = functools.partial(_decode_body, pages=pages, page=page, n_steps=n_steps,
                             out_scale=out_scale)
    out = pl.pallas_call(
        body,
        grid_spec=pltpu.PrefetchScalarGridSpec(
            num_scalar_prefetch=1, grid=(bs, n_steps), in_specs=in_specs,
            out_specs=head_spec,
            scratch_shapes=[pltpu.VMEM((rows, width), BF16), pltpu.VMEM((rows, page), F32),
                            pltpu.VMEM((rows, 1), F32), pltpu.VMEM((rows, 1), F32),
                            pltpu.VMEM((N_HEADS, rows, V_DIM), F32)]),
        out_shape=jax.ShapeDtypeStruct((bs, N_HEADS, V_DIM), BF16),
        compiler_params=_params(("arbitrary", "arbitrary")),
        name="sample_attention")(
            page_table, lam, q.reshape(bs, 1, width), k_new.reshape(bs, 1, width),
            v_new.reshape(bs, N_HEADS, V_DIM), rbt, bucket_last, subln_g.reshape(1, V_DIM),
            *([cache_k] * pages), *([cache_v] * pages))
    return out.reshape(bs, width)


def _s5_params(a_re, a_im, log_dt, b_re, b_im, c_re, c_im):
    dt = jnp.exp(log_dt)[:, None]
    mag = jnp.exp(a_re * dt)
    ang = a_im * dt
    abar_re = mag * jnp.cos(ang)
    abar_im = mag * jnp.sin(ang)
    den = a_re * a_re + a_im * a_im
    f_re = ((abar_re - 1.0) * a_re + abar_im * a_im) / den
    f_im = (abar_im * a_re - (abar_re - 1.0) * a_im) / den
    bb_re = f_re[..., None] * b_re - f_im[..., None] * b_im
    bb_im = f_re[..., None] * b_im + f_im[..., None] * b_re
    eye = jnp.eye(SLAB_GROUPS, dtype=F32)

    def in_slabs(bb):
        tt = bb.reshape(N_SLABS, SLAB_GROUPS, STATE_DIM, SSM_GROUP).transpose(0, 1, 3, 2)
        full = tt[:, :, :, None, :] * eye[None, :, None, :, None]
        return full.reshape(N_SLABS, SLAB_IN, SLAB_STATE).astype(BF16)

    def out_slabs(cc):
        tt = cc.reshape(N_SLABS, SLAB_GROUPS, SSM_GROUP, STATE_DIM).transpose(0, 1, 3, 2)
        full = tt[:, :, :, None, :] * eye[None, :, None, :, None]
        return full.reshape(N_SLABS, SLAB_STATE, SLAB_IN).astype(BF16)

    return (abar_re.reshape(N_SLABS, SLAB_STATE), abar_im.reshape(N_SLABS, SLAB_STATE),
            in_slabs(bb_re), in_slabs(bb_im), out_slabs(c_re), out_slabs(c_im))


def _glu_out(y, u, d_ref, wglu_ref):
    g = jax.nn.gelu(y + d_ref[...] * u)
    gate = jnp.dot(g.astype(BF16), wglu_ref[...].astype(BF16), preferred_element_type=F32)
    return g * jax.nn.sigmoid(gate)


def _s5_prompt_body(u_ref, ar_ref, ai_ref, bbr_ref, bbi_ref, cr_ref, ci_ref, d_ref, wglu_ref,
                    o_ref, sre_ref, sim_ref, xr_sc, xi_sc, pw_sc, cr_sc, ci_sc, perm_sc, *, batch):
    c = pl.program_id(0)
    sub = SUBLANES
    _, tc, width = u_ref.shape
    n_rows = batch * tc
    t = lax.broadcasted_iota(jnp.int32, (sub, SLAB_STATE), 0)
    first = t < batch

    def cmul(pr, pi, qr, qi):
        return pr * qr - pi * qi, pr * qi + pi * qr

    @pl.when(c == 0)
    def _():
        for j in range(N_SLABS):
            a = (jnp.broadcast_to(ar_ref[j:j + 1, :], t.shape), jnp.broadcast_to(ai_ref[j:j + 1, :], t.shape))
            a2 = cmul(*a, *a)
            for part in range(2):
                pw_sc[j, part] = jnp.where(first, 0.0, a[part])
                pw_sc[j, 2 + part] = jnp.where(first, a[part], a2[part])
        cr_sc[...] = jnp.zeros(cr_sc.shape, F32)
        ci_sc[...] = jnp.zeros(ci_sc.shape, F32)
        r = lax.broadcasted_iota(jnp.int32, (n_rows, n_rows), 0)
        k = lax.broadcasted_iota(jnp.int32, (n_rows, n_rows), 1)
        perm_sc[0] = jnp.where(k == (r % batch) * tc + r // batch, 1.0, 0.0).astype(BF16)
        perm_sc[1] = jnp.where(r == (k % batch) * tc + k // batch, 1.0, 0.0).astype(BF16)

    def permute(which, x):
        return jnp.dot(perm_sc[which], x, preferred_element_type=F32)

    u_seq = u_ref[...].reshape(n_rows, width)
    hi = u_seq.astype(BF16)
    rest = u_seq - hi.astype(F32)
    mid = rest.astype(BF16)
    lo = (rest - mid.astype(F32)).astype(BF16)
    u_hi = permute(0, hi)
    ub = u_hi.astype(BF16)
    u = u_hi + permute(0, mid) + permute(0, lo)
    for j in range(N_SLABS):
        uj = ub[:, j * SLAB_IN:(j + 1) * SLAB_IN]
        xr_sc[j] = jnp.dot(uj, bbr_ref[j], preferred_element_type=F32)
        xi_sc[j] = jnp.dot(uj, bbi_ref[j], preferred_element_type=F32)

    for j in range(N_SLABS):
        pr, pi = cr_sc[j], ci_sc[j]
        for v in range(n_rows // sub):
            rows = slice(v * sub, (v + 1) * sub)
            br = xr_sc[j, rows, :]
            bi = xi_sc[j, rows, :]
            lr = jnp.where(first, pltpu.roll(pr, batch, 0), pr)
            li = jnp.where(first, pltpu.roll(pi, batch, 0), pi)
            dr, di = cmul(pw_sc[j, 0], pw_sc[j, 1], pltpu.roll(br, batch, 0), pltpu.roll(bi, batch, 0))
            er, ei = cmul(pw_sc[j, 2], pw_sc[j, 3], lr, li)
            pr, pi = br + dr + er, bi + di + ei
            xr_sc[j, rows, :] = pr
            xi_sc[j, rows, :] = pi
        cr_sc[j] = pr
        ci_sc[j] = pi
    sre_ref[...] = cr_sc[...]
    sim_ref[...] = ci_sc[...]

    ys = []
    for j in range(N_SLABS):
        ys.append(jnp.dot(xr_sc[j].astype(BF16), cr_ref[j], preferred_element_type=F32)
                  - jnp.dot(xi_sc[j].astype(BF16), ci_ref[j], preferred_element_type=F32))
    out = _glu_out(jnp.concatenate(ys, axis=1), u, d_ref, wglu_ref).astype(BF16)
    o_ref[...] = permute(1, out).astype(o_ref.dtype).reshape(batch, tc, width)


def _s5_prompt(u, sp, d, w_glu, *, batch, seq, tc):
    assert SUBLANES == 2 * batch, "tile = two positions of every sequence"
    ar, ai, bbr, bbi, cr, ci = sp
    width = u.shape[1]
    rows = batch * tc
    const = lambda shape: pl.BlockSpec(shape, lambda c: tuple(0 for _ in shape))
    block = pl.BlockSpec((batch, tc, width), lambda c: (0, c, 0))
    state_spec = const((N_SLABS, SUBLANES, SLAB_STATE))
    state_shape = jax.ShapeDtypeStruct((N_SLABS, SUBLANES, SLAB_STATE), F32)
    out, s_re, s_im = pl.pallas_call(
        functools.partial(_s5_prompt_body, batch=batch), grid=(seq // tc,),
        in_specs=[block, const(ar.shape), const(ai.shape), const(bbr.shape), const(bbi.shape),
                  const(cr.shape), const(ci.shape), const((1, width)), const(w_glu.shape)],
        out_specs=[block, state_spec, state_spec],
        out_shape=[jax.ShapeDtypeStruct((batch, seq, width), BF16), state_shape, state_shape],
        scratch_shapes=[pltpu.VMEM((N_SLABS, rows, SLAB_STATE), F32),
                        pltpu.VMEM((N_SLABS, rows, SLAB_STATE), F32),
                        pltpu.VMEM((N_SLABS, 4, SUBLANES, SLAB_STATE), F32),
                        pltpu.VMEM((N_SLABS, SUBLANES, SLAB_STATE), F32),
                        pltpu.VMEM((N_SLABS, SUBLANES, SLAB_STATE), F32),
                        pltpu.VMEM((2, rows, rows), BF16)],
        compiler_params=_params(("arbitrary",)),
        name="s5_prompt")(u.reshape(batch, seq, width), ar, ai, bbr, bbi, cr, ci,
                          d.reshape(1, width), w_glu)
    return out.reshape(batch * seq, width), s_re, s_im


def _s5_sample_body(u_ref, x0r_ref, x0i_ref, ar_ref, ai_ref, bbr_ref, bbi_ref, cr_ref, ci_ref,
                    d_ref, wglu_ref, o_ref, sre_ref, sim_ref):
    u = u_ref[...]
    ub = u.astype(BF16)
    ys = []
    for j in range(N_SLABS):
        uj = ub[:, j * SLAB_IN:(j + 1) * SLAB_IN]
        ar = ar_ref[j:j + 1, :]
        ai = ai_ref[j:j + 1, :]
        x0r = x0r_ref[j]
        x0i = x0i_ref[j]
        xr = ar * x0r - ai * x0i + jnp.dot(uj, bbr_ref[j], preferred_element_type=F32)
        xi = ar * x0i + ai * x0r + jnp.dot(uj, bbi_ref[j], preferred_element_type=F32)
        sre_ref[j] = xr
        sim_ref[j] = xi
        ys.append(jnp.dot(xr.astype(BF16), cr_ref[j], preferred_element_type=F32)
                  - jnp.dot(xi.astype(BF16), ci_ref[j], preferred_element_type=F32))
    o_ref[...] = _glu_out(jnp.concatenate(ys, axis=1), u, d_ref, wglu_ref).astype(o_ref.dtype)


def _s5_sample(u, x0_re, x0_im, sp, d, w_glu):
    ar, ai, bbr, bbi, cr, ci = sp
    bs, width = u.shape
    state_shape = jax.ShapeDtypeStruct((N_SLABS, bs, SLAB_STATE), F32)
    return pl.pallas_call(
        _s5_sample_body,
        out_shape=[jax.ShapeDtypeStruct((bs, width), BF16), state_shape, state_shape],
        compiler_params=pltpu.CompilerParams(vmem_limit_bytes=VMEM_LIMIT_BYTES),
        name="s5_sample")(u, x0_re, x0_im, ar, ai, bbr, bbi, cr, ci, d.reshape(1, width), w_glu)


def _gated(c):
    gate = c[:, :LANES]
    return gate * jax.nn.sigmoid(gate) * c[:, LANES:]


def _up_prompt_body(h_ref, wg_ref, wv_ref, cwg_ref, cwv_ref, cbg_ref, cbv_ref,
                    act_ref, cg_ref, cv_ref, w_sc, *, rows):
    w_sc[:, 0:LANES] = wg_ref[...].astype(BF16)
    w_sc[:, LANES:2 * LANES] = wv_ref[...].astype(BF16)
    cw = jnp.concatenate([cwg_ref[...], cwv_ref[...]], axis=1)
    cb = jnp.concatenate([cbg_ref[...], cbv_ref[...]], axis=1)
    seq = h_ref.shape[0]
    sub = SUBLANES
    row8 = lax.broadcasted_iota(jnp.int32, (sub, 2 * LANES), 0)

    prev = jnp.zeros((sub, 2 * LANES), F32)
    for c in range(seq // rows):
        up = jnp.dot(h_ref[c * rows:(c + 1) * rows, :], w_sc[...], preferred_element_type=F32)
        cur = cb + cw[CONV_W - 1:CONV_W] * up
        for back in range(1, CONV_W):
            rolled = pltpu.roll(up, back, 0)
            head = jnp.where(row8 < back, pltpu.roll(prev, back, 0), rolled[0:sub])
            shifted = jnp.concatenate([head, rolled[sub:]], axis=0)
            cur = cur + cw[CONV_W - 1 - back:CONV_W - back] * shifted
        act_ref[c * rows:(c + 1) * rows, :] = _gated(cur).astype(act_ref.dtype)
        prev = up[rows - sub:, :]
    tail = prev[sub - (CONV_W - 1):, :]
    cg_ref[...] = tail[:, :LANES]
    cv_ref[...] = tail[:, LANES:]


def _up_prompt(hb, w_up, conv_w, conv_b, *, batch, seq):
    d_model = hb.shape[1]
    d_ff = w_up.shape[1] // 2
    nb = d_ff // LANES
    cb = conv_b.reshape(1, 2 * d_ff)
    col = lambda off: (lambda b, j: (0, j + off))
    tail_spec = pl.BlockSpec((None, CONV_W - 1, LANES), lambda b, j: (b, 0, j))
    tail_shape = jax.ShapeDtypeStruct((batch, CONV_W - 1, d_ff), F32)
    return pl.pallas_call(
        functools.partial(_up_prompt_body, rows=UP_ROW_CHUNK), grid=(batch, nb),
        scratch_shapes=[pltpu.VMEM((d_model, 2 * LANES), BF16)],
        in_specs=[pl.BlockSpec((seq, d_model), lambda b, j: (b, 0)),
                  pl.BlockSpec((d_model, LANES), col(0)), pl.BlockSpec((d_model, LANES), col(nb)),
                  pl.BlockSpec((CONV_W, LANES), col(0)), pl.BlockSpec((CONV_W, LANES), col(nb)),
                  pl.BlockSpec((1, LANES), col(0)), pl.BlockSpec((1, LANES), col(nb))],
        out_specs=[pl.BlockSpec((seq, LANES), lambda b, j: (b, j)), tail_spec, tail_spec],
        out_shape=[jax.ShapeDtypeStruct((batch * seq, d_ff), BF16), tail_shape, tail_shape],
        compiler_params=_params(("arbitrary", "arbitrary")),
        name="up_prompt")(hb, w_up, w_up, conv_w, conv_w, cb, cb)


def _up_sample_body(h_ref, wg_ref, wv_ref, cwg_ref, cwv_ref, cbg_ref, cbv_ref, sg_ref, sv_ref,
                    act_ref, cg_ref, cv_ref):
    w = jnp.concatenate([wg_ref[...].astype(BF16), wv_ref[...].astype(BF16)], axis=1)
    up = jnp.dot(h_ref[...], w, preferred_element_type=F32)
    cw = jnp.concatenate([cwg_ref[...], cwv_ref[...]], axis=1)
    cb = jnp.concatenate([cbg_ref[...], cbv_ref[...]], axis=1)
    c = cb + cw[CONV_W - 1:CONV_W] * up
    for tap in range(CONV_W - 1):
        st = jnp.concatenate([sg_ref[tap], sv_ref[tap]], axis=1)
        c = c + cw[tap:tap + 1] * st
    act_ref[...] = _gated(c).astype(act_ref.dtype)
    for tap in range(1, CONV_W - 1):
        cg_ref[tap - 1] = sg_ref[tap]
        cv_ref[tap - 1] = sv_ref[tap]
    cg_ref[CONV_W - 2] = up[:, :LANES]
    cv_ref[CONV_W - 2] = up[:, LANES:]


def _up_sample(hb, w_up, conv_w, conv_b, state):
    bs, d_model = hb.shape
    d_ff = w_up.shape[1] // 2
    nb = d_ff // LANES
    cb = conv_b.reshape(1, 2 * d_ff)
    col = lambda off: (lambda j: (0, j + off))
    st = lambda off: pl.BlockSpec((CONV_W - 1, bs, LANES), lambda j, off=off: (0, 0, j + off))
    tail_spec = pl.BlockSpec((CONV_W - 1, bs, LANES), lambda j: (0, 0, j))
    tail_shape = jax.ShapeDtypeStruct((CONV_W - 1, bs, d_ff), F32)
    return pl.pallas_call(
        _up_sample_body, grid=(nb,),
        in_specs=[pl.BlockSpec((bs, d_model), lambda j: (0, 0)),
                  pl.BlockSpec((d_model, LANES), col(0)), pl.BlockSpec((d_model, LANES), col(nb)),
                  pl.BlockSpec((CONV_W, LANES), col(0)), pl.BlockSpec((CONV_W, LANES), col(nb)),
                  pl.BlockSpec((1, LANES), col(0)), pl.BlockSpec((1, LANES), col(nb)),
                  st(0), st(nb)],
        out_specs=[pl.BlockSpec((bs, LANES), lambda j: (0, j)), tail_spec, tail_spec],
        out_shape=[jax.ShapeDtypeStruct((bs, d_ff), BF16), tail_shape, tail_shape],
        compiler_params=_params(("arbitrary",)),
        name="up_sample")(hb, w_up, w_up, conv_w, conv_w, cb, cb, state, state)


def _one(x):
    return (x,)


def _sigmoid_out(acc):
    return (jax.nn.sigmoid(acc),)


def _both(acc):
    return (acc, acc)


def _merge(pa, ps, ga, gs):
    return (ga * pa + gs * ps,)


def _layer(xp, xs, w, *, batch, seq, alpha, attend_p, attend_s, ssm_p, ssm_s, up_p, up_s):
    d_model = xp.shape[1]
    qk_w = N_HEADS * 2 * HEAD_DIM
    v_w = N_HEADS * V_DIM
    ssm_w = d_model // 2
    w_in = w["w_in"]
    g1, b1 = w["ln1_g"].reshape(1, d_model), w["ln1_b"].reshape(1, d_model)
    big = dict(tm=MM_ROWS, tn=MM_COLS)
    c = 0
    (q,), (q_s,), xb = _matmul([(xp, xs, w_in, c)], [], [BF16], _one, n_cols=qk_w, name="proj_q",
                               emit_lhs=True, **big)
    c += qk_w
    kt, ktb, k_s = _proj_transposed(xb, xs, w_in, c, n_cols=qk_w, batch=batch, seq=seq, tm=MM_ROWS,
                                    name="proj_kt")
    c += qk_w
    (v, vb), (v_s, _) = _matmul([(xb, xs, w_in, c)], [], [F32, BF16], _both, n_cols=v_w, name="proj_v", **big)
    c += v_w
    (u,), (u_s,) = _matmul([(xb, xs, w_in, c)], [], [F32], _one, n_cols=ssm_w, name="proj_u", **big)
    c += ssm_w
    (ga,), (ga_s,) = _matmul([(xb, xs, w_in, c)], [], [F32], _sigmoid_out, n_cols=d_model, name="gate_a", **big)
    c += d_model
    (gs,), (gs_s,) = _matmul([(xb, xs, w_in, c)], [], [F32], _sigmoid_out, n_cols=d_model, name="gate_s", **big)

    attn = attend_p(q, ktb, vb)
    attn_s = attend_s(q_s, k_s, v_s)
    ssm_out, re_p, im_p = ssm_p(u)
    ssm_out_s, re_s, im_s = ssm_s(u_s)

    (merged,), (merged_s,) = _matmul(
        [(attn, attn_s, w["w_proj_attn"], 0), (ssm_out, ssm_out_s, w["w_proj_ssm"], 0)],
        [(ga, ga_s), (gs, gs_s)], [BF16], _merge, n_cols=d_model, name="merge", **big)

    def post_ln1(acc, res, g, b):
        h = _ln(alpha * res + acc, g, b)
        return h, h

    (h, hb), (h_s, hb_s) = _matmul([(merged, merged_s, w["w_out"], 0)], [(xp, xs), g1, b1], [F32, BF16],
                                   post_ln1, n_cols=d_model, tm=LN_MM_ROWS, tn=d_model, name="out_proj_ln1")
    act, conv_p = up_p(hb)
    act_s, conv_s = up_s(hb_s)
    (r2,), (r2_s,) = _matmul([(act, act_s, w["w_down"], 0)], [(h, h_s)], [F32],
                             lambda acc, res: (alpha * res + acc,), n_cols=d_model,
                             tm=DOWN_ROWS, tn=DOWN_COLS, name="down_proj")
    y = _layer_norm(r2, w["ln2_g"], w["ln2_b"], tm=LN_ROWS, name="ln2")
    y_s = _layer_norm(r2_s, w["ln2_g"], w["ln2_b"], tm=LN_ROWS, name="ln2_sample")
    return (y, kt, v, re_p, im_p, conv_p), (y_s, k_s, v_s, re_s, im_s, conv_s)


def kernel(x_prompt, x_sample, cache_k, cache_v, state_ssm_re, state_ssm_im, state_conv, page_table, rel_bias, w_in, lambda_q1, lambda_k1, lambda_q2, lambda_k2, subln_g, ssm_a_re, ssm_a_im, ssm_log_dt, ssm_b_re, ssm_b_im, ssm_c_re, ssm_c_im, ssm_d, w_glu, w_proj_attn, w_proj_ssm, w_out, ln1_g, ln1_b, w_up, conv_w, conv_b, w_down, ln2_g, ln2_b):
    depth = w_in.shape[0]
    assert depth == 1, "single-layer trunk"
    bp, seq, d_model = x_prompt.shape
    bs, dec_seq, _ = x_sample.shape
    assert dec_seq == 1
    n_pool, page = cache_k.shape[1], cache_k.shape[2]
    d_ff = w_down.shape[1]
    n_groups = ssm_a_re.shape[1]
    assert n_groups == N_SLABS * SLAB_GROUPS and d_ff % LANES == 0
    alpha = (2.0 * depth) ** 0.25
    width = N_HEADS * V_DIM

    hp = x_prompt.reshape(bp * seq, d_model)
    hs = x_sample.reshape(bs, d_model)
    outs = {}
    for l in range(depth):
        lam_init = 0.8 - 0.6 * math.exp(-0.3 * l)
        out_scale = 1.0 - lam_init
        lam = (jnp.exp(jnp.sum(lambda_q1[l] * lambda_k1[l]))
               - jnp.exp(jnp.sum(lambda_q2[l] * lambda_k2[l])) + lam_init).reshape(1)
        w = dict(w_in=w_in[l], w_proj_attn=w_proj_attn[l], w_proj_ssm=w_proj_ssm[l], w_out=w_out[l],
                 ln1_g=ln1_g[l], ln1_b=ln1_b[l], w_down=w_down[l], ln2_g=ln2_g[l], ln2_b=ln2_b[l])
        sp = _s5_params(ssm_a_re[l], ssm_a_im[l], ssm_log_dt[l], ssm_b_re[l], ssm_b_im[l],
                        ssm_c_re[l], ssm_c_im[l])

        def attend_p(q, kt, v):
            return _prompt_attention(q, kt, v, rel_bias, lam, subln_g[l], batch=bp, seq=seq,
                                     t=ATTN_BLOCK, out_scale=out_scale)

        def ssm_p(u):
            o, sr, si = _s5_prompt(u, sp, ssm_d[l], w_glu[l], batch=bp, seq=seq, tc=S5_POSITIONS)
            last = lambda st: st[:, SUBLANES - bp:, :].transpose(1, 0, 2)
            return o, last(sr), last(si)

        def up_p(hb):
            act, cg, cv = _up_prompt(hb, w_up[l], conv_w[l], conv_b[l], batch=bp, seq=seq)
            return act, jnp.concatenate([cg, cv], axis=-1)

        ck = cache_k[l].transpose(0, 2, 3, 4, 1).reshape(n_pool, width, page)
        cv_ = cache_v[l].reshape(n_pool, page * N_HEADS, V_DIM)

        def attend_s(q, k, v):
            return _sample_attention(q.astype(F32), k, v, ck, cv_, page_table, rel_bias, lam,
                                     subln_g[l], pages=DECODE_PAGES, out_scale=out_scale)

        def ssm_s(u):
            x0r = state_ssm_re[l].reshape(bs, N_SLABS, SLAB_STATE).transpose(1, 0, 2)
            x0i = state_ssm_im[l].reshape(bs, N_SLABS, SLAB_STATE).transpose(1, 0, 2)
            o, sr, si = _s5_sample(u, x0r, x0i, sp, ssm_d[l], w_glu[l])
            return o, sr.transpose(1, 0, 2), si.transpose(1, 0, 2)

        def up_s(hb):
            act, cg, cv = _up_sample(hb, w_up[l], conv_w[l], conv_b[l],
                                     state_conv[l].transpose(1, 0, 2))
            return act, jnp.concatenate([cg, cv], axis=-1).transpose(1, 0, 2)

        (hp, kt_p, v_p, re_p, im_p, c_p), (hs, k_s, v_s, re_s, im_s, c_s) = _layer(
            hp, hs, w, batch=bp, seq=seq, alpha=alpha, attend_p=attend_p, attend_s=attend_s,
            ssm_p=ssm_p, ssm_s=ssm_s, up_p=up_p, up_s=up_s)
        k_p = kt_p.reshape(bp, N_HEADS, 2, HEAD_DIM, seq).transpose(0, 4, 1, 2, 3)

        for name, val in (("kp", k_p.reshape(bp, seq, N_HEADS, 2, HEAD_DIM)),
                          ("vp", v_p.reshape(bp, seq, N_HEADS, V_DIM)),
                          ("rep", re_p.reshape(bp, n_groups, STATE_DIM)),
                          ("imp", im_p.reshape(bp, n_groups, STATE_DIM)),
                          ("cp", c_p),
                          ("ks", k_s.reshape(bs, 1, N_HEADS, 2, HEAD_DIM)),
                          ("vs", v_s.reshape(bs, 1, N_HEADS, V_DIM)),
                          ("res", re_s.reshape(bs, n_groups, STATE_DIM)),
                          ("ims", im_s.reshape(bs, n_groups, STATE_DIM)),
                          ("cs", c_s)):
            outs.setdefault(name, []).append(val)

    st = {k: jnp.stack(v, axis=0) for k, v in outs.items()}
    return (hp.reshape(bp, seq, d_model), hs.reshape(bs, 1, d_model), st["kp"], st["vp"], st["rep"],
            st["imp"], st["cp"], st["ks"], st["vs"], st["res"], st["ims"], st["cs"])
```

```python
import functools
import math

import jax
import jax.numpy as jnp
from jax import lax
from jax.experimental import pallas as pl
from jax.experimental.pallas import tpu as pltpu

F32 = jnp.float32
BF16 = jnp.bfloat16

N_HEADS = 8
HEAD_DIM = 64
V_DIM = 2 * HEAD_DIM
SSM_GROUP = 16
STATE_DIM = 64
CONV_W = 3
NUM_BUCKETS = 32
MAX_EXACT = NUM_BUCKETS // 2
MAX_DISTANCE = 128
LN_EPS = 1e-5
NEG_INF = -1e30

VMEM_LIMIT_BYTES = 56 * 1024 * 1024
LANES = 128
SLAB_GROUPS = 8
N_SLABS = 8
SLAB_IN = SLAB_GROUPS * SSM_GROUP
SLAB_STATE = SLAB_GROUPS * STATE_DIM
SUBLANES = 8
MM_ROWS, MM_COLS = 1024, 1024
LN_MM_ROWS = 512
DOWN_ROWS, DOWN_COLS = 512, 512
LN_ROWS = 512
ATTN_BLOCK = 256
ATTN_KEY_BLOCK = 256
ATTN_HEADS_PER_STEP = 4
S5_POSITIONS = 64
DECODE_PAGES = 16
UP_ROW_CHUNK = 512


def _params(sem):
    return pltpu.CompilerParams(dimension_semantics=sem, vmem_limit_bytes=VMEM_LIMIT_BYTES)


def _mm_body(*refs, n_pairs, n_extra, n_out, epilogue, emit_lhs):
    it = iter(refs)
    take = lambda n: [next(it) for _ in range(n)]
    x_refs, xs_refs, w_refs = take(n_pairs), take(n_pairs), take(n_pairs)
    extra, extra_s = take(n_extra), take(n_extra)
    outs, outs_s = take(n_out), take(n_out)
    lhs_out = take(1) if emit_lhs else []
    wbf = take(n_pairs)

    def apply(lhs_refs, extra_refs, out_refs, keep):
        lhs = [x[...].astype(BF16) for x in lhs_refs]
        for ref in keep:
            ref[...] = lhs[0]
        accs = [jnp.dot(x, s[...], preferred_element_type=F32) for x, s in zip(lhs, wbf)]
        for o, r in zip(out_refs, epilogue(*accs, *[e[...] for e in extra_refs])):
            o[...] = r.astype(o.dtype)

    @pl.when(pl.program_id(1) == 0)
    def _():
        for w, s in zip(w_refs, wbf):
            s[...] = w[...].astype(BF16)
        apply(xs_refs, extra_s, outs_s, [])

    apply(x_refs, extra, outs, lhs_out)


def _matmul(pairs, extras, out_dtypes, epilogue, *, n_cols, tm, tn, name, emit_lhs=False):
    m = pairs[0][0].shape[0]
    ms = pairs[0][1].shape[0]
    grid = (n_cols // tn, m // tm)
    assert not emit_lhs or grid[0] == 1
    w_mode = dict(pipeline_mode=pl.Buffered(1)) if grid[0] == 1 else {}
    specs_x, specs_xs, specs_w, scratch = [], [], [], []
    for x, xs, w, c0 in pairs:
        k = x.shape[1]
        specs_x.append(pl.BlockSpec((tm, k), lambda j, i: (i, 0)))
        specs_xs.append(pl.BlockSpec((ms, k), lambda j, i: (0, 0)))
        specs_w.append(pl.BlockSpec((k, tn), lambda j, i, off=c0 // tn: (0, j + off), **w_mode))
        scratch.append(pltpu.VMEM((k, tn), BF16))
    specs_e, specs_es, args_e, args_es = [], [], [], []
    for e in extras:
        if isinstance(e, tuple):
            specs_e.append(pl.BlockSpec((tm, tn), lambda j, i: (i, j)))
            specs_es.append(pl.BlockSpec((ms, tn), lambda j, i: (0, j)))
            args_e.append(e[0])
            args_es.append(e[1])
        else:
            specs_e.append(pl.BlockSpec((1, tn), lambda j, i: (0, j)))
            specs_es.append(pl.BlockSpec((1, tn), lambda j, i: (0, j)))
            args_e.append(e)
            args_es.append(e)
    out_shape = ([jax.ShapeDtypeStruct((m, n_cols), d) for d in out_dtypes]
                 + [jax.ShapeDtypeStruct((ms, n_cols), d) for d in out_dtypes])
    out_specs = ([pl.BlockSpec((tm, tn), lambda j, i: (i, j)) for _ in out_dtypes]
                 + [pl.BlockSpec((ms, tn), lambda j, i: (0, j)) for _ in out_dtypes])
    if emit_lhs:
        k0 = pairs[0][0].shape[1]
        out_shape.append(jax.ShapeDtypeStruct((m, k0), BF16))
        out_specs.append(pl.BlockSpec((tm, k0), lambda j, i: (i, 0)))
    body = functools.partial(_mm_body, n_pairs=len(pairs), n_extra=len(extras),
                             n_out=len(out_dtypes), epilogue=epilogue, emit_lhs=emit_lhs)
    outs = pl.pallas_call(
        body, grid=grid, in_specs=specs_x + specs_xs + specs_w + specs_e + specs_es,
        out_specs=out_specs, out_shape=out_shape, scratch_shapes=scratch,
        compiler_params=_params(("arbitrary", "arbitrary")), name=name)(
            *[p[0] for p in pairs], *[p[1] for p in pairs], *[p[2] for p in pairs],
            *args_e, *args_es)
    n = len(out_dtypes)
    if emit_lhs:
        return outs[:n], outs[n:2 * n], outs[2 * n]
    return outs[:n], outs[n:]


def _proj_t_body(x_ref, xs_ref, w_ref, o_ref, ob_ref, os_ref, wt_sc):
    @pl.when(pl.program_id(0) == 0)
    def _():
        for c in range(w_ref.shape[1] // LANES):
            cols = slice(c * LANES, (c + 1) * LANES)
            wt_sc[cols, :] = w_ref[:, cols].T.astype(BF16)
        os_ref[...] = lax.dot_general(xs_ref[...].astype(BF16), wt_sc[...], (((1,), (1,)), ((), ())),
                                      preferred_element_type=F32)

    kt = lax.dot_general(wt_sc[...], x_ref[...], (((1,), (1,)), ((), ())),
                         preferred_element_type=F32)
    o_ref[...] = kt
    ob_ref[...] = kt.astype(BF16)


def _proj_transposed(x, xs, w, c0, *, n_cols, batch, seq, tm, name):
    k = x.shape[1]
    ms = xs.shape[0]
    per_b = seq // tm
    out_spec = pl.BlockSpec((None, n_cols, tm), lambda i: (i // per_b, 0, i % per_b))
    return pl.pallas_call(
        _proj_t_body, grid=(batch * per_b,),
        in_specs=[pl.BlockSpec((tm, k), lambda i: (i, 0)),
                  pl.BlockSpec((ms, k), lambda i: (0, 0)),
                  pl.BlockSpec((k, n_cols), lambda i, off=c0 // n_cols: (0, off),
                               pipeline_mode=pl.Buffered(1))],
        out_specs=[out_spec, out_spec, pl.BlockSpec((ms, n_cols), lambda i: (0, 0))],
        out_shape=[jax.ShapeDtypeStruct((batch, n_cols, seq), F32),
                   jax.ShapeDtypeStruct((batch, n_cols, seq), BF16),
                   jax.ShapeDtypeStruct((ms, n_cols), F32)],
        scratch_shapes=[pltpu.VMEM((n_cols, k), BF16)],
        compiler_params=_params(("arbitrary",)), name=name)(x, xs, w)


def _ln(x, g, b):
    mu = jnp.mean(x, axis=-1, keepdims=True)
    xc = x - mu
    var = jnp.mean(xc * xc, axis=-1, keepdims=True)
    return xc * lax.rsqrt(var + LN_EPS) * g + b


def _ln_body(x_ref, g_ref, b_ref, o_ref):
    o_ref[...] = _ln(x_ref[...], g_ref[...], b_ref[...])


def _layer_norm(x, g, b, *, tm, name):
    m, d = x.shape
    tm = min(tm, m)
    return pl.pallas_call(
        _ln_body, grid=(m // tm,),
        in_specs=[pl.BlockSpec((tm, d), lambda i: (i, 0)),
                  pl.BlockSpec((1, d), lambda i: (0, 0)),
                  pl.BlockSpec((1, d), lambda i: (0, 0))],
        out_specs=pl.BlockSpec((tm, d), lambda i: (i, 0)),
        out_shape=jax.ShapeDtypeStruct((m, d), F32),
        compiler_params=_params(("arbitrary",)), name=name)(x, g.reshape(1, d), b.reshape(1, d))


def _rel_bucket(n):
    n = jnp.maximum(n, 0)
    nf = jnp.maximum(n, 1).astype(F32)
    large = MAX_EXACT + (jnp.log(nf / MAX_EXACT) / math.log(MAX_DISTANCE / MAX_EXACT)
                         * (NUM_BUCKETS - MAX_EXACT)).astype(jnp.int32)
    large = jnp.minimum(large, NUM_BUCKETS - 1)
    return jnp.where(n < MAX_EXACT, n, large)


def _bucket_lookup(bucket, table_fn):
    out = jnp.zeros(jnp.broadcast_shapes(bucket.shape, table_fn(0).shape), F32)
    for b in range(NUM_BUCKETS):
        out = out + jnp.where(bucket == b, table_fn(b), 0.0)
    return out


def _bias_tile_body(rb_ref, bucket_ref, o_ref):
    h = pl.program_id(0)
    o_ref[...] = _bucket_lookup(bucket_ref[...], lambda bk: rb_ref[bk, h])


def _bias_tiles(rel_bias, t):
    r = jnp.arange(t, dtype=jnp.int32)
    c = jnp.arange(2 * t, dtype=jnp.int32)
    buckets = _rel_bucket(r[:, None] + t - c[None, :])
    return pl.pallas_call(
        _bias_tile_body, grid=(N_HEADS,),
        in_specs=[pl.BlockSpec(memory_space=pltpu.SMEM), pl.BlockSpec((t, 2 * t), lambda h: (0, 0))],
        out_specs=pl.BlockSpec((None, t, 2 * t), lambda h: (h, 0, 0)),
        out_shape=jax.ShapeDtypeStruct((N_HEADS, t, 2 * t), F32),
        compiler_params=_params(("arbitrary",)), name="bias_tiles")(rel_bias, buckets)


def _attn_body(lam_ref, rb_ref, q_ref, k_ref, v_ref, bias_ref, g_ref, o_ref, *scratch,
               t, n_far, near, heads, out_scale):
    s_bufs, p_bufs, m_bufs, mf_bufs, l_bufs, a_bufs = (scratch[0:2], scratch[2:4], scratch[4:6],
                                                       scratch[6:8], scratch[8:10], scratch[10:12])
    hg = pl.program_id(0)
    far = n_far * t
    kb_w = ATTN_KEY_BLOCK
    n_kb = (far + near) // kb_w
    n_tiles = kb_w // LANES

    def fold(x, op):
        out = x[:, 0:LANES]
        for c in range(1, n_tiles):
            out = op(out, x[:, c * LANES:(c + 1) * LANES])
        return out

    def head_cols(hh):
        return slice(hh * V_DIM, (hh + 1) * V_DIM)

    def score_block(hh, kb):
        par = hh % 2
        q = q_ref[:, head_cols(hh)] * (HEAD_DIM ** -0.5)
        lane = lax.broadcasted_iota(jnp.int32, q.shape, 1)
        zero = jnp.zeros_like(q)
        q2 = jnp.concatenate([jnp.where(lane < HEAD_DIM, q, zero), jnp.where(lane >= HEAD_DIM, q, zero)],
                             axis=0)
        cols = slice(kb * kb_w, (kb + 1) * kb_w)
        s = jnp.dot(q2, k_ref[head_cols(hh), cols], preferred_element_type=F32)
        bias_far = rb_ref[NUM_BUCKETS - 1, hg * heads + hh]
        if kb * kb_w < far:
            top = fold(s, jnp.maximum) + bias_far
        else:
            off = kb * kb_w - far
            lo = 2 * t - near + off
            bias = bias_ref[hh, :, lo:lo + kb_w]
            row = lax.broadcasted_iota(jnp.int32, (t, kb_w), 0)
            col = lax.broadcasted_iota(jnp.int32, (t, kb_w), 1)
            keep = col + (off - (near - t)) <= row
            s = jnp.where(jnp.concatenate([keep, keep], axis=0),
                          s + jnp.concatenate([bias, bias], axis=0), NEG_INF)
            top = fold(s, jnp.maximum)
        s_bufs[par][:, cols] = s
        if kb == 0:
            m_bufs[par][...] = top
        else:
            m_bufs[par][...] = jnp.maximum(m_bufs[par][...], top)
        if kb == n_kb - 1:
            m = jnp.broadcast_to(jnp.max(m_bufs[par][...], axis=1, keepdims=True), m_bufs[par].shape)
            m_bufs[par][...] = m
            mf_bufs[par][...] = m - bias_far

    def exp_block(hh, kb):
        par = hh % 2
        cols = slice(kb * kb_w, (kb + 1) * kb_w)
        m = (mf_bufs if kb * kb_w < far else m_bufs)[par][...]
        p = jnp.exp(s_bufs[par][:, cols] - jnp.concatenate([m] * n_tiles, axis=1))
        if kb == 0:
            l_bufs[par][...] = fold(p, jnp.add)
        else:
            l_bufs[par][...] += fold(p, jnp.add)
        p_bufs[par][:, cols] = p.astype(BF16)

    def value_block(hh, kb):
        par = hh % 2
        rows = slice(kb * kb_w, (kb + 1) * kb_w)
        pv = jnp.dot(p_bufs[par][:, rows], v_ref[rows, head_cols(hh)], preferred_element_type=F32)
        if kb == 0:
            a_bufs[par][...] = pv
        else:
            a_bufs[par][...] += pv
        if kb == n_kb - 1:
            nrm = a_bufs[par][...] / jnp.sum(l_bufs[par][...], axis=1, keepdims=True)
            o = nrm[0:t] - lam_ref[0] * nrm[t:2 * t]
            ms = jnp.mean(o * o, axis=-1, keepdims=True)
            o_ref[:, head_cols(hh)] = (o * lax.rsqrt(ms + LN_EPS) * g_ref[...] * out_scale
                                       ).astype(o_ref.dtype)

    for stage in range(heads + 2):
        for kb in range(n_kb):
            if stage < heads:
                score_block(stage, kb)
            if 0 <= stage - 1 < heads:
                exp_block(stage - 1, kb)
            if 0 <= stage - 2 < heads:
                value_block(stage - 2, kb)


def _prompt_attention(q, k, v, rel_bias, lam, subln_g, *, batch, seq, t, out_scale):
    assert t >= MAX_DISTANCE
    nq = seq // t
    width = N_HEADS * V_DIM
    q3 = q.reshape(batch, seq, width)
    v3 = v.reshape(batch, seq, width)
    bias = _bias_tiles(rel_bias, t)
    smem = pl.BlockSpec(memory_space=pltpu.SMEM)
    pieces = []
    for i in range(nq):
        n_far = max(i - 1, 0)
        near = min(i + 1, 2) * t
        keys = n_far * t + near
        hp = ATTN_HEADS_PER_STEP
        body = functools.partial(_attn_body, t=t, n_far=n_far, near=near, heads=hp, out_scale=out_scale)
        pair = lambda shape, dtype: [pltpu.VMEM(shape, dtype)] * 2
        pieces.append(pl.pallas_call(
            body, grid=(N_HEADS // hp, batch),
            in_specs=[smem, smem,
                      pl.BlockSpec((None, t, hp * V_DIM), lambda h, b, i=i: (b, i, h)),
                      pl.BlockSpec((None, hp * V_DIM, keys), lambda h, b: (b, h, 0)),
                      pl.BlockSpec((None, keys, hp * V_DIM), lambda h, b: (b, 0, h)),
                      pl.BlockSpec((hp, t, 2 * t), lambda h, b: (h, 0, 0)),
                      pl.BlockSpec((1, V_DIM), lambda h, b: (0, 0))],
            out_specs=pl.BlockSpec((None, t, hp * V_DIM), lambda h, b: (b, 0, h)),
            out_shape=jax.ShapeDtypeStruct((batch, t, width), BF16),
            scratch_shapes=(pair((2 * t, keys), F32) + pair((2 * t, keys), BF16)
                            + pair((2 * t, LANES), F32) + pair((2 * t, LANES), F32)
                            + pair((2 * t, LANES), F32) + pair((2 * t, V_DIM), F32)),
            compiler_params=_params(("arbitrary", "arbitrary")),
            name=f"prompt_attention_q{i}")(lam, rel_bias, q3, k, v3, bias, subln_g.reshape(1, V_DIM)))
    return jnp.stack(pieces, axis=1).reshape(batch * seq, width)


def _decode_body(pt_ref, lam_ref, q_ref, kn_ref, vn_ref, rbt_ref, bucket_ref, g_ref, *rest,
                 pages, page, n_steps, out_scale):
    k_refs = rest[:pages]
    v_refs = rest[pages:2 * pages]
    o_ref = rest[2 * pages]
    qexp_sc, bias_sc, m_sc, l_sc, acc_sc = rest[2 * pages + 1:]
    step = pl.program_id(1)
    rows = 2 * N_HEADS
    width = N_HEADS * V_DIM

    @pl.when(step == 0)
    def _():
        row = lax.broadcasted_iota(jnp.int32, (rows, width), 0)
        col = lax.broadcasted_iota(jnp.int32, (rows, width), 1)
        own_qk = (col // HEAD_DIM) == (row % N_HEADS) * 2 + row // N_HEADS
        q = (q_ref[...] * (HEAD_DIM ** -0.5)).astype(BF16).astype(F32)
        qexp = jnp.where(own_qk, jnp.broadcast_to(q, (rows, width)), 0.0)
        qexp_sc[...] = qexp.astype(BF16)
        bias_sc[...] = _bucket_lookup(bucket_ref[...], lambda bk: rbt_ref[:, bk:bk + 1])
        kn = kn_ref[...].astype(BF16).astype(F32)
        s_self = jnp.sum(qexp * kn, axis=1, keepdims=True) + rbt_ref[:, 0:1]
        m_sc[...] = s_self
        l_sc[...] = jnp.ones(l_sc.shape, F32)
        vn = vn_ref[...].astype(BF16).astype(F32)
        acc_sc[...] = jnp.broadcast_to(vn[:, None, :], acc_sc.shape)

    qexp = qexp_sc[...]
    bias_far = rbt_ref[:, NUM_BUCKETS - 1:NUM_BUCKETS]
    is_last = step == n_steps - 1
    s_parts = []
    for p in range(pages):
        s = jnp.dot(qexp, k_refs[p][...].astype(BF16), preferred_element_type=F32)
        if p == pages - 1:
            s = s + jnp.where(is_last, bias_sc[...], bias_far)
        else:
            s = s + bias_far
        s_parts.append(s)
    m_old = m_sc[...]
    m_new = m_old
    for s in s_parts:
        m_new = jnp.maximum(m_new, jnp.max(s, axis=1, keepdims=True))
    a = jnp.exp(m_old - m_new)
    l_new = a * l_sc[...]
    probs = []
    for p in range(pages):
        pr = jnp.exp(s_parts[p] - m_new)
        l_new = l_new + jnp.sum(pr, axis=1, keepdims=True)
        probs.append(pr.astype(BF16))
    probs = jnp.concatenate(probs, axis=1)
    m_sc[...] = m_new
    l_sc[...] = l_new
    for h in range(N_HEADS):
        vh = jnp.concatenate([v_refs[p][pl.ds(h, page, stride=N_HEADS), :].astype(BF16)
                              for p in range(pages)], axis=0)
        acc_sc[h] = a * acc_sc[h] + jnp.dot(probs, vh, preferred_element_type=F32)

    @pl.when(is_last)
    def _():
        nrm = acc_sc[...] / l_new[None]
        r = lax.broadcasted_iota(jnp.int32, nrm.shape, 1)
        hh = lax.broadcasted_iota(jnp.int32, nrm.shape, 0)
        coef = jnp.where(r == hh, 1.0, jnp.where(r == hh + N_HEADS, -lam_ref[0], 0.0))
        d = jnp.sum(coef * nrm, axis=1)
        ms = jnp.mean(d * d, axis=-1, keepdims=True)
        o_ref[...] = (d * lax.rsqrt(ms + LN_EPS) * g_ref[...] * out_scale).astype(o_ref.dtype)


def _sample_attention(q, k_new, v_new, cache_k, cache_v, page_table, rel_bias, lam, subln_g,
                      *, pages, out_scale):
    bs, width = q.shape
    page = cache_k.shape[2]
    n_pages = page_table.shape[1]
    n_steps = n_pages // pages
    past = n_pages * page
    kpos = past - page + jnp.arange(page, dtype=jnp.int32)
    bucket_last = _rel_bucket(past - kpos).reshape(1, page)
    rbt = jnp.tile(rel_bias.T, (2, 1))

    def tok_spec():
        return pl.BlockSpec((None, 1, width), lambda b, s, pt: (b, 0, 0))

    def page_spec(shape, p):
        return pl.BlockSpec((None,) + shape, lambda b, s, pt, p=p: (pt[b, s * pages + p], 0, 0))

    full = lambda shape: pl.BlockSpec(shape, lambda b, s, pt: tuple(0 for _ in shape))
    head_spec = pl.BlockSpec((None, N_HEADS, V_DIM), lambda b, s, pt: (b, 0, 0))
    in_specs = ([pl.BlockSpec(memory_space=pltpu.SMEM), tok_spec(), tok_spec(), head_spec,
                 full((2 * N_HEADS, NUM_BUCKETS)), full((1, page)), full((1, V_DIM))]
                + [page_spec((width, page), p) for p in range(pages)]
                + [page_spec((page * N_HEADS, V_DIM), p) for p in range(pages)])
    rows = 2 * N_HEADS
    body = functools.partial(_decode_body, pages=pages, page=page, n_steps=n_steps,
                             out_scale=out_scale)
    out = pl.pallas_call(
        body,
        grid_spec=pltpu.PrefetchScalarGridSpec(
            num_scalar_prefetch=1, grid=(bs, n_steps), in_specs=in_specs,
            out_specs=head_spec,
            scratch_shapes=[pltpu.VMEM((rows, width), BF16), pltpu.VMEM((rows, page), F32),
                            pltpu.VMEM((rows, 1), F32), pltpu.VMEM((rows, 1), F32),
                            pltpu.VMEM((N_HEADS, rows, V_DIM), F32)]),
        out_shape=jax.ShapeDtypeStruct((bs, N_HEADS, V_DIM), BF16),
        compiler_params=_params(("arbitrary", "arbitrary")),
        name="sample_attention")(
            page_table, lam, q.reshape(bs, 1, width), k_new.reshape(bs, 1, width),
            v_new.reshape(bs, N_HEADS, V_DIM), rbt, bucket_last, subln_g.reshape(1, V_DIM),
            *([cache_k] * pages), *([cache_v] * pages))
    return out.reshape(bs, width)


def _s5_params(a_re, a_im, log_dt, b_re, b_im, c_re, c_im):
    dt = jnp.exp(log_dt)[:, None]
    mag = jnp.exp(a_re * dt)
    ang = a_im * dt
    abar_re = mag * jnp.cos(ang)
    abar_im = mag * jnp.sin(ang)
    den = a_re * a_re + a_im * a_im
    f_re = ((abar_re - 1.0) * a_re + abar_im * a_im) / den
    f_im = (abar_im * a_re - (abar_re - 1.0) * a_im) / den
    bb_re = f_re[..., None] * b_re - f_im[..., None] * b_im
    bb_im = f_re[..., None] * b_im + f_im[..., None] * b_re
    eye = jnp.eye(SLAB_GROUPS, dtype=F32)

    def in_slabs(bb):
        tt = bb.reshape(N_SLABS, SLAB_GROUPS, STATE_DIM, SSM_GROUP).transpose(0, 1, 3, 2)
        full = tt[:, :, :, None, :] * eye[None, :, None, :, None]
        return full.reshape(N_SLABS, SLAB_IN, SLAB_STATE).astype(BF16)

    ab_re = abar_re[..., None] * bb_re - abar_im[..., None] * bb_im
    ab_im = abar_re[..., None] * bb_im + abar_im[..., None] * bb_re

    def out_slabs(cc):
        tt = cc.reshape(N_SLABS, SLAB_GROUPS, SSM_GROUP, STATE_DIM).transpose(0, 1, 3, 2)
        full = tt[:, :, :, None, :] * eye[None, :, None, :, None]
        return full.reshape(N_SLABS, SLAB_STATE, SLAB_IN).astype(BF16)

    return (abar_re.reshape(N_SLABS, SLAB_STATE), abar_im.reshape(N_SLABS, SLAB_STATE),
            in_slabs(bb_re), in_slabs(bb_im), out_slabs(c_re), out_slabs(c_im),
            in_slabs(ab_re), in_slabs(ab_im))


def _glu_out(y, u, d_ref, wglu_ref):
    g = jax.nn.gelu(y + d_ref[...] * u)
    gate = jnp.dot(g.astype(BF16), wglu_ref[...].astype(BF16), preferred_element_type=F32)
    return g * jax.nn.sigmoid(gate)


def _s5_prompt_body(u_ref, ar_ref, ai_ref, bbr_ref, bbi_ref, cr_ref, ci_ref, d_ref, wglu_ref,
                    o_ref, sre_ref, sim_ref, xr_sc, xi_sc, pw_sc, cr_sc, ci_sc, perm_sc, *, batch):
    c = pl.program_id(0)
    sub = SUBLANES
    _, tc, width = u_ref.shape
    n_rows = batch * tc
    t = lax.broadcasted_iota(jnp.int32, (sub, SLAB_STATE), 0)
    first = t < batch

    def cmul(pr, pi, qr, qi):
        return pr * qr - pi * qi, pr * qi + pi * qr

    @pl.when(c == 0)
    def _():
        for j in range(N_SLABS):
            a = (jnp.broadcast_to(ar_ref[j:j + 1, :], t.shape), jnp.broadcast_to(ai_ref[j:j + 1, :], t.shape))
            a2 = cmul(*a, *a)
            for part in range(2):
                pw_sc[j, part] = jnp.where(first, a[part], a2[part])
        cr_sc[...] = jnp.zeros(cr_sc.shape, F32)
        ci_sc[...] = jnp.zeros(ci_sc.shape, F32)
        r = lax.broadcasted_iota(jnp.int32, (n_rows, n_rows), 0)
        k = lax.broadcasted_iota(jnp.int32, (n_rows, n_rows), 1)
        perm_sc[0] = jnp.where(k == (r % batch) * tc + r // batch, 1.0, 0.0).astype(BF16)
        perm_sc[1] = jnp.where(r == (k % batch) * tc + k // batch, 1.0, 0.0).astype(BF16)

    def permute(which, x):
        return jnp.dot(perm_sc[which], x, preferred_element_type=F32)

    u_seq = u_ref[...].reshape(n_rows, width)
    hi = u_seq.astype(BF16)
    rest = u_seq - hi.astype(F32)
    mid = rest.astype(BF16)
    lo = (rest - mid.astype(F32)).astype(BF16)
    u_hi = permute(0, hi)
    ub = u_hi.astype(BF16)
    u = u_hi + permute(0, mid) + permute(0, lo)
    row = lax.broadcasted_iota(jnp.int32, u_hi.shape, 0)
    u_before = jnp.where(row % sub >= batch, pltpu.roll(u_hi, batch, 0), 0.0).astype(BF16)
    for j in range(N_SLABS):
        cols = slice(j * SLAB_IN, (j + 1) * SLAB_IN)
        uj = jnp.concatenate([ub[:, cols], u_before[:, cols]], axis=1)
        xr_sc[j] = jnp.dot(uj, bbr_ref[j], preferred_element_type=F32)
        xi_sc[j] = jnp.dot(uj, bbi_ref[j], preferred_element_type=F32)

    for j in range(N_SLABS):
        pr, pi = cr_sc[j], ci_sc[j]
        for v in range(n_rows // sub):
            rows = slice(v * sub, (v + 1) * sub)
            lr = jnp.where(first, pltpu.roll(pr, batch, 0), pr)
            li = jnp.where(first, pltpu.roll(pi, batch, 0), pi)
            er, ei = cmul(pw_sc[j, 0], pw_sc[j, 1], lr, li)
            pr, pi = xr_sc[j, rows, :] + er, xi_sc[j, rows, :] + ei
            xr_sc[j, rows, :] = pr
            xi_sc[j, rows, :] = pi
        cr_sc[j] = pr
        ci_sc[j] = pi
    sre_ref[...] = cr_sc[...]
    sim_ref[...] = ci_sc[...]

    ys = []
    for j in range(N_SLABS):
        ys.append(jnp.dot(xr_sc[j].astype(BF16), cr_ref[j], preferred_element_type=F32)
                  - jnp.dot(xi_sc[j].astype(BF16), ci_ref[j], preferred_element_type=F32))
    out = _glu_out(jnp.concatenate(ys, axis=1), u, d_ref, wglu_ref).astype(BF16)
    o_ref[...] = permute(1, out).astype(o_ref.dtype).reshape(batch, tc, width)


def _s5_prompt(u, sp, d, w_glu, *, batch, seq, tc):
    assert SUBLANES == 2 * batch, "tile = two positions of every sequence"
    ar, ai, bbr, bbi, cr, ci, abr, abi = sp
    bbr = jnp.concatenate([bbr, abr], axis=1)
    bbi = jnp.concatenate([bbi, abi], axis=1)
    width = u.shape[1]
    rows = batch * tc
    const = lambda shape: pl.BlockSpec(shape, lambda c: tuple(0 for _ in shape))
    block = pl.BlockSpec((batch, tc, width), lambda c: (0, c, 0))
    state_spec = const((N_SLABS, SUBLANES, SLAB_STATE))
    state_shape = jax.ShapeDtypeStruct((N_SLABS, SUBLANES, SLAB_STATE), F32)
    out, s_re, s_im = pl.pallas_call(
        functools.partial(_s5_prompt_body, batch=batch), grid=(seq // tc,),
        in_specs=[block, const(ar.shape), const(ai.shape), const(bbr.shape), const(bbi.shape),
                  const(cr.shape), const(ci.shape), const((1, width)), const(w_glu.shape)],
        out_specs=[block, state_spec, state_spec],
        out_shape=[jax.ShapeDtypeStruct((batch, seq, width), BF16), state_shape, state_shape],
        scratch_shapes=[pltpu.VMEM((N_SLABS, rows, SLAB_STATE), F32),
                        pltpu.VMEM((N_SLABS, rows, SLAB_STATE), F32),
                        pltpu.VMEM((N_SLABS, 2, SUBLANES, SLAB_STATE), F32),
                        pltpu.VMEM((N_SLABS, SUBLANES, SLAB_STATE), F32),
                        pltpu.VMEM((N_SLABS, SUBLANES, SLAB_STATE), F32),
                        pltpu.VMEM((2, rows, rows), BF16)],
        compiler_params=_params(("arbitrary",)),
        name="s5_prompt")(u.reshape(batch, seq, width), ar, ai, bbr, bbi, cr, ci,
                          d.reshape(1, width), w_glu)
    return out.reshape(batch * seq, width), s_re, s_im


def _s5_sample_body(u_ref, x0r_ref, x0i_ref, ar_ref, ai_ref, bbr_ref, bbi_ref, cr_ref, ci_ref,
                    d_ref, wglu_ref, o_ref, sre_ref, sim_ref):
    u = u_ref[...]
    ub = u.astype(BF16)
    ys = []
    for j in range(N_SLABS):
        uj = ub[:, j * SLAB_IN:(j + 1) * SLAB_IN]
        ar = ar_ref[j:j + 1, :]
        ai = ai_ref[j:j + 1, :]
        x0r = x0r_ref[j]
        x0i = x0i_ref[j]
        xr = ar * x0r - ai * x0i + jnp.dot(uj, bbr_ref[j], preferred_element_type=F32)
        xi = ar * x0i + ai * x0r + jnp.dot(uj, bbi_ref[j], preferred_element_type=F32)
        sre_ref[j] = xr
        sim_ref[j] = xi
        ys.append(jnp.dot(xr.astype(BF16), cr_ref[j], preferred_element_type=F32)
                  - jnp.dot(xi.astype(BF16), ci_ref[j], preferred_element_type=F32))
    o_ref[...] = _glu_out(jnp.concatenate(ys, axis=1), u, d_ref, wglu_ref).astype(o_ref.dtype)


def _s5_sample(u, x0_re, x0_im, sp, d, w_glu):
    ar, ai, bbr, bbi, cr, ci = sp[:6]
    bs, width = u.shape
    state_shape = jax.ShapeDtypeStruct((N_SLABS, bs, SLAB_STATE), F32)
    return pl.pallas_call(
        _s5_sample_body,
        out_shape=[jax.ShapeDtypeStruct((bs, width), BF16), state_shape, state_shape],
        compiler_params=pltpu.CompilerParams(vmem_limit_bytes=VMEM_LIMIT_BYTES),
        name="s5_sample")(u, x0_re, x0_im, ar, ai, bbr, bbi, cr, ci, d.reshape(1, width), w_glu)


def _gated(c):
    gate = c[:, :LANES]
    return gate * jax.nn.sigmoid(gate) * c[:, LANES:]


def _up_prompt_body(h_ref, wg_ref, wv_ref, cwg_ref, cwv_ref, cbg_ref, cbv_ref,
                    act_ref, cg_ref, cv_ref, w_sc, *, rows):
    w_sc[:, 0:LANES] = wg_ref[...].astype(BF16)
    w_sc[:, LANES:2 * LANES] = wv_ref[...].astype(BF16)
    cw = jnp.concatenate([cwg_ref[...], cwv_ref[...]], axis=1)
    cb = jnp.concatenate([cbg_ref[...], cbv_ref[...]], axis=1)
    seq = h_ref.shape[0]
    sub = SUBLANES
    row8 = lax.broadcasted_iota(jnp.int32, (sub, 2 * LANES), 0)

    prev = jnp.zeros((sub, 2 * LANES), F32)
    for c in range(seq // rows):
        up = jnp.dot(h_ref[c * rows:(c + 1) * rows, :], w_sc[...], preferred_element_type=F32)
        cur = cb + cw[CONV_W - 1:CONV_W] * up
        for back in range(1, CONV_W):
            rolled = pltpu.roll(up, back, 0)
            head = jnp.where(row8 < back, pltpu.roll(prev, back, 0), rolled[0:sub])
            shifted = jnp.concatenate([head, rolled[sub:]], axis=0)
            cur = cur + cw[CONV_W - 1 - back:CONV_W - back] * shifted
        act_ref[c * rows:(c + 1) * rows, :] = _gated(cur).astype(act_ref.dtype)
        prev = up[rows - sub:, :]
    tail = prev[sub - (CONV_W - 1):, :]
    cg_ref[...] = tail[:, :LANES]
    cv_ref[...] = tail[:, LANES:]


def _up_prompt(hb, w_up, conv_w, conv_b, *, batch, seq):
    d_model = hb.shape[1]
    d_ff = w_up.shape[1] // 2
    nb = d_ff // LANES
    cb = conv_b.reshape(1, 2 * d_ff)
    col = lambda off: (lambda b, j: (0, j + off))
    tail_spec = pl.BlockSpec((None, CONV_W - 1, LANES), lambda b, j: (b, 0, j))
    tail_shape = jax.ShapeDtypeStruct((batch, CONV_W - 1, d_ff), F32)
    return pl.pallas_call(
        functools.partial(_up_prompt_body, rows=UP_ROW_CHUNK), grid=(batch, nb),
        scratch_shapes=[pltpu.VMEM((d_model, 2 * LANES), BF16)],
        in_specs=[pl.BlockSpec((seq, d_model), lambda b, j: (b, 0)),
                  pl.BlockSpec((d_model, LANES), col(0)), pl.BlockSpec((d_model, LANES), col(nb)),
                  pl.BlockSpec((CONV_W, LANES), col(0)), pl.BlockSpec((CONV_W, LANES), col(nb)),
                  pl.BlockSpec((1, LANES), col(0)), pl.BlockSpec((1, LANES), col(nb))],
        out_specs=[pl.BlockSpec((seq, LANES), lambda b, j: (b, j)), tail_spec, tail_spec],
        out_shape=[jax.ShapeDtypeStruct((batch * seq, d_ff), BF16), tail_shape, tail_shape],
        compiler_params=_params(("arbitrary", "arbitrary")),
        name="up_prompt")(hb, w_up, w_up, conv_w, conv_w, cb, cb)


def _up_sample_body(h_ref, wg_ref, wv_ref, cwg_ref, cwv_ref, cbg_ref, cbv_ref, sg_ref, sv_ref,
                    act_ref, cg_ref, cv_ref):
    w = jnp.concatenate([wg_ref[...].astype(BF16), wv_ref[...].astype(BF16)], axis=1)
    up = jnp.dot(h_ref[...], w, preferred_element_type=F32)
    cw = jnp.concatenate([cwg_ref[...], cwv_ref[...]], axis=1)
    cb = jnp.concatenate([cbg_ref[...], cbv_ref[...]], axis=1)
    c = cb + cw[CONV_W - 1:CONV_W] * up
    for tap in range(CONV_W - 1):
        st = jnp.concatenate([sg_ref[tap], sv_ref[tap]], axis=1)
        c = c + cw[tap:tap + 1] * st
    act_ref[...] = _gated(c).astype(act_ref.dtype)
    for tap in range(1, CONV_W - 1):
        cg_ref[tap - 1] = sg_ref[tap]
        cv_ref[tap - 1] = sv_ref[tap]
    cg_ref[CONV_W - 2] = up[:, :LANES]
    cv_ref[CONV_W - 2] = up[:, LANES:]


def _up_sample(hb, w_up, conv_w, conv_b, state):
    bs, d_model = hb.shape
    d_ff = w_up.shape[1] // 2
    nb = d_ff // LANES
    cb = conv_b.reshape(1, 2 * d_ff)
    col = lambda off: (lambda j: (0, j + off))
    st = lambda off: pl.BlockSpec((CONV_W - 1, bs, LANES), lambda j, off=off: (0, 0, j + off))
    tail_spec = pl.BlockSpec((CONV_W - 1, bs, LANES), lambda j: (0, 0, j))
    tail_shape = jax.ShapeDtypeStruct((CONV_W - 1, bs, d_ff), F32)
    return pl.pallas_call(
        _up_sample_body, grid=(nb,),
        in_specs=[pl.BlockSpec((bs, d_model), lambda j: (0, 0)),
                  pl.BlockSpec((d_model, LANES), col(0)), pl.BlockSpec((d_model, LANES), col(nb)),
                  pl.BlockSpec((CONV_W, LANES), col(0)), pl.BlockSpec((CONV_W, LANES), col(nb)),
                  pl.BlockSpec((1, LANES), col(0)), pl.BlockSpec((1, LANES), col(nb)),
                  st(0), st(nb)],
        out_specs=[pl.BlockSpec((bs, LANES), lambda j: (0, j)), tail_spec, tail_spec],
        out_shape=[jax.ShapeDtypeStruct((bs, d_ff), BF16), tail_shape, tail_shape],
        compiler_params=_params(("arbitrary",)),
        name="up_sample")(hb, w_up, w_up, conv_w, conv_w, cb, cb, state, state)


def _one(x):
    return (x,)


def _sigmoid_out(acc):
    return (jax.nn.sigmoid(acc),)


def _both(acc):
    return (acc, acc)


def _merge(pa, ps, ga, gs):
    return (ga * pa + gs * ps,)


def _layer(xp, xs, w, *, batch, seq, alpha, attend_p, attend_s, ssm_p, ssm_s, up_p, up_s):
    d_model = xp.shape[1]
    qk_w = N_HEADS * 2 * HEAD_DIM
    v_w = N_HEADS * V_DIM
    ssm_w = d_model // 2
    w_in = w["w_in"]
    g1, b1 = w["ln1_g"].reshape(1, d_model), w["ln1_b"].reshape(1, d_model)
    big = dict(tm=MM_ROWS, tn=MM_COLS)
    c = 0
    (q,), (q_s,), xb = _matmul([(xp, xs, w_in, c)], [], [BF16], _one, n_cols=qk_w, name="proj_q",
                               emit_lhs=True, **big)
    c += qk_w
    kt, ktb, k_s = _proj_transposed(xb, xs, w_in, c, n_cols=qk_w, batch=batch, seq=seq, tm=MM_ROWS,
                                    name="proj_kt")
    c += qk_w
    c_v = c
    c += v_w
    (u,), (u_s,) = _matmul([(xb, xs, w_in, c)], [], [F32], _one, n_cols=ssm_w, name="proj_u", **big)
    c += ssm_w
    (ga,), (ga_s,) = _matmul([(xb, xs, w_in, c)], [], [F32], _sigmoid_out, n_cols=d_model, name="gate_a", **big)
    c += d_model
    (gs,), (gs_s,) = _matmul([(xb, xs, w_in, c)], [], [F32], _sigmoid_out, n_cols=d_model, name="gate_s", **big)
    (v, vb), (v_s, _) = _matmul([(xb, xs, w_in, c_v)], [], [F32, BF16], _both, n_cols=v_w, name="proj_v", **big)

    attn = attend_p(q, ktb, vb)
    ssm_out, re_p, im_p = ssm_p(u)
    ssm_out_s, re_s, im_s = ssm_s(u_s)
    attn_s = attend_s(q_s, k_s, v_s)

    (merged,), (merged_s,) = _matmul(
        [(attn, attn_s, w["w_proj_attn"], 0), (ssm_out, ssm_out_s, w["w_proj_ssm"], 0)],
        [(ga, ga_s), (gs, gs_s)], [BF16], _merge, n_cols=d_model, name="merge", **big)

    def post_ln1(acc, res, g, b):
        h = _ln(alpha * res + acc, g, b)
        return h, h

    (h, hb), (h_s, hb_s) = _matmul([(merged, merged_s, w["w_out"], 0)], [(xp, xs), g1, b1], [F32, BF16],
                                   post_ln1, n_cols=d_model, tm=LN_MM_ROWS, tn=d_model, name="out_proj_ln1")
    act, conv_p = up_p(hb)
    act_s, conv_s = up_s(hb_s)
    (r2,), (r2_s,) = _matmul([(act, act_s, w["w_down"], 0)], [(h, h_s)], [F32],
                             lambda acc, res: (alpha * res + acc,), n_cols=d_model,
                             tm=DOWN_ROWS, tn=DOWN_COLS, name="down_proj")
    y = _layer_norm(r2, w["ln2_g"], w["ln2_b"], tm=LN_ROWS, name="ln2")
    y_s = _layer_norm(r2_s, w["ln2_g"], w["ln2_b"], tm=LN_ROWS, name="ln2_sample")
    return (y, kt, v, re_p, im_p, conv_p), (y_s, k_s, v_s, re_s, im_s, conv_s)


def kernel(x_prompt, x_sample, cache_k, cache_v, state_ssm_re, state_ssm_im, state_conv, page_table, rel_bias, w_in, lambda_q1, lambda_k1, lambda_q2, lambda_k2, subln_g, ssm_a_re, ssm_a_im, ssm_log_dt, ssm_b_re, ssm_b_im, ssm_c_re, ssm_c_im, ssm_d, w_glu, w_proj_attn, w_proj_ssm, w_out, ln1_g, ln1_b, w_up, conv_w, conv_b, w_down, ln2_g, ln2_b):
    depth = w_in.shape[0]
    assert depth == 1, "single-layer trunk"
    bp, seq, d_model = x_prompt.shape
    bs, dec_seq, _ = x_sample.shape
    assert dec_seq == 1
    n_pool, page = cache_k.shape[1], cache_k.shape[2]
    d_ff = w_down.shape[1]
    n_groups = ssm_a_re.shape[1]
    assert n_groups == N_SLABS * SLAB_GROUPS and d_ff % LANES == 0
    alpha = (2.0 * depth) ** 0.25
    width = N_HEADS * V_DIM

    hp = x_prompt.reshape(bp * seq, d_model)
    hs = x_sample.reshape(bs, d_model)
    outs = {}
    for l in range(depth):
        lam_init = 0.8 - 0.6 * math.exp(-0.3 * l)
        out_scale = 1.0 - lam_init
        lam = (jnp.exp(jnp.sum(lambda_q1[l] * lambda_k1[l]))
               - jnp.exp(jnp.sum(lambda_q2[l] * lambda_k2[l])) + lam_init).reshape(1)
        w = dict(w_in=w_in[l], w_proj_attn=w_proj_attn[l], w_proj_ssm=w_proj_ssm[l], w_out=w_out[l],
                 ln1_g=ln1_g[l], ln1_b=ln1_b[l], w_down=w_down[l], ln2_g=ln2_g[l], ln2_b=ln2_b[l])
        sp = _s5_params(ssm_a_re[l], ssm_a_im[l], ssm_log_dt[l], ssm_b_re[l], ssm_b_im[l],
                        ssm_c_re[l], ssm_c_im[l])

        def attend_p(q, kt, v):
            return _prompt_attention(q, kt, v, rel_bias, lam, subln_g[l], batch=bp, seq=seq,
                                     t=ATTN_BLOCK, out_scale=out_scale)

        def ssm_p(u):
            o, sr, si = _s5_prompt(u, sp, ssm_d[l], w_glu[l], batch=bp, seq=seq, tc=S5_POSITIONS)
            last = lambda st: st[:, SUBLANES - bp:, :].transpose(1, 0, 2)
            return o, last(sr), last(si)

        def up_p(hb):
            act, cg, cv = _up_prompt(hb, w_up[l], conv_w[l], conv_b[l], batch=bp, seq=seq)
            return act, jnp.concatenate([cg, cv], axis=-1)

        ck = cache_k[l].transpose(0, 2, 3, 4, 1).reshape(n_pool, width, page)
        cv_ = cache_v[l].reshape(n_pool, page * N_HEADS, V_DIM)

        def attend_s(q, k, v):
            return _sample_attention(q.astype(F32), k, v, ck, cv_, page_table, rel_bias, lam,
                                     subln_g[l], pages=DECODE_PAGES, out_scale=out_scale)

        def ssm_s(u):
            x0r = state_ssm_re[l].reshape(bs, N_SLABS, SLAB_STATE).transpose(1, 0, 2)
            x0i = state_ssm_im[l].reshape(bs, N_SLABS, SLAB_STATE).transpose(1, 0, 2)
            o, sr, si = _s5_sample(u, x0r, x0i, sp, ssm_d[l], w_glu[l])
            return o, sr.transpose(1, 0, 2), si.transpose(1, 0, 2)

        def up_s(hb):
            act, cg, cv = _up_sample(hb, w_up[l], conv_w[l], conv_b[l],
                                     state_conv[l].transpose(1, 0, 2))
            return act, jnp.concatenate([cg, cv], axis=-1).transpose(1, 0, 2)

        (hp, kt_p, v_p, re_p, im_p, c_p), (hs, k_s, v_s, re_s, im_s, c_s) = _layer(
            hp, hs, w, batch=bp, seq=seq, alpha=alpha, attend_p=attend_p, attend_s=attend_s,
            ssm_p=ssm_p, ssm_s=ssm_s, up_p=up_p, up_s=up_s)
        k_p = kt_p.reshape(bp, N_HEADS, 2, HEAD_DIM, seq).transpose(0, 4, 1, 2, 3)

        for name, val in (("kp", k_p.reshape(bp, seq, N_HEADS, 2, HEAD_DIM)),
                          ("vp", v_p.reshape(bp, seq, N_HEADS, V_DIM)),
                          ("rep", re_p.reshape(bp, n_groups, STATE_DIM)),
                          ("imp", im_p.reshape(bp, n_groups, STATE_DIM)),
                          ("cp", c_p),
                          ("ks", k_s.reshape(bs, 1, N_HEADS, 2, HEAD_DIM)),
                          ("vs", v_s.reshape(bs, 1, N_HEADS, V_DIM)),
                          ("res", re_s.reshape(bs, n_groups, STATE_DIM)),
                          ("ims", im_s.reshape(bs, n_groups, STATE_DIM)),
                          ("cs", c_s)):
            outs.setdefault(name, []).append(val)

    st = {k: jnp.stack(v, axis=0) for k, v in outs.items()}
    return (hp.reshape(bp, seq, d_model), hs.reshape(bs, 1, d_model), st["kp"], st["vp"], st["rep"],
            st["imp"], st["cp"], st["ks"], st["vs"], st["res"], st["ims"], st["cs"])
```

```python
import functools
import math

import jax
import jax.numpy as jnp
from jax import lax
from jax.experimental import pallas as pl
from jax.experimental.pallas import tpu as pltpu

F32 = jnp.float32
BF16 = jnp.bfloat16

N_HEADS = 8
HEAD_DIM = 64
V_DIM = 2 * HEAD_DIM
SSM_GROUP = 16
STATE_DIM = 64
CONV_W = 3
NUM_BUCKETS = 32
MAX_EXACT = NUM_BUCKETS // 2
MAX_DISTANCE = 128
LN_EPS = 1e-5
NEG_INF = -1e30

VMEM_LIMIT_BYTES = 56 * 1024 * 1024
LANES = 128
SLAB_GROUPS = 8
N_SLABS = 8
SLAB_IN = SLAB_GROUPS * SSM_GROUP
SLAB_STATE = SLAB_GROUPS * STATE_DIM
SUBLANES = 8
MM_ROWS, MM_COLS = 1024, 1024
LN_MM_ROWS = 512
DOWN_ROWS, DOWN_COLS = 512, 512
LN_ROWS = 512
ATTN_BLOCK = 256
ATTN_KEY_BLOCK = 256
ATTN_HEADS_PER_STEP = 4
S5_POSITIONS = 64
DECODE_PAGES = 16
UP_ROW_CHUNK = 512


def _params(sem):
    return pltpu.CompilerParams(dimension_semantics=sem, vmem_limit_bytes=VMEM_LIMIT_BYTES)


def _mm_body(*refs, n_pairs, n_extra, n_out, epilogue, emit_lhs):
    it = iter(refs)
    take = lambda n: [next(it) for _ in range(n)]
    x_refs, xs_refs, w_refs = take(n_pairs), take(n_pairs), take(n_pairs)
    extra, extra_s = take(n_extra), take(n_extra)
    outs, outs_s = take(n_out), take(n_out)
    lhs_out = take(1) if emit_lhs else []
    wbf = take(n_pairs)

    def apply(lhs_refs, extra_refs, out_refs, keep):
        lhs = [x[...].astype(BF16) for x in lhs_refs]
        for ref in keep:
            ref[...] = lhs[0]
        accs = [jnp.dot(x, s[...], preferred_element_type=F32) for x, s in zip(lhs, wbf)]
        for o, r in zip(out_refs, epilogue(*accs, *[e[...] for e in extra_refs])):
            o[...] = r.astype(o.dtype)

    @pl.when(pl.program_id(1) == 0)
    def _():
        for w, s in zip(w_refs, wbf):
            s[...] = w[...].astype(BF16)
        apply(xs_refs, extra_s, outs_s, [])

    apply(x_refs, extra, outs, lhs_out)


def _matmul(pairs, extras, out_dtypes, epilogue, *, n_cols, tm, tn, name, emit_lhs=False):
    m = pairs[0][0].shape[0]
    ms = pairs[0][1].shape[0]
    grid = (n_cols // tn, m // tm)
    assert not emit_lhs or grid[0] == 1
    w_mode = dict(pipeline_mode=pl.Buffered(1)) if grid[0] == 1 else {}
    specs_x, specs_xs, specs_w, scratch = [], [], [], []
    for x, xs, w, c0 in pairs:
        k = x.shape[1]
        specs_x.append(pl.BlockSpec((tm, k), lambda j, i: (i, 0)))
        specs_xs.append(pl.BlockSpec((ms, k), lambda j, i: (0, 0)))
        specs_w.append(pl.BlockSpec((k, tn), lambda j, i, off=c0 // tn: (0, j + off), **w_mode))
        scratch.append(pltpu.VMEM((k, tn), BF16))
    specs_e, specs_es, args_e, args_es = [], [], [], []
    for e in extras:
        if isinstance(e, tuple):
            specs_e.append(pl.BlockSpec((tm, tn), lambda j, i: (i, j)))
            specs_es.append(pl.BlockSpec((ms, tn), lambda j, i: (0, j)))
            args_e.append(e[0])
            args_es.append(e[1])
        else:
            specs_e.append(pl.BlockSpec((1, tn), lambda j, i: (0, j)))
            specs_es.append(pl.BlockSpec((1, tn), lambda j, i: (0, j)))
            args_e.append(e)
            args_es.append(e)
    out_shape = ([jax.ShapeDtypeStruct((m, n_cols), d) for d in out_dtypes]
                 + [jax.ShapeDtypeStruct((ms, n_cols), d) for d in out_dtypes])
    out_specs = ([pl.BlockSpec((tm, tn), lambda j, i: (i, j)) for _ in out_dtypes]
                 + [pl.BlockSpec((ms, tn), lambda j, i: (0, j)) for _ in out_dtypes])
    if emit_lhs:
        k0 = pairs[0][0].shape[1]
        out_shape.append(jax.ShapeDtypeStruct((m, k0), BF16))
        out_specs.append(pl.BlockSpec((tm, k0), lambda j, i: (i, 0)))
    body = functools.partial(_mm_body, n_pairs=len(pairs), n_extra=len(extras),
                             n_out=len(out_dtypes), epilogue=epilogue, emit_lhs=emit_lhs)
    outs = pl.pallas_call(
        body, grid=grid, in_specs=specs_x + specs_xs + specs_w + specs_e + specs_es,
        out_specs=out_specs, out_shape=out_shape, scratch_shapes=scratch,
        compiler_params=_params(("arbitrary", "arbitrary")), name=name)(
            *[p[0] for p in pairs], *[p[1] for p in pairs], *[p[2] for p in pairs],
            *args_e, *args_es)
    n = len(out_dtypes)
    if emit_lhs:
        return outs[:n], outs[n:2 * n], outs[2 * n]
    return outs[:n], outs[n:]


def _proj_t_body(x_ref, xs_ref, w_ref, o_ref, ob_ref, os_ref, wt_sc):
    @pl.when(pl.program_id(0) == 0)
    def _():
        for c in range(w_ref.shape[1] // LANES):
            cols = slice(c * LANES, (c + 1) * LANES)
            wt_sc[cols, :] = w_ref[:, cols].T.astype(BF16)
        os_ref[...] = lax.dot_general(xs_ref[...].astype(BF16), wt_sc[...], (((1,), (1,)), ((), ())),
                                      preferred_element_type=F32)

    kt = lax.dot_general(wt_sc[...], x_ref[...], (((1,), (1,)), ((), ())),
                         preferred_element_type=F32)
    o_ref[...] = kt
    ob_ref[...] = kt.astype(BF16)


def _proj_transposed(x, xs, w, c0, *, n_cols, batch, seq, tm, name):
    k = x.shape[1]
    ms = xs.shape[0]
    per_b = seq // tm
    out_spec = pl.BlockSpec((None, n_cols, tm), lambda i: (i // per_b, 0, i % per_b))
    return pl.pallas_call(
        _proj_t_body, grid=(batch * per_b,),
        in_specs=[pl.BlockSpec((tm, k), lambda i: (i, 0)),
                  pl.BlockSpec((ms, k), lambda i: (0, 0)),
                  pl.BlockSpec((k, n_cols), lambda i, off=c0 // n_cols: (0, off),
                               pipeline_mode=pl.Buffered(1))],
        out_specs=[out_spec, out_spec, pl.BlockSpec((ms, n_cols), lambda i: (0, 0))],
        out_shape=[jax.ShapeDtypeStruct((batch, n_cols, seq), F32),
                   jax.ShapeDtypeStruct((batch, n_cols, seq), BF16),
                   jax.ShapeDtypeStruct((ms, n_cols), F32)],
        scratch_shapes=[pltpu.VMEM((n_cols, k), BF16)],
        compiler_params=_params(("arbitrary",)), name=name)(x, xs, w)


def _ln(x, g, b):
    mu = jnp.mean(x, axis=-1, keepdims=True)
    xc = x - mu
    var = jnp.mean(xc * xc, axis=-1, keepdims=True)
    return xc * lax.rsqrt(var + LN_EPS) * g + b


def _ln_body(x_ref, g_ref, b_ref, o_ref):
    o_ref[...] = _ln(x_ref[...], g_ref[...], b_ref[...])


def _layer_norm(x, g, b, *, tm, name):
    m, d = x.shape
    tm = min(tm, m)
    return pl.pallas_call(
        _ln_body, grid=(m // tm,),
        in_specs=[pl.BlockSpec((tm, d), lambda i: (i, 0)),
                  pl.BlockSpec((1, d), lambda i: (0, 0)),
                  pl.BlockSpec((1, d), lambda i: (0, 0))],
        out_specs=pl.BlockSpec((tm, d), lambda i: (i, 0)),
        out_shape=jax.ShapeDtypeStruct((m, d), F32),
        compiler_params=_params(("arbitrary",)), name=name)(x, g.reshape(1, d), b.reshape(1, d))


def _rel_bucket(n):
    n = jnp.maximum(n, 0)
    nf = jnp.maximum(n, 1).astype(F32)
    large = MAX_EXACT + jnp.floor(jnp.log(nf / MAX_EXACT) / math.log(MAX_DISTANCE / MAX_EXACT)
                                  * (NUM_BUCKETS - MAX_EXACT)).astype(jnp.int32)
    large = jnp.minimum(large, NUM_BUCKETS - 1)
    return jnp.where(n < MAX_EXACT, n, large)


def _bucket_lookup(bucket, table_fn):
    out = jnp.zeros(jnp.broadcast_shapes(bucket.shape, table_fn(0).shape), F32)
    for b in range(NUM_BUCKETS):
        out = out + jnp.where(bucket == b, table_fn(b), 0.0)
    return out


def _bias_tile_body(rb_ref, bucket_ref, o_ref):
    h = pl.program_id(0)
    o_ref[...] = _bucket_lookup(bucket_ref[...], lambda bk: rb_ref[bk, h])


def _bias_tiles(rel_bias, t):
    r = jnp.arange(t, dtype=jnp.int32)
    c = jnp.arange(2 * t, dtype=jnp.int32)
    buckets = _rel_bucket(r[:, None] + t - c[None, :])
    return pl.pallas_call(
        _bias_tile_body, grid=(N_HEADS,),
        in_specs=[pl.BlockSpec(memory_space=pltpu.SMEM), pl.BlockSpec((t, 2 * t), lambda h: (0, 0))],
        out_specs=pl.BlockSpec((None, t, 2 * t), lambda h: (h, 0, 0)),
        out_shape=jax.ShapeDtypeStruct((N_HEADS, t, 2 * t), F32),
        compiler_params=_params(("arbitrary",)), name="bias_tiles")(rel_bias, buckets)


def _attn_body(lam_ref, rb_ref, q_ref, k_ref, v_ref, bias_ref, g_ref, o_ref, *scratch,
               t, n_far, near, heads, out_scale):
    s_bufs, p_bufs, m_bufs, mf_bufs, l_bufs, a_bufs = (scratch[0:2], scratch[2:4], scratch[4:6],
                                                       scratch[6:8], scratch[8:10], scratch[10:12])
    hg = pl.program_id(0)
    far = n_far * t
    kb_w = ATTN_KEY_BLOCK
    n_kb = (far + near) // kb_w
    n_tiles = kb_w // LANES

    def fold(x, op):
        out = x[:, 0:LANES]
        for c in range(1, n_tiles):
            out = op(out, x[:, c * LANES:(c + 1) * LANES])
        return out

    def head_cols(hh):
        return slice(hh * V_DIM, (hh + 1) * V_DIM)

    def score_block(hh, kb):
        par = hh % 2
        q = q_ref[:, head_cols(hh)] * (HEAD_DIM ** -0.5)
        lane = lax.broadcasted_iota(jnp.int32, q.shape, 1)
        zero = jnp.zeros_like(q)
        q2 = jnp.concatenate([jnp.where(lane < HEAD_DIM, q, zero), jnp.where(lane >= HEAD_DIM, q, zero)],
                             axis=0)
        cols = slice(kb * kb_w, (kb + 1) * kb_w)
        s = jnp.dot(q2, k_ref[head_cols(hh), cols], preferred_element_type=F32)
        bias_far = rb_ref[NUM_BUCKETS - 1, hg * heads + hh]
        if kb * kb_w < far:
            top = fold(s, jnp.maximum) + bias_far
        else:
            off = kb * kb_w - far
            lo = 2 * t - near + off
            bias = bias_ref[hh, :, lo:lo + kb_w]
            row = lax.broadcasted_iota(jnp.int32, (t, kb_w), 0)
            col = lax.broadcasted_iota(jnp.int32, (t, kb_w), 1)
            keep = col + (off - (near - t)) <= row
            s = jnp.where(jnp.concatenate([keep, keep], axis=0),
                          s + jnp.concatenate([bias, bias], axis=0), NEG_INF)
            top = fold(s, jnp.maximum)
        s_bufs[par][:, cols] = s
        if kb == 0:
            m_bufs[par][...] = top
        else:
            m_bufs[par][...] = jnp.maximum(m_bufs[par][...], top)
        if kb == n_kb - 1:
            m = jnp.broadcast_to(jnp.max(m_bufs[par][...], axis=1, keepdims=True), m_bufs[par].shape)
            m_bufs[par][...] = m
            mf_bufs[par][...] = m - bias_far

    def exp_block(hh, kb):
        par = hh % 2
        cols = slice(kb * kb_w, (kb + 1) * kb_w)
        m = (mf_bufs if kb * kb_w < far else m_bufs)[par][...]
        p = jnp.exp(s_bufs[par][:, cols] - jnp.concatenate([m] * n_tiles, axis=1))
        if kb == 0:
            l_bufs[par][...] = fold(p, jnp.add)
        else:
            l_bufs[par][...] += fold(p, jnp.add)
        p_bufs[par][:, cols] = p.astype(BF16)

    def value_block(hh, kb):
        par = hh % 2
        rows = slice(kb * kb_w, (kb + 1) * kb_w)
        pv = jnp.dot(p_bufs[par][:, rows], v_ref[rows, head_cols(hh)], preferred_element_type=F32)
        if kb == 0:
            a_bufs[par][...] = pv
        else:
            a_bufs[par][...] += pv
        if kb == n_kb - 1:
            nrm = a_bufs[par][...] / jnp.sum(l_bufs[par][...], axis=1, keepdims=True)
            o = nrm[0:t] - lam_ref[0] * nrm[t:2 * t]
            ms = jnp.mean(o * o, axis=-1, keepdims=True)
            o_ref[:, head_cols(hh)] = (o * lax.rsqrt(ms + LN_EPS) * g_ref[...] * out_scale
                                       ).astype(o_ref.dtype)

    for stage in range(heads + 2):
        for kb in range(n_kb):
            if stage < heads:
                score_block(stage, kb)
            if 0 <= stage - 1 < heads:
                exp_block(stage - 1, kb)
            if 0 <= stage - 2 < heads:
                value_block(stage - 2, kb)


def _prompt_attention(q, k, v, rel_bias, lam, subln_g, *, batch, seq, t, out_scale):
    assert t >= MAX_DISTANCE
    nq = seq // t
    width = N_HEADS * V_DIM
    q3 = q.reshape(batch, seq, width)
    v3 = v.reshape(batch, seq, width)
    bias = _bias_tiles(rel_bias, t)
    smem = pl.BlockSpec(memory_space=pltpu.SMEM)
    pieces = []
    for i in range(nq):
        n_far = max(i - 1, 0)
        near = min(i + 1, 2) * t
        keys = n_far * t + near
        hp = ATTN_HEADS_PER_STEP
        body = functools.partial(_attn_body, t=t, n_far=n_far, near=near, heads=hp, out_scale=out_scale)
        pair = lambda shape, dtype: [pltpu.VMEM(shape, dtype)] * 2
        pieces.append(pl.pallas_call(
            body, grid=(N_HEADS // hp, batch),
            in_specs=[smem, smem,
                      pl.BlockSpec((None, t, hp * V_DIM), lambda h, b, i=i: (b, i, h)),
                      pl.BlockSpec((None, hp * V_DIM, keys), lambda h, b: (b, h, 0)),
                      pl.BlockSpec((None, keys, hp * V_DIM), lambda h, b: (b, 0, h)),
                      pl.BlockSpec((hp, t, 2 * t), lambda h, b: (h, 0, 0)),
                      pl.BlockSpec((1, V_DIM), lambda h, b: (0, 0))],
            out_specs=pl.BlockSpec((None, t, hp * V_DIM), lambda h, b: (b, 0, h)),
            out_shape=jax.ShapeDtypeStruct((batch, t, width), BF16),
            scratch_shapes=(pair((2 * t, keys), F32) + pair((2 * t, keys), BF16)
                            + pair((2 * t, LANES), F32) + pair((2 * t, LANES), F32)
                            + pair((2 * t, LANES), F32) + pair((2 * t, V_DIM), F32)),
            compiler_params=_params(("arbitrary", "arbitrary")),
            name=f"prompt_attention_q{i}")(lam, rel_bias, q3, k, v3, bias, subln_g.reshape(1, V_DIM)))
    return jnp.stack(pieces, axis=1).reshape(batch * seq, width)


def _decode_body(pt_ref, lam_ref, q_ref, kn_ref, vn_ref, rbt_ref, bucket_ref, g_ref, *rest,
                 pages, page, n_steps, out_scale):
    k_refs = rest[:pages]
    v_refs = rest[pages:2 * pages]
    o_ref = rest[2 * pages]
    qexp_sc, bias_sc, m_sc, l_sc, acc_sc = rest[2 * pages + 1:]
    step = pl.program_id(1)
    rows = 2 * N_HEADS
    width = N_HEADS * V_DIM

    @pl.when(step == 0)
    def _():
        row = lax.broadcasted_iota(jnp.int32, (rows, width), 0)
        col = lax.broadcasted_iota(jnp.int32, (rows, width), 1)
        own_qk = (col // HEAD_DIM) == (row % N_HEADS) * 2 + row // N_HEADS
        q = (q_ref[...] * (HEAD_DIM ** -0.5)).astype(BF16).astype(F32)
        qexp = jnp.where(own_qk, jnp.broadcast_to(q, (rows, width)), 0.0)
        qexp_sc[...] = qexp.astype(BF16)
        bias_sc[...] = _bucket_lookup(bucket_ref[...], lambda bk: rbt_ref[:, bk:bk + 1])
        kn = kn_ref[...].astype(BF16).astype(F32)
        s_self = jnp.sum(qexp * kn, axis=1, keepdims=True) + rbt_ref[:, 0:1]
        m_sc[...] = s_self
        l_sc[...] = jnp.ones(l_sc.shape, F32)
        vn = vn_ref[...].astype(BF16).astype(F32)
        acc_sc[...] = jnp.broadcast_to(vn[:, None, :], acc_sc.shape)

    qexp = qexp_sc[...]
    bias_far = rbt_ref[:, NUM_BUCKETS - 1:NUM_BUCKETS]
    is_last = step == n_steps - 1
    s_parts = []
    for p in range(pages):
        s = jnp.dot(qexp, k_refs[p][...].astype(BF16), preferred_element_type=F32)
        if p == pages - 1:
            s = s + jnp.where(is_last, bias_sc[...], bias_far)
        else:
            s = s + bias_far
        s_parts.append(s)
    m_old = m_sc[...]
    m_new = m_old
    for s in s_parts:
        m_new = jnp.maximum(m_new, jnp.max(s, axis=1, keepdims=True))
    a = jnp.exp(m_old - m_new)
    l_new = a * l_sc[...]
    probs = []
    for p in range(pages):
        pr = jnp.exp(s_parts[p] - m_new)
        l_new = l_new + jnp.sum(pr, axis=1, keepdims=True)
        probs.append(pr.astype(BF16))
    probs = jnp.concatenate(probs, axis=1)
    m_sc[...] = m_new
    l_sc[...] = l_new
    for h in range(N_HEADS):
        vh = jnp.concatenate([v_refs[p][pl.ds(h, page, stride=N_HEADS), :].astype(BF16)
                              for p in range(pages)], axis=0)
        acc_sc[h] = a * acc_sc[h] + jnp.dot(probs, vh, preferred_element_type=F32)

    @pl.when(is_last)
    def _():
        nrm = acc_sc[...] / l_new[None]
        r = lax.broadcasted_iota(jnp.int32, nrm.shape, 1)
        hh = lax.broadcasted_iota(jnp.int32, nrm.shape, 0)
        coef = jnp.where(r == hh, 1.0, jnp.where(r == hh + N_HEADS, -lam_ref[0], 0.0))
        d = jnp.sum(coef * nrm, axis=1)
        ms = jnp.mean(d * d, axis=-1, keepdims=True)
        o_ref[...] = (d * lax.rsqrt(ms + LN_EPS) * g_ref[...] * out_scale).astype(o_ref.dtype)


def _sample_attention(q, k_new, v_new, cache_k, cache_v, page_table, rel_bias, lam, subln_g,
                      *, pages, out_scale):
    bs, width = q.shape
    page = cache_k.shape[2]
    n_pages = page_table.shape[1]
    n_steps = n_pages // pages
    past = n_pages * page
    kpos = past - page + jnp.arange(page, dtype=jnp.int32)
    bucket_last = _rel_bucket(past - kpos).reshape(1, page)
    rbt = jnp.tile(rel_bias.T, (2, 1))

    def tok_spec():
        return pl.BlockSpec((None, 1, width), lambda b, s, pt: (b, 0, 0))

    def page_spec(shape, p):
        return pl.BlockSpec((None,) + shape, lambda b, s, pt, p=p: (pt[b, s * pages + p], 0, 0))

    full = lambda shape: pl.BlockSpec(shape, lambda b, s, pt: tuple(0 for _ in shape))
    head_spec = pl.BlockSpec((None, N_HEADS, V_DIM), lambda b, s, pt: (b, 0, 0))
    in_specs = ([pl.BlockSpec(memory_space=pltpu.SMEM), tok_spec(), tok_spec(), head_spec,
                 full((2 * N_HEADS, NUM_BUCKETS)), full((1, page)), full((1, V_DIM))]
                + [page_spec((width, page), p) for p in range(pages)]
                + [page_spec((page * N_HEADS, V_DIM), p) for p in range(pages)])
    rows = 2 * N_HEADS
    body = functools.partial(_decode_body, pages=pages, page=page, n_steps=n_steps,
                             out_scale=out_scale)
    out = pl.pallas_call(
        body,
        grid_spec=pltpu.PrefetchScalarGridSpec(
            num_scalar_prefetch=1, grid=(bs, n_steps), in_specs=in_specs,
            out_specs=head_spec,
            scratch_shapes=[pltpu.VMEM((rows, width), BF16), pltpu.VMEM((rows, page), F32),
                            pltpu.VMEM((rows, 1), F32), pltpu.VMEM((rows, 1), F32),
                            pltpu.VMEM((N_HEADS, rows, V_DIM), F32)]),
        out_shape=jax.ShapeDtypeStruct((bs, N_HEADS, V_DIM), BF16),
        compiler_params=_params(("arbitrary", "arbitrary")),
        name="sample_attention")(
            page_table, lam, q.reshape(bs, 1, width), k_new.reshape(bs, 1, width),
            v_new.reshape(bs, N_HEADS, V_DIM), rbt, bucket_last, subln_g.reshape(1, V_DIM),
            *([cache_k] * pages), *([cache_v] * pages))
    return out.reshape(bs, width)


def _s5_params(a_re, a_im, log_dt, b_re, b_im, c_re, c_im):
    dt = jnp.exp(log_dt)[:, None]
    mag = jnp.exp(a_re * dt)
    ang = a_im * dt
    abar_re = mag * jnp.cos(ang)
    abar_im = mag * jnp.sin(ang)
    den = a_re * a_re + a_im * a_im
    f_re = ((abar_re - 1.0) * a_re + abar_im * a_im) / den
    f_im = (abar_im * a_re - (abar_re - 1.0) * a_im) / den
    bb_re = f_re[..., None] * b_re - f_im[..., None] * b_im
    bb_im = f_re[..., None] * b_im + f_im[..., None] * b_re
    eye = jnp.eye(SLAB_GROUPS, dtype=F32)

    def in_slabs(bb):
        tt = bb.reshape(N_SLABS, SLAB_GROUPS, STATE_DIM, SSM_GROUP).transpose(0, 1, 3, 2)
        full = tt[:, :, :, None, :] * eye[None, :, None, :, None]
        return full.reshape(N_SLABS, SLAB_IN, SLAB_STATE).astype(BF16)

    ab_re = abar_re[..., None] * bb_re - abar_im[..., None] * bb_im
    ab_im = abar_re[..., None] * bb_im + abar_im[..., None] * bb_re

    def out_slabs(cc):
        tt = cc.reshape(N_SLABS, SLAB_GROUPS, SSM_GROUP, STATE_DIM).transpose(0, 1, 3, 2)
        full = tt[:, :, :, None, :] * eye[None, :, None, :, None]
        return full.reshape(N_SLABS, SLAB_STATE, SLAB_IN).astype(BF16)

    return (abar_re.reshape(N_SLABS, SLAB_STATE), abar_im.reshape(N_SLABS, SLAB_STATE),
            in_slabs(bb_re), in_slabs(bb_im), out_slabs(c_re), out_slabs(c_im),
            in_slabs(ab_re), in_slabs(ab_im))


def _glu_out(y, u, d_ref, wglu_ref):
    g = jax.nn.gelu(y + d_ref[...] * u)
    gate = jnp.dot(g.astype(BF16), wglu_ref[...].astype(BF16), preferred_element_type=F32)
    return g * jax.nn.sigmoid(gate)


def _s5_prompt_body(u_ref, ar_ref, ai_ref, bbr_ref, bbi_ref, cr_ref, ci_ref, d_ref, wglu_ref,
                    o_ref, sre_ref, sim_ref, xr_sc, xi_sc, pw_sc, cr_sc, ci_sc, perm_sc, *, batch):
    c = pl.program_id(0)
    sub = SUBLANES
    _, tc, width = u_ref.shape
    n_rows = batch * tc
    t = lax.broadcasted_iota(jnp.int32, (sub, SLAB_STATE), 0)
    first = t < batch

    def cmul(pr, pi, qr, qi):
        return pr * qr - pi * qi, pr * qi + pi * qr

    @pl.when(c == 0)
    def _():
        for j in range(N_SLABS):
            a = (jnp.broadcast_to(ar_ref[j:j + 1, :], t.shape), jnp.broadcast_to(ai_ref[j:j + 1, :], t.shape))
            a2 = cmul(*a, *a)
            for part in range(2):
                pw_sc[j, part] = jnp.where(first, a[part], a2[part])
        cr_sc[...] = jnp.zeros(cr_sc.shape, F32)
        ci_sc[...] = jnp.zeros(ci_sc.shape, F32)
        r = lax.broadcasted_iota(jnp.int32, (n_rows, n_rows), 0)
        k = lax.broadcasted_iota(jnp.int32, (n_rows, n_rows), 1)
        perm_sc[0] = jnp.where(k == (r % batch) * tc + r // batch, 1.0, 0.0).astype(BF16)
        perm_sc[1] = jnp.where(r == (k % batch) * tc + k // batch, 1.0, 0.0).astype(BF16)

    def permute(which, x):
        return jnp.dot(perm_sc[which], x, preferred_element_type=F32)

    u_seq = u_ref[...].reshape(n_rows, width)
    hi = u_seq.astype(BF16)
    rest = u_seq - hi.astype(F32)
    mid = rest.astype(BF16)
    lo = (rest - mid.astype(F32)).astype(BF16)
    u_hi = permute(0, hi)
    ub = u_hi.astype(BF16)
    u = u_hi + permute(0, mid) + permute(0, lo)
    row = lax.broadcasted_iota(jnp.int32, u_hi.shape, 0)
    u_before = jnp.where(row % sub >= batch, pltpu.roll(u_hi, batch, 0), 0.0).astype(BF16)
    for j in range(N_SLABS):
        cols = slice(j * SLAB_IN, (j + 1) * SLAB_IN)
        uj = jnp.concatenate([ub[:, cols], u_before[:, cols]], axis=1)
        xr_sc[j] = jnp.dot(uj, bbr_ref[j], preferred_element_type=F32)
        xi_sc[j] = jnp.dot(uj, bbi_ref[j], preferred_element_type=F32)

    for j in range(N_SLABS):
        pr, pi = cr_sc[j], ci_sc[j]
        for v in range(n_rows // sub):
            rows = slice(v * sub, (v + 1) * sub)
            lr = jnp.where(first, pltpu.roll(pr, batch, 0), pr)
            li = jnp.where(first, pltpu.roll(pi, batch, 0), pi)
            er, ei = cmul(pw_sc[j, 0], pw_sc[j, 1], lr, li)
            pr, pi = xr_sc[j, rows, :] + er, xi_sc[j, rows, :] + ei
            xr_sc[j, rows, :] = pr
            xi_sc[j, rows, :] = pi
        cr_sc[j] = pr
        ci_sc[j] = pi
    sre_ref[...] = cr_sc[...]
    sim_ref[...] = ci_sc[...]

    ys = []
    for j in range(N_SLABS):
        ys.append(jnp.dot(xr_sc[j].astype(BF16), cr_ref[j], preferred_element_type=F32)
                  - jnp.dot(xi_sc[j].astype(BF16), ci_ref[j], preferred_element_type=F32))
    out = _glu_out(jnp.concatenate(ys, axis=1), u, d_ref, wglu_ref).astype(BF16)
    o_ref[...] = permute(1, out).astype(o_ref.dtype).reshape(batch, tc, width)


def _s5_prompt(u, sp, d, w_glu, *, batch, seq, tc):
    assert SUBLANES == 2 * batch, "tile = two positions of every sequence"
    ar, ai, bbr, bbi, cr, ci, abr, abi = sp
    bbr = jnp.concatenate([bbr, abr], axis=1)
    bbi = jnp.concatenate([bbi, abi], axis=1)
    width = u.shape[1]
    rows = batch * tc
    const = lambda shape: pl.BlockSpec(shape, lambda c: tuple(0 for _ in shape))
    block = pl.BlockSpec((batch, tc, width), lambda c: (0, c, 0))
    state_spec = const((N_SLABS, SUBLANES, SLAB_STATE))
    state_shape = jax.ShapeDtypeStruct((N_SLABS, SUBLANES, SLAB_STATE), F32)
    out, s_re, s_im = pl.pallas_call(
        functools.partial(_s5_prompt_body, batch=batch), grid=(seq // tc,),
        in_specs=[block, const(ar.shape), const(ai.shape), const(bbr.shape), const(bbi.shape),
                  const(cr.shape), const(ci.shape), const((1, width)), const(w_glu.shape)],
        out_specs=[block, state_spec, state_spec],
        out_shape=[jax.ShapeDtypeStruct((batch, seq, width), BF16), state_shape, state_shape],
        scratch_shapes=[pltpu.VMEM((N_SLABS, rows, SLAB_STATE), F32),
                        pltpu.VMEM((N_SLABS, rows, SLAB_STATE), F32),
                        pltpu.VMEM((N_SLABS, 2, SUBLANES, SLAB_STATE), F32),
                        pltpu.VMEM((N_SLABS, SUBLANES, SLAB_STATE), F32),
                        pltpu.VMEM((N_SLABS, SUBLANES, SLAB_STATE), F32),
                        pltpu.VMEM((2, rows, rows), BF16)],
        compiler_params=_params(("arbitrary",)),
        name="s5_prompt")(u.reshape(batch, seq, width), ar, ai, bbr, bbi, cr, ci,
                          d.reshape(1, width), w_glu)
    return out.reshape(batch * seq, width), s_re, s_im


def _s5_sample_body(u_ref, x0r_ref, x0i_ref, ar_ref, ai_ref, bbr_ref, bbi_ref, cr_ref, ci_ref,
                    d_ref, wglu_ref, o_ref, sre_ref, sim_ref):
    u = u_ref[...]
    ub = u.astype(BF16)
    ys = []
    for j in range(N_SLABS):
        uj = ub[:, j * SLAB_IN:(j + 1) * SLAB_IN]
        ar = ar_ref[j:j + 1, :]
        ai = ai_ref[j:j + 1, :]
        x0r = x0r_ref[j]
        x0i = x0i_ref[j]
        xr = ar * x0r - ai * x0i + jnp.dot(uj, bbr_ref[j], preferred_element_type=F32)
        xi = ar * x0i + ai * x0r + jnp.dot(uj, bbi_ref[j], preferred_element_type=F32)
        sre_ref[j] = xr
        sim_ref[j] = xi
        ys.append(jnp.dot(xr.astype(BF16), cr_ref[j], preferred_element_type=F32)
                  - jnp.dot(xi.astype(BF16), ci_ref[j], preferred_element_type=F32))
    o_ref[...] = _glu_out(jnp.concatenate(ys, axis=1), u, d_ref, wglu_ref).astype(o_ref.dtype)


def _s5_sample(u, x0_re, x0_im, sp, d, w_glu):
    ar, ai, bbr, bbi, cr, ci = sp[:6]
    bs, width = u.shape
    state_shape = jax.ShapeDtypeStruct((N_SLABS, bs, SLAB_STATE), F32)
    return pl.pallas_call(
        _s5_sample_body,
        out_shape=[jax.ShapeDtypeStruct((bs, width), BF16), state_shape, state_shape],
        compiler_params=pltpu.CompilerParams(vmem_limit_bytes=VMEM_LIMIT_BYTES),
        name="s5_sample")(u, x0_re, x0_im, ar, ai, bbr, bbi, cr, ci, d.reshape(1, width), w_glu)


def _gated(c):
    gate = c[:, :LANES]
    return gate * jax.nn.sigmoid(gate) * c[:, LANES:]


def _up_prompt_body(h_ref, wg_ref, wv_ref, cwg_ref, cwv_ref, cbg_ref, cbv_ref,
                    act_ref, cg_ref, cv_ref, w_sc, *, rows):
    w_sc[:, 0:LANES] = wg_ref[...].astype(BF16)
    w_sc[:, LANES:2 * LANES] = wv_ref[...].astype(BF16)
    cw = jnp.concatenate([cwg_ref[...], cwv_ref[...]], axis=1)
    cb = jnp.concatenate([cbg_ref[...], cbv_ref[...]], axis=1)
    seq = h_ref.shape[0]
    sub = SUBLANES
    row8 = lax.broadcasted_iota(jnp.int32, (sub, 2 * LANES), 0)

    prev = jnp.zeros((sub, 2 * LANES), F32)
    for c in range(seq // rows):
        up = jnp.dot(h_ref[c * rows:(c + 1) * rows, :], w_sc[...], preferred_element_type=F32)
        cur = cb + cw[CONV_W - 1:CONV_W] * up
        for back in range(1, CONV_W):
            rolled = pltpu.roll(up, back, 0)
            head = jnp.where(row8 < back, pltpu.roll(prev, back, 0), rolled[0:sub])
            shifted = jnp.concatenate([head, rolled[sub:]], axis=0)
            cur = cur + cw[CONV_W - 1 - back:CONV_W - back] * shifted
        act_ref[c * rows:(c + 1) * rows, :] = _gated(cur).astype(act_ref.dtype)
        prev = up[rows - sub:, :]
    tail = prev[sub - (CONV_W - 1):, :]
    cg_ref[...] = tail[:, :LANES]
    cv_ref[...] = tail[:, LANES:]


def _up_prompt(hb, w_up, conv_w, conv_b, *, batch, seq):
    d_model = hb.shape[1]
    d_ff = w_up.shape[1] // 2
    nb = d_ff // LANES
    cb = conv_b.reshape(1, 2 * d_ff)
    col = lambda off: (lambda b, j: (0, j + off))
    tail_spec = pl.BlockSpec((None, CONV_W - 1, LANES), lambda b, j: (b, 0, j))
    tail_shape = jax.ShapeDtypeStruct((batch, CONV_W - 1, d_ff), F32)
    return pl.pallas_call(
        functools.partial(_up_prompt_body, rows=UP_ROW_CHUNK), grid=(batch, nb),
        scratch_shapes=[pltpu.VMEM((d_model, 2 * LANES), BF16)],
        in_specs=[pl.BlockSpec((seq, d_model), lambda b, j: (b, 0)),
                  pl.BlockSpec((d_model, LANES), col(0)), pl.BlockSpec((d_model, LANES), col(nb)),
                  pl.BlockSpec((CONV_W, LANES), col(0)), pl.BlockSpec((CONV_W, LANES), col(nb)),
                  pl.BlockSpec((1, LANES), col(0)), pl.BlockSpec((1, LANES), col(nb))],
        out_specs=[pl.BlockSpec((seq, LANES), lambda b, j: (b, j)), tail_spec, tail_spec],
        out_shape=[jax.ShapeDtypeStruct((batch * seq, d_ff), BF16), tail_shape, tail_shape],
        compiler_params=_params(("arbitrary", "arbitrary")),
        name="up_prompt")(hb, w_up, w_up, conv_w, conv_w, cb, cb)


def _up_sample_body(h_ref, wg_ref, wv_ref, cwg_ref, cwv_ref, cbg_ref, cbv_ref, sg_ref, sv_ref,
                    act_ref, cg_ref, cv_ref):
    w = jnp.concatenate([wg_ref[...].astype(BF16), wv_ref[...].astype(BF16)], axis=1)
    up = jnp.dot(h_ref[...], w, preferred_element_type=F32)
    cw = jnp.concatenate([cwg_ref[...], cwv_ref[...]], axis=1)
    cb = jnp.concatenate([cbg_ref[...], cbv_ref[...]], axis=1)
    c = cb + cw[CONV_W - 1:CONV_W] * up
    for tap in range(CONV_W - 1):
        st = jnp.concatenate([sg_ref[tap], sv_ref[tap]], axis=1)
        c = c + cw[tap:tap + 1] * st
    act_ref[...] = _gated(c).astype(act_ref.dtype)
    for tap in range(1, CONV_W - 1):
        cg_ref[tap - 1] = sg_ref[tap]
        cv_ref[tap - 1] = sv_ref[tap]
    cg_ref[CONV_W - 2] = up[:, :LANES]
    cv_ref[CONV_W - 2] = up[:, LANES:]


def _up_sample(hb, w_up, conv_w, conv_b, state):
    bs, d_model = hb.shape
    d_ff = w_up.shape[1] // 2
    nb = d_ff // LANES
    cb = conv_b.reshape(1, 2 * d_ff)
    col = lambda off: (lambda j: (0, j + off))
    st = lambda off: pl.BlockSpec((CONV_W - 1, bs, LANES), lambda j, off=off: (0, 0, j + off))
    tail_spec = pl.BlockSpec((CONV_W - 1, bs, LANES), lambda j: (0, 0, j))
    tail_shape = jax.ShapeDtypeStruct((CONV_W - 1, bs, d_ff), F32)
    return pl.pallas_call(
        _up_sample_body, grid=(nb,),
        in_specs=[pl.BlockSpec((bs, d_model), lambda j: (0, 0)),
                  pl.BlockSpec((d_model, LANES), col(0)), pl.BlockSpec((d_model, LANES), col(nb)),
                  pl.BlockSpec((CONV_W, LANES), col(0)), pl.BlockSpec((CONV_W, LANES), col(nb)),
                  pl.BlockSpec((1, LANES), col(0)), pl.BlockSpec((1, LANES), col(nb)),
                  st(0), st(nb)],
        out_specs=[pl.BlockSpec((bs, LANES), lambda j: (0, j)), tail_spec, tail_spec],
        out_shape=[jax.ShapeDtypeStruct((bs, d_ff), BF16), tail_shape, tail_shape],
        compiler_params=_params(("arbitrary",)),
        name="up_sample")(hb, w_up, w_up, conv_w, conv_w, cb, cb, state, state)


def _one(x):
    return (x,)


def _sigmoid_out(acc):
    return (jax.nn.sigmoid(acc),)


def _both(acc):
    return (acc, acc)


def _merge(pa, ps, ga, gs):
    return (ga * pa + gs * ps,)


def _layer(xp, xs, w, *, batch, seq, alpha, attend_p, attend_s, ssm_p, ssm_s, up_p, up_s):
    d_model = xp.shape[1]
    qk_w = N_HEADS * 2 * HEAD_DIM
    v_w = N_HEADS * V_DIM
    ssm_w = d_model // 2
    w_in = w["w_in"]
    g1, b1 = w["ln1_g"].reshape(1, d_model), w["ln1_b"].reshape(1, d_model)
    big = dict(tm=MM_ROWS, tn=MM_COLS)
    c = 0
    (q,), (q_s,), xb = _matmul([(xp, xs, w_in, c)], [], [BF16], _one, n_cols=qk_w, name="proj_q",
                               emit_lhs=True, **big)
    c += qk_w
    kt, ktb, k_s = _proj_transposed(xb, xs, w_in, c, n_cols=qk_w, batch=batch, seq=seq, tm=MM_ROWS,
                                    name="proj_kt")
    c += qk_w
    c_v = c
    c += v_w
    (u,), (u_s,) = _matmul([(xb, xs, w_in, c)], [], [F32], _one, n_cols=ssm_w, name="proj_u", **big)
    c += ssm_w
    (ga,), (ga_s,) = _matmul([(xb, xs, w_in, c)], [], [F32], _sigmoid_out, n_cols=d_model, name="gate_a", **big)
    c += d_model
    (gs,), (gs_s,) = _matmul([(xb, xs, w_in, c)], [], [F32], _sigmoid_out, n_cols=d_model, name="gate_s", **big)
    (v, vb), (v_s, _) = _matmul([(xb, xs, w_in, c_v)], [], [F32, BF16], _both, n_cols=v_w, name="proj_v", **big)

    attn = attend_p(q, ktb, vb)
    ssm_out, re_p, im_p = ssm_p(u)
    ssm_out_s, re_s, im_s = ssm_s(u_s)
    attn_s = attend_s(q_s, k_s, v_s)

    (merged,), (merged_s,) = _matmul(
        [(attn, attn_s, w["w_proj_attn"], 0), (ssm_out, ssm_out_s, w["w_proj_ssm"], 0)],
        [(ga, ga_s), (gs, gs_s)], [BF16], _merge, n_cols=d_model, name="merge", **big)

    def post_ln1(acc, res, g, b):
        h = _ln(alpha * res + acc, g, b)
        return h, h

    (h, hb), (h_s, hb_s) = _matmul([(merged, merged_s, w["w_out"], 0)], [(xp, xs), g1, b1], [F32, BF16],
                                   post_ln1, n_cols=d_model, tm=LN_MM_ROWS, tn=d_model, name="out_proj_ln1")
    act, conv_p = up_p(hb)
    act_s, conv_s = up_s(hb_s)
    (r2,), (r2_s,) = _matmul([(act, act_s, w["w_down"], 0)], [(h, h_s)], [F32],
                             lambda acc, res: (alpha * res + acc,), n_cols=d_model,
                             tm=DOWN_ROWS, tn=DOWN_COLS, name="down_proj")
    y = _layer_norm(r2, w["ln2_g"], w["ln2_b"], tm=LN_ROWS, name="ln2")
    y_s = _layer_norm(r2_s, w["ln2_g"], w["ln2_b"], tm=LN_ROWS, name="ln2_sample")
    return (y, kt, v, re_p, im_p, conv_p), (y_s, k_s, v_s, re_s, im_s, conv_s)


def kernel(x_prompt, x_sample, cache_k, cache_v, state_ssm_re, state_ssm_im, state_conv, page_table, rel_bias, w_in, lambda_q1, lambda_k1, lambda_q2, lambda_k2, subln_g, ssm_a_re, ssm_a_im, ssm_log_dt, ssm_b_re, ssm_b_im, ssm_c_re, ssm_c_im, ssm_d, w_glu, w_proj_attn, w_proj_ssm, w_out, ln1_g, ln1_b, w_up, conv_w, conv_b, w_down, ln2_g, ln2_b):
    depth = w_in.shape[0]
    assert depth == 1, "single-layer trunk"
    bp, seq, d_model = x_prompt.shape
    bs, dec_seq, _ = x_sample.shape
    assert dec_seq == 1
    n_pool, page = cache_k.shape[1], cache_k.shape[2]
    d_ff = w_down.shape[1]
    n_groups = ssm_a_re.shape[1]
    assert n_groups == N_SLABS * SLAB_GROUPS and d_ff % LANES == 0
    alpha = (2.0 * depth) ** 0.25
    width = N_HEADS * V_DIM

    hp = x_prompt.reshape(bp * seq, d_model)
    hs = x_sample.reshape(bs, d_model)
    outs = {}
    for l in range(depth):
        lam_init = 0.8 - 0.6 * math.exp(-0.3 * l)
        out_scale = 1.0 - lam_init
        lam = (jnp.exp(jnp.sum(lambda_q1[l] * lambda_k1[l]))
               - jnp.exp(jnp.sum(lambda_q2[l] * lambda_k2[l])) + lam_init).reshape(1)
        w = dict(w_in=w_in[l], w_proj_attn=w_proj_attn[l], w_proj_ssm=w_proj_ssm[l], w_out=w_out[l],
                 ln1_g=ln1_g[l], ln1_b=ln1_b[l], w_down=w_down[l], ln2_g=ln2_g[l], ln2_b=ln2_b[l])
        sp = _s5_params(ssm_a_re[l], ssm_a_im[l], ssm_log_dt[l], ssm_b_re[l], ssm_b_im[l],
                        ssm_c_re[l], ssm_c_im[l])

        def attend_p(q, kt, v):
            return _prompt_attention(q, kt, v, rel_bias, lam, subln_g[l], batch=bp, seq=seq,
                                     t=ATTN_BLOCK, out_scale=out_scale)

        def ssm_p(u):
            o, sr, si = _s5_prompt(u, sp, ssm_d[l], w_glu[l], batch=bp, seq=seq, tc=S5_POSITIONS)
            last = lambda st: st[:, SUBLANES - bp:, :].transpose(1, 0, 2)
            return o, last(sr), last(si)

        def up_p(hb):
            act, cg, cv = _up_prompt(hb, w_up[l], conv_w[l], conv_b[l], batch=bp, seq=seq)
            return act, jnp.concatenate([cg, cv], axis=-1)

        ck = cache_k[l].transpose(0, 2, 3, 4, 1).reshape(n_pool, width, page)
        cv_ = cache_v[l].reshape(n_pool, page * N_HEADS, V_DIM)

        def attend_s(q, k, v):
            return _sample_attention(q.astype(F32), k, v, ck, cv_, page_table, rel_bias, lam,
                                     subln_g[l], pages=DECODE_PAGES, out_scale=out_scale)

        def ssm_s(u):
            x0r = state_ssm_re[l].reshape(bs, N_SLABS, SLAB_STATE).transpose(1, 0, 2)
            x0i = state_ssm_im[l].reshape(bs, N_SLABS, SLAB_STATE).transpose(1, 0, 2)
            o, sr, si = _s5_sample(u, x0r, x0i, sp, ssm_d[l], w_glu[l])
            return o, sr.transpose(1, 0, 2), si.transpose(1, 0, 2)

        def up_s(hb):
            act, cg, cv = _up_sample(hb, w_up[l], conv_w[l], conv_b[l],
                                     state_conv[l].transpose(1, 0, 2))
            return act, jnp.concatenate([cg, cv], axis=-1).transpose(1, 0, 2)

        (hp, kt_p, v_p, re_p, im_p, c_p), (hs, k_s, v_s, re_s, im_s, c_s) = _layer(
            hp, hs, w, batch=bp, seq=seq, alpha=alpha, attend_p=attend_p, attend_s=attend_s,
            ssm_p=ssm_p, ssm_s=ssm_s, up_p=up_p, up_s=up_s)
        k_p = kt_p.reshape(bp, N_HEADS, 2, HEAD_DIM, seq).transpose(0, 4, 1, 2, 3)

        for name, val in (("kp", k_p.reshape(bp, seq, N_HEADS, 2, HEAD_DIM)),
                          ("vp", v_p.reshape(bp, seq, N_HEADS, V_DIM)),
                          ("rep", re_p.reshape(bp, n_groups, STATE_DIM)),
                          ("imp", im_p.reshape(bp, n_groups, STATE_DIM)),
                          ("cp", c_p),
                          ("ks", k_s.reshape(bs, 1, N_HEADS, 2, HEAD_DIM)),
                          ("vs", v_s.reshape(bs, 1, N_HEADS, V_DIM)),
                          ("res", re_s.reshape(bs, n_groups, STATE_DIM)),
                          ("ims", im_s.reshape(bs, n_groups, STATE_DIM)),
                          ("cs", c_s)):
            outs.setdefault(name, []).append(val)

    st = {k: jnp.stack(v, axis=0) for k, v in outs.items()}
    return (hp.reshape(bp, seq, d_model), hs.reshape(bs, 1, d_model), st["kp"], st["vp"], st["rep"],
            st["imp"], st["cp"], st["ks"], st["vs"], st["res"], st["ims"], st["cs"])
```

```python
import functools
import math

import jax
import jax.numpy as jnp
from jax import lax
from jax.experimental import pallas as pl
from jax.experimental.pallas import tpu as pltpu

F32 = jnp.float32
BF16 = jnp.bfloat16

N_HEADS = 8
HEAD_DIM = 64
V_DIM = 2 * HEAD_DIM
SSM_GROUP = 16
STATE_DIM = 64
CONV_W = 3
NUM_BUCKETS = 32
MAX_EXACT = NUM_BUCKETS // 2
MAX_DISTANCE = 128
LN_EPS = 1e-5
NEG_INF = -1e30

VMEM_LIMIT_BYTES = 56 * 1024 * 1024
LANES = 128
SLAB_GROUPS = 8
N_SLABS = 8
SLAB_IN = SLAB_GROUPS * SSM_GROUP
SLAB_STATE = SLAB_GROUPS * STATE_DIM
SUBLANES = 8
MM_ROWS, MM_COLS = 1024, 1024
LN_MM_ROWS = 512
DOWN_ROWS, DOWN_COLS = 512, 512
LN_ROWS = 512
ATTN_BLOCK = 256
ATTN_KEY_BLOCK = 256
ATTN_HEADS_PER_STEP = 4
S5_POSITIONS = 64
DECODE_PAGES = 16
UP_ROW_CHUNK = 512


def _params(sem):
    return pltpu.CompilerParams(dimension_semantics=sem, vmem_limit_bytes=VMEM_LIMIT_BYTES)


def _mm_body(*refs, n_pairs, n_extra, n_out, epilogue, emit_lhs):
    it = iter(refs)
    take = lambda n: [next(it) for _ in range(n)]
    x_refs, xs_refs, w_refs = take(n_pairs), take(n_pairs), take(n_pairs)
    extra, extra_s = take(n_extra), take(n_extra)
    outs, outs_s = take(n_out), take(n_out)
    lhs_out = take(1) if emit_lhs else []
    wbf = take(n_pairs)

    def apply(lhs_refs, extra_refs, out_refs, keep):
        lhs = [x[...].astype(BF16) for x in lhs_refs]
        for ref in keep:
            ref[...] = lhs[0]
        accs = [jnp.dot(x, s[...], preferred_element_type=F32) for x, s in zip(lhs, wbf)]
        for o, r in zip(out_refs, epilogue(*accs, *[e[...] for e in extra_refs])):
            o[...] = r.astype(o.dtype)

    @pl.when(pl.program_id(1) == 0)
    def _():
        for w, s in zip(w_refs, wbf):
            s[...] = w[...].astype(BF16)
        apply(xs_refs, extra_s, outs_s, [])

    apply(x_refs, extra, outs, lhs_out)


def _matmul(pairs, extras, out_dtypes, epilogue, *, n_cols, tm, tn, name, emit_lhs=False):
    m = pairs[0][0].shape[0]
    ms = pairs[0][1].shape[0]
    grid = (n_cols // tn, m // tm)
    assert not emit_lhs or grid[0] == 1
    w_mode = dict(pipeline_mode=pl.Buffered(1)) if grid[0] == 1 else {}
    specs_x, specs_xs, specs_w, scratch = [], [], [], []
    for x, xs, w, c0 in pairs:
        k = x.shape[1]
        specs_x.append(pl.BlockSpec((tm, k), lambda j, i: (i, 0)))
        specs_xs.append(pl.BlockSpec((ms, k), lambda j, i: (0, 0)))
        specs_w.append(pl.BlockSpec((k, tn), lambda j, i, off=c0 // tn: (0, j + off), **w_mode))
        scratch.append(pltpu.VMEM((k, tn), BF16))
    specs_e, specs_es, args_e, args_es = [], [], [], []
    for e in extras:
        if isinstance(e, tuple):
            specs_e.append(pl.BlockSpec((tm, tn), lambda j, i: (i, j)))
            specs_es.append(pl.BlockSpec((ms, tn), lambda j, i: (0, j)))
            args_e.append(e[0])
            args_es.append(e[1])
        else:
            specs_e.append(pl.BlockSpec((1, tn), lambda j, i: (0, j)))
            specs_es.append(pl.BlockSpec((1, tn), lambda j, i: (0, j)))
            args_e.append(e)
            args_es.append(e)
    out_shape = ([jax.ShapeDtypeStruct((m, n_cols), d) for d in out_dtypes]
                 + [jax.ShapeDtypeStruct((ms, n_cols), d) for d in out_dtypes])
    out_specs = ([pl.BlockSpec((tm, tn), lambda j, i: (i, j)) for _ in out_dtypes]
                 + [pl.BlockSpec((ms, tn), lambda j, i: (0, j)) for _ in out_dtypes])
    if emit_lhs:
        k0 = pairs[0][0].shape[1]
        out_shape.append(jax.ShapeDtypeStruct((m, k0), BF16))
        out_specs.append(pl.BlockSpec((tm, k0), lambda j, i: (i, 0)))
    body = functools.partial(_mm_body, n_pairs=len(pairs), n_extra=len(extras),
                             n_out=len(out_dtypes), epilogue=epilogue, emit_lhs=emit_lhs)
    outs = pl.pallas_call(
        body, grid=grid, in_specs=specs_x + specs_xs + specs_w + specs_e + specs_es,
        out_specs=out_specs, out_shape=out_shape, scratch_shapes=scratch,
        compiler_params=_params(("arbitrary", "arbitrary")), name=name)(
            *[p[0] for p in pairs], *[p[1] for p in pairs], *[p[2] for p in pairs],
            *args_e, *args_es)
    n = len(out_dtypes)
    if emit_lhs:
        return outs[:n], outs[n:2 * n], outs[2 * n]
    return outs[:n], outs[n:]


def _proj_t_body(x_ref, xs_ref, w_ref, o_ref, ob_ref, os_ref, wt_sc):
    @pl.when(pl.program_id(0) == 0)
    def _():
        for c in range(w_ref.shape[1] // LANES):
            cols = slice(c * LANES, (c + 1) * LANES)
            wt_sc[cols, :] = w_ref[:, cols].T.astype(BF16)
        os_ref[...] = lax.dot_general(xs_ref[...].astype(BF16), wt_sc[...], (((1,), (1,)), ((), ())),
                                      preferred_element_type=F32)

    kt = lax.dot_general(wt_sc[...], x_ref[...], (((1,), (1,)), ((), ())),
                         preferred_element_type=F32)
    o_ref[...] = kt
    ob_ref[...] = kt.astype(BF16)


def _proj_transposed(x, xs, w, c0, *, n_cols, batch, seq, tm, name):
    k = x.shape[1]
    ms = xs.shape[0]
    per_b = seq // tm
    out_spec = pl.BlockSpec((None, n_cols, tm), lambda i: (i // per_b, 0, i % per_b))
    return pl.pallas_call(
        _proj_t_body, grid=(batch * per_b,),
        in_specs=[pl.BlockSpec((tm, k), lambda i: (i, 0)),
                  pl.BlockSpec((ms, k), lambda i: (0, 0)),
                  pl.BlockSpec((k, n_cols), lambda i, off=c0 // n_cols: (0, off),
                               pipeline_mode=pl.Buffered(1))],
        out_specs=[out_spec, out_spec, pl.BlockSpec((ms, n_cols), lambda i: (0, 0))],
        out_shape=[jax.ShapeDtypeStruct((batch, n_cols, seq), F32),
                   jax.ShapeDtypeStruct((batch, n_cols, seq), BF16),
                   jax.ShapeDtypeStruct((ms, n_cols), F32)],
        scratch_shapes=[pltpu.VMEM((n_cols, k), BF16)],
        compiler_params=_params(("arbitrary",)), name=name)(x, xs, w)


def _ln(x, g, b):
    mu = jnp.mean(x, axis=-1, keepdims=True)
    xc = x - mu
    var = jnp.mean(xc * xc, axis=-1, keepdims=True)
    return xc * lax.rsqrt(var + LN_EPS) * g + b


def _ln_body(x_ref, g_ref, b_ref, o_ref):
    o_ref[...] = _ln(x_ref[...], g_ref[...], b_ref[...])


def _layer_norm(x, g, b, *, tm, name):
    m, d = x.shape
    tm = min(tm, m)
    return pl.pallas_call(
        _ln_body, grid=(m // tm,),
        in_specs=[pl.BlockSpec((tm, d), lambda i: (i, 0)),
                  pl.BlockSpec((1, d), lambda i: (0, 0)),
                  pl.BlockSpec((1, d), lambda i: (0, 0))],
        out_specs=pl.BlockSpec((tm, d), lambda i: (i, 0)),
        out_shape=jax.ShapeDtypeStruct((m, d), F32),
        compiler_params=_params(("arbitrary",)), name=name)(x, g.reshape(1, d), b.reshape(1, d))


def _rel_bucket(n):
    n = jnp.maximum(n, 0)
    nf = jnp.maximum(n, 1).astype(F32)
    large = MAX_EXACT + jnp.floor(jnp.log(nf / MAX_EXACT) / math.log(MAX_DISTANCE / MAX_EXACT)
                                  * (NUM_BUCKETS - MAX_EXACT)).astype(jnp.int32)
    large = jnp.minimum(large, NUM_BUCKETS - 1)
    return jnp.where(n < MAX_EXACT, n, large)


def _bucket_lookup(bucket, table_fn):
    out = jnp.zeros(jnp.broadcast_shapes(bucket.shape, table_fn(0).shape), F32)
    for b in range(NUM_BUCKETS):
        out = out + jnp.where(bucket == b, table_fn(b), 0.0)
    return out


def _bias_tile_body(rb_ref, bucket_ref, o_ref):
    h = pl.program_id(0)
    o_ref[...] = _bucket_lookup(bucket_ref[...], lambda bk: rb_ref[bk, h])


def _bias_tiles(rel_bias, t):
    r = jnp.arange(t, dtype=jnp.int32)
    c = jnp.arange(2 * t, dtype=jnp.int32)
    buckets = _rel_bucket(r[:, None] + t - c[None, :])
    return pl.pallas_call(
        _bias_tile_body, grid=(N_HEADS,),
        in_specs=[pl.BlockSpec(memory_space=pltpu.SMEM), pl.BlockSpec((t, 2 * t), lambda h: (0, 0))],
        out_specs=pl.BlockSpec((None, t, 2 * t), lambda h: (h, 0, 0)),
        out_shape=jax.ShapeDtypeStruct((N_HEADS, t, 2 * t), F32),
        compiler_params=_params(("arbitrary",)), name="bias_tiles")(rel_bias, buckets)


def _attn_body(lam_ref, rb_ref, q_ref, k_ref, v_ref, bias_ref, g_ref, o_ref, *scratch,
               t, n_far, near, heads, out_scale):
    s_bufs, p_bufs, m_bufs, mf_bufs, l_bufs, a_bufs = (scratch[0:2], scratch[2:4], scratch[4:6],
                                                       scratch[6:8], scratch[8:10], scratch[10:12])
    hg = pl.program_id(0)
    far = n_far * t
    kb_w = ATTN_KEY_BLOCK
    n_kb = (far + near) // kb_w
    n_tiles = kb_w // LANES

    def fold(x, op):
        out = x[:, 0:LANES]
        for c in range(1, n_tiles):
            out = op(out, x[:, c * LANES:(c + 1) * LANES])
        return out

    def head_cols(hh):
        return slice(hh * V_DIM, (hh + 1) * V_DIM)

    def score_block(hh, kb):
        par = hh % 2
        q = q_ref[:, head_cols(hh)] * (HEAD_DIM ** -0.5)
        lane = lax.broadcasted_iota(jnp.int32, q.shape, 1)
        zero = jnp.zeros_like(q)
        q2 = jnp.concatenate([jnp.where(lane < HEAD_DIM, q, zero), jnp.where(lane >= HEAD_DIM, q, zero)],
                             axis=0)
        cols = slice(kb * kb_w, (kb + 1) * kb_w)
        s = jnp.dot(q2, k_ref[head_cols(hh), cols], preferred_element_type=F32)
        bias_far = rb_ref[NUM_BUCKETS - 1, hg * heads + hh]
        if kb * kb_w < far:
            top = fold(s, jnp.maximum) + bias_far
        else:
            off = kb * kb_w - far
            lo = 2 * t - near + off
            bias = bias_ref[hh, :, lo:lo + kb_w]
            row = lax.broadcasted_iota(jnp.int32, (t, kb_w), 0)
            col = lax.broadcasted_iota(jnp.int32, (t, kb_w), 1)
            keep = col + (off - (near - t)) <= row
            s = jnp.where(jnp.concatenate([keep, keep], axis=0),
                          s + jnp.concatenate([bias, bias], axis=0), NEG_INF)
            top = fold(s, jnp.maximum)
        s_bufs[par][:, cols] = s
        if kb == 0:
            m_bufs[par][...] = top
        else:
            m_bufs[par][...] = jnp.maximum(m_bufs[par][...], top)
        if kb == n_kb - 1:
            m = jnp.broadcast_to(jnp.max(m_bufs[par][...], axis=1, keepdims=True), m_bufs[par].shape)
            m_bufs[par][...] = m
            mf_bufs[par][...] = m - bias_far

    def exp_block(hh, kb):
        par = hh % 2
        cols = slice(kb * kb_w, (kb + 1) * kb_w)
        m = (mf_bufs if kb * kb_w < far else m_bufs)[par][...]
        p = jnp.exp(s_bufs[par][:, cols] - jnp.concatenate([m] * n_tiles, axis=1))
        if kb == 0:
            l_bufs[par][...] = fold(p, jnp.add)
        else:
            l_bufs[par][...] += fold(p, jnp.add)
        p_bufs[par][:, cols] = p.astype(BF16)

    def value_block(hh, kb):
        par = hh % 2
        rows = slice(kb * kb_w, (kb + 1) * kb_w)
        pv = jnp.dot(p_bufs[par][:, rows], v_ref[rows, head_cols(hh)], preferred_element_type=F32)
        if kb == 0:
            a_bufs[par][...] = pv
        else:
            a_bufs[par][...] += pv
        if kb == n_kb - 1:
            nrm = a_bufs[par][...] / jnp.sum(l_bufs[par][...], axis=1, keepdims=True)
            o = nrm[0:t] - lam_ref[0] * nrm[t:2 * t]
            ms = jnp.mean(o * o, axis=-1, keepdims=True)
            o_ref[:, head_cols(hh)] = (o * lax.rsqrt(ms + LN_EPS) * g_ref[...] * out_scale
                                       ).astype(o_ref.dtype)

    for stage in range(heads + 2):
        for kb in range(n_kb):
            if stage < heads:
                score_block(stage, kb)
            if 0 <= stage - 1 < heads:
                exp_block(stage - 1, kb)
            if 0 <= stage - 2 < heads:
                value_block(stage - 2, kb)


def _prompt_attention(q, k, v, rel_bias, lam, subln_g, *, batch, seq, t, out_scale):
    assert t >= MAX_DISTANCE
    nq = seq // t
    width = N_HEADS * V_DIM
    q3 = q.reshape(batch, seq, width)
    v3 = v.reshape(batch, seq, width)
    bias = _bias_tiles(rel_bias, t)
    smem = pl.BlockSpec(memory_space=pltpu.SMEM)
    pieces = []
    for i in range(nq):
        n_far = max(i - 1, 0)
        near = min(i + 1, 2) * t
        keys = n_far * t + near
        hp = ATTN_HEADS_PER_STEP
        body = functools.partial(_attn_body, t=t, n_far=n_far, near=near, heads=hp, out_scale=out_scale)
        pair = lambda shape, dtype: [pltpu.VMEM(shape, dtype)] * 2
        pieces.append(pl.pallas_call(
            body, grid=(N_HEADS // hp, batch),
            in_specs=[smem, smem,
                      pl.BlockSpec((None, t, hp * V_DIM), lambda h, b, i=i: (b, i, h)),
                      pl.BlockSpec((None, hp * V_DIM, keys), lambda h, b: (b, h, 0)),
                      pl.BlockSpec((None, keys, hp * V_DIM), lambda h, b: (b, 0, h)),
                      pl.BlockSpec((hp, t, 2 * t), lambda h, b: (h, 0, 0)),
                      pl.BlockSpec((1, V_DIM), lambda h, b: (0, 0))],
            out_specs=pl.BlockSpec((None, t, hp * V_DIM), lambda h, b: (b, 0, h)),
            out_shape=jax.ShapeDtypeStruct((batch, t, width), BF16),
            scratch_shapes=(pair((2 * t, keys), F32) + pair((2 * t, keys), BF16)
                            + pair((2 * t, LANES), F32) + pair((2 * t, LANES), F32)
                            + pair((2 * t, LANES), F32) + pair((2 * t, V_DIM), F32)),
            compiler_params=_params(("arbitrary", "arbitrary")),
            name=f"prompt_attention_q{i}")(lam, rel_bias, q3, k, v3, bias, subln_g.reshape(1, V_DIM)))
    return jnp.stack(pieces, axis=1).reshape(batch * seq, width)


def _decode_body(pt_ref, lam_ref, q_ref, kn_ref, vn_ref, rbt_ref, bucket_ref, g_ref, *rest,
                 pages, page, n_steps, out_scale):
    k_refs = rest[:pages]
    v_refs = rest[pages:2 * pages]
    o_ref = rest[2 * pages]
    qexp_sc, bias_sc, m_sc, l_sc, acc_sc = rest[2 * pages + 1:]
    step = pl.program_id(1)
    rows = 2 * N_HEADS
    width = N_HEADS * V_DIM

    @pl.when(step == 0)
    def _():
        row = lax.broadcasted_iota(jnp.int32, (rows, width), 0)
        col = lax.broadcasted_iota(jnp.int32, (rows, width), 1)
        own_qk = (col // HEAD_DIM) == (row % N_HEADS) * 2 + row // N_HEADS
        q = (q_ref[...] * (HEAD_DIM ** -0.5)).astype(BF16).astype(F32)
        qexp = jnp.where(own_qk, jnp.broadcast_to(q, (rows, width)), 0.0)
        qexp_sc[...] = qexp.astype(BF16)
        bias_sc[...] = _bucket_lookup(bucket_ref[...], lambda bk: rbt_ref[:, bk:bk + 1])
        kn = kn_ref[...].astype(BF16).astype(F32)
        s_self = jnp.sum(qexp * kn, axis=1, keepdims=True) + rbt_ref[:, 0:1]
        m_sc[...] = s_self
        l_sc[...] = jnp.ones(l_sc.shape, F32)
        vn = vn_ref[...].astype(BF16).astype(F32)
        acc_sc[...] = jnp.broadcast_to(vn[:, None, :], acc_sc.shape)

    qexp = qexp_sc[...]
    bias_far = rbt_ref[:, NUM_BUCKETS - 1:NUM_BUCKETS]
    is_last = step == n_steps - 1
    s_parts = []
    for p in range(pages):
        s = jnp.dot(qexp, k_refs[p][...].astype(BF16), preferred_element_type=F32)
        if p == pages - 1:
            s = s + jnp.where(is_last, bias_sc[...], bias_far)
        else:
            s = s + bias_far
        s_parts.append(s)
    m_old = m_sc[...]
    m_new = m_old
    for s in s_parts:
        m_new = jnp.maximum(m_new, jnp.max(s, axis=1, keepdims=True))
    a = jnp.exp(m_old - m_new)
    l_new = a * l_sc[...]
    probs = []
    for p in range(pages):
        pr = jnp.exp(s_parts[p] - m_new)
        l_new = l_new + jnp.sum(pr, axis=1, keepdims=True)
        probs.append(pr.astype(BF16))
    probs = jnp.concatenate(probs, axis=1)
    m_sc[...] = m_new
    l_sc[...] = l_new
    for h in range(N_HEADS):
        vh = jnp.concatenate([v_refs[p][pl.ds(h, page, stride=N_HEADS), :].astype(BF16)
                              for p in range(pages)], axis=0)
        acc_sc[h] = a * acc_sc[h] + jnp.dot(probs, vh, preferred_element_type=F32)

    @pl.when(is_last)
    def _():
        nrm = acc_sc[...] / l_new[None]
        r = lax.broadcasted_iota(jnp.int32, nrm.shape, 1)
        hh = lax.broadcasted_iota(jnp.int32, nrm.shape, 0)
        coef = jnp.where(r == hh, 1.0, jnp.where(r == hh + N_HEADS, -lam_ref[0], 0.0))
        d = jnp.sum(coef * nrm, axis=1)
        ms = jnp.mean(d * d, axis=-1, keepdims=True)
        o_ref[...] = (d * lax.rsqrt(ms + LN_EPS) * g_ref[...] * out_scale).astype(o_ref.dtype)


def _sample_attention(q, k_new, v_new, cache_k, cache_v, page_table, rel_bias, lam, subln_g,
                      *, pages, out_scale):
    bs, width = q.shape
    page = cache_k.shape[2]
    n_pages = page_table.shape[1]
    n_steps = n_pages // pages
    past = n_pages * page
    kpos = past - page + jnp.arange(page, dtype=jnp.int32)
    bucket_last = _rel_bucket(past - kpos).reshape(1, page)
    rbt = jnp.tile(rel_bias.T, (2, 1))

    def tok_spec():
        return pl.BlockSpec((None, 1, width), lambda b, s, pt: (b, 0, 0))

    def page_spec(shape, p):
        return pl.BlockSpec((None,) + shape, lambda b, s, pt, p=p: (pt[b, s * pages + p], 0, 0))

    full = lambda shape: pl.BlockSpec(shape, lambda b, s, pt: tuple(0 for _ in shape))
    head_spec = pl.BlockSpec((None, N_HEADS, V_DIM), lambda b, s, pt: (b, 0, 0))
    in_specs = ([pl.BlockSpec(memory_space=pltpu.SMEM), tok_spec(), tok_spec(), head_spec,
                 full((2 * N_HEADS, NUM_BUCKETS)), full((1, page)), full((1, V_DIM))]
                + [page_spec((width, page), p) for p in range(pages)]
                + [page_spec((page * N_HEADS, V_DIM), p) for p in range(pages)])
    rows = 2 * N_HEADS
    body = functools.partial(_decode_body, pages=pages, page=page, n_steps=n_steps,
                             out_scale=out_scale)
    out = pl.pallas_call(
        body,
        grid_spec=pltpu.PrefetchScalarGridSpec(
            num_scalar_prefetch=1, grid=(bs, n_steps), in_specs=in_specs,
            out_specs=head_spec,
            scratch_shapes=[pltpu.VMEM((rows, width), BF16), pltpu.VMEM((rows, page), F32),
                            pltpu.VMEM((rows, 1), F32), pltpu.VMEM((rows, 1), F32),
                            pltpu.VMEM((N_HEADS, rows, V_DIM), F32)]),
        out_shape=jax.ShapeDtypeStruct((bs, N_HEADS, V_DIM), BF16),
        compiler_params=_params(("arbitrary", "arbitrary")),
        name="sample_attention")(
            page_table, lam, q.reshape(bs, 1, width), k_new.reshape(bs, 1, width),
            v_new.reshape(bs, N_HEADS, V_DIM), rbt, bucket_last, subln_g.reshape(1, V_DIM),
            *([cache_k] * pages), *([cache_v] * pages))
    return out.reshape(bs, width)


def _s5_params(a_re, a_im, log_dt, b_re, b_im, c_re, c_im):
    dt = jnp.exp(log_dt)[:, None]
    mag = jnp.exp(a_re * dt)
    ang = a_im * dt
    abar_re = mag * jnp.cos(ang)
    abar_im = mag * jnp.sin(ang)
    den = a_re * a_re + a_im * a_im
    f_re = ((abar_re - 1.0) * a_re + abar_im * a_im) / den
    f_im = (abar_im * a_re - (abar_re - 1.0) * a_im) / den
    bb_re = f_re[..., None] * b_re - f_im[..., None] * b_im
    bb_im = f_re[..., None] * b_im + f_im[..., None] * b_re
    eye = jnp.eye(SLAB_GROUPS, dtype=F32)

    def in_slabs(bb):
        tt = bb.reshape(N_SLABS, SLAB_GROUPS, STATE_DIM, SSM_GROUP).transpose(0, 1, 3, 2)
        full = tt[:, :, :, None, :] * eye[None, :, None, :, None]
        return full.reshape(N_SLABS, SLAB_IN, SLAB_STATE).astype(BF16)

    ab_re = abar_re[..., None] * bb_re - abar_im[..., None] * bb_im
    ab_im = abar_re[..., None] * bb_im + abar_im[..., None] * bb_re

    def out_slabs(cc):
        tt = cc.reshape(N_SLABS, SLAB_GROUPS, SSM_GROUP, STATE_DIM).transpose(0, 1, 3, 2)
        full = tt[:, :, :, None, :] * eye[None, :, None, :, None]
        return full.reshape(N_SLABS, SLAB_STATE, SLAB_IN).astype(BF16)

    return (abar_re.reshape(N_SLABS, SLAB_STATE), abar_im.reshape(N_SLABS, SLAB_STATE),
            in_slabs(bb_re), in_slabs(bb_im), out_slabs(c_re), out_slabs(c_im),
            in_slabs(ab_re), in_slabs(ab_im))


def _glu_out(y, u, d_ref, wglu_ref):
    g = jax.nn.gelu(y + d_ref[...] * u)
    gate = jnp.dot(g.astype(BF16), wglu_ref[...].astype(BF16), preferred_element_type=F32)
    return g * jax.nn.sigmoid(gate)


def _s5_prompt_body(u_ref, ar_ref, ai_ref, bbr_ref, bbi_ref, cr_ref, ci_ref, d_ref, wglu_ref,
                    o_ref, sre_ref, sim_ref, xr_sc, xi_sc, pw_sc, cr_sc, ci_sc, perm_sc, *, batch):
    c = pl.program_id(0)
    sub = SUBLANES
    _, tc, width = u_ref.shape
    n_rows = batch * tc
    t = lax.broadcasted_iota(jnp.int32, (sub, SLAB_STATE), 0)
    first = t < batch

    def cmul(pr, pi, qr, qi):
        return pr * qr - pi * qi, pr * qi + pi * qr

    @pl.when(c == 0)
    def _():
        for j in range(N_SLABS):
            a = (jnp.broadcast_to(ar_ref[j:j + 1, :], t.shape), jnp.broadcast_to(ai_ref[j:j + 1, :], t.shape))
            a2 = cmul(*a, *a)
            for part in range(2):
                pw_sc[j, part] = jnp.where(first, a[part], a2[part])
        cr_sc[...] = jnp.zeros(cr_sc.shape, F32)
        ci_sc[...] = jnp.zeros(ci_sc.shape, F32)
        r = lax.broadcasted_iota(jnp.int32, (n_rows, n_rows), 0)
        k = lax.broadcasted_iota(jnp.int32, (n_rows, n_rows), 1)
        perm_sc[0] = jnp.where(k == (r % batch) * tc + r // batch, 1.0, 0.0).astype(BF16)
        perm_sc[1] = jnp.where(r == (k % batch) * tc + k // batch, 1.0, 0.0).astype(BF16)

    def permute(which, x):
        return jnp.dot(perm_sc[which], x, preferred_element_type=F32)

    u_seq = u_ref[...].reshape(n_rows, width)
    hi = u_seq.astype(BF16)
    rest = u_seq - hi.astype(F32)
    mid = rest.astype(BF16)
    lo = (rest - mid.astype(F32)).astype(BF16)
    u_hi = permute(0, hi)
    ub = u_hi.astype(BF16)
    u = u_hi + permute(0, mid) + permute(0, lo)
    row = lax.broadcasted_iota(jnp.int32, u_hi.shape, 0)
    u_before = jnp.where(row % sub >= batch, pltpu.roll(u_hi, batch, 0), 0.0).astype(BF16)
    for j in range(N_SLABS):
        cols = slice(j * SLAB_IN, (j + 1) * SLAB_IN)
        uj = jnp.concatenate([ub[:, cols], u_before[:, cols]], axis=1)
        xr_sc[j] = jnp.dot(uj, bbr_ref[j], preferred_element_type=F32)
        xi_sc[j] = jnp.dot(uj, bbi_ref[j], preferred_element_type=F32)

    for j in range(N_SLABS):
        pr, pi = cr_sc[j], ci_sc[j]
        for v in range(n_rows // sub):
            rows = slice(v * sub, (v + 1) * sub)
            lr = jnp.where(first, pltpu.roll(pr, batch, 0), pr)
            li = jnp.where(first, pltpu.roll(pi, batch, 0), pi)
            er, ei = cmul(pw_sc[j, 0], pw_sc[j, 1], lr, li)
            pr, pi = xr_sc[j, rows, :] + er, xi_sc[j, rows, :] + ei
            xr_sc[j, rows, :] = pr
            xi_sc[j, rows, :] = pi
        cr_sc[j] = pr
        ci_sc[j] = pi
    sre_ref[...] = cr_sc[...]
    sim_ref[...] = ci_sc[...]

    ys = []
    for j in range(N_SLABS):
        ys.append(jnp.dot(xr_sc[j].astype(BF16), cr_ref[j], preferred_element_type=F32)
                  - jnp.dot(xi_sc[j].astype(BF16), ci_ref[j], preferred_element_type=F32))
    out = _glu_out(jnp.concatenate(ys, axis=1), u, d_ref, wglu_ref).astype(BF16)
    o_ref[...] = permute(1, out).astype(o_ref.dtype).reshape(batch, tc, width)


def _s5_prompt(u, sp, d, w_glu, *, batch, seq, tc):
    assert SUBLANES == 2 * batch, "tile = two positions of every sequence"
    ar, ai, bbr, bbi, cr, ci, abr, abi = sp
    bbr = jnp.concatenate([bbr, abr], axis=1)
    bbi = jnp.concatenate([bbi, abi], axis=1)
    width = u.shape[1]
    rows = batch * tc
    const = lambda shape: pl.BlockSpec(shape, lambda c: tuple(0 for _ in shape))
    block = pl.BlockSpec((batch, tc, width), lambda c: (0, c, 0))
    state_spec = const((N_SLABS, SUBLANES, SLAB_STATE))
    state_shape = jax.ShapeDtypeStruct((N_SLABS, SUBLANES, SLAB_STATE), F32)
    out, s_re, s_im = pl.pallas_call(
        functools.partial(_s5_prompt_body, batch=batch), grid=(seq // tc,),
        in_specs=[block, const(ar.shape), const(ai.shape), const(bbr.shape), const(bbi.shape),
                  const(cr.shape), const(ci.shape), const((1, width)), const(w_glu.shape)],
        out_specs=[block, state_spec, state_spec],
        out_shape=[jax.ShapeDtypeStruct((batch, seq, width), BF16), state_shape, state_shape],
        scratch_shapes=[pltpu.VMEM((N_SLABS, rows, SLAB_STATE), F32),
                        pltpu.VMEM((N_SLABS, rows, SLAB_STATE), F32),
                        pltpu.VMEM((N_SLABS, 2, SUBLANES, SLAB_STATE), F32),
                        pltpu.VMEM((N_SLABS, SUBLANES, SLAB_STATE), F32),
                        pltpu.VMEM((N_SLABS, SUBLANES, SLAB_STATE), F32),
                        pltpu.VMEM((2, rows, rows), BF16)],
        compiler_params=_params(("arbitrary",)),
        name="s5_prompt")(u.reshape(batch, seq, width), ar, ai, bbr, bbi, cr, ci,
                          d.reshape(1, width), w_glu)
    return out.reshape(batch * seq, width), s_re, s_im


def _s5_sample_body(u_ref, x0r_ref, x0i_ref, ar_ref, ai_ref, bbr_ref, bbi_ref, cr_ref, ci_ref,
                    d_ref, wglu_ref, o_ref, sre_ref, sim_ref):
    u = u_ref[...]
    ub = u.astype(BF16)
    ys = []
    for j in range(N_SLABS):
        uj = ub[:, j * SLAB_IN:(j + 1) * SLAB_IN]
        ar = ar_ref[j:j + 1, :]
        ai = ai_ref[j:j + 1, :]
        x0r = x0r_ref[j]
        x0i = x0i_ref[j]
        xr = ar * x0r - ai * x0i + jnp.dot(uj, bbr_ref[j], preferred_element_type=F32)
        xi = ar * x0i + ai * x0r + jnp.dot(uj, bbi_ref[j], preferred_element_type=F32)
        sre_ref[j] = xr
        sim_ref[j] = xi
        ys.append(jnp.dot(xr.astype(BF16), cr_ref[j], preferred_element_type=F32)
                  - jnp.dot(xi.astype(BF16), ci_ref[j], preferred_element_type=F32))
    o_ref[...] = _glu_out(jnp.concatenate(ys, axis=1), u, d_ref, wglu_ref).astype(o_ref.dtype)


def _s5_sample(u, x0_re, x0_im, sp, d, w_glu):
    ar, ai, bbr, bbi, cr, ci = sp[:6]
    bs, width = u.shape
    state_shape = jax.ShapeDtypeStruct((N_SLABS, bs, SLAB_STATE), F32)
    return pl.pallas_call(
        _s5_sample_body,
        out_shape=[jax.ShapeDtypeStruct((bs, width), BF16), state_shape, state_shape],
        compiler_params=pltpu.CompilerParams(vmem_limit_bytes=VMEM_LIMIT_BYTES),
        name="s5_sample")(u, x0_re, x0_im, ar, ai, bbr, bbi, cr, ci, d.reshape(1, width), w_glu)


def _gated(c):
    gate = c[:, :LANES]
    return gate * jax.nn.sigmoid(gate) * c[:, LANES:]


def _up_prompt_body(h_ref, wg_ref, wv_ref, cwg_ref, cwv_ref, cbg_ref, cbv_ref,
                    act_ref, cg_ref, cv_ref, w_sc, *, rows):
    w_sc[:, 0:LANES] = wg_ref[...].astype(BF16)
    w_sc[:, LANES:2 * LANES] = wv_ref[...].astype(BF16)
    cw = jnp.concatenate([cwg_ref[...], cwv_ref[...]], axis=1)
    cb = jnp.concatenate([cbg_ref[...], cbv_ref[...]], axis=1)
    seq = h_ref.shape[0]
    sub = SUBLANES
    row8 = lax.broadcasted_iota(jnp.int32, (sub, 2 * LANES), 0)

    prev = jnp.zeros((sub, 2 * LANES), F32)
    for c in range(seq // rows):
        up = jnp.dot(h_ref[c * rows:(c + 1) * rows, :], w_sc[...], preferred_element_type=F32)
        cur = cb + cw[CONV_W - 1:CONV_W] * up
        for back in range(1, CONV_W):
            rolled = pltpu.roll(up, back, 0)
            head = jnp.where(row8 < back, pltpu.roll(prev, back, 0), rolled[0:sub])
            shifted = jnp.concatenate([head, rolled[sub:]], axis=0)
            cur = cur + cw[CONV_W - 1 - back:CONV_W - back] * shifted
        act_ref[c * rows:(c + 1) * rows, :] = _gated(cur).astype(act_ref.dtype)
        prev = up[rows - sub:, :]
    tail = prev[sub - (CONV_W - 1):, :]
    cg_ref[...] = tail[:, :LANES]
    cv_ref[...] = tail[:, LANES:]


def _up_prompt(hb, w_up, conv_w, conv_b, *, batch, seq):
    d_model = hb.shape[1]
    d_ff = w_up.shape[1] // 2
    nb = d_ff // LANES
    cb = conv_b.reshape(1, 2 * d_ff)
    col = lambda off: (lambda b, j: (0, j + off))
    tail_spec = pl.BlockSpec((None, CONV_W - 1, LANES), lambda b, j: (b, 0, j))
    tail_shape = jax.ShapeDtypeStruct((batch, CONV_W - 1, d_ff), F32)
    return pl.pallas_call(
        functools.partial(_up_prompt_body, rows=UP_ROW_CHUNK), grid=(batch, nb),
        scratch_shapes=[pltpu.VMEM((d_model, 2 * LANES), BF16)],
        in_specs=[pl.BlockSpec((seq, d_model), lambda b, j: (b, 0)),
                  pl.BlockSpec((d_model, LANES), col(0)), pl.BlockSpec((d_model, LANES), col(nb)),
                  pl.BlockSpec((CONV_W, LANES), col(0)), pl.BlockSpec((CONV_W, LANES), col(nb)),
                  pl.BlockSpec((1, LANES), col(0)), pl.BlockSpec((1, LANES), col(nb))],
        out_specs=[pl.BlockSpec((seq, LANES), lambda b, j: (b, j)), tail_spec, tail_spec],
        out_shape=[jax.ShapeDtypeStruct((batch * seq, d_ff), BF16), tail_shape, tail_shape],
        compiler_params=_params(("arbitrary", "arbitrary")),
        name="up_prompt")(hb, w_up, w_up, conv_w, conv_w, cb, cb)


def _up_sample_body(h_ref, wg_ref, wva_ref, wvb_ref, cwg_ref, cwva_ref, cwvb_ref, cbg_ref, cbva_ref,
                    cbvb_ref, sg_ref, sva_ref, svb_ref, act_ref, cg_ref, cv_ref):
    w2 = 2 * LANES
    w = jnp.concatenate([wg_ref[...].astype(BF16), wva_ref[...].astype(BF16),
                         wvb_ref[...].astype(BF16)], axis=1)
    up = jnp.dot(h_ref[...], w, preferred_element_type=F32)
    cw = jnp.concatenate([cwg_ref[...], cwva_ref[...], cwvb_ref[...]], axis=1)
    cb = jnp.concatenate([cbg_ref[...], cbva_ref[...], cbvb_ref[...]], axis=1)
    c = cb + cw[CONV_W - 1:CONV_W] * up
    for tap in range(CONV_W - 1):
        st = jnp.concatenate([sg_ref[tap], sva_ref[tap], svb_ref[tap]], axis=1)
        c = c + cw[tap:tap + 1] * st
    gate = c[:, 0:w2]
    act_ref[...] = (gate * jax.nn.sigmoid(gate) * c[:, w2:2 * w2]).astype(act_ref.dtype)
    for tap in range(1, CONV_W - 1):
        cg_ref[tap - 1] = sg_ref[tap]
        cv_ref[tap - 1] = jnp.concatenate([sva_ref[tap], svb_ref[tap]], axis=1)
    cg_ref[CONV_W - 2] = up[:, 0:w2]
    cv_ref[CONV_W - 2] = up[:, w2:2 * w2]


def _up_sample(hb, w_up, conv_w, conv_b, state):
    bs, d_model = hb.shape
    d_ff = w_up.shape[1] // 2
    nb = d_ff // LANES
    n_pairs = (nb + 1) // 2
    last = 2 * nb - 1
    w2 = 2 * LANES
    cb = conv_b.reshape(1, 2 * d_ff)
    gate = lambda shape: pl.BlockSpec(shape, lambda j: (0,) * (len(shape) - 1) + (j,))
    val = lambda shape, k: pl.BlockSpec(
        shape, lambda j: (0,) * (len(shape) - 1) + (jnp.minimum(nb + 2 * j + k, last),))
    operands = []
    specs = [pl.BlockSpec((bs, d_model), lambda j: (0, 0))]
    for arr, lead in ((w_up, (d_model,)), (conv_w, (CONV_W,)), (cb, (1,)), (state, (CONV_W - 1, bs))):
        specs += [gate(lead + (w2,)), val(lead + (LANES,), 0), val(lead + (LANES,), 1)]
        operands += [arr, arr, arr]
    tail_spec = pl.BlockSpec((CONV_W - 1, bs, w2), lambda j: (0, 0, j))
    tail_shape = jax.ShapeDtypeStruct((CONV_W - 1, bs, d_ff), F32)
    return pl.pallas_call(
        _up_sample_body, grid=(n_pairs,), in_specs=specs,
        out_specs=[pl.BlockSpec((bs, w2), lambda j: (0, j)), tail_spec, tail_spec],
        out_shape=[jax.ShapeDtypeStruct((bs, d_ff), BF16), tail_shape, tail_shape],
        compiler_params=_params(("arbitrary",)),
        name="up_sample")(hb, *operands)


def _one(x):
    return (x,)


def _sigmoid_out(acc):
    return (jax.nn.sigmoid(acc),)


def _both(acc):
    return (acc, acc)


def _merge(pa, ps, ga, gs):
    return (ga * pa + gs * ps,)


def _layer(xp, xs, w, *, batch, seq, alpha, attend_p, attend_s, ssm_p, ssm_s, up_p, up_s):
    d_model = xp.shape[1]
    qk_w = N_HEADS * 2 * HEAD_DIM
    v_w = N_HEADS * V_DIM
    ssm_w = d_model // 2
    w_in = w["w_in"]
    g1, b1 = w["ln1_g"].reshape(1, d_model), w["ln1_b"].reshape(1, d_model)
    big = dict(tm=MM_ROWS, tn=MM_COLS)
    c = 0
    (q,), (q_s,), xb = _matmul([(xp, xs, w_in, c)], [], [BF16], _one, n_cols=qk_w, name="proj_q",
                               emit_lhs=True, **big)
    c += qk_w
    kt, ktb, k_s = _proj_transposed(xb, xs, w_in, c, n_cols=qk_w, batch=batch, seq=seq, tm=MM_ROWS,
                                    name="proj_kt")
    c += qk_w
    c_v = c
    c += v_w
    (u,), (u_s,) = _matmul([(xb, xs, w_in, c)], [], [F32], _one, n_cols=ssm_w, name="proj_u", **big)
    c += ssm_w
    (ga,), (ga_s,) = _matmul([(xb, xs, w_in, c)], [], [F32], _sigmoid_out, n_cols=d_model, name="gate_a", **big)
    c += d_model
    (gs,), (gs_s,) = _matmul([(xb, xs, w_in, c)], [], [F32], _sigmoid_out, n_cols=d_model, name="gate_s", **big)
    (v, vb), (v_s, _) = _matmul([(xb, xs, w_in, c_v)], [], [F32, BF16], _both, n_cols=v_w, name="proj_v", **big)

    attn = attend_p(q, ktb, vb)
    ssm_out, re_p, im_p = ssm_p(u)
    ssm_out_s, re_s, im_s = ssm_s(u_s)
    attn_s = attend_s(q_s, k_s, v_s)

    (merged,), (merged_s,) = _matmul(
        [(attn, attn_s, w["w_proj_attn"], 0), (ssm_out, ssm_out_s, w["w_proj_ssm"], 0)],
        [(ga, ga_s), (gs, gs_s)], [BF16], _merge, n_cols=d_model, name="merge", **big)

    def post_ln1(acc, res, g, b):
        h = _ln(alpha * res + acc, g, b)
        return h, h

    (h, hb), (h_s, hb_s) = _matmul([(merged, merged_s, w["w_out"], 0)], [(xp, xs), g1, b1], [F32, BF16],
                                   post_ln1, n_cols=d_model, tm=LN_MM_ROWS, tn=d_model, name="out_proj_ln1")
    act, conv_p = up_p(hb)
    act_s, conv_s = up_s(hb_s)
    (r2,), (r2_s,) = _matmul([(act, act_s, w["w_down"], 0)], [(h, h_s)], [F32],
                             lambda acc, res: (alpha * res + acc,), n_cols=d_model,
                             tm=DOWN_ROWS, tn=DOWN_COLS, name="down_proj")
    y = _layer_norm(r2, w["ln2_g"], w["ln2_b"], tm=LN_ROWS, name="ln2")
    y_s = _layer_norm(r2_s, w["ln2_g"], w["ln2_b"], tm=LN_ROWS, name="ln2_sample")
    return (y, kt, v, re_p, im_p, conv_p), (y_s, k_s, v_s, re_s, im_s, conv_s)


def kernel(x_prompt, x_sample, cache_k, cache_v, state_ssm_re, state_ssm_im, state_conv, page_table, rel_bias, w_in, lambda_q1, lambda_k1, lambda_q2, lambda_k2, subln_g, ssm_a_re, ssm_a_im, ssm_log_dt, ssm_b_re, ssm_b_im, ssm_c_re, ssm_c_im, ssm_d, w_glu, w_proj_attn, w_proj_ssm, w_out, ln1_g, ln1_b, w_up, conv_w, conv_b, w_down, ln2_g, ln2_b):
    depth = w_in.shape[0]
    assert depth == 1, "single-layer trunk"
    bp, seq, d_model = x_prompt.shape
    bs, dec_seq, _ = x_sample.shape
    assert dec_seq == 1
    n_pool, page = cache_k.shape[1], cache_k.shape[2]
    d_ff = w_down.shape[1]
    n_groups = ssm_a_re.shape[1]
    assert n_groups == N_SLABS * SLAB_GROUPS and d_ff % LANES == 0
    alpha = (2.0 * depth) ** 0.25
    width = N_HEADS * V_DIM

    hp = x_prompt.reshape(bp * seq, d_model)
    hs = x_sample.reshape(bs, d_model)
    outs = {}
    for l in range(depth):
        lam_init = 0.8 - 0.6 * math.exp(-0.3 * l)
        out_scale = 1.0 - lam_init
        lam = (jnp.exp(jnp.sum(lambda_q1[l] * lambda_k1[l]))
               - jnp.exp(jnp.sum(lambda_q2[l] * lambda_k2[l])) + lam_init).reshape(1)
        w = dict(w_in=w_in[l], w_proj_attn=w_proj_attn[l], w_proj_ssm=w_proj_ssm[l], w_out=w_out[l],
                 ln1_g=ln1_g[l], ln1_b=ln1_b[l], w_down=w_down[l], ln2_g=ln2_g[l], ln2_b=ln2_b[l])
        sp = _s5_params(ssm_a_re[l], ssm_a_im[l], ssm_log_dt[l], ssm_b_re[l], ssm_b_im[l],
                        ssm_c_re[l], ssm_c_im[l])

        def attend_p(q, kt, v):
            return _prompt_attention(q, kt, v, rel_bias, lam, subln_g[l], batch=bp, seq=seq,
                                     t=ATTN_BLOCK, out_scale=out_scale)

        def ssm_p(u):
            o, sr, si = _s5_prompt(u, sp, ssm_d[l], w_glu[l], batch=bp, seq=seq, tc=S5_POSITIONS)
            last = lambda st: st[:, SUBLANES - bp:, :].transpose(1, 0, 2)
            return o, last(sr), last(si)

        def up_p(hb):
            act, cg, cv = _up_prompt(hb, w_up[l], conv_w[l], conv_b[l], batch=bp, seq=seq)
            return act, jnp.concatenate([cg, cv], axis=-1)

        ck = cache_k[l].transpose(0, 2, 3, 4, 1).reshape(n_pool, width, page)
        cv_ = cache_v[l].reshape(n_pool, page * N_HEADS, V_DIM)

        def attend_s(q, k, v):
            return _sample_attention(q.astype(F32), k, v, ck, cv_, page_table, rel_bias, lam,
                                     subln_g[l], pages=DECODE_PAGES, out_scale=out_scale)

        def ssm_s(u):
            x0r = state_ssm_re[l].reshape(bs, N_SLABS, SLAB_STATE).transpose(1, 0, 2)
            x0i = state_ssm_im[l].reshape(bs, N_SLABS, SLAB_STATE).transpose(1, 0, 2)
            o, sr, si = _s5_sample(u, x0r, x0i, sp, ssm_d[l], w_glu[l])
            return o, sr.transpose(1, 0, 2), si.transpose(1, 0, 2)

        def up_s(hb):
            act, cg, cv = _up_sample(hb, w_up[l], conv_w[l], conv_b[l],
                                     state_conv[l].transpose(1, 0, 2))
            return act, jnp.concatenate([cg, cv], axis=-1).transpose(1, 0, 2)

        (hp, kt_p, v_p, re_p, im_p, c_p), (hs, k_s, v_s, re_s, im_s, c_s) = _layer(
            hp, hs, w, batch=bp, seq=seq, alpha=alpha, attend_p=attend_p, attend_s=attend_s,
            ssm_p=ssm_p, ssm_s=ssm_s, up_p=up_p, up_s=up_s)
        k_p = kt_p.reshape(bp, N_HEADS, 2, HEAD_DIM, seq).transpose(0, 4, 1, 2, 3)

        for name, val in (("kp", k_p.reshape(bp, seq, N_HEADS, 2, HEAD_DIM)),
                          ("vp", v_p.reshape(bp, seq, N_HEADS, V_DIM)),
                          ("rep", re_p.reshape(bp, n_groups, STATE_DIM)),
                          ("imp", im_p.reshape(bp, n_groups, STATE_DIM)),
                          ("cp", c_p),
                          ("ks", k_s.reshape(bs, 1, N_HEADS, 2, HEAD_DIM)),
                          ("vs", v_s.reshape(bs, 1, N_HEADS, V_DIM)),
                          ("res", re_s.reshape(bs, n_groups, STATE_DIM)),
                          ("ims", im_s.reshape(bs, n_groups, STATE_DIM)),
                          ("cs", c_s)):
            outs.setdefault(name, []).append(val)

    st = {k: jnp.stack(v, axis=0) for k, v in outs.items()}
    return (hp.reshape(bp, seq, d_model), hs.reshape(bs, 1, d_model), st["kp"], st["vp"], st["rep"],
            st["imp"], st["cp"], st["ks"], st["vs"], st["res"], st["ims"], st["cs"])
```

```python
import functools
import math

import jax
import jax.numpy as jnp
from jax import lax
from jax.experimental import pallas as pl
from jax.experimental.pallas import tpu as pltpu

F32 = jnp.float32
BF16 = jnp.bfloat16

N_HEADS = 8
HEAD_DIM = 64
V_DIM = 2 * HEAD_DIM
SSM_GROUP = 16
STATE_DIM = 64
CONV_W = 3
NUM_BUCKETS = 32
MAX_EXACT = NUM_BUCKETS // 2
MAX_DISTANCE = 128
LN_EPS = 1e-5
NEG_INF = -1e30

VMEM_LIMIT_BYTES = 56 * 1024 * 1024
LANES = 128
SLAB_GROUPS = 8
N_SLABS = 8
SLAB_IN = SLAB_GROUPS * SSM_GROUP
SLAB_STATE = SLAB_GROUPS * STATE_DIM
SUBLANES = 8
MM_ROWS, MM_COLS = 1024, 1024
LN_MM_ROWS = 512
DOWN_ROWS, DOWN_COLS = 512, 512
LN_ROWS = 512
ATTN_BLOCK = 256
ATTN_KEY_BLOCK = 256
ATTN_HEADS_PER_STEP = 4
S5_POSITIONS = 64
DECODE_PAGES = 16
UP_ROW_CHUNK = 512


def _params(sem):
    return pltpu.CompilerParams(dimension_semantics=sem, vmem_limit_bytes=VMEM_LIMIT_BYTES)


def _mm_body(*refs, n_pairs, n_extra, n_out, epilogue, emit_lhs):
    it = iter(refs)
    take = lambda n: [next(it) for _ in range(n)]
    x_refs, xs_refs, w_refs = take(n_pairs), take(n_pairs), take(n_pairs)
    extra, extra_s = take(n_extra), take(n_extra)
    outs, outs_s = take(n_out), take(n_out)
    lhs_out = take(1) if emit_lhs else []
    wbf = take(n_pairs)

    def apply(lhs_refs, extra_refs, out_refs, keep):
        lhs = [x[...].astype(BF16) for x in lhs_refs]
        for ref in keep:
            ref[...] = lhs[0]
        accs = [jnp.dot(x, s[...], preferred_element_type=F32) for x, s in zip(lhs, wbf)]
        for o, r in zip(out_refs, epilogue(*accs, *[e[...] for e in extra_refs])):
            o[...] = r.astype(o.dtype)

    @pl.when(pl.program_id(1) == 0)
    def _():
        for w, s in zip(w_refs, wbf):
            s[...] = w[...].astype(BF16)
        apply(xs_refs, extra_s, outs_s, [])

    apply(x_refs, extra, outs, lhs_out)


def _matmul(pairs, extras, out_dtypes, epilogue, *, n_cols, tm, tn, name, emit_lhs=False):
    m = pairs[0][0].shape[0]
    ms = pairs[0][1].shape[0]
    grid = (n_cols // tn, m // tm)
    assert not emit_lhs or grid[0] == 1
    w_mode = dict(pipeline_mode=pl.Buffered(1)) if grid[0] == 1 else {}
    specs_x, specs_xs, specs_w, scratch = [], [], [], []
    for x, xs, w, c0 in pairs:
        k = x.shape[1]
        specs_x.append(pl.BlockSpec((tm, k), lambda j, i: (i, 0)))
        specs_xs.append(pl.BlockSpec((ms, k), lambda j, i: (0, 0)))
        specs_w.append(pl.BlockSpec((k, tn), lambda j, i, off=c0 // tn: (0, j + off), **w_mode))
        scratch.append(pltpu.VMEM((k, tn), BF16))
    specs_e, specs_es, args_e, args_es = [], [], [], []
    for e in extras:
        if isinstance(e, tuple):
            specs_e.append(pl.BlockSpec((tm, tn), lambda j, i: (i, j)))
            specs_es.append(pl.BlockSpec((ms, tn), lambda j, i: (0, j)))
            args_e.append(e[0])
            args_es.append(e[1])
        else:
            specs_e.append(pl.BlockSpec((1, tn), lambda j, i: (0, j)))
            specs_es.append(pl.BlockSpec((1, tn), lambda j, i: (0, j)))
            args_e.append(e)
            args_es.append(e)
    out_shape = ([jax.ShapeDtypeStruct((m, n_cols), d) for d in out_dtypes]
                 + [jax.ShapeDtypeStruct((ms, n_cols), d) for d in out_dtypes])
    out_specs = ([pl.BlockSpec((tm, tn), lambda j, i: (i, j)) for _ in out_dtypes]
                 + [pl.BlockSpec((ms, tn), lambda j, i: (0, j)) for _ in out_dtypes])
    if emit_lhs:
        k0 = pairs[0][0].shape[1]
        out_shape.append(jax.ShapeDtypeStruct((m, k0), BF16))
        out_specs.append(pl.BlockSpec((tm, k0), lambda j, i: (i, 0)))
    body = functools.partial(_mm_body, n_pairs=len(pairs), n_extra=len(extras),
                             n_out=len(out_dtypes), epilogue=epilogue, emit_lhs=emit_lhs)
    outs = pl.pallas_call(
        body, grid=grid, in_specs=specs_x + specs_xs + specs_w + specs_e + specs_es,
        out_specs=out_specs, out_shape=out_shape, scratch_shapes=scratch,
        compiler_params=_params(("arbitrary", "arbitrary")), name=name)(
            *[p[0] for p in pairs], *[p[1] for p in pairs], *[p[2] for p in pairs],
            *args_e, *args_es)
    n = len(out_dtypes)
    if emit_lhs:
        return outs[:n], outs[n:2 * n], outs[2 * n]
    return outs[:n], outs[n:]


def _proj_t_body(x_ref, xs_ref, w_ref, o_ref, ob_ref, os_ref, wt_sc):
    @pl.when(pl.program_id(0) == 0)
    def _():
        for c in range(w_ref.shape[1] // LANES):
            cols = slice(c * LANES, (c + 1) * LANES)
            wt_sc[cols, :] = w_ref[:, cols].T.astype(BF16)
        os_ref[...] = lax.dot_general(xs_ref[...].astype(BF16), wt_sc[...], (((1,), (1,)), ((), ())),
                                      preferred_element_type=F32)

    kt = lax.dot_general(wt_sc[...], x_ref[...], (((1,), (1,)), ((), ())),
                         preferred_element_type=F32)
    o_ref[...] = kt
    ob_ref[...] = kt.astype(BF16)


def _proj_transposed(x, xs, w, c0, *, n_cols, batch, seq, tm, name):
    k = x.shape[1]
    ms = xs.shape[0]
    per_b = seq // tm
    out_spec = pl.BlockSpec((None, n_cols, tm), lambda i: (i // per_b, 0, i % per_b))
    return pl.pallas_call(
        _proj_t_body, grid=(batch * per_b,),
        in_specs=[pl.BlockSpec((tm, k), lambda i: (i, 0)),
                  pl.BlockSpec((ms, k), lambda i: (0, 0)),
                  pl.BlockSpec((k, n_cols), lambda i, off=c0 // n_cols: (0, off),
                               pipeline_mode=pl.Buffered(1))],
        out_specs=[out_spec, out_spec, pl.BlockSpec((ms, n_cols), lambda i: (0, 0))],
        out_shape=[jax.ShapeDtypeStruct((batch, n_cols, seq), F32),
                   jax.ShapeDtypeStruct((batch, n_cols, seq), BF16),
                   jax.ShapeDtypeStruct((ms, n_cols), F32)],
        scratch_shapes=[pltpu.VMEM((n_cols, k), BF16)],
        compiler_params=_params(("arbitrary",)), name=name)(x, xs, w)


def _ln(x, g, b):
    mu = jnp.mean(x, axis=-1, keepdims=True)
    xc = x - mu
    var = jnp.mean(xc * xc, axis=-1, keepdims=True)
    return xc * lax.rsqrt(var + LN_EPS) * g + b


def _ln_body(x_ref, g_ref, b_ref, o_ref):
    o_ref[...] = _ln(x_ref[...], g_ref[...], b_ref[...])


def _layer_norm(x, g, b, *, tm, name):
    m, d = x.shape
    tm = min(tm, m)
    return pl.pallas_call(
        _ln_body, grid=(m // tm,),
        in_specs=[pl.BlockSpec((tm, d), lambda i: (i, 0)),
                  pl.BlockSpec((1, d), lambda i: (0, 0)),
                  pl.BlockSpec((1, d), lambda i: (0, 0))],
        out_specs=pl.BlockSpec((tm, d), lambda i: (i, 0)),
        out_shape=jax.ShapeDtypeStruct((m, d), F32),
        compiler_params=_params(("arbitrary",)), name=name)(x, g.reshape(1, d), b.reshape(1, d))


def _rel_bucket(n):
    n = jnp.maximum(n, 0)
    nf = jnp.maximum(n, 1).astype(F32)
    large = MAX_EXACT + jnp.floor(jnp.log(nf / MAX_EXACT) / math.log(MAX_DISTANCE / MAX_EXACT)
                                  * (NUM_BUCKETS - MAX_EXACT)).astype(jnp.int32)
    large = jnp.minimum(large, NUM_BUCKETS - 1)
    return jnp.where(n < MAX_EXACT, n, large)


def _bucket_lookup(bucket, table_fn):
    out = jnp.zeros(jnp.broadcast_shapes(bucket.shape, table_fn(0).shape), F32)
    for b in range(NUM_BUCKETS):
        out = out + jnp.where(bucket == b, table_fn(b), 0.0)
    return out


def _bias_tile_body(rb_ref, bucket_ref, o_ref):
    h = pl.program_id(0)
    o_ref[...] = _bucket_lookup(bucket_ref[...], lambda bk: rb_ref[bk, h])


def _bias_tiles(rel_bias, t):
    r = jnp.arange(t, dtype=jnp.int32)
    c = jnp.arange(2 * t, dtype=jnp.int32)
    buckets = _rel_bucket(r[:, None] + t - c[None, :])
    return pl.pallas_call(
        _bias_tile_body, grid=(N_HEADS,),
        in_specs=[pl.BlockSpec(memory_space=pltpu.SMEM), pl.BlockSpec((t, 2 * t), lambda h: (0, 0))],
        out_specs=pl.BlockSpec((None, t, 2 * t), lambda h: (h, 0, 0)),
        out_shape=jax.ShapeDtypeStruct((N_HEADS, t, 2 * t), F32),
        compiler_params=_params(("arbitrary",)), name="bias_tiles")(rel_bias, buckets)


def _attn_body(lam_ref, rb_ref, q_ref, k_ref, v_ref, bias_ref, g_ref, _result_so_far, o_ref, *scratch,
               t, n_far, near, heads, out_scale):
    s_bufs, p_bufs, m_bufs, mf_bufs, l_bufs, a_bufs = (scratch[0:2], scratch[2:4], scratch[4:6],
                                                       scratch[6:8], scratch[8:10], scratch[10:12])
    hg = pl.program_id(0)
    far = n_far * t
    kb_w = ATTN_KEY_BLOCK
    n_kb = (far + near) // kb_w
    n_tiles = kb_w // LANES

    def fold(x, op):
        out = x[:, 0:LANES]
        for c in range(1, n_tiles):
            out = op(out, x[:, c * LANES:(c + 1) * LANES])
        return out

    def head_cols(hh):
        return slice(hh * V_DIM, (hh + 1) * V_DIM)

    def score_block(hh, kb):
        par = hh % 2
        q = q_ref[:, head_cols(hh)] * (HEAD_DIM ** -0.5)
        lane = lax.broadcasted_iota(jnp.int32, q.shape, 1)
        zero = jnp.zeros_like(q)
        q2 = jnp.concatenate([jnp.where(lane < HEAD_DIM, q, zero), jnp.where(lane >= HEAD_DIM, q, zero)],
                             axis=0)
        cols = slice(kb * kb_w, (kb + 1) * kb_w)
        s = jnp.dot(q2, k_ref[head_cols(hh), cols], preferred_element_type=F32)
        bias_far = rb_ref[NUM_BUCKETS - 1, hg * heads + hh]
        if kb * kb_w < far:
            top = fold(s, jnp.maximum) + bias_far
        else:
            off = kb * kb_w - far
            lo = 2 * t - near + off
            bias = bias_ref[hh, :, lo:lo + kb_w]
            row = lax.broadcasted_iota(jnp.int32, (t, kb_w), 0)
            col = lax.broadcasted_iota(jnp.int32, (t, kb_w), 1)
            keep = col + (off - (near - t)) <= row
            s = jnp.where(jnp.concatenate([keep, keep], axis=0),
                          s + jnp.concatenate([bias, bias], axis=0), NEG_INF)
            top = fold(s, jnp.maximum)
        s_bufs[par][:, cols] = s
        if kb == 0:
            m_bufs[par][...] = top
        else:
            m_bufs[par][...] = jnp.maximum(m_bufs[par][...], top)
        if kb == n_kb - 1:
            m = jnp.broadcast_to(jnp.max(m_bufs[par][...], axis=1, keepdims=True), m_bufs[par].shape)
            m_bufs[par][...] = m
            mf_bufs[par][...] = m - bias_far

    def exp_block(hh, kb):
        par = hh % 2
        cols = slice(kb * kb_w, (kb + 1) * kb_w)
        m = (mf_bufs if kb * kb_w < far else m_bufs)[par][...]
        p = jnp.exp(s_bufs[par][:, cols] - jnp.concatenate([m] * n_tiles, axis=1))
        if kb == 0:
            l_bufs[par][...] = fold(p, jnp.add)
        else:
            l_bufs[par][...] += fold(p, jnp.add)
        p_bufs[par][:, cols] = p.astype(BF16)

    def value_block(hh, kb):
        par = hh % 2
        rows = slice(kb * kb_w, (kb + 1) * kb_w)
        pv = jnp.dot(p_bufs[par][:, rows], v_ref[rows, head_cols(hh)], preferred_element_type=F32)
        if kb == 0:
            a_bufs[par][...] = pv
        else:
            a_bufs[par][...] += pv
        if kb == n_kb - 1:
            nrm = a_bufs[par][...] / jnp.sum(l_bufs[par][...], axis=1, keepdims=True)
            o = nrm[0:t] - lam_ref[0] * nrm[t:2 * t]
            ms = jnp.mean(o * o, axis=-1, keepdims=True)
            o_ref[:, head_cols(hh)] = (o * lax.rsqrt(ms + LN_EPS) * g_ref[...] * out_scale
                                       ).astype(o_ref.dtype)

    for stage in range(heads + 2):
        for kb in range(n_kb):
            if stage < heads:
                score_block(stage, kb)
            if 0 <= stage - 1 < heads:
                exp_block(stage - 1, kb)
            if 0 <= stage - 2 < heads:
                value_block(stage - 2, kb)


def _prompt_attention(q, k, v, rel_bias, lam, subln_g, *, batch, seq, t, out_scale):
    assert t >= MAX_DISTANCE
    nq = seq // t
    width = N_HEADS * V_DIM
    q3 = q.reshape(batch, seq, width)
    v3 = v.reshape(batch, seq, width)
    bias = _bias_tiles(rel_bias, t)
    smem = pl.BlockSpec(memory_space=pltpu.SMEM)
    out = jnp.zeros((batch, seq, width), BF16)
    for i in range(nq):
        n_far = max(i - 1, 0)
        near = min(i + 1, 2) * t
        keys = n_far * t + near
        hp = ATTN_HEADS_PER_STEP
        body = functools.partial(_attn_body, t=t, n_far=n_far, near=near, heads=hp, out_scale=out_scale)
        pair = lambda shape, dtype: [pltpu.VMEM(shape, dtype)] * 2
        in_specs = [smem, smem,
                    pl.BlockSpec((None, t, hp * V_DIM), lambda h, b, i=i: (b, i, h)),
                    pl.BlockSpec((None, hp * V_DIM, keys), lambda h, b: (b, h, 0)),
                    pl.BlockSpec((None, keys, hp * V_DIM), lambda h, b: (b, 0, h)),
                    pl.BlockSpec((hp, t, 2 * t), lambda h, b: (h, 0, 0)),
                    pl.BlockSpec((1, V_DIM), lambda h, b: (0, 0)),
                    pl.BlockSpec(memory_space=pl.ANY)]
        args = [lam, rel_bias, q3, k, v3, bias, subln_g.reshape(1, V_DIM), out]
        out = pl.pallas_call(
            body, grid=(N_HEADS // hp, batch), in_specs=in_specs,
            out_specs=pl.BlockSpec((None, t, hp * V_DIM), lambda h, b, i=i: (b, i, h)),
            out_shape=jax.ShapeDtypeStruct((batch, seq, width), BF16),
            input_output_aliases={len(args) - 1: 0},
            scratch_shapes=(pair((2 * t, keys), F32) + pair((2 * t, keys), BF16)
                            + pair((2 * t, LANES), F32) + pair((2 * t, LANES), F32)
                            + pair((2 * t, LANES), F32) + pair((2 * t, V_DIM), F32)),
            compiler_params=_params(("arbitrary", "arbitrary")),
            name=f"prompt_attention_q{i}")(*args)
    return out.reshape(batch * seq, width)


def _decode_body(pt_ref, lam_ref, q_ref, kn_ref, vn_ref, rbt_ref, bucket_ref, g_ref, *rest,
                 pages, page, n_steps, out_scale):
    k_refs = rest[:pages]
    v_refs = rest[pages:2 * pages]
    o_ref = rest[2 * pages]
    qexp_sc, bias_sc, m_sc, l_sc, acc_sc = rest[2 * pages + 1:]
    step = pl.program_id(1)
    rows = 2 * N_HEADS
    width = N_HEADS * V_DIM

    @pl.when(step == 0)
    def _():
        row = lax.broadcasted_iota(jnp.int32, (rows, width), 0)
        col = lax.broadcasted_iota(jnp.int32, (rows, width), 1)
        own_qk = (col // HEAD_DIM) == (row % N_HEADS) * 2 + row // N_HEADS
        q = (q_ref[...] * (HEAD_DIM ** -0.5)).astype(BF16).astype(F32)
        qexp = jnp.where(own_qk, jnp.broadcast_to(q, (rows, width)), 0.0)
        qexp_sc[...] = qexp.astype(BF16)
        bias_sc[...] = _bucket_lookup(bucket_ref[...], lambda bk: rbt_ref[:, bk:bk + 1])
        kn = kn_ref[...].astype(BF16).astype(F32)
        s_self = jnp.sum(qexp * kn, axis=1, keepdims=True) + rbt_ref[:, 0:1]
        m_sc[...] = s_self
        l_sc[...] = jnp.ones(l_sc.shape, F32)
        vn = vn_ref[...].astype(BF16).astype(F32)
        acc_sc[...] = jnp.broadcast_to(vn[:, None, :], acc_sc.shape)

    qexp = qexp_sc[...]
    bias_far = rbt_ref[:, NUM_BUCKETS - 1:NUM_BUCKETS]
    is_last = step == n_steps - 1
    s_parts = []
    for p in range(pages):
        s = jnp.dot(qexp, k_refs[p][...].astype(BF16), preferred_element_type=F32)
        if p == pages - 1:
            s = s + jnp.where(is_last, bias_sc[...], bias_far)
        else:
            s = s + bias_far
        s_parts.append(s)
    m_old = m_sc[...]
    m_new = m_old
    for s in s_parts:
        m_new = jnp.maximum(m_new, jnp.max(s, axis=1, keepdims=True))
    a = jnp.exp(m_old - m_new)
    l_new = a * l_sc[...]
    probs = []
    for p in range(pages):
        pr = jnp.exp(s_parts[p] - m_new)
        l_new = l_new + jnp.sum(pr, axis=1, keepdims=True)
        probs.append(pr.astype(BF16))
    probs = jnp.concatenate(probs, axis=1)
    m_sc[...] = m_new
    l_sc[...] = l_new
    for h in range(N_HEADS):
        vh = jnp.concatenate([v_refs[p][pl.ds(h, page, stride=N_HEADS), :].astype(BF16)
                              for p in range(pages)], axis=0)
        acc_sc[h] = a * acc_sc[h] + jnp.dot(probs, vh, preferred_element_type=F32)

    @pl.when(is_last)
    def _():
        nrm = acc_sc[...] / l_new[None]
        r = lax.broadcasted_iota(jnp.int32, nrm.shape, 1)
        hh = lax.broadcasted_iota(jnp.int32, nrm.shape, 0)
        coef = jnp.where(r == hh, 1.0, jnp.where(r == hh + N_HEADS, -lam_ref[0], 0.0))
        d = jnp.sum(coef * nrm, axis=1)
        ms = jnp.mean(d * d, axis=-1, keepdims=True)
        o_ref[...] = (d * lax.rsqrt(ms + LN_EPS) * g_ref[...] * out_scale).astype(o_ref.dtype)


def _sample_attention(q, k_new, v_new, cache_k, cache_v, page_table, rel_bias, lam, subln_g,
                      *, pages, out_scale):
    bs, width = q.shape
    page = cache_k.shape[2]
    n_pages = page_table.shape[1]
    n_steps = n_pages // pages
    past = n_pages * page
    kpos = past - page + jnp.arange(page, dtype=jnp.int32)
    bucket_last = _rel_bucket(past - kpos).reshape(1, page)
    rbt = jnp.tile(rel_bias.T, (2, 1))

    def tok_spec():
        return pl.BlockSpec((None, 1, width), lambda b, s, pt: (b, 0, 0))

    def page_spec(shape, p):
        return pl.BlockSpec((None,) + shape, lambda b, s, pt, p=p: (pt[b, s * pages + p], 0, 0))

    full = lambda shape: pl.BlockSpec(shape, lambda b, s, pt: tuple(0 for _ in shape))
    head_spec = pl.BlockSpec((None, N_HEADS, V_DIM), lambda b, s, pt: (b, 0, 0))
    in_specs = ([pl.BlockSpec(memory_space=pltpu.SMEM), tok_spec(), tok_spec(), head_spec,
                 full((2 * N_HEADS, NUM_BUCKETS)), full((1, page)), full((1, V_DIM))]
                + [page_spec((width, page), p) for p in range(pages)]
                + [page_spec((page * N_HEADS, V_DIM), p) for p in range(pages)])
    rows = 2 * N_HEADS
    body = functools.partial(_decode_body, pages=pages, page=page, n_steps=n_steps,
                             out_scale=out_scale)
    out = pl.pallas_call(
        body,
        grid_spec=pltpu.PrefetchScalarGridSpec(
            num_scalar_prefetch=1, grid=(bs, n_steps), in_specs=in_specs,
            out_specs=head_spec,
            scratch_shapes=[pltpu.VMEM((rows, width), BF16), pltpu.VMEM((rows, page), F32),
                            pltpu.VMEM((rows, 1), F32), pltpu.VMEM((rows, 1), F32),
                            pltpu.VMEM((N_HEADS, rows, V_DIM), F32)]),
        out_shape=jax.ShapeDtypeStruct((bs, N_HEADS, V_DIM), BF16),
        compiler_params=_params(("arbitrary", "arbitrary")),
        name="sample_attention")(
            page_table, lam, q.reshape(bs, 1, width), k_new.reshape(bs, 1, width),
            v_new.reshape(bs, N_HEADS, V_DIM), rbt, bucket_last, subln_g.reshape(1, V_DIM),
            *([cache_k] * pages), *([cache_v] * pages))
    return out.reshape(bs, width)


def _s5_params(a_re, a_im, log_dt, b_re, b_im, c_re, c_im):
    dt = jnp.exp(log_dt)[:, None]
    mag = jnp.exp(a_re * dt)
    ang = a_im * dt
    abar_re = mag * jnp.cos(ang)
    abar_im = mag * jnp.sin(ang)
    den = a_re * a_re + a_im * a_im
    f_re = ((abar_re - 1.0) * a_re + abar_im * a_im) / den
    f_im = (abar_im * a_re - (abar_re - 1.0) * a_im) / den
    bb_re = f_re[..., None] * b_re - f_im[..., None] * b_im
    bb_im = f_re[..., None] * b_im + f_im[..., None] * b_re
    eye = jnp.eye(SLAB_GROUPS, dtype=F32)

    def in_slabs(bb):
        tt = bb.reshape(N_SLABS, SLAB_GROUPS, STATE_DIM, SSM_GROUP).transpose(0, 1, 3, 2)
        full = tt[:, :, :, None, :] * eye[None, :, None, :, None]
        return full.reshape(N_SLABS, SLAB_IN, SLAB_STATE).astype(BF16)

    ab_re = abar_re[..., None] * bb_re - abar_im[..., None] * bb_im
    ab_im = abar_re[..., None] * bb_im + abar_im[..., None] * bb_re

    def out_slabs(cc):
        tt = cc.reshape(N_SLABS, SLAB_GROUPS, SSM_GROUP, STATE_DIM).transpose(0, 1, 3, 2)
        full = tt[:, :, :, None, :] * eye[None, :, None, :, None]
        return full.reshape(N_SLABS, SLAB_STATE, SLAB_IN).astype(BF16)

    return (abar_re.reshape(N_SLABS, SLAB_STATE), abar_im.reshape(N_SLABS, SLAB_STATE),
            in_slabs(bb_re), in_slabs(bb_im), out_slabs(c_re), out_slabs(c_im),
            in_slabs(ab_re), in_slabs(ab_im))


def _glu_out(y, u, d_ref, wglu_ref):
    g = jax.nn.gelu(y + d_ref[...] * u)
    gate = jnp.dot(g.astype(BF16), wglu_ref[...].astype(BF16), preferred_element_type=F32)
    return g * jax.nn.sigmoid(gate)


def _s5_prompt_body(u_ref, ar_ref, ai_ref, bbr_ref, bbi_ref, cr_ref, ci_ref, d_ref, wglu_ref,
                    o_ref, sre_ref, sim_ref, xr_sc, xi_sc, pw_sc, cr_sc, ci_sc, perm_sc, *, batch):
    c = pl.program_id(0)
    sub = SUBLANES
    _, tc, width = u_ref.shape
    n_rows = batch * tc
    t = lax.broadcasted_iota(jnp.int32, (sub, SLAB_STATE), 0)
    first = t < batch

    def cmul(pr, pi, qr, qi):
        return pr * qr - pi * qi, pr * qi + pi * qr

    @pl.when(c == 0)
    def _():
        for j in range(N_SLABS):
            a = (jnp.broadcast_to(ar_ref[j:j + 1, :], t.shape), jnp.broadcast_to(ai_ref[j:j + 1, :], t.shape))
            a2 = cmul(*a, *a)
            for part in range(2):
                pw_sc[j, part] = jnp.where(first, a[part], a2[part])
        cr_sc[...] = jnp.zeros(cr_sc.shape, F32)
        ci_sc[...] = jnp.zeros(ci_sc.shape, F32)
        r = lax.broadcasted_iota(jnp.int32, (n_rows, n_rows), 0)
        k = lax.broadcasted_iota(jnp.int32, (n_rows, n_rows), 1)
        perm_sc[0] = jnp.where(k == (r % batch) * tc + r // batch, 1.0, 0.0).astype(BF16)
        perm_sc[1] = jnp.where(r == (k % batch) * tc + k // batch, 1.0, 0.0).astype(BF16)

    def permute(which, x):
        return jnp.dot(perm_sc[which], x, preferred_element_type=F32)

    u_seq = u_ref[...].reshape(n_rows, width)
    hi = u_seq.astype(BF16)
    rest = u_seq - hi.astype(F32)
    mid = rest.astype(BF16)
    lo = (rest - mid.astype(F32)).astype(BF16)
    u_hi = permute(0, hi)
    ub = u_hi.astype(BF16)
    u = u_hi + permute(0, mid) + permute(0, lo)
    row = lax.broadcasted_iota(jnp.int32, u_hi.shape, 0)
    u_before = jnp.where(row % sub >= batch, pltpu.roll(u_hi, batch, 0), 0.0).astype(BF16)
    for j in range(N_SLABS):
        cols = slice(j * SLAB_IN, (j + 1) * SLAB_IN)
        uj = jnp.concatenate([ub[:, cols], u_before[:, cols]], axis=1)
        xr_sc[j] = jnp.dot(uj, bbr_ref[j], preferred_element_type=F32)
        xi_sc[j] = jnp.dot(uj, bbi_ref[j], preferred_element_type=F32)

    for j in range(N_SLABS):
        pr, pi = cr_sc[j], ci_sc[j]
        for v in range(n_rows // sub):
            rows = slice(v * sub, (v + 1) * sub)
            lr = jnp.where(first, pltpu.roll(pr, batch, 0), pr)
            li = jnp.where(first, pltpu.roll(pi, batch, 0), pi)
            er, ei = cmul(pw_sc[j, 0], pw_sc[j, 1], lr, li)
            pr, pi = xr_sc[j, rows, :] + er, xi_sc[j, rows, :] + ei
            xr_sc[j, rows, :] = pr
            xi_sc[j, rows, :] = pi
        cr_sc[j] = pr
        ci_sc[j] = pi
    sre_ref[...] = cr_sc[...]
    sim_ref[...] = ci_sc[...]

    ys = []
    for j in range(N_SLABS):
        ys.append(jnp.dot(xr_sc[j].astype(BF16), cr_ref[j], preferred_element_type=F32)
                  - jnp.dot(xi_sc[j].astype(BF16), ci_ref[j], preferred_element_type=F32))
    out = _glu_out(jnp.concatenate(ys, axis=1), u, d_ref, wglu_ref).astype(BF16)
    o_ref[...] = permute(1, out).astype(o_ref.dtype).reshape(batch, tc, width)


def _s5_prompt(u, sp, d, w_glu, *, batch, seq, tc):
    assert SUBLANES == 2 * batch, "tile = two positions of every sequence"
    ar, ai, bbr, bbi, cr, ci, abr, abi = sp
    bbr = jnp.concatenate([bbr, abr], axis=1)
    bbi = jnp.concatenate([bbi, abi], axis=1)
    width = u.shape[1]
    rows = batch * tc
    const = lambda shape: pl.BlockSpec(shape, lambda c: tuple(0 for _ in shape))
    block = pl.BlockSpec((batch, tc, width), lambda c: (0, c, 0))
    state_spec = const((N_SLABS, SUBLANES, SLAB_STATE))
    state_shape = jax.ShapeDtypeStruct((N_SLABS, SUBLANES, SLAB_STATE), F32)
    out, s_re, s_im = pl.pallas_call(
        functools.partial(_s5_prompt_body, batch=batch), grid=(seq // tc,),
        in_specs=[block, const(ar.shape), const(ai.shape), const(bbr.shape), const(bbi.shape),
                  const(cr.shape), const(ci.shape), const((1, width)), const(w_glu.shape)],
        out_specs=[block, state_spec, state_spec],
        out_shape=[jax.ShapeDtypeStruct((batch, seq, width), BF16), state_shape, state_shape],
        scratch_shapes=[pltpu.VMEM((N_SLABS, rows, SLAB_STATE), F32),
                        pltpu.VMEM((N_SLABS, rows, SLAB_STATE), F32),
                        pltpu.VMEM((N_SLABS, 2, SUBLANES, SLAB_STATE), F32),
                        pltpu.VMEM((N_SLABS, SUBLANES, SLAB_STATE), F32),
                        pltpu.VMEM((N_SLABS, SUBLANES, SLAB_STATE), F32),
                        pltpu.VMEM((2, rows, rows), BF16)],
        compiler_params=_params(("arbitrary",)),
        name="s5_prompt")(u.reshape(batch, seq, width), ar, ai, bbr, bbi, cr, ci,
                          d.reshape(1, width), w_glu)
    return out.reshape(batch * seq, width), s_re, s_im


def _s5_sample_body(u_ref, x0r_ref, x0i_ref, ar_ref, ai_ref, bbr_ref, bbi_ref, cr_ref, ci_ref,
                    d_ref, wglu_ref, o_ref, sre_ref, sim_ref):
    u = u_ref[...]
    ub = u.astype(BF16)
    ys = []
    for j in range(N_SLABS):
        uj = ub[:, j * SLAB_IN:(j + 1) * SLAB_IN]
        ar = ar_ref[j:j + 1, :]
        ai = ai_ref[j:j + 1, :]
        x0r = x0r_ref[j]
        x0i = x0i_ref[j]
        xr = ar * x0r - ai * x0i + jnp.dot(uj, bbr_ref[j], preferred_element_type=F32)
        xi = ar * x0i + ai * x0r + jnp.dot(uj, bbi_ref[j], preferred_element_type=F32)
        sre_ref[j] = xr
        sim_ref[j] = xi
        ys.append(jnp.dot(xr.astype(BF16), cr_ref[j], preferred_element_type=F32)
                  - jnp.dot(xi.astype(BF16), ci_ref[j], preferred_element_type=F32))
    o_ref[...] = _glu_out(jnp.concatenate(ys, axis=1), u, d_ref, wglu_ref).astype(o_ref.dtype)


def _s5_sample(u, x0_re, x0_im, sp, d, w_glu):
    ar, ai, bbr, bbi, cr, ci = sp[:6]
    bs, width = u.shape
    state_shape = jax.ShapeDtypeStruct((N_SLABS, bs, SLAB_STATE), F32)
    return pl.pallas_call(
        _s5_sample_body,
        out_shape=[jax.ShapeDtypeStruct((bs, width), BF16), state_shape, state_shape],
        compiler_params=pltpu.CompilerParams(vmem_limit_bytes=VMEM_LIMIT_BYTES),
        name="s5_sample")(u, x0_re, x0_im, ar, ai, bbr, bbi, cr, ci, d.reshape(1, width), w_glu)


def _gated(c):
    gate = c[:, :LANES]
    return gate * jax.nn.sigmoid(gate) * c[:, LANES:]


def _up_prompt_body(h_ref, wg_ref, wv_ref, cwg_ref, cwv_ref, cbg_ref, cbv_ref,
                    act_ref, cg_ref, cv_ref, w_sc, *, rows):
    w_sc[:, 0:LANES] = wg_ref[...].astype(BF16)
    w_sc[:, LANES:2 * LANES] = wv_ref[...].astype(BF16)
    cw = jnp.concatenate([cwg_ref[...], cwv_ref[...]], axis=1)
    cb = jnp.concatenate([cbg_ref[...], cbv_ref[...]], axis=1)
    seq = h_ref.shape[0]
    sub = SUBLANES
    row8 = lax.broadcasted_iota(jnp.int32, (sub, 2 * LANES), 0)

    prev = jnp.zeros((sub, 2 * LANES), F32)
    for c in range(seq // rows):
        up = jnp.dot(h_ref[c * rows:(c + 1) * rows, :], w_sc[...], preferred_element_type=F32)
        cur = cb + cw[CONV_W - 1:CONV_W] * up
        for back in range(1, CONV_W):
            rolled = pltpu.roll(up, back, 0)
            head = jnp.where(row8 < back, pltpu.roll(prev, back, 0), rolled[0:sub])
            shifted = jnp.concatenate([head, rolled[sub:]], axis=0)
            cur = cur + cw[CONV_W - 1 - back:CONV_W - back] * shifted
        act_ref[c * rows:(c + 1) * rows, :] = _gated(cur).astype(act_ref.dtype)
        prev = up[rows - sub:, :]
    tail = prev[sub - (CONV_W - 1):, :]
    cg_ref[...] = tail[:, :LANES]
    cv_ref[...] = tail[:, LANES:]


def _up_prompt(hb, w_up, conv_w, conv_b, *, batch, seq):
    d_model = hb.shape[1]
    d_ff = w_up.shape[1] // 2
    nb = d_ff // LANES
    cb = conv_b.reshape(1, 2 * d_ff)
    col = lambda off: (lambda b, j: (0, j + off))
    tail_spec = pl.BlockSpec((None, CONV_W - 1, LANES), lambda b, j: (b, 0, j))
    tail_shape = jax.ShapeDtypeStruct((batch, CONV_W - 1, d_ff), F32)
    return pl.pallas_call(
        functools.partial(_up_prompt_body, rows=UP_ROW_CHUNK), grid=(batch, nb),
        scratch_shapes=[pltpu.VMEM((d_model, 2 * LANES), BF16)],
        in_specs=[pl.BlockSpec((seq, d_model), lambda b, j: (b, 0)),
                  pl.BlockSpec((d_model, LANES), col(0)), pl.BlockSpec((d_model, LANES), col(nb)),
                  pl.BlockSpec((CONV_W, LANES), col(0)), pl.BlockSpec((CONV_W, LANES), col(nb)),
                  pl.BlockSpec((1, LANES), col(0)), pl.BlockSpec((1, LANES), col(nb))],
        out_specs=[pl.BlockSpec((seq, LANES), lambda b, j: (b, j)), tail_spec, tail_spec],
        out_shape=[jax.ShapeDtypeStruct((batch * seq, d_ff), BF16), tail_shape, tail_shape],
        compiler_params=_params(("arbitrary", "arbitrary")),
        name="up_prompt")(hb, w_up, w_up, conv_w, conv_w, cb, cb)


def _up_sample_body(h_ref, wg_ref, wva_ref, wvb_ref, cwg_ref, cwva_ref, cwvb_ref, cbg_ref, cbva_ref,
                    cbvb_ref, sg_ref, sva_ref, svb_ref, act_ref, cg_ref, cv_ref):
    w2 = 2 * LANES
    w = jnp.concatenate([wg_ref[...].astype(BF16), wva_ref[...].astype(BF16),
                         wvb_ref[...].astype(BF16)], axis=1)
    up = jnp.dot(h_ref[...], w, preferred_element_type=F32)
    cw = jnp.concatenate([cwg_ref[...], cwva_ref[...], cwvb_ref[...]], axis=1)
    cb = jnp.concatenate([cbg_ref[...], cbva_ref[...], cbvb_ref[...]], axis=1)
    c = cb + cw[CONV_W - 1:CONV_W] * up
    for tap in range(CONV_W - 1):
        st = jnp.concatenate([sg_ref[tap], sva_ref[tap], svb_ref[tap]], axis=1)
        c = c + cw[tap:tap + 1] * st
    gate = c[:, 0:w2]
    act_ref[...] = (gate * jax.nn.sigmoid(gate) * c[:, w2:2 * w2]).astype(act_ref.dtype)
    for tap in range(1, CONV_W - 1):
        cg_ref[tap - 1] = sg_ref[tap]
        cv_ref[tap - 1] = jnp.concatenate([sva_ref[tap], svb_ref[tap]], axis=1)
    cg_ref[CONV_W - 2] = up[:, 0:w2]
    cv_ref[CONV_W - 2] = up[:, w2:2 * w2]


def _up_sample(hb, w_up, conv_w, conv_b, state):
    bs, d_model = hb.shape
    d_ff = w_up.shape[1] // 2
    nb = d_ff // LANES
    n_pairs = (nb + 1) // 2
    last = 2 * nb - 1
    w2 = 2 * LANES
    cb = conv_b.reshape(1, 2 * d_ff)
    gate = lambda shape: pl.BlockSpec(shape, lambda j: (0,) * (len(shape) - 1) + (j,))
    val = lambda shape, k: pl.BlockSpec(
        shape, lambda j: (0,) * (len(shape) - 1) + (jnp.minimum(nb + 2 * j + k, last),))
    operands = []
    specs = [pl.BlockSpec((bs, d_model), lambda j: (0, 0))]
    for arr, lead in ((w_up, (d_model,)), (conv_w, (CONV_W,)), (cb, (1,)), (state, (CONV_W - 1, bs))):
        specs += [gate(lead + (w2,)), val(lead + (LANES,), 0), val(lead + (LANES,), 1)]
        operands += [arr, arr, arr]
    tail_spec = pl.BlockSpec((CONV_W - 1, bs, w2), lambda j: (0, 0, j))
    tail_shape = jax.ShapeDtypeStruct((CONV_W - 1, bs, d_ff), F32)
    return pl.pallas_call(
        _up_sample_body, grid=(n_pairs,), in_specs=specs,
        out_specs=[pl.BlockSpec((bs, w2), lambda j: (0, j)), tail_spec, tail_spec],
        out_shape=[jax.ShapeDtypeStruct((bs, d_ff), BF16), tail_shape, tail_shape],
        compiler_params=_params(("arbitrary",)),
        name="up_sample")(hb, *operands)


def _one(x):
    return (x,)


def _sigmoid_out(acc):
    return (jax.nn.sigmoid(acc),)


def _both(acc):
    return (acc, acc)


def _merge(pa, ps, ga, gs):
    return (ga * pa + gs * ps,)


def _layer(xp, xs, w, *, batch, seq, alpha, attend_p, attend_s, ssm_p, ssm_s, up_p, up_s):
    d_model = xp.shape[1]
    qk_w = N_HEADS * 2 * HEAD_DIM
    v_w = N_HEADS * V_DIM
    ssm_w = d_model // 2
    w_in = w["w_in"]
    g1, b1 = w["ln1_g"].reshape(1, d_model), w["ln1_b"].reshape(1, d_model)
    big = dict(tm=MM_ROWS, tn=MM_COLS)
    c = 0
    (q,), (q_s,), xb = _matmul([(xp, xs, w_in, c)], [], [BF16], _one, n_cols=qk_w, name="proj_q",
                               emit_lhs=True, **big)
    c += qk_w
    kt, ktb, k_s = _proj_transposed(xb, xs, w_in, c, n_cols=qk_w, batch=batch, seq=seq, tm=MM_ROWS,
                                    name="proj_kt")
    c += qk_w
    c_v = c
    c += v_w
    (u,), (u_s,) = _matmul([(xb, xs, w_in, c)], [], [F32], _one, n_cols=ssm_w, name="proj_u", **big)
    c += ssm_w
    (ga,), (ga_s,) = _matmul([(xb, xs, w_in, c)], [], [F32], _sigmoid_out, n_cols=d_model, name="gate_a", **big)
    c += d_model
    (gs,), (gs_s,) = _matmul([(xb, xs, w_in, c)], [], [F32], _sigmoid_out, n_cols=d_model, name="gate_s", **big)
    (v, vb), (v_s, _) = _matmul([(xb, xs, w_in, c_v)], [], [F32, BF16], _both, n_cols=v_w, name="proj_v", **big)

    attn = attend_p(q, ktb, vb)
    ssm_out, re_p, im_p = ssm_p(u)
    ssm_out_s, re_s, im_s = ssm_s(u_s)
    attn_s = attend_s(q_s, k_s, v_s)

    (merged,), (merged_s,) = _matmul(
        [(attn, attn_s, w["w_proj_attn"], 0), (ssm_out, ssm_out_s, w["w_proj_ssm"], 0)],
        [(ga, ga_s), (gs, gs_s)], [BF16], _merge, n_cols=d_model, name="merge", **big)

    def post_ln1(acc, res, g, b):
        h = _ln(alpha * res + acc, g, b)
        return h, h

    (h, hb), (h_s, hb_s) = _matmul([(merged, merged_s, w["w_out"], 0)], [(xp, xs), g1, b1], [F32, BF16],
                                   post_ln1, n_cols=d_model, tm=LN_MM_ROWS, tn=d_model, name="out_proj_ln1")
    act, conv_p = up_p(hb)
    act_s, conv_s = up_s(hb_s)
    (r2,), (r2_s,) = _matmul([(act, act_s, w["w_down"], 0)], [(h, h_s)], [F32],
                             lambda acc, res: (alpha * res + acc,), n_cols=d_model,
                             tm=DOWN_ROWS, tn=DOWN_COLS, name="down_proj")
    y = _layer_norm(r2, w["ln2_g"], w["ln2_b"], tm=LN_ROWS, name="ln2")
    y_s = _layer_norm(r2_s, w["ln2_g"], w["ln2_b"], tm=LN_ROWS, name="ln2_sample")
    return (y, kt, v, re_p, im_p, conv_p), (y_s, k_s, v_s, re_s, im_s, conv_s)


def kernel(x_prompt, x_sample, cache_k, cache_v, state_ssm_re, state_ssm_im, state_conv, page_table, rel_bias, w_in, lambda_q1, lambda_k1, lambda_q2, lambda_k2, subln_g, ssm_a_re, ssm_a_im, ssm_log_dt, ssm_b_re, ssm_b_im, ssm_c_re, ssm_c_im, ssm_d, w_glu, w_proj_attn, w_proj_ssm, w_out, ln1_g, ln1_b, w_up, conv_w, conv_b, w_down, ln2_g, ln2_b):
    depth = w_in.shape[0]
    assert depth == 1, "single-layer trunk"
    bp, seq, d_model = x_prompt.shape
    bs, dec_seq, _ = x_sample.shape
    assert dec_seq == 1
    n_pool, page = cache_k.shape[1], cache_k.shape[2]
    d_ff = w_down.shape[1]
    n_groups = ssm_a_re.shape[1]
    assert n_groups == N_SLABS * SLAB_GROUPS and d_ff % LANES == 0
    alpha = (2.0 * depth) ** 0.25
    width = N_HEADS * V_DIM

    hp = x_prompt.reshape(bp * seq, d_model)
    hs = x_sample.reshape(bs, d_model)
    outs = {}
    for l in range(depth):
        lam_init = 0.8 - 0.6 * math.exp(-0.3 * l)
        out_scale = 1.0 - lam_init
        lam = (jnp.exp(jnp.sum(lambda_q1[l] * lambda_k1[l]))
               - jnp.exp(jnp.sum(lambda_q2[l] * lambda_k2[l])) + lam_init).reshape(1)
        w = dict(w_in=w_in[l], w_proj_attn=w_proj_attn[l], w_proj_ssm=w_proj_ssm[l], w_out=w_out[l],
                 ln1_g=ln1_g[l], ln1_b=ln1_b[l], w_down=w_down[l], ln2_g=ln2_g[l], ln2_b=ln2_b[l])
        sp = _s5_params(ssm_a_re[l], ssm_a_im[l], ssm_log_dt[l], ssm_b_re[l], ssm_b_im[l],
                        ssm_c_re[l], ssm_c_im[l])

        def attend_p(q, kt, v):
            return _prompt_attention(q, kt, v, rel_bias, lam, subln_g[l], batch=bp, seq=seq,
                                     t=ATTN_BLOCK, out_scale=out_scale)

        def ssm_p(u):
            o, sr, si = _s5_prompt(u, sp, ssm_d[l], w_glu[l], batch=bp, seq=seq, tc=S5_POSITIONS)
            last = lambda st: st[:, SUBLANES - bp:, :].transpose(1, 0, 2)
            return o, last(sr), last(si)

        def up_p(hb):
            act, cg, cv = _up_prompt(hb, w_up[l], conv_w[l], conv_b[l], batch=bp, seq=seq)
            return act, jnp.concatenate([cg, cv], axis=-1)

        ck = cache_k[l].transpose(0, 2, 3, 4, 1).reshape(n_pool, width, page)
        cv_ = cache_v[l].reshape(n_pool, page * N_HEADS, V_DIM)

        def attend_s(q, k, v):
            return _sample_attention(q.astype(F32), k, v, ck, cv_, page_table, rel_bias, lam,
                                     subln_g[l], pages=DECODE_PAGES, out_scale=out_scale)

        def ssm_s(u):
            x0r = state_ssm_re[l].reshape(bs, N_SLABS, SLAB_STATE).transpose(1, 0, 2)
            x0i = state_ssm_im[l].reshape(bs, N_SLABS, SLAB_STATE).transpose(1, 0, 2)
            o, sr, si = _s5_sample(u, x0r, x0i, sp, ssm_d[l], w_glu[l])
            return o, sr.transpose(1, 0, 2), si.transpose(1, 0, 2)

        def up_s(hb):
            act, cg, cv = _up_sample(hb, w_up[l], conv_w[l], conv_b[l],
                                     state_conv[l].transpose(1, 0, 2))
            return act, jnp.concatenate([cg, cv], axis=-1).transpose(1, 0, 2)

        (hp, kt_p, v_p, re_p, im_p, c_p), (hs, k_s, v_s, re_s, im_s, c_s) = _layer(
            hp, hs, w, batch=bp, seq=seq, alpha=alpha, attend_p=attend_p, attend_s=attend_s,
            ssm_p=ssm_p, ssm_s=ssm_s, up_p=up_p, up_s=up_s)
        k_p = kt_p.reshape(bp, N_HEADS, 2, HEAD_DIM, seq).transpose(0, 4, 1, 2, 3)

        for name, val in (("kp", k_p.reshape(bp, seq, N_HEADS, 2, HEAD_DIM)),
                          ("vp", v_p.reshape(bp, seq, N_HEADS, V_DIM)),
                          ("rep", re_p.reshape(bp, n_groups, STATE_DIM)),
                          ("imp", im_p.reshape(bp, n_groups, STATE_DIM)),
                          ("cp", c_p),
                          ("ks", k_s.reshape(bs, 1, N_HEADS, 2, HEAD_DIM)),
                          ("vs", v_s.reshape(bs, 1, N_HEADS, V_DIM)),
                          ("res", re_s.reshape(bs, n_groups, STATE_DIM)),
                          ("ims", im_s.reshape(bs, n_groups, STATE_DIM)),
                          ("cs", c_s)):
            outs.setdefault(name, []).append(val)

    st = {k: jnp.stack(v, axis=0) for k, v in outs.items()}
    return (hp.reshape(bp, seq, d_model), hs.reshape(bs, 1, d_model), st["kp"], st["vp"], st["rep"],
            st["imp"], st["cp"], st["ks"], st["vs"], st["res"], st["ims"], st["cs"])
```

```python
import functools
import math

import jax
import jax.numpy as jnp
from jax import lax
from jax.experimental import pallas as pl
from jax.experimental.pallas import tpu as pltpu

F32 = jnp.float32
BF16 = jnp.bfloat16

N_HEADS = 8
HEAD_DIM = 64
V_DIM = 2 * HEAD_DIM
SSM_GROUP = 16
STATE_DIM = 64
CONV_W = 3
NUM_BUCKETS = 32
MAX_EXACT = NUM_BUCKETS // 2
MAX_DISTANCE = 128
LN_EPS = 1e-5
NEG_INF = -1e30

VMEM_LIMIT_BYTES = 56 * 1024 * 1024
LANES = 128
SLAB_GROUPS = 8
N_SLABS = 8
SLAB_IN = SLAB_GROUPS * SSM_GROUP
SLAB_STATE = SLAB_GROUPS * STATE_DIM
SUBLANES = 8
MM_ROWS, MM_COLS = 1024, 1024
LN_MM_ROWS = 512
DOWN_ROWS, DOWN_COLS = 512, 512
LN_ROWS = 512
ATTN_BLOCK = 256
ATTN_KEY_BLOCK = 256
ATTN_HEADS_PER_STEP = 4
ATTN_BLOCKS_PER_CALL = 2
S5_POSITIONS = 64
DECODE_PAGES = 16
UP_ROW_CHUNK = 512


def _params(sem):
    return pltpu.CompilerParams(dimension_semantics=sem, vmem_limit_bytes=VMEM_LIMIT_BYTES)


def _mm_body(*refs, n_pairs, n_extra, n_out, epilogue, emit_lhs):
    it = iter(refs)
    take = lambda n: [next(it) for _ in range(n)]
    x_refs, xs_refs, w_refs = take(n_pairs), take(n_pairs), take(n_pairs)
    extra, extra_s = take(n_extra), take(n_extra)
    outs, outs_s = take(n_out), take(n_out)
    lhs_out = take(1) if emit_lhs else []
    wbf = take(n_pairs)

    def apply(lhs_refs, extra_refs, out_refs, keep):
        lhs = [x[...].astype(BF16) for x in lhs_refs]
        for ref in keep:
            ref[...] = lhs[0]
        accs = [jnp.dot(x, s[...], preferred_element_type=F32) for x, s in zip(lhs, wbf)]
        for o, r in zip(out_refs, epilogue(*accs, *[e[...] for e in extra_refs])):
            o[...] = r.astype(o.dtype)

    @pl.when(pl.program_id(1) == 0)
    def _():
        for w, s in zip(w_refs, wbf):
            s[...] = w[...].astype(BF16)
        apply(xs_refs, extra_s, outs_s, [])

    apply(x_refs, extra, outs, lhs_out)


def _matmul(pairs, extras, out_dtypes, epilogue, *, n_cols, tm, tn, name, emit_lhs=False):
    m = pairs[0][0].shape[0]
    ms = pairs[0][1].shape[0]
    grid = (n_cols // tn, m // tm)
    assert not emit_lhs or grid[0] == 1
    w_mode = dict(pipeline_mode=pl.Buffered(1)) if grid[0] == 1 else {}
    specs_x, specs_xs, specs_w, scratch = [], [], [], []
    for x, xs, w, c0 in pairs:
        k = x.shape[1]
        specs_x.append(pl.BlockSpec((tm, k), lambda j, i: (i, 0)))
        specs_xs.append(pl.BlockSpec((ms, k), lambda j, i: (0, 0)))
        specs_w.append(pl.BlockSpec((k, tn), lambda j, i, off=c0 // tn: (0, j + off), **w_mode))
        scratch.append(pltpu.VMEM((k, tn), BF16))
    specs_e, specs_es, args_e, args_es = [], [], [], []
    for e in extras:
        if isinstance(e, tuple):
            specs_e.append(pl.BlockSpec((tm, tn), lambda j, i: (i, j)))
            specs_es.append(pl.BlockSpec((ms, tn), lambda j, i: (0, j)))
            args_e.append(e[0])
            args_es.append(e[1])
        else:
            specs_e.append(pl.BlockSpec((1, tn), lambda j, i: (0, j)))
            specs_es.append(pl.BlockSpec((1, tn), lambda j, i: (0, j)))
            args_e.append(e)
            args_es.append(e)
    out_shape = ([jax.ShapeDtypeStruct((m, n_cols), d) for d in out_dtypes]
                 + [jax.ShapeDtypeStruct((ms, n_cols), d) for d in out_dtypes])
    out_specs = ([pl.BlockSpec((tm, tn), lambda j, i: (i, j)) for _ in out_dtypes]
                 + [pl.BlockSpec((ms, tn), lambda j, i: (0, j)) for _ in out_dtypes])
    if emit_lhs:
        k0 = pairs[0][0].shape[1]
        out_shape.append(jax.ShapeDtypeStruct((m, k0), BF16))
        out_specs.append(pl.BlockSpec((tm, k0), lambda j, i: (i, 0)))
    body = functools.partial(_mm_body, n_pairs=len(pairs), n_extra=len(extras),
                             n_out=len(out_dtypes), epilogue=epilogue, emit_lhs=emit_lhs)
    outs = pl.pallas_call(
        body, grid=grid, in_specs=specs_x + specs_xs + specs_w + specs_e + specs_es,
        out_specs=out_specs, out_shape=out_shape, scratch_shapes=scratch,
        compiler_params=_params(("arbitrary", "arbitrary")), name=name)(
            *[p[0] for p in pairs], *[p[1] for p in pairs], *[p[2] for p in pairs],
            *args_e, *args_es)
    n = len(out_dtypes)
    if emit_lhs:
        return outs[:n], outs[n:2 * n], outs[2 * n]
    return outs[:n], outs[n:]


def _proj_t_body(x_ref, xs_ref, w_ref, o_ref, ob_ref, os_ref, wt_sc):
    @pl.when(pl.program_id(0) == 0)
    def _():
        for c in range(w_ref.shape[1] // LANES):
            cols = slice(c * LANES, (c + 1) * LANES)
            wt_sc[cols, :] = w_ref[:, cols].T.astype(BF16)
        os_ref[...] = lax.dot_general(xs_ref[...].astype(BF16), wt_sc[...], (((1,), (1,)), ((), ())),
                                      preferred_element_type=F32)

    kt = lax.dot_general(wt_sc[...], x_ref[...], (((1,), (1,)), ((), ())),
                         preferred_element_type=F32)
    o_ref[...] = kt
    ob_ref[...] = kt.astype(BF16)


def _proj_transposed(x, xs, w, c0, *, n_cols, batch, seq, tm, name):
    k = x.shape[1]
    ms = xs.shape[0]
    per_b = seq // tm
    out_spec = pl.BlockSpec((None, n_cols, tm), lambda i: (i // per_b, 0, i % per_b))
    return pl.pallas_call(
        _proj_t_body, grid=(batch * per_b,),
        in_specs=[pl.BlockSpec((tm, k), lambda i: (i, 0)),
                  pl.BlockSpec((ms, k), lambda i: (0, 0)),
                  pl.BlockSpec((k, n_cols), lambda i, off=c0 // n_cols: (0, off),
                               pipeline_mode=pl.Buffered(1))],
        out_specs=[out_spec, out_spec, pl.BlockSpec((ms, n_cols), lambda i: (0, 0))],
        out_shape=[jax.ShapeDtypeStruct((batch, n_cols, seq), F32),
                   jax.ShapeDtypeStruct((batch, n_cols, seq), BF16),
                   jax.ShapeDtypeStruct((ms, n_cols), F32)],
        scratch_shapes=[pltpu.VMEM((n_cols, k), BF16)],
        compiler_params=_params(("arbitrary",)), name=name)(x, xs, w)


def _ln(x, g, b):
    mu = jnp.mean(x, axis=-1, keepdims=True)
    xc = x - mu
    var = jnp.mean(xc * xc, axis=-1, keepdims=True)
    return xc * lax.rsqrt(var + LN_EPS) * g + b


def _ln_body(x_ref, g_ref, b_ref, o_ref):
    o_ref[...] = _ln(x_ref[...], g_ref[...], b_ref[...])


def _layer_norm(x, g, b, *, tm, name):
    m, d = x.shape
    tm = min(tm, m)
    return pl.pallas_call(
        _ln_body, grid=(m // tm,),
        in_specs=[pl.BlockSpec((tm, d), lambda i: (i, 0)),
                  pl.BlockSpec((1, d), lambda i: (0, 0)),
                  pl.BlockSpec((1, d), lambda i: (0, 0))],
        out_specs=pl.BlockSpec((tm, d), lambda i: (i, 0)),
        out_shape=jax.ShapeDtypeStruct((m, d), F32),
        compiler_params=_params(("arbitrary",)), name=name)(x, g.reshape(1, d), b.reshape(1, d))


def _rel_bucket(n):
    n = jnp.maximum(n, 0)
    nf = jnp.maximum(n, 1).astype(F32)
    large = MAX_EXACT + jnp.floor(jnp.log(nf / MAX_EXACT) / math.log(MAX_DISTANCE / MAX_EXACT)
                                  * (NUM_BUCKETS - MAX_EXACT)).astype(jnp.int32)
    large = jnp.minimum(large, NUM_BUCKETS - 1)
    return jnp.where(n < MAX_EXACT, n, large)


def _bucket_lookup(bucket, table_fn):
    out = jnp.zeros(jnp.broadcast_shapes(bucket.shape, table_fn(0).shape), F32)
    for b in range(NUM_BUCKETS):
        out = out + jnp.where(bucket == b, table_fn(b), 0.0)
    return out


def _bias_tile_body(rb_ref, bucket_ref, o_ref):
    h = pl.program_id(0)
    o_ref[...] = _bucket_lookup(bucket_ref[...], lambda bk: rb_ref[bk, h])


def _bias_tiles(rel_bias, t):
    r = jnp.arange(t, dtype=jnp.int32)
    c = jnp.arange(2 * t, dtype=jnp.int32)
    buckets = _rel_bucket(r[:, None] + t - c[None, :])
    return pl.pallas_call(
        _bias_tile_body, grid=(N_HEADS,),
        in_specs=[pl.BlockSpec(memory_space=pltpu.SMEM), pl.BlockSpec((t, 2 * t), lambda h: (0, 0))],
        out_specs=pl.BlockSpec((None, t, 2 * t), lambda h: (h, 0, 0)),
        out_shape=jax.ShapeDtypeStruct((N_HEADS, t, 2 * t), F32),
        compiler_params=_params(("arbitrary",)), name="bias_tiles")(rel_bias, buckets)


def _attn_body(lam_ref, rb_ref, q_ref, k_ref, v_ref, bias_ref, g_ref, o_ref, *scratch,
               t, first, count, heads, out_scale):
    for sub in range(count):
        i = first + sub
        _attn_block(lam_ref, rb_ref, q_ref, k_ref, v_ref, bias_ref, g_ref, o_ref, scratch, t=t,
                    n_far=max(i - 1, 0), near=min(i + 1, 2) * t, heads=heads, out_scale=out_scale,
                    row0=sub * t)


def _attn_block(lam_ref, rb_ref, q_ref, k_ref, v_ref, bias_ref, g_ref, o_ref, scratch,
                *, t, n_far, near, heads, out_scale, row0):
    rows_q = slice(row0, row0 + t)
    s_bufs, p_bufs, m_bufs, mf_bufs, l_bufs, a_bufs = (scratch[0:2], scratch[2:4], scratch[4:6],
                                                       scratch[6:8], scratch[8:10], scratch[10:12])
    hg = pl.program_id(0)
    far = n_far * t
    kb_w = ATTN_KEY_BLOCK
    n_kb = (far + near) // kb_w
    n_tiles = kb_w // LANES

    def fold(x, op):
        out = x[:, 0:LANES]
        for c in range(1, n_tiles):
            out = op(out, x[:, c * LANES:(c + 1) * LANES])
        return out

    def head_cols(hh):
        return slice(hh * V_DIM, (hh + 1) * V_DIM)

    def score_block(hh, kb):
        par = hh % 2
        q = q_ref[rows_q, head_cols(hh)] * (HEAD_DIM ** -0.5)
        lane = lax.broadcasted_iota(jnp.int32, q.shape, 1)
        zero = jnp.zeros_like(q)
        q2 = jnp.concatenate([jnp.where(lane < HEAD_DIM, q, zero), jnp.where(lane >= HEAD_DIM, q, zero)],
                             axis=0)
        cols = slice(kb * kb_w, (kb + 1) * kb_w)
        s = jnp.dot(q2, k_ref[head_cols(hh), cols], preferred_element_type=F32)
        bias_far = rb_ref[NUM_BUCKETS - 1, hg * heads + hh]
        if kb * kb_w < far:
            top = fold(s, jnp.maximum) + bias_far
        else:
            off = kb * kb_w - far
            lo = 2 * t - near + off
            bias = bias_ref[hh, :, lo:lo + kb_w]
            row = lax.broadcasted_iota(jnp.int32, (t, kb_w), 0)
            col = lax.broadcasted_iota(jnp.int32, (t, kb_w), 1)
            keep = col + (off - (near - t)) <= row
            s = jnp.where(jnp.concatenate([keep, keep], axis=0),
                          s + jnp.concatenate([bias, bias], axis=0), NEG_INF)
            top = fold(s, jnp.maximum)
        s_bufs[par][:, cols] = s
        if kb == 0:
            m_bufs[par][...] = top
        else:
            m_bufs[par][...] = jnp.maximum(m_bufs[par][...], top)
        if kb == n_kb - 1:
            m = jnp.broadcast_to(jnp.max(m_bufs[par][...], axis=1, keepdims=True), m_bufs[par].shape)
            m_bufs[par][...] = m
            mf_bufs[par][...] = m - bias_far

    def exp_block(hh, kb):
        par = hh % 2
        cols = slice(kb * kb_w, (kb + 1) * kb_w)
        m = (mf_bufs if kb * kb_w < far else m_bufs)[par][...]
        p = jnp.exp(s_bufs[par][:, cols] - jnp.concatenate([m] * n_tiles, axis=1))
        if kb == 0:
            l_bufs[par][...] = fold(p, jnp.add)
        else:
            l_bufs[par][...] += fold(p, jnp.add)
        p_bufs[par][:, cols] = p.astype(BF16)

    def value_block(hh, kb):
        par = hh % 2
        rows = slice(kb * kb_w, (kb + 1) * kb_w)
        pv = jnp.dot(p_bufs[par][:, rows], v_ref[rows, head_cols(hh)], preferred_element_type=F32)
        if kb == 0:
            a_bufs[par][...] = pv
        else:
            a_bufs[par][...] += pv
        if kb == n_kb - 1:
            nrm = a_bufs[par][...] / jnp.sum(l_bufs[par][...], axis=1, keepdims=True)
            o = nrm[0:t] - lam_ref[0] * nrm[t:2 * t]
            ms = jnp.mean(o * o, axis=-1, keepdims=True)
            o_ref[rows_q, head_cols(hh)] = (o * lax.rsqrt(ms + LN_EPS) * g_ref[...] * out_scale
                                            ).astype(o_ref.dtype)

    for stage in range(heads + 2):
        for kb in range(n_kb):
            if stage < heads:
                score_block(stage, kb)
            if 0 <= stage - 1 < heads:
                exp_block(stage - 1, kb)
            if 0 <= stage - 2 < heads:
                value_block(stage - 2, kb)


def _prompt_attention(q, k, v, rel_bias, lam, subln_g, *, batch, seq, t, out_scale):
    assert t >= MAX_DISTANCE
    nq = seq // t
    width = N_HEADS * V_DIM
    q3 = q.reshape(batch, seq, width)
    v3 = v.reshape(batch, seq, width)
    bias = _bias_tiles(rel_bias, t)
    smem = pl.BlockSpec(memory_space=pltpu.SMEM)
    qo = q3
    group = ATTN_BLOCKS_PER_CALL
    hp = ATTN_HEADS_PER_STEP
    for first in range(0, nq, group):
        keys = (first + group) * t
        body = functools.partial(_attn_body, t=t, first=first, count=group, heads=hp,
                                 out_scale=out_scale)
        pair = lambda shape, dtype: [pltpu.VMEM(shape, dtype)] * 2
        rows_spec = pl.BlockSpec((None, group * t, hp * V_DIM),
                                 lambda h, b, p=first // group: (b, p, h))
        qo = pl.pallas_call(
            body, grid=(N_HEADS // hp, batch),
            in_specs=[smem, smem, rows_spec,
                      pl.BlockSpec((None, hp * V_DIM, keys), lambda h, b: (b, h, 0)),
                      pl.BlockSpec((None, keys, hp * V_DIM), lambda h, b: (b, 0, h)),
                      pl.BlockSpec((hp, t, 2 * t), lambda h, b: (h, 0, 0)),
                      pl.BlockSpec((1, V_DIM), lambda h, b: (0, 0))],
            out_specs=rows_spec,
            out_shape=jax.ShapeDtypeStruct((batch, seq, width), BF16),
            input_output_aliases={2: 0},
            scratch_shapes=(pair((2 * t, keys), F32) + pair((2 * t, keys), BF16)
                            + pair((2 * t, LANES), F32) + pair((2 * t, LANES), F32)
                            + pair((2 * t, LANES), F32) + pair((2 * t, V_DIM), F32)),
            compiler_params=_params(("arbitrary", "arbitrary")),
            name=f"prompt_attention_q{first}")(lam, rel_bias, qo, k, v3, bias, subln_g.reshape(1, V_DIM))
    return qo.reshape(batch * seq, width)


def _decode_body(pt_ref, lam_ref, q_ref, kn_ref, vn_ref, rbt_ref, bucket_ref, g_ref, *rest,
                 pages, page, n_steps, out_scale):
    k_refs = rest[:pages]
    v_refs = rest[pages:2 * pages]
    o_ref = rest[2 * pages]
    qexp_sc, bias_sc, m_sc, l_sc, acc_sc = rest[2 * pages + 1:]
    step = pl.program_id(1)
    rows = 2 * N_HEADS
    width = N_HEADS * V_DIM

    @pl.when(step == 0)
    def _():
        row = lax.broadcasted_iota(jnp.int32, (rows, width), 0)
        col = lax.broadcasted_iota(jnp.int32, (rows, width), 1)
        own_qk = (col // HEAD_DIM) == (row % N_HEADS) * 2 + row // N_HEADS
        q = (q_ref[...] * (HEAD_DIM ** -0.5)).astype(BF16).astype(F32)
        qexp = jnp.where(own_qk, jnp.broadcast_to(q, (rows, width)), 0.0)
        qexp_sc[...] = qexp.astype(BF16)
        bias_sc[...] = _bucket_lookup(bucket_ref[...], lambda bk: rbt_ref[:, bk:bk + 1])
        kn = kn_ref[...].astype(BF16).astype(F32)
        s_self = jnp.sum(qexp * kn, axis=1, keepdims=True) + rbt_ref[:, 0:1]
        m_sc[...] = s_self
        l_sc[...] = jnp.ones(l_sc.shape, F32)
        vn = vn_ref[...].astype(BF16).astype(F32)
        acc_sc[...] = jnp.broadcast_to(vn[:, None, :], acc_sc.shape)

    qexp = qexp_sc[...]
    bias_far = rbt_ref[:, NUM_BUCKETS - 1:NUM_BUCKETS]
    is_last = step == n_steps - 1
    s_parts = []
    for p in range(pages):
        s = jnp.dot(qexp, k_refs[p][...].astype(BF16), preferred_element_type=F32)
        if p == pages - 1:
            s = s + jnp.where(is_last, bias_sc[...], bias_far)
        else:
            s = s + bias_far
        s_parts.append(s)
    m_old = m_sc[...]
    m_new = m_old
    for s in s_parts:
        m_new = jnp.maximum(m_new, jnp.max(s, axis=1, keepdims=True))
    a = jnp.exp(m_old - m_new)
    l_new = a * l_sc[...]
    probs = []
    for p in range(pages):
        pr = jnp.exp(s_parts[p] - m_new)
        l_new = l_new + jnp.sum(pr, axis=1, keepdims=True)
        probs.append(pr.astype(BF16))
    probs = jnp.concatenate(probs, axis=1)
    m_sc[...] = m_new
    l_sc[...] = l_new
    for h in range(N_HEADS):
        vh = jnp.concatenate([v_refs[p][pl.ds(h, page, stride=N_HEADS), :].astype(BF16)
                              for p in range(pages)], axis=0)
        acc_sc[h] = a * acc_sc[h] + jnp.dot(probs, vh, preferred_element_type=F32)

    @pl.when(is_last)
    def _():
        nrm = acc_sc[...] / l_new[None]
        r = lax.broadcasted_iota(jnp.int32, nrm.shape, 1)
        hh = lax.broadcasted_iota(jnp.int32, nrm.shape, 0)
        coef = jnp.where(r == hh, 1.0, jnp.where(r == hh + N_HEADS, -lam_ref[0], 0.0))
        d = jnp.sum(coef * nrm, axis=1)
        ms = jnp.mean(d * d, axis=-1, keepdims=True)
        o_ref[...] = (d * lax.rsqrt(ms + LN_EPS) * g_ref[...] * out_scale).astype(o_ref.dtype)


def _sample_attention(q, k_new, v_new, cache_k, cache_v, page_table, rel_bias, lam, subln_g,
                      *, pages, out_scale):
    bs, width = q.shape
    page = cache_k.shape[2]
    n_pages = page_table.shape[1]
    n_steps = n_pages // pages
    past = n_pages * page
    kpos = past - page + jnp.arange(page, dtype=jnp.int32)
    bucket_last = _rel_bucket(past - kpos).reshape(1, page)
    rbt = jnp.tile(rel_bias.T, (2, 1))

    def tok_spec():
        return pl.BlockSpec((None, 1, width), lambda b, s, pt: (b, 0, 0))

    def page_spec(shape, p):
        return pl.BlockSpec((None,) + shape, lambda b, s, pt, p=p: (pt[b, s * pages + p], 0, 0))

    full = lambda shape: pl.BlockSpec(shape, lambda b, s, pt: tuple(0 for _ in shape))
    head_spec = pl.BlockSpec((None, N_HEADS, V_DIM), lambda b, s, pt: (b, 0, 0))
    in_specs = ([pl.BlockSpec(memory_space=pltpu.SMEM), tok_spec(), tok_spec(), head_spec,
                 full((2 * N_HEADS, NUM_BUCKETS)), full((1, page)), full((1, V_DIM))]
                + [page_spec((width, page), p) for p in range(pages)]
                + [page_spec((page * N_HEADS, V_DIM), p) for p in range(pages)])
    rows = 2 * N_HEADS
    body = functools.partial(_decode_body, pages=pages, page=page, n_steps=n_steps,
                             out_scale=out_scale)
    out = pl.pallas_call(
        body,
        grid_spec=pltpu.PrefetchScalarGridSpec(
            num_scalar_prefetch=1, grid=(bs, n_steps), in_specs=in_specs,
            out_specs=head_spec,
            scratch_shapes=[pltpu.VMEM((rows, width), BF16), pltpu.VMEM((rows, page), F32),
                            pltpu.VMEM((rows, 1), F32), pltpu.VMEM((rows, 1), F32),
                            pltpu.VMEM((N_HEADS, rows, V_DIM), F32)]),
        out_shape=jax.ShapeDtypeStruct((bs, N_HEADS, V_DIM), BF16),
        compiler_params=_params(("arbitrary", "arbitrary")),
        name="sample_attention")(
            page_table, lam, q.reshape(bs, 1, width), k_new.reshape(bs, 1, width),
            v_new.reshape(bs, N_HEADS, V_DIM), rbt, bucket_last, subln_g.reshape(1, V_DIM),
            *([cache_k] * pages), *([cache_v] * pages))
    return out.reshape(bs, width)


def _s5_params(a_re, a_im, log_dt, b_re, b_im, c_re, c_im):
    dt = jnp.exp(log_dt)[:, None]
    mag = jnp.exp(a_re * dt)
    ang = a_im * dt
    abar_re = mag * jnp.cos(ang)
    abar_im = mag * jnp.sin(ang)
    den = a_re * a_re + a_im * a_im
    f_re = ((abar_re - 1.0) * a_re + abar_im * a_im) / den
    f_im = (abar_im * a_re - (abar_re - 1.0) * a_im) / den
    bb_re = f_re[..., None] * b_re - f_im[..., None] * b_im
    bb_im = f_re[..., None] * b_im + f_im[..., None] * b_re
    eye = jnp.eye(SLAB_GROUPS, dtype=F32)

    def in_slabs(bb):
        tt = bb.reshape(N_SLABS, SLAB_GROUPS, STATE_DIM, SSM_GROUP).transpose(0, 1, 3, 2)
        full = tt[:, :, :, None, :] * eye[None, :, None, :, None]
        return full.reshape(N_SLABS, SLAB_IN, SLAB_STATE).astype(BF16)

    ab_re = abar_re[..., None] * bb_re - abar_im[..., None] * bb_im
    ab_im = abar_re[..., None] * bb_im + abar_im[..., None] * bb_re

    def out_slabs(cc):
        tt = cc.reshape(N_SLABS, SLAB_GROUPS, SSM_GROUP, STATE_DIM).transpose(0, 1, 3, 2)
        full = tt[:, :, :, None, :] * eye[None, :, None, :, None]
        return full.reshape(N_SLABS, SLAB_STATE, SLAB_IN).astype(BF16)

    return (abar_re.reshape(N_SLABS, SLAB_STATE), abar_im.reshape(N_SLABS, SLAB_STATE),
            in_slabs(bb_re), in_slabs(bb_im), out_slabs(c_re), out_slabs(c_im),
            in_slabs(ab_re), in_slabs(ab_im))


def _glu_out(y, u, d_ref, wglu_ref):
    g = jax.nn.gelu(y + d_ref[...] * u)
    gate = jnp.dot(g.astype(BF16), wglu_ref[...].astype(BF16), preferred_element_type=F32)
    return g * jax.nn.sigmoid(gate)


def _s5_prompt_body(u_ref, ar_ref, ai_ref, bbr_ref, bbi_ref, cr_ref, ci_ref, d_ref, wglu_ref,
                    o_ref, sre_ref, sim_ref, xr_sc, xi_sc, pw_sc, cr_sc, ci_sc, perm_sc, *, batch):
    c = pl.program_id(0)
    sub = SUBLANES
    _, tc, width = u_ref.shape
    n_rows = batch * tc
    t = lax.broadcasted_iota(jnp.int32, (sub, SLAB_STATE), 0)
    first = t < batch

    def cmul(pr, pi, qr, qi):
        return pr * qr - pi * qi, pr * qi + pi * qr

    @pl.when(c == 0)
    def _():
        for j in range(N_SLABS):
            a = (jnp.broadcast_to(ar_ref[j:j + 1, :], t.shape), jnp.broadcast_to(ai_ref[j:j + 1, :], t.shape))
            a2 = cmul(*a, *a)
            for part in range(2):
                pw_sc[j, part] = jnp.where(first, a[part], a2[part])
        cr_sc[...] = jnp.zeros(cr_sc.shape, F32)
        ci_sc[...] = jnp.zeros(ci_sc.shape, F32)
        r = lax.broadcasted_iota(jnp.int32, (n_rows, n_rows), 0)
        k = lax.broadcasted_iota(jnp.int32, (n_rows, n_rows), 1)
        perm_sc[0] = jnp.where(k == (r % batch) * tc + r // batch, 1.0, 0.0).astype(BF16)
        perm_sc[1] = jnp.where(r == (k % batch) * tc + k // batch, 1.0, 0.0).astype(BF16)

    def permute(which, x):
        return jnp.dot(perm_sc[which], x, preferred_element_type=F32)

    u_seq = u_ref[...].reshape(n_rows, width)
    hi = u_seq.astype(BF16)
    rest = u_seq - hi.astype(F32)
    mid = rest.astype(BF16)
    lo = (rest - mid.astype(F32)).astype(BF16)
    u_hi = permute(0, hi)
    ub = u_hi.astype(BF16)
    u = u_hi + permute(0, mid) + permute(0, lo)
    row = lax.broadcasted_iota(jnp.int32, u_hi.shape, 0)
    u_before = jnp.where(row % sub >= batch, pltpu.roll(u_hi, batch, 0), 0.0).astype(BF16)
    for j in range(N_SLABS):
        cols = slice(j * SLAB_IN, (j + 1) * SLAB_IN)
        uj = jnp.concatenate([ub[:, cols], u_before[:, cols]], axis=1)
        xr_sc[j] = jnp.dot(uj, bbr_ref[j], preferred_element_type=F32)
        xi_sc[j] = jnp.dot(uj, bbi_ref[j], preferred_element_type=F32)

    for j in range(N_SLABS):
        pr, pi = cr_sc[j], ci_sc[j]
        for v in range(n_rows // sub):
            rows = slice(v * sub, (v + 1) * sub)
            lr = jnp.where(first, pltpu.roll(pr, batch, 0), pr)
            li = jnp.where(first, pltpu.roll(pi, batch, 0), pi)
            er, ei = cmul(pw_sc[j, 0], pw_sc[j, 1], lr, li)
            pr, pi = xr_sc[j, rows, :] + er, xi_sc[j, rows, :] + ei
            xr_sc[j, rows, :] = pr
            xi_sc[j, rows, :] = pi
        cr_sc[j] = pr
        ci_sc[j] = pi
    sre_ref[...] = cr_sc[...]
    sim_ref[...] = ci_sc[...]

    ys = []
    for j in range(N_SLABS):
        ys.append(jnp.dot(xr_sc[j].astype(BF16), cr_ref[j], preferred_element_type=F32)
                  - jnp.dot(xi_sc[j].astype(BF16), ci_ref[j], preferred_element_type=F32))
    out = _glu_out(jnp.concatenate(ys, axis=1), u, d_ref, wglu_ref).astype(BF16)
    o_ref[...] = permute(1, out).astype(o_ref.dtype).reshape(batch, tc, width)


def _s5_prompt(u, sp, d, w_glu, *, batch, seq, tc):
    assert SUBLANES == 2 * batch, "tile = two positions of every sequence"
    ar, ai, bbr, bbi, cr, ci, abr, abi = sp
    bbr = jnp.concatenate([bbr, abr], axis=1)
    bbi = jnp.concatenate([bbi, abi], axis=1)
    width = u.shape[1]
    rows = batch * tc
    const = lambda shape: pl.BlockSpec(shape, lambda c: tuple(0 for _ in shape))
    block = pl.BlockSpec((batch, tc, width), lambda c: (0, c, 0))
    state_spec = const((N_SLABS, SUBLANES, SLAB_STATE))
    state_shape = jax.ShapeDtypeStruct((N_SLABS, SUBLANES, SLAB_STATE), F32)
    out, s_re, s_im = pl.pallas_call(
        functools.partial(_s5_prompt_body, batch=batch), grid=(seq // tc,),
        in_specs=[block, const(ar.shape), const(ai.shape), const(bbr.shape), const(bbi.shape),
                  const(cr.shape), const(ci.shape), const((1, width)), const(w_glu.shape)],
        out_specs=[block, state_spec, state_spec],
        out_shape=[jax.ShapeDtypeStruct((batch, seq, width), BF16), state_shape, state_shape],
        scratch_shapes=[pltpu.VMEM((N_SLABS, rows, SLAB_STATE), F32),
                        pltpu.VMEM((N_SLABS, rows, SLAB_STATE), F32),
                        pltpu.VMEM((N_SLABS, 2, SUBLANES, SLAB_STATE), F32),
                        pltpu.VMEM((N_SLABS, SUBLANES, SLAB_STATE), F32),
                        pltpu.VMEM((N_SLABS, SUBLANES, SLAB_STATE), F32),
                        pltpu.VMEM((2, rows, rows), BF16)],
        compiler_params=_params(("arbitrary",)),
        name="s5_prompt")(u.reshape(batch, seq, width), ar, ai, bbr, bbi, cr, ci,
                          d.reshape(1, width), w_glu)
    return out.reshape(batch * seq, width), s_re, s_im


def _s5_sample_body(u_ref, x0r_ref, x0i_ref, ar_ref, ai_ref, bbr_ref, bbi_ref, cr_ref, ci_ref,
                    d_ref, wglu_ref, o_ref, sre_ref, sim_ref):
    u = u_ref[...]
    ub = u.astype(BF16)
    ys = []
    for j in range(N_SLABS):
        uj = ub[:, j * SLAB_IN:(j + 1) * SLAB_IN]
        ar = ar_ref[j:j + 1, :]
        ai = ai_ref[j:j + 1, :]
        x0r = x0r_ref[j]
        x0i = x0i_ref[j]
        xr = ar * x0r - ai * x0i + jnp.dot(uj, bbr_ref[j], preferred_element_type=F32)
        xi = ar * x0i + ai * x0r + jnp.dot(uj, bbi_ref[j], preferred_element_type=F32)
        sre_ref[j] = xr
        sim_ref[j] = xi
        ys.append(jnp.dot(xr.astype(BF16), cr_ref[j], preferred_element_type=F32)
                  - jnp.dot(xi.astype(BF16), ci_ref[j], preferred_element_type=F32))
    o_ref[...] = _glu_out(jnp.concatenate(ys, axis=1), u, d_ref, wglu_ref).astype(o_ref.dtype)


def _s5_sample(u, x0_re, x0_im, sp, d, w_glu):
    ar, ai, bbr, bbi, cr, ci = sp[:6]
    bs, width = u.shape
    state_shape = jax.ShapeDtypeStruct((N_SLABS, bs, SLAB_STATE), F32)
    return pl.pallas_call(
        _s5_sample_body,
        out_shape=[jax.ShapeDtypeStruct((bs, width), BF16), state_shape, state_shape],
        compiler_params=pltpu.CompilerParams(vmem_limit_bytes=VMEM_LIMIT_BYTES),
        name="s5_sample")(u, x0_re, x0_im, ar, ai, bbr, bbi, cr, ci, d.reshape(1, width), w_glu)


def _gated(c):
    gate = c[:, :LANES]
    return gate * jax.nn.sigmoid(gate) * c[:, LANES:]


def _up_prompt_body(h_ref, wg_ref, wv_ref, cwg_ref, cwv_ref, cbg_ref, cbv_ref,
                    act_ref, cg_ref, cv_ref, w_sc, *, rows):
    w_sc[:, 0:LANES] = wg_ref[...].astype(BF16)
    w_sc[:, LANES:2 * LANES] = wv_ref[...].astype(BF16)
    cw = jnp.concatenate([cwg_ref[...], cwv_ref[...]], axis=1)
    cb = jnp.concatenate([cbg_ref[...], cbv_ref[...]], axis=1)
    seq = h_ref.shape[0]
    sub = SUBLANES
    row8 = lax.broadcasted_iota(jnp.int32, (sub, 2 * LANES), 0)

    prev = jnp.zeros((sub, 2 * LANES), F32)
    for c in range(seq // rows):
        up = jnp.dot(h_ref[c * rows:(c + 1) * rows, :], w_sc[...], preferred_element_type=F32)
        cur = cb + cw[CONV_W - 1:CONV_W] * up
        for back in range(1, CONV_W):
            rolled = pltpu.roll(up, back, 0)
            head = jnp.where(row8 < back, pltpu.roll(prev, back, 0), rolled[0:sub])
            shifted = jnp.concatenate([head, rolled[sub:]], axis=0)
            cur = cur + cw[CONV_W - 1 - back:CONV_W - back] * shifted
        act_ref[c * rows:(c + 1) * rows, :] = _gated(cur).astype(act_ref.dtype)
        prev = up[rows - sub:, :]
    tail = prev[sub - (CONV_W - 1):, :]
    cg_ref[...] = tail[:, :LANES]
    cv_ref[...] = tail[:, LANES:]


def _up_prompt(hb, w_up, conv_w, conv_b, *, batch, seq):
    d_model = hb.shape[1]
    d_ff = w_up.shape[1] // 2
    nb = d_ff // LANES
    cb = conv_b.reshape(1, 2 * d_ff)
    col = lambda off: (lambda b, j: (0, j + off))
    tail_spec = pl.BlockSpec((None, CONV_W - 1, LANES), lambda b, j: (b, 0, j))
    tail_shape = jax.ShapeDtypeStruct((batch, CONV_W - 1, d_ff), F32)
    return pl.pallas_call(
        functools.partial(_up_prompt_body, rows=UP_ROW_CHUNK), grid=(batch, nb),
        scratch_shapes=[pltpu.VMEM((d_model, 2 * LANES), BF16)],
        in_specs=[pl.BlockSpec((seq, d_model), lambda b, j: (b, 0)),
                  pl.BlockSpec((d_model, LANES), col(0)), pl.BlockSpec((d_model, LANES), col(nb)),
                  pl.BlockSpec((CONV_W, LANES), col(0)), pl.BlockSpec((CONV_W, LANES), col(nb)),
                  pl.BlockSpec((1, LANES), col(0)), pl.BlockSpec((1, LANES), col(nb))],
        out_specs=[pl.BlockSpec((seq, LANES), lambda b, j: (b, j)), tail_spec, tail_spec],
        out_shape=[jax.ShapeDtypeStruct((batch * seq, d_ff), BF16), tail_shape, tail_shape],
        compiler_params=_params(("arbitrary", "arbitrary")),
        name="up_prompt")(hb, w_up, w_up, conv_w, conv_w, cb, cb)


def _up_sample_body(h_ref, wg_ref, wva_ref, wvb_ref, cwg_ref, cwva_ref, cwvb_ref, cbg_ref, cbva_ref,
                    cbvb_ref, sg_ref, sva_ref, svb_ref, act_ref, cg_ref, cv_ref):
    w2 = 2 * LANES
    w = jnp.concatenate([wg_ref[...].astype(BF16), wva_ref[...].astype(BF16),
                         wvb_ref[...].astype(BF16)], axis=1)
    up = jnp.dot(h_ref[...], w, preferred_element_type=F32)
    cw = jnp.concatenate([cwg_ref[...], cwva_ref[...], cwvb_ref[...]], axis=1)
    cb = jnp.concatenate([cbg_ref[...], cbva_ref[...], cbvb_ref[...]], axis=1)
    c = cb + cw[CONV_W - 1:CONV_W] * up
    for tap in range(CONV_W - 1):
        st = jnp.concatenate([sg_ref[tap], sva_ref[tap], svb_ref[tap]], axis=1)
        c = c + cw[tap:tap + 1] * st
    gate = c[:, 0:w2]
    act_ref[...] = (gate * jax.nn.sigmoid(gate) * c[:, w2:2 * w2]).astype(act_ref.dtype)
    for tap in range(1, CONV_W - 1):
        cg_ref[tap - 1] = sg_ref[tap]
        cv_ref[tap - 1] = jnp.concatenate([sva_ref[tap], svb_ref[tap]], axis=1)
    cg_ref[CONV_W - 2] = up[:, 0:w2]
    cv_ref[CONV_W - 2] = up[:, w2:2 * w2]


def _up_sample(hb, w_up, conv_w, conv_b, state):
    bs, d_model = hb.shape
    d_ff = w_up.shape[1] // 2
    nb = d_ff // LANES
    n_pairs = (nb + 1) // 2
    last = 2 * nb - 1
    w2 = 2 * LANES
    cb = conv_b.reshape(1, 2 * d_ff)
    gate = lambda shape: pl.BlockSpec(shape, lambda j: (0,) * (len(shape) - 1) + (j,))
    val = lambda shape, k: pl.BlockSpec(
        shape, lambda j: (0,) * (len(shape) - 1) + (jnp.minimum(nb + 2 * j + k, last),))
    operands = []
    specs = [pl.BlockSpec((bs, d_model), lambda j: (0, 0))]
    for arr, lead in ((w_up, (d_model,)), (conv_w, (CONV_W,)), (cb, (1,)), (state, (CONV_W - 1, bs))):
        specs += [gate(lead + (w2,)), val(lead + (LANES,), 0), val(lead + (LANES,), 1)]
        operands += [arr, arr, arr]
    tail_spec = pl.BlockSpec((CONV_W - 1, bs, w2), lambda j: (0, 0, j))
    tail_shape = jax.ShapeDtypeStruct((CONV_W - 1, bs, d_ff), F32)
    return pl.pallas_call(
        _up_sample_body, grid=(n_pairs,), in_specs=specs,
        out_specs=[pl.BlockSpec((bs, w2), lambda j: (0, j)), tail_spec, tail_spec],
        out_shape=[jax.ShapeDtypeStruct((bs, d_ff), BF16), tail_shape, tail_shape],
        compiler_params=_params(("arbitrary",)),
        name="up_sample")(hb, *operands)


def _one(x):
    return (x,)


def _sigmoid_out(acc):
    return (jax.nn.sigmoid(acc),)


def _both(acc):
    return (acc, acc)


def _merge(pa, ps, ga, gs):
    return (ga * pa + gs * ps,)


def _layer(xp, xs, w, *, batch, seq, alpha, attend_p, attend_s, ssm_p, ssm_s, up_p, up_s):
    d_model = xp.shape[1]
    qk_w = N_HEADS * 2 * HEAD_DIM
    v_w = N_HEADS * V_DIM
    ssm_w = d_model // 2
    w_in = w["w_in"]
    g1, b1 = w["ln1_g"].reshape(1, d_model), w["ln1_b"].reshape(1, d_model)
    big = dict(tm=MM_ROWS, tn=MM_COLS)
    c = 0
    (q,), (q_s,), xb = _matmul([(xp, xs, w_in, c)], [], [BF16], _one, n_cols=qk_w, name="proj_q",
                               emit_lhs=True, **big)
    c += qk_w
    kt, ktb, k_s = _proj_transposed(xb, xs, w_in, c, n_cols=qk_w, batch=batch, seq=seq, tm=MM_ROWS,
                                    name="proj_kt")
    c += qk_w
    c_v = c
    c += v_w
    (u,), (u_s,) = _matmul([(xb, xs, w_in, c)], [], [F32], _one, n_cols=ssm_w, name="proj_u", **big)
    c += ssm_w
    (ga,), (ga_s,) = _matmul([(xb, xs, w_in, c)], [], [F32], _sigmoid_out, n_cols=d_model, name="gate_a", **big)
    c += d_model
    (gs,), (gs_s,) = _matmul([(xb, xs, w_in, c)], [], [F32], _sigmoid_out, n_cols=d_model, name="gate_s", **big)
    (v, vb), (v_s, _) = _matmul([(xb, xs, w_in, c_v)], [], [F32, BF16], _both, n_cols=v_w, name="proj_v", **big)

    attn = attend_p(q, ktb, vb)
    ssm_out, re_p, im_p = ssm_p(u)
    ssm_out_s, re_s, im_s = ssm_s(u_s)
    attn_s = attend_s(q_s, k_s, v_s)

    (merged,), (merged_s,) = _matmul(
        [(attn, attn_s, w["w_proj_attn"], 0), (ssm_out, ssm_out_s, w["w_proj_ssm"], 0)],
        [(ga, ga_s), (gs, gs_s)], [BF16], _merge, n_cols=d_model, name="merge", **big)

    def post_ln1(acc, res, g, b):
        h = _ln(alpha * res + acc, g, b)
        return h, h

    (h, hb), (h_s, hb_s) = _matmul([(merged, merged_s, w["w_out"], 0)], [(xp, xs), g1, b1], [F32, BF16],
                                   post_ln1, n_cols=d_model, tm=LN_MM_ROWS, tn=d_model, name="out_proj_ln1")
    act, conv_p = up_p(hb)
    act_s, conv_s = up_s(hb_s)
    (r2,), (r2_s,) = _matmul([(act, act_s, w["w_down"], 0)], [(h, h_s)], [F32],
                             lambda acc, res: (alpha * res + acc,), n_cols=d_model,
                             tm=DOWN_ROWS, tn=DOWN_COLS, name="down_proj")
    y = _layer_norm(r2, w["ln2_g"], w["ln2_b"], tm=LN_ROWS, name="ln2")
    y_s = _layer_norm(r2_s, w["ln2_g"], w["ln2_b"], tm=LN_ROWS, name="ln2_sample")
    return (y, kt, v, re_p, im_p, conv_p), (y_s, k_s, v_s, re_s, im_s, conv_s)


def kernel(x_prompt, x_sample, cache_k, cache_v, state_ssm_re, state_ssm_im, state_conv, page_table, rel_bias, w_in, lambda_q1, lambda_k1, lambda_q2, lambda_k2, subln_g, ssm_a_re, ssm_a_im, ssm_log_dt, ssm_b_re, ssm_b_im, ssm_c_re, ssm_c_im, ssm_d, w_glu, w_proj_attn, w_proj_ssm, w_out, ln1_g, ln1_b, w_up, conv_w, conv_b, w_down, ln2_g, ln2_b):
    depth = w_in.shape[0]
    assert depth == 1, "single-layer trunk"
    bp, seq, d_model = x_prompt.shape
    bs, dec_seq, _ = x_sample.shape
    assert dec_seq == 1
    n_pool, page = cache_k.shape[1], cache_k.shape[2]
    d_ff = w_down.shape[1]
    n_groups = ssm_a_re.shape[1]
    assert n_groups == N_SLABS * SLAB_GROUPS and d_ff % LANES == 0
    alpha = (2.0 * depth) ** 0.25
    width = N_HEADS * V_DIM

    hp = x_prompt.reshape(bp * seq, d_model)
    hs = x_sample.reshape(bs, d_model)
    outs = {}
    for l in range(depth):
        lam_init = 0.8 - 0.6 * math.exp(-0.3 * l)
        out_scale = 1.0 - lam_init
        lam = (jnp.exp(jnp.sum(lambda_q1[l] * lambda_k1[l]))
               - jnp.exp(jnp.sum(lambda_q2[l] * lambda_k2[l])) + lam_init).reshape(1)
        w = dict(w_in=w_in[l], w_proj_attn=w_proj_attn[l], w_proj_ssm=w_proj_ssm[l], w_out=w_out[l],
                 ln1_g=ln1_g[l], ln1_b=ln1_b[l], w_down=w_down[l], ln2_g=ln2_g[l], ln2_b=ln2_b[l])
        sp = _s5_params(ssm_a_re[l], ssm_a_im[l], ssm_log_dt[l], ssm_b_re[l], ssm_b_im[l],
                        ssm_c_re[l], ssm_c_im[l])

        def attend_p(q, kt, v):
            return _prompt_attention(q, kt, v, rel_bias, lam, subln_g[l], batch=bp, seq=seq,
                                     t=ATTN_BLOCK, out_scale=out_scale)

        def ssm_p(u):
            o, sr, si = _s5_prompt(u, sp, ssm_d[l], w_glu[l], batch=bp, seq=seq, tc=S5_POSITIONS)
            last = lambda st: st[:, SUBLANES - bp:, :].transpose(1, 0, 2)
            return o, last(sr), last(si)

        def up_p(hb):
            act, cg, cv = _up_prompt(hb, w_up[l], conv_w[l], conv_b[l], batch=bp, seq=seq)
            return act, jnp.concatenate([cg, cv], axis=-1)

        ck = cache_k[l].transpose(0, 2, 3, 4, 1).reshape(n_pool, width, page)
        cv_ = cache_v[l].reshape(n_pool, page * N_HEADS, V_DIM)

        def attend_s(q, k, v):
            return _sample_attention(q.astype(F32), k, v, ck, cv_, page_table, rel_bias, lam,
                                     subln_g[l], pages=DECODE_PAGES, out_scale=out_scale)

        def ssm_s(u):
            x0r = state_ssm_re[l].reshape(bs, N_SLABS, SLAB_STATE).transpose(1, 0, 2)
            x0i = state_ssm_im[l].reshape(bs, N_SLABS, SLAB_STATE).transpose(1, 0, 2)
            o, sr, si = _s5_sample(u, x0r, x0i, sp, ssm_d[l], w_glu[l])
            return o, sr.transpose(1, 0, 2), si.transpose(1, 0, 2)

        def up_s(hb):
            act, cg, cv = _up_sample(hb, w_up[l], conv_w[l], conv_b[l],
                                     state_conv[l].transpose(1, 0, 2))
            return act, jnp.concatenate([cg, cv], axis=-1).transpose(1, 0, 2)

        (hp, kt_p, v_p, re_p, im_p, c_p), (hs, k_s, v_s, re_s, im_s, c_s) = _layer(
            hp, hs, w, batch=bp, seq=seq, alpha=alpha, attend_p=attend_p, attend_s=attend_s,
            ssm_p=ssm_p, ssm_s=ssm_s, up_p=up_p, up_s=up_s)
        k_p = kt_p.reshape(bp, N_HEADS, 2, HEAD_DIM, seq).transpose(0, 4, 1, 2, 3)

        for name, val in (("kp", k_p.reshape(bp, seq, N_HEADS, 2, HEAD_DIM)),
                          ("vp", v_p.reshape(bp, seq, N_HEADS, V_DIM)),
                          ("rep", re_p.reshape(bp, n_groups, STATE_DIM)),
                          ("imp", im_p.reshape(bp, n_groups, STATE_DIM)),
                          ("cp", c_p),
                          ("ks", k_s.reshape(bs, 1, N_HEADS, 2, HEAD_DIM)),
                          ("vs", v_s.reshape(bs, 1, N_HEADS, V_DIM)),
                          ("res", re_s.reshape(bs, n_groups, STATE_DIM)),
                          ("ims", im_s.reshape(bs, n_groups, STATE_DIM)),
                          ("cs", c_s)):
            outs.setdefault(name, []).append(val)

    st = {k: jnp.stack(v, axis=0) for k, v in outs.items()}
    return (hp.reshape(bp, seq, d_model), hs.reshape(bs, 1, d_model), st["kp"], st["vp"], st["rep"],
            st["imp"], st["cp"], st["ks"], st["vs"], st["res"], st["ims"], st["cs"])
```

```python
import functools
import math

import jax
import jax.numpy as jnp
from jax import lax
from jax.experimental import pallas as pl
from jax.experimental.pallas import tpu as pltpu

F32 = jnp.float32
BF16 = jnp.bfloat16

N_HEADS = 8
HEAD_DIM = 64
V_DIM = 2 * HEAD_DIM
SSM_GROUP = 16
STATE_DIM = 64
CONV_W = 3
NUM_BUCKETS = 32
MAX_EXACT = NUM_BUCKETS // 2
MAX_DISTANCE = 128
LN_EPS = 1e-5
NEG_INF = -1e30

VMEM_LIMIT_BYTES = 56 * 1024 * 1024
LANES = 128
SLAB_GROUPS = 8
N_SLABS = 8
SLAB_IN = SLAB_GROUPS * SSM_GROUP
SLAB_STATE = SLAB_GROUPS * STATE_DIM
SUBLANES = 8
MM_ROWS, MM_COLS = 1024, 1024
LN_MM_ROWS = 512
DOWN_ROWS, DOWN_COLS = 512, 512
LN_ROWS = 512
ATTN_BLOCK = 256
ATTN_KEY_BLOCK = 256
ATTN_HEADS_PER_STEP = 4
ATTN_BLOCKS_PER_CALL = 2
S5_POSITIONS = 64
DECODE_PAGES = 16
UP_ROW_CHUNK = 512


def _params(sem):
    return pltpu.CompilerParams(dimension_semantics=sem, vmem_limit_bytes=VMEM_LIMIT_BYTES)


def _mm_body(*refs, n_pairs, n_extra, n_out, epilogue, emit_lhs):
    it = iter(refs)
    take = lambda n: [next(it) for _ in range(n)]
    x_refs, xs_refs, w_refs = take(n_pairs), take(n_pairs), take(n_pairs)
    extra, extra_s = take(n_extra), take(n_extra)
    outs, outs_s = take(n_out), take(n_out)
    lhs_out = take(1) if emit_lhs else []
    wbf = take(n_pairs)

    def apply(lhs_refs, extra_refs, out_refs, keep):
        lhs = [x[...].astype(BF16) for x in lhs_refs]
        for ref in keep:
            ref[...] = lhs[0]
        accs = [jnp.dot(x, s[...], preferred_element_type=F32) for x, s in zip(lhs, wbf)]
        for o, r in zip(out_refs, epilogue(*accs, *[e[...] for e in extra_refs])):
            o[...] = r.astype(o.dtype)

    @pl.when(pl.program_id(1) == 0)
    def _():
        for w, s in zip(w_refs, wbf):
            s[...] = w[...].astype(BF16)
        apply(xs_refs, extra_s, outs_s, [])

    apply(x_refs, extra, outs, lhs_out)


def _matmul(pairs, extras, out_dtypes, epilogue, *, n_cols, tm, tn, name, emit_lhs=False):
    m = pairs[0][0].shape[0]
    ms = pairs[0][1].shape[0]
    grid = (n_cols // tn, m // tm)
    assert not emit_lhs or grid[0] == 1
    w_mode = dict(pipeline_mode=pl.Buffered(1)) if grid[0] == 1 else {}
    specs_x, specs_xs, specs_w, scratch = [], [], [], []
    for x, xs, w, c0 in pairs:
        k = x.shape[1]
        specs_x.append(pl.BlockSpec((tm, k), lambda j, i: (i, 0)))
        specs_xs.append(pl.BlockSpec((ms, k), lambda j, i: (0, 0)))
        specs_w.append(pl.BlockSpec((k, tn), lambda j, i, off=c0 // tn: (0, j + off), **w_mode))
        scratch.append(pltpu.VMEM((k, tn), BF16))
    specs_e, specs_es, args_e, args_es = [], [], [], []
    for e in extras:
        if isinstance(e, tuple):
            specs_e.append(pl.BlockSpec((tm, tn), lambda j, i: (i, j)))
            specs_es.append(pl.BlockSpec((ms, tn), lambda j, i: (0, j)))
            args_e.append(e[0])
            args_es.append(e[1])
        else:
            specs_e.append(pl.BlockSpec((1, tn), lambda j, i: (0, j)))
            specs_es.append(pl.BlockSpec((1, tn), lambda j, i: (0, j)))
            args_e.append(e)
            args_es.append(e)
    out_shape = ([jax.ShapeDtypeStruct((m, n_cols), d) for d in out_dtypes]
                 + [jax.ShapeDtypeStruct((ms, n_cols), d) for d in out_dtypes])
    out_specs = ([pl.BlockSpec((tm, tn), lambda j, i: (i, j)) for _ in out_dtypes]
                 + [pl.BlockSpec((ms, tn), lambda j, i: (0, j)) for _ in out_dtypes])
    if emit_lhs:
        k0 = pairs[0][0].shape[1]
        out_shape.append(jax.ShapeDtypeStruct((m, k0), BF16))
        out_specs.append(pl.BlockSpec((tm, k0), lambda j, i: (i, 0)))
    body = functools.partial(_mm_body, n_pairs=len(pairs), n_extra=len(extras),
                             n_out=len(out_dtypes), epilogue=epilogue, emit_lhs=emit_lhs)
    outs = pl.pallas_call(
        body, grid=grid, in_specs=specs_x + specs_xs + specs_w + specs_e + specs_es,
        out_specs=out_specs, out_shape=out_shape, scratch_shapes=scratch,
        compiler_params=_params(("arbitrary", "arbitrary")), name=name)(
            *[p[0] for p in pairs], *[p[1] for p in pairs], *[p[2] for p in pairs],
            *args_e, *args_es)
    n = len(out_dtypes)
    if emit_lhs:
        return outs[:n], outs[n:2 * n], outs[2 * n]
    return outs[:n], outs[n:]


def _proj_t_body(x_ref, xs_ref, w_ref, o_ref, ob_ref, os_ref, wt_sc):
    @pl.when(pl.program_id(0) == 0)
    def _():
        for c in range(w_ref.shape[1] // LANES):
            cols = slice(c * LANES, (c + 1) * LANES)
            wt_sc[cols, :] = w_ref[:, cols].T.astype(BF16)
        os_ref[...] = lax.dot_general(xs_ref[...].astype(BF16), wt_sc[...], (((1,), (1,)), ((), ())),
                                      preferred_element_type=F32)

    kt = lax.dot_general(wt_sc[...], x_ref[...], (((1,), (1,)), ((), ())),
                         preferred_element_type=F32)
    o_ref[...] = kt
    ob_ref[...] = kt.astype(BF16)


def _proj_transposed(x, xs, w, c0, *, n_cols, batch, seq, tm, name):
    k = x.shape[1]
    ms = xs.shape[0]
    per_b = seq // tm
    out_spec = pl.BlockSpec((None, n_cols, tm), lambda i: (i // per_b, 0, i % per_b))
    return pl.pallas_call(
        _proj_t_body, grid=(batch * per_b,),
        in_specs=[pl.BlockSpec((tm, k), lambda i: (i, 0)),
                  pl.BlockSpec((ms, k), lambda i: (0, 0)),
                  pl.BlockSpec((k, n_cols), lambda i, off=c0 // n_cols: (0, off),
                               pipeline_mode=pl.Buffered(1))],
        out_specs=[out_spec, out_spec, pl.BlockSpec((ms, n_cols), lambda i: (0, 0))],
        out_shape=[jax.ShapeDtypeStruct((batch, n_cols, seq), F32),
                   jax.ShapeDtypeStruct((batch, n_cols, seq), BF16),
                   jax.ShapeDtypeStruct((ms, n_cols), F32)],
        scratch_shapes=[pltpu.VMEM((n_cols, k), BF16)],
        compiler_params=_params(("arbitrary",)), name=name)(x, xs, w)


def _ln(x, g, b):
    mu = jnp.mean(x, axis=-1, keepdims=True)
    xc = x - mu
    var = jnp.mean(xc * xc, axis=-1, keepdims=True)
    return xc * lax.rsqrt(var + LN_EPS) * g + b


def _ln_body(x_ref, g_ref, b_ref, o_ref):
    o_ref[...] = _ln(x_ref[...], g_ref[...], b_ref[...])


def _layer_norm(x, g, b, *, tm, name):
    m, d = x.shape
    tm = min(tm, m)
    return pl.pallas_call(
        _ln_body, grid=(m // tm,),
        in_specs=[pl.BlockSpec((tm, d), lambda i: (i, 0)),
                  pl.BlockSpec((1, d), lambda i: (0, 0)),
                  pl.BlockSpec((1, d), lambda i: (0, 0))],
        out_specs=pl.BlockSpec((tm, d), lambda i: (i, 0)),
        out_shape=jax.ShapeDtypeStruct((m, d), F32),
        compiler_params=_params(("arbitrary",)), name=name)(x, g.reshape(1, d), b.reshape(1, d))


def _rel_bucket(n):
    n = jnp.maximum(n, 0)
    nf = jnp.maximum(n, 1).astype(F32)
    large = MAX_EXACT + jnp.floor(jnp.log(nf / MAX_EXACT) / math.log(MAX_DISTANCE / MAX_EXACT)
                                  * (NUM_BUCKETS - MAX_EXACT)).astype(jnp.int32)
    large = jnp.minimum(large, NUM_BUCKETS - 1)
    return jnp.where(n < MAX_EXACT, n, large)


def _bucket_lookup(bucket, table_fn):
    out = jnp.zeros(jnp.broadcast_shapes(bucket.shape, table_fn(0).shape), F32)
    for b in range(NUM_BUCKETS):
        out = out + jnp.where(bucket == b, table_fn(b), 0.0)
    return out


def _bias_tile_body(rb_ref, bucket_ref, o_ref):
    h = pl.program_id(0)
    o_ref[...] = _bucket_lookup(bucket_ref[...], lambda bk: rb_ref[bk, h])


def _bias_tiles(rel_bias, t):
    r = jnp.arange(t, dtype=jnp.int32)
    c = jnp.arange(2 * t, dtype=jnp.int32)
    buckets = _rel_bucket(r[:, None] + t - c[None, :])
    return pl.pallas_call(
        _bias_tile_body, grid=(N_HEADS,),
        in_specs=[pl.BlockSpec(memory_space=pltpu.SMEM), pl.BlockSpec((t, 2 * t), lambda h: (0, 0))],
        out_specs=pl.BlockSpec((None, t, 2 * t), lambda h: (h, 0, 0)),
        out_shape=jax.ShapeDtypeStruct((N_HEADS, t, 2 * t), F32),
        compiler_params=_params(("arbitrary",)), name="bias_tiles")(rel_bias, buckets)


def _attn_body(lam_ref, rb_ref, q_ref, k_ref, v_ref, bias_ref, g_ref, o_ref, *scratch,
               t, first, count, heads, out_scale):
    for sub in range(count):
        i = first + sub
        _attn_block(lam_ref, rb_ref, q_ref, k_ref, v_ref, bias_ref, g_ref, o_ref, scratch, t=t,
                    n_far=max(i - 1, 0), near=min(i + 1, 2) * t, heads=heads, out_scale=out_scale,
                    row0=sub * t)


def _attn_block(lam_ref, rb_ref, q_ref, k_ref, v_ref, bias_ref, g_ref, o_ref, scratch,
                *, t, n_far, near, heads, out_scale, row0):
    rows_q = slice(row0, row0 + t)
    s_bufs, p_bufs, m_bufs, mf_bufs, l_bufs, a_bufs = (scratch[0:2], scratch[2:4], scratch[4:6],
                                                       scratch[6:8], scratch[8:10], scratch[10:12])
    hg = pl.program_id(0)
    far = n_far * t
    kb_w = ATTN_KEY_BLOCK
    n_kb = (far + near) // kb_w
    n_tiles = kb_w // LANES

    def fold(x, op):
        out = x[:, 0:LANES]
        for c in range(1, n_tiles):
            out = op(out, x[:, c * LANES:(c + 1) * LANES])
        return out

    def head_cols(hh):
        return slice(hh * V_DIM, (hh + 1) * V_DIM)

    def score_block(hh, kb):
        par = hh % 2
        q = q_ref[rows_q, head_cols(hh)] * (HEAD_DIM ** -0.5)
        lane = lax.broadcasted_iota(jnp.int32, q.shape, 1)
        zero = jnp.zeros_like(q)
        q2 = jnp.concatenate([jnp.where(lane < HEAD_DIM, q, zero), jnp.where(lane >= HEAD_DIM, q, zero)],
                             axis=0)
        cols = slice(kb * kb_w, (kb + 1) * kb_w)
        s = jnp.dot(q2, k_ref[head_cols(hh), cols], preferred_element_type=F32)
        bias_far = rb_ref[NUM_BUCKETS - 1, hg * heads + hh]
        if kb * kb_w < far:
            top = fold(s, jnp.maximum) + bias_far
        else:
            off = kb * kb_w - far
            lo = 2 * t - near + off
            bias = bias_ref[hh, :, lo:lo + kb_w]
            row = lax.broadcasted_iota(jnp.int32, (t, kb_w), 0)
            col = lax.broadcasted_iota(jnp.int32, (t, kb_w), 1)
            keep = col + (off - (near - t)) <= row
            s = jnp.where(jnp.concatenate([keep, keep], axis=0),
                          s + jnp.concatenate([bias, bias], axis=0), NEG_INF)
            top = fold(s, jnp.maximum)
        s_bufs[par][:, cols] = s
        if kb == 0:
            m_bufs[par][...] = top
        else:
            m_bufs[par][...] = jnp.maximum(m_bufs[par][...], top)
        if kb == n_kb - 1:
            m = jnp.broadcast_to(jnp.max(m_bufs[par][...], axis=1, keepdims=True), m_bufs[par].shape)
            m_bufs[par][...] = m
            mf_bufs[par][...] = m - bias_far

    def exp_block(hh, kb):
        par = hh % 2
        cols = slice(kb * kb_w, (kb + 1) * kb_w)
        m = (mf_bufs if kb * kb_w < far else m_bufs)[par][...]
        p = jnp.exp(s_bufs[par][:, cols] - jnp.concatenate([m] * n_tiles, axis=1))
        if kb == 0:
            l_bufs[par][...] = fold(p, jnp.add)
        else:
            l_bufs[par][...] += fold(p, jnp.add)
        p_bufs[par][:, cols] = p.astype(BF16)

    def value_block(hh, kb):
        par = hh % 2
        rows = slice(kb * kb_w, (kb + 1) * kb_w)
        pv = jnp.dot(p_bufs[par][:, rows], v_ref[rows, head_cols(hh)], preferred_element_type=F32)
        if kb == 0:
            a_bufs[par][...] = pv
        else:
            a_bufs[par][...] += pv
        if kb == n_kb - 1:
            nrm = a_bufs[par][...] / jnp.sum(l_bufs[par][...], axis=1, keepdims=True)
            o = nrm[0:t] - lam_ref[0] * nrm[t:2 * t]
            ms = jnp.mean(o * o, axis=-1, keepdims=True)
            o_ref[rows_q, head_cols(hh)] = (o * lax.rsqrt(ms + LN_EPS) * g_ref[...] * out_scale
                                            ).astype(o_ref.dtype)

    for stage in range(heads + 2):
        for kb in range(n_kb):
            if stage < heads:
                score_block(stage, kb)
            if 0 <= stage - 1 < heads:
                exp_block(stage - 1, kb)
            if 0 <= stage - 2 < heads:
                value_block(stage - 2, kb)


def _prompt_attention(q, k, v, rel_bias, lam, subln_g, *, batch, seq, t, out_scale):
    assert t >= MAX_DISTANCE
    nq = seq // t
    width = N_HEADS * V_DIM
    q3 = q.reshape(batch, seq, width)
    v3 = v.reshape(batch, seq, width)
    bias = _bias_tiles(rel_bias, t)
    smem = pl.BlockSpec(memory_space=pltpu.SMEM)
    qo = q3
    group = ATTN_BLOCKS_PER_CALL
    hp = ATTN_HEADS_PER_STEP
    for first in range(0, nq, group):
        keys = (first + group) * t
        body = functools.partial(_attn_body, t=t, first=first, count=group, heads=hp,
                                 out_scale=out_scale)
        pair = lambda shape, dtype: [pltpu.VMEM(shape, dtype)] * 2
        rows_spec = pl.BlockSpec((None, group * t, hp * V_DIM),
                                 lambda h, b, p=first // group: (b, p, h))
        qo = pl.pallas_call(
            body, grid=(N_HEADS // hp, batch),
            in_specs=[smem, smem, rows_spec,
                      pl.BlockSpec((None, hp * V_DIM, keys), lambda h, b: (b, h, 0)),
                      pl.BlockSpec((None, keys, hp * V_DIM), lambda h, b: (b, 0, h)),
                      pl.BlockSpec((hp, t, 2 * t), lambda h, b: (h, 0, 0)),
                      pl.BlockSpec((1, V_DIM), lambda h, b: (0, 0))],
            out_specs=rows_spec,
            out_shape=jax.ShapeDtypeStruct((batch, seq, width), BF16),
            input_output_aliases={2: 0},
            scratch_shapes=(pair((2 * t, keys), F32) + pair((2 * t, keys), BF16)
                            + pair((2 * t, LANES), F32) + pair((2 * t, LANES), F32)
                            + pair((2 * t, LANES), F32) + pair((2 * t, V_DIM), F32)),
            compiler_params=_params(("arbitrary", "arbitrary")),
            name=f"prompt_attention_q{first}")(lam, rel_bias, qo, k, v3, bias, subln_g.reshape(1, V_DIM))
    return qo.reshape(batch * seq, width)


def _decode_body(pt_ref, lam_ref, q_ref, kn_ref, vn_ref, rbt_ref, bucket_ref, g_ref, *rest,
                 pages, page, n_steps, out_scale):
    k_refs = rest[:pages]
    v_refs = rest[pages:2 * pages]
    o_ref = rest[2 * pages]
    qexp_sc, bias_sc, m_sc, l_sc, acc_sc = rest[2 * pages + 1:]
    step = pl.program_id(1)
    rows = 2 * N_HEADS
    width = N_HEADS * V_DIM

    @pl.when(step == 0)
    def _():
        row = lax.broadcasted_iota(jnp.int32, (rows, width), 0)
        col = lax.broadcasted_iota(jnp.int32, (rows, width), 1)
        own_qk = (col // HEAD_DIM) == (row % N_HEADS) * 2 + row // N_HEADS
        q = (q_ref[...] * (HEAD_DIM ** -0.5)).astype(BF16).astype(F32)
        qexp = jnp.where(own_qk, jnp.broadcast_to(q, (rows, width)), 0.0)
        qexp_sc[...] = qexp.astype(BF16)
        bias_sc[...] = _bucket_lookup(bucket_ref[...], lambda bk: rbt_ref[:, bk:bk + 1])
        kn = kn_ref[...].astype(BF16).astype(F32)
        s_self = jnp.sum(qexp * kn, axis=1, keepdims=True) + rbt_ref[:, 0:1]
        m_sc[...] = s_self
        l_sc[...] = jnp.ones(l_sc.shape, F32)
        vn = vn_ref[...].astype(BF16).astype(F32)
        acc_sc[...] = jnp.broadcast_to(vn[:, None, :], acc_sc.shape)

    qexp = qexp_sc[...]
    bias_far = rbt_ref[:, NUM_BUCKETS - 1:NUM_BUCKETS]
    is_last = step == n_steps - 1
    s_parts = []
    for p in range(pages):
        s = jnp.dot(qexp, k_refs[p][...].astype(BF16), preferred_element_type=F32)
        if p == pages - 1:
            s = s + jnp.where(is_last, bias_sc[...], bias_far)
        else:
            s = s + bias_far
        s_parts.append(s)
    m_old = m_sc[...]
    m_new = m_old
    for s in s_parts:
        m_new = jnp.maximum(m_new, jnp.max(s, axis=1, keepdims=True))
    a = jnp.exp(m_old - m_new)
    l_new = a * l_sc[...]
    probs = []
    for p in range(pages):
        pr = jnp.exp(s_parts[p] - m_new)
        l_new = l_new + jnp.sum(pr, axis=1, keepdims=True)
        probs.append(pr.astype(BF16))
    probs = jnp.concatenate(probs, axis=1)
    m_sc[...] = m_new
    l_sc[...] = l_new
    for h in range(N_HEADS):
        vh = jnp.concatenate([v_refs[p][pl.ds(h, page, stride=N_HEADS), :].astype(BF16)
                              for p in range(pages)], axis=0)
        acc_sc[h] = a * acc_sc[h] + jnp.dot(probs, vh, preferred_element_type=F32)

    @pl.when(is_last)
    def _():
        nrm = acc_sc[...] / l_new[None]
        r = lax.broadcasted_iota(jnp.int32, nrm.shape, 1)
        hh = lax.broadcasted_iota(jnp.int32, nrm.shape, 0)
        coef = jnp.where(r == hh, 1.0, jnp.where(r == hh + N_HEADS, -lam_ref[0], 0.0))
        d = jnp.sum(coef * nrm, axis=1)
        ms = jnp.mean(d * d, axis=-1, keepdims=True)
        o_ref[...] = (d * lax.rsqrt(ms + LN_EPS) * g_ref[...] * out_scale).astype(o_ref.dtype)


def _sample_attention(q, k_new, v_new, cache_k, cache_v, page_table, rel_bias, lam, subln_g,
                      *, pages, out_scale):
    bs, width = q.shape
    page = cache_k.shape[2]
    n_pages = page_table.shape[1]
    n_steps = n_pages // pages
    past = n_pages * page
    kpos = past - page + jnp.arange(page, dtype=jnp.int32)
    bucket_last = _rel_bucket(past - kpos).reshape(1, page)
    rbt = jnp.tile(rel_bias.T, (2, 1))

    def tok_spec():
        return pl.BlockSpec((None, 1, width), lambda b, s, pt: (b, 0, 0))

    def page_spec(shape, p):
        return pl.BlockSpec((None,) + shape, lambda b, s, pt, p=p: (pt[b, s * pages + p], 0, 0))

    full = lambda shape: pl.BlockSpec(shape, lambda b, s, pt: tuple(0 for _ in shape))
    head_spec = pl.BlockSpec((None, N_HEADS, V_DIM), lambda b, s, pt: (b, 0, 0))
    in_specs = ([pl.BlockSpec(memory_space=pltpu.SMEM), tok_spec(), tok_spec(), head_spec,
                 full((2 * N_HEADS, NUM_BUCKETS)), full((1, page)), full((1, V_DIM))]
                + [page_spec((width, page), p) for p in range(pages)]
                + [page_spec((page * N_HEADS, V_DIM), p) for p in range(pages)])
    rows = 2 * N_HEADS
    body = functools.partial(_decode_body, pages=pages, page=page, n_steps=n_steps,
                             out_scale=out_scale)
    out = pl.pallas_call(
        body,
        grid_spec=pltpu.PrefetchScalarGridSpec(
            num_scalar_prefetch=1, grid=(bs, n_steps), in_specs=in_specs,
            out_specs=head_spec,
            scratch_shapes=[pltpu.VMEM((rows, width), BF16), pltpu.VMEM((rows, page), F32),
                            pltpu.VMEM((rows, 1), F32), pltpu.VMEM((rows, 1), F32),
                            pltpu.VMEM((N_HEADS, rows, V_DIM), F32)]),
        out_shape=jax.ShapeDtypeStruct((bs, N_HEADS, V_DIM), BF16),
        compiler_params=_params(("arbitrary", "arbitrary")),
        name="sample_attention")(
            page_table, lam, q.reshape(bs, 1, width), k_new.reshape(bs, 1, width),
            v_new.reshape(bs, N_HEADS, V_DIM), rbt, bucket_last, subln_g.reshape(1, V_DIM),
            *([cache_k] * pages), *([cache_v] * pages))
    return out.reshape(bs, width)


def _s5_params(a_re, a_im, log_dt, b_re, b_im, c_re, c_im):
    dt = jnp.exp(log_dt)[:, None]
    mag = jnp.exp(a_re * dt)
    ang = a_im * dt
    abar_re = mag * jnp.cos(ang)
    abar_im = mag * jnp.sin(ang)
    den = a_re * a_re + a_im * a_im
    f_re = ((abar_re - 1.0) * a_re + abar_im * a_im) / den
    f_im = (abar_im * a_re - (abar_re - 1.0) * a_im) / den
    bb_re = f_re[..., None] * b_re - f_im[..., None] * b_im
    bb_im = f_re[..., None] * b_im + f_im[..., None] * b_re
    eye = jnp.eye(SLAB_GROUPS, dtype=F32)

    def in_slabs(bb):
        tt = bb.reshape(N_SLABS, SLAB_GROUPS, STATE_DIM, SSM_GROUP).transpose(0, 1, 3, 2)
        full = tt[:, :, :, None, :] * eye[None, :, None, :, None]
        return full.reshape(N_SLABS, SLAB_IN, SLAB_STATE).astype(BF16)

    ab_re = abar_re[..., None] * bb_re - abar_im[..., None] * bb_im
    ab_im = abar_re[..., None] * bb_im + abar_im[..., None] * bb_re

    def out_slabs(cc):
        tt = cc.reshape(N_SLABS, SLAB_GROUPS, SSM_GROUP, STATE_DIM).transpose(0, 1, 3, 2)
        full = tt[:, :, :, None, :] * eye[None, :, None, :, None]
        return full.reshape(N_SLABS, SLAB_STATE, SLAB_IN).astype(BF16)

    return (abar_re.reshape(N_SLABS, SLAB_STATE), abar_im.reshape(N_SLABS, SLAB_STATE),
            in_slabs(bb_re), in_slabs(bb_im), out_slabs(c_re), out_slabs(c_im),
            in_slabs(ab_re), in_slabs(ab_im))


def _glu_out(y, u, d_ref, wglu_ref):
    g = jax.nn.gelu(y + d_ref[...] * u)
    gate = jnp.dot(g.astype(BF16), wglu_ref[...].astype(BF16), preferred_element_type=F32)
    return g * jax.nn.sigmoid(gate)


def _s5_prompt_body(u_ref, ar_ref, ai_ref, bbr_ref, bbi_ref, cr_ref, ci_ref, d_ref, wglu_ref,
                    o_ref, sre_ref, sim_ref, xr_sc, xi_sc, pw_sc, cr_sc, ci_sc, perm_sc, *, batch):
    c = pl.program_id(0)
    sub = SUBLANES
    _, tc, width = u_ref.shape
    n_rows = batch * tc
    t = lax.broadcasted_iota(jnp.int32, (sub, SLAB_STATE), 0)
    first = t < batch

    def cmul(pr, pi, qr, qi):
        return pr * qr - pi * qi, pr * qi + pi * qr

    @pl.when(c == 0)
    def _():
        for j in range(N_SLABS):
            a = (jnp.broadcast_to(ar_ref[j:j + 1, :], t.shape), jnp.broadcast_to(ai_ref[j:j + 1, :], t.shape))
            a2 = cmul(*a, *a)
            for part in range(2):
                pw_sc[j, part] = jnp.where(first, a[part], a2[part])
        cr_sc[...] = jnp.zeros(cr_sc.shape, F32)
        ci_sc[...] = jnp.zeros(ci_sc.shape, F32)
        r = lax.broadcasted_iota(jnp.int32, (n_rows, n_rows), 0)
        k = lax.broadcasted_iota(jnp.int32, (n_rows, n_rows), 1)
        perm_sc[0] = jnp.where(k == (r % batch) * tc + r // batch, 1.0, 0.0).astype(BF16)
        perm_sc[1] = jnp.where(r == (k % batch) * tc + k // batch, 1.0, 0.0).astype(BF16)

    def permute(which, x):
        return jnp.dot(perm_sc[which], x, preferred_element_type=F32)

    u_seq = u_ref[...].reshape(n_rows, width)
    hi = u_seq.astype(BF16)
    rest = u_seq - hi.astype(F32)
    mid = rest.astype(BF16)
    lo = (rest - mid.astype(F32)).astype(BF16)
    u_hi = permute(0, hi)
    ub = u_hi.astype(BF16)
    u = u_hi + permute(0, mid) + permute(0, lo)
    row = lax.broadcasted_iota(jnp.int32, u_hi.shape, 0)
    u_before = jnp.where(row % sub >= batch, pltpu.roll(u_hi, batch, 0), 0.0).astype(BF16)
    for j in range(N_SLABS):
        cols = slice(j * SLAB_IN, (j + 1) * SLAB_IN)
        uj = jnp.concatenate([ub[:, cols], u_before[:, cols]], axis=1)
        xr_sc[j] = jnp.dot(uj, bbr_ref[j], preferred_element_type=F32)
        xi_sc[j] = jnp.dot(uj, bbi_ref[j], preferred_element_type=F32)

    for j in range(N_SLABS):
        pr, pi = cr_sc[j], ci_sc[j]
        for v in range(n_rows // sub):
            rows = slice(v * sub, (v + 1) * sub)
            lr = jnp.where(first, pltpu.roll(pr, batch, 0), pr)
            li = jnp.where(first, pltpu.roll(pi, batch, 0), pi)
            er, ei = cmul(pw_sc[j, 0], pw_sc[j, 1], lr, li)
            pr, pi = xr_sc[j, rows, :] + er, xi_sc[j, rows, :] + ei
            xr_sc[j, rows, :] = pr
            xi_sc[j, rows, :] = pi
        cr_sc[j] = pr
        ci_sc[j] = pi
    sre_ref[...] = cr_sc[...]
    sim_ref[...] = ci_sc[...]

    ys = []
    for j in range(N_SLABS):
        ys.append(jnp.dot(xr_sc[j].astype(BF16), cr_ref[j], preferred_element_type=F32)
                  - jnp.dot(xi_sc[j].astype(BF16), ci_ref[j], preferred_element_type=F32))
    out = _glu_out(jnp.concatenate(ys, axis=1), u, d_ref, wglu_ref).astype(BF16)
    o_ref[...] = permute(1, out).astype(o_ref.dtype).reshape(batch, tc, width)


def _s5_prompt(u, sp, d, w_glu, *, batch, seq, tc):
    assert SUBLANES == 2 * batch, "tile = two positions of every sequence"
    ar, ai, bbr, bbi, cr, ci, abr, abi = sp
    bbr = jnp.concatenate([bbr, abr], axis=1)
    bbi = jnp.concatenate([bbi, abi], axis=1)
    width = u.shape[1]
    rows = batch * tc
    const = lambda shape: pl.BlockSpec(shape, lambda c: tuple(0 for _ in shape))
    block = pl.BlockSpec((batch, tc, width), lambda c: (0, c, 0))
    state_spec = const((N_SLABS, SUBLANES, SLAB_STATE))
    state_shape = jax.ShapeDtypeStruct((N_SLABS, SUBLANES, SLAB_STATE), F32)
    out, s_re, s_im = pl.pallas_call(
        functools.partial(_s5_prompt_body, batch=batch), grid=(seq // tc,),
        in_specs=[block, const(ar.shape), const(ai.shape), const(bbr.shape), const(bbi.shape),
                  const(cr.shape), const(ci.shape), const((1, width)), const(w_glu.shape)],
        out_specs=[block, state_spec, state_spec],
        out_shape=[jax.ShapeDtypeStruct((batch, seq, width), BF16), state_shape, state_shape],
        scratch_shapes=[pltpu.VMEM((N_SLABS, rows, SLAB_STATE), F32),
                        pltpu.VMEM((N_SLABS, rows, SLAB_STATE), F32),
                        pltpu.VMEM((N_SLABS, 2, SUBLANES, SLAB_STATE), F32),
                        pltpu.VMEM((N_SLABS, SUBLANES, SLAB_STATE), F32),
                        pltpu.VMEM((N_SLABS, SUBLANES, SLAB_STATE), F32),
                        pltpu.VMEM((2, rows, rows), BF16)],
        compiler_params=_params(("arbitrary",)),
        name="s5_prompt")(u.reshape(batch, seq, width), ar, ai, bbr, bbi, cr, ci,
                          d.reshape(1, width), w_glu)
    return out.reshape(batch * seq, width), s_re, s_im


def _s5_sample_body(u_ref, x0r_ref, x0i_ref, ar_ref, ai_ref, bbr_ref, bbi_ref, cr_ref, ci_ref,
                    d_ref, wglu_ref, o_ref, sre_ref, sim_ref):
    u = u_ref[...]
    ub = u.astype(BF16)
    ys = []
    for j in range(N_SLABS):
        uj = ub[:, j * SLAB_IN:(j + 1) * SLAB_IN]
        ar = ar_ref[j:j + 1, :]
        ai = ai_ref[j:j + 1, :]
        x0r = x0r_ref[j]
        x0i = x0i_ref[j]
        xr = ar * x0r - ai * x0i + jnp.dot(uj, bbr_ref[j], preferred_element_type=F32)
        xi = ar * x0i + ai * x0r + jnp.dot(uj, bbi_ref[j], preferred_element_type=F32)
        sre_ref[j] = xr
        sim_ref[j] = xi
        ys.append(jnp.dot(xr.astype(BF16), cr_ref[j], preferred_element_type=F32)
                  - jnp.dot(xi.astype(BF16), ci_ref[j], preferred_element_type=F32))
    o_ref[...] = _glu_out(jnp.concatenate(ys, axis=1), u, d_ref, wglu_ref).astype(o_ref.dtype)


def _s5_sample(u, x0_re, x0_im, sp, d, w_glu):
    ar, ai, bbr, bbi, cr, ci = sp[:6]
    bs, width = u.shape
    state_shape = jax.ShapeDtypeStruct((N_SLABS, bs, SLAB_STATE), F32)
    return pl.pallas_call(
        _s5_sample_body,
        out_shape=[jax.ShapeDtypeStruct((bs, width), BF16), state_shape, state_shape],
        compiler_params=pltpu.CompilerParams(vmem_limit_bytes=VMEM_LIMIT_BYTES),
        name="s5_sample")(u, x0_re, x0_im, ar, ai, bbr, bbi, cr, ci, d.reshape(1, width), w_glu)


def _gated(c):
    gate = c[:, :LANES]
    return gate * jax.nn.sigmoid(gate) * c[:, LANES:]


def _up_prompt_body(h_ref, wg_ref, wv_ref, cwg_ref, cwv_ref, cbg_ref, cbv_ref,
                    act_ref, cg_ref, cv_ref, w_sc, *, rows):
    w_sc[:, 0:LANES] = wg_ref[...].astype(BF16)
    w_sc[:, LANES:2 * LANES] = wv_ref[...].astype(BF16)
    cw = jnp.concatenate([cwg_ref[...], cwv_ref[...]], axis=1)
    cb = jnp.concatenate([cbg_ref[...], cbv_ref[...]], axis=1)
    seq = h_ref.shape[0]
    sub = SUBLANES
    row8 = lax.broadcasted_iota(jnp.int32, (sub, 2 * LANES), 0)

    prev = jnp.zeros((sub, 2 * LANES), F32)
    for c in range(seq // rows):
        up = jnp.dot(h_ref[c * rows:(c + 1) * rows, :], w_sc[...], preferred_element_type=F32)
        cur = cb + cw[CONV_W - 1:CONV_W] * up
        for back in range(1, CONV_W):
            rolled = pltpu.roll(up, back, 0)
            head = jnp.where(row8 < back, pltpu.roll(prev, back, 0), rolled[0:sub])
            shifted = jnp.concatenate([head, rolled[sub:]], axis=0)
            cur = cur + cw[CONV_W - 1 - back:CONV_W - back] * shifted
        act_ref[c * rows:(c + 1) * rows, :] = _gated(cur).astype(act_ref.dtype)
        prev = up[rows - sub:, :]
    tail = prev[sub - (CONV_W - 1):, :]
    cg_ref[...] = tail[:, :LANES]
    cv_ref[...] = tail[:, LANES:]


def _up_prompt(hb, w_up, conv_w, conv_b, *, batch, seq):
    d_model = hb.shape[1]
    d_ff = w_up.shape[1] // 2
    nb = d_ff // LANES
    cb = conv_b.reshape(1, 2 * d_ff)
    col = lambda off: (lambda b, j: (0, j + off))
    tail_spec = pl.BlockSpec((None, CONV_W - 1, LANES), lambda b, j: (b, 0, j))
    tail_shape = jax.ShapeDtypeStruct((batch, CONV_W - 1, d_ff), F32)
    return pl.pallas_call(
        functools.partial(_up_prompt_body, rows=UP_ROW_CHUNK), grid=(batch, nb),
        scratch_shapes=[pltpu.VMEM((d_model, 2 * LANES), BF16)],
        in_specs=[pl.BlockSpec((seq, d_model), lambda b, j: (b, 0)),
                  pl.BlockSpec((d_model, LANES), col(0)), pl.BlockSpec((d_model, LANES), col(nb)),
                  pl.BlockSpec((CONV_W, LANES), col(0)), pl.BlockSpec((CONV_W, LANES), col(nb)),
                  pl.BlockSpec((1, LANES), col(0)), pl.BlockSpec((1, LANES), col(nb))],
        out_specs=[pl.BlockSpec((seq, LANES), lambda b, j: (b, j)), tail_spec, tail_spec],
        out_shape=[jax.ShapeDtypeStruct((batch * seq, d_ff), BF16), tail_shape, tail_shape],
        compiler_params=_params(("arbitrary", "arbitrary")),
        name="up_prompt")(hb, w_up, w_up, conv_w, conv_w, cb, cb)


def _up_sample_body(h_ref, wg_ref, wva_ref, wvb_ref, cwg_ref, cwva_ref, cwvb_ref, cbg_ref, cbva_ref,
                    cbvb_ref, sg_ref, sva_ref, svb_ref, act_ref, cg_ref, cv_ref):
    w2 = 2 * LANES
    w = jnp.concatenate([wg_ref[...].astype(BF16), wva_ref[...].astype(BF16),
                         wvb_ref[...].astype(BF16)], axis=1)
    up = jnp.dot(h_ref[...], w, preferred_element_type=F32)
    cw = jnp.concatenate([cwg_ref[...], cwva_ref[...], cwvb_ref[...]], axis=1)
    cb = jnp.concatenate([cbg_ref[...], cbva_ref[...], cbvb_ref[...]], axis=1)
    c = cb + cw[CONV_W - 1:CONV_W] * up
    for tap in range(CONV_W - 1):
        st = jnp.concatenate([sg_ref[tap], sva_ref[tap], svb_ref[tap]], axis=1)
        c = c + cw[tap:tap + 1] * st
    gate = c[:, 0:w2]
    act_ref[...] = (gate * jax.nn.sigmoid(gate) * c[:, w2:2 * w2]).astype(act_ref.dtype)
    for tap in range(1, CONV_W - 1):
        cg_ref[tap - 1] = sg_ref[tap]
        cv_ref[tap - 1] = jnp.concatenate([sva_ref[tap], svb_ref[tap]], axis=1)
    cg_ref[CONV_W - 2] = up[:, 0:w2]
    cv_ref[CONV_W - 2] = up[:, w2:2 * w2]


def _up_sample(hb, w_up, conv_w, conv_b, state):
    bs, d_model = hb.shape
    d_ff = w_up.shape[1] // 2
    nb = d_ff // LANES
    n_pairs = (nb + 1) // 2
    last = 2 * nb - 1
    w2 = 2 * LANES
    cb = conv_b.reshape(1, 2 * d_ff)
    gate = lambda shape: pl.BlockSpec(shape, lambda j: (0,) * (len(shape) - 1) + (j,))
    val = lambda shape, k: pl.BlockSpec(
        shape, lambda j: (0,) * (len(shape) - 1) + (jnp.minimum(nb + 2 * j + k, last),))
    operands = []
    specs = [pl.BlockSpec((bs, d_model), lambda j: (0, 0))]
    for arr, lead in ((w_up, (d_model,)), (conv_w, (CONV_W,)), (cb, (1,)), (state, (CONV_W - 1, bs))):
        specs += [gate(lead + (w2,)), val(lead + (LANES,), 0), val(lead + (LANES,), 1)]
        operands += [arr, arr, arr]
    tail_spec = pl.BlockSpec((CONV_W - 1, bs, w2), lambda j: (0, 0, j))
    tail_shape = jax.ShapeDtypeStruct((CONV_W - 1, bs, d_ff), F32)
    return pl.pallas_call(
        _up_sample_body, grid=(n_pairs,), in_specs=specs,
        out_specs=[pl.BlockSpec((bs, w2), lambda j: (0, j)), tail_spec, tail_spec],
        out_shape=[jax.ShapeDtypeStruct((bs, d_ff), BF16), tail_shape, tail_shape],
        compiler_params=_params(("arbitrary",)),
        name="up_sample")(hb, *operands)


def _one(x):
    return (x,)


def _sigmoid_out(acc):
    return (jax.nn.sigmoid(acc),)


def _both(acc):
    return (acc, acc)


def _merge(pa, ps, ga, gs):
    return (ga * pa + gs * ps,)


def _layer(xp, xs, w, *, batch, seq, alpha, attend_p, attend_s, ssm_p, ssm_s, up_p, up_s):
    d_model = xp.shape[1]
    qk_w = N_HEADS * 2 * HEAD_DIM
    v_w = N_HEADS * V_DIM
    ssm_w = d_model // 2
    w_in = w["w_in"]
    g1, b1 = w["ln1_g"].reshape(1, d_model), w["ln1_b"].reshape(1, d_model)
    big = dict(tm=MM_ROWS, tn=MM_COLS)
    c = 0
    (q,), (q_s,), xb = _matmul([(xp, xs, w_in, c)], [], [BF16], _one, n_cols=qk_w, name="proj_q",
                               emit_lhs=True, **big)
    c += qk_w
    kt, ktb, k_s = _proj_transposed(xb, xs, w_in, c, n_cols=qk_w, batch=batch, seq=seq, tm=MM_ROWS,
                                    name="proj_kt")
    c += qk_w
    c_v = c
    c += v_w
    (u,), (u_s,) = _matmul([(xb, xs, w_in, c)], [], [F32], _one, n_cols=ssm_w, name="proj_u", **big)
    c += ssm_w
    wide = dict(tm=MM_ROWS, tn=d_model)
    (ga,), (ga_s,) = _matmul([(xb, xs, w_in, c)], [], [F32], _sigmoid_out, n_cols=d_model, name="gate_a", **wide)
    c += d_model
    (gs,), (gs_s,) = _matmul([(xb, xs, w_in, c)], [], [F32], _sigmoid_out, n_cols=d_model, name="gate_s", **wide)
    (v, vb), (v_s, _) = _matmul([(xb, xs, w_in, c_v)], [], [F32, BF16], _both, n_cols=v_w, name="proj_v", **big)

    attn = attend_p(q, ktb, vb)
    ssm_out, re_p, im_p = ssm_p(u)
    ssm_out_s, re_s, im_s = ssm_s(u_s)
    attn_s = attend_s(q_s, k_s, v_s)

    (merged,), (merged_s,) = _matmul(
        [(attn, attn_s, w["w_proj_attn"], 0), (ssm_out, ssm_out_s, w["w_proj_ssm"], 0)],
        [(ga, ga_s), (gs, gs_s)], [BF16], _merge, n_cols=d_model, name="merge", **big)

    def post_ln1(acc, res, g, b):
        h = _ln(alpha * res + acc, g, b)
        return h, h

    (h, hb), (h_s, hb_s) = _matmul([(merged, merged_s, w["w_out"], 0)], [(xp, xs), g1, b1], [F32, BF16],
                                   post_ln1, n_cols=d_model, tm=LN_MM_ROWS, tn=d_model, name="out_proj_ln1")
    act, conv_p = up_p(hb)
    act_s, conv_s = up_s(hb_s)
    (r2,), (r2_s,) = _matmul([(act, act_s, w["w_down"], 0)], [(h, h_s)], [F32],
                             lambda acc, res: (alpha * res + acc,), n_cols=d_model,
                             tm=DOWN_ROWS, tn=DOWN_COLS, name="down_proj")
    y = _layer_norm(r2, w["ln2_g"], w["ln2_b"], tm=LN_ROWS, name="ln2")
    y_s = _layer_norm(r2_s, w["ln2_g"], w["ln2_b"], tm=LN_ROWS, name="ln2_sample")
    return (y, kt, v, re_p, im_p, conv_p), (y_s, k_s, v_s, re_s, im_s, conv_s)


def kernel(x_prompt, x_sample, cache_k, cache_v, state_ssm_re, state_ssm_im, state_conv, page_table, rel_bias, w_in, lambda_q1, lambda_k1, lambda_q2, lambda_k2, subln_g, ssm_a_re, ssm_a_im, ssm_log_dt, ssm_b_re, ssm_b_im, ssm_c_re, ssm_c_im, ssm_d, w_glu, w_proj_attn, w_proj_ssm, w_out, ln1_g, ln1_b, w_up, conv_w, conv_b, w_down, ln2_g, ln2_b):
    depth = w_in.shape[0]
    assert depth == 1, "single-layer trunk"
    bp, seq, d_model = x_prompt.shape
    bs, dec_seq, _ = x_sample.shape
    assert dec_seq == 1
    n_pool, page = cache_k.shape[1], cache_k.shape[2]
    d_ff = w_down.shape[1]
    n_groups = ssm_a_re.shape[1]
    assert n_groups == N_SLABS * SLAB_GROUPS and d_ff % LANES == 0
    alpha = (2.0 * depth) ** 0.25
    width = N_HEADS * V_DIM

    hp = x_prompt.reshape(bp * seq, d_model)
    hs = x_sample.reshape(bs, d_model)
    outs = {}
    for l in range(depth):
        lam_init = 0.8 - 0.6 * math.exp(-0.3 * l)
        out_scale = 1.0 - lam_init
        lam = (jnp.exp(jnp.sum(lambda_q1[l] * lambda_k1[l]))
               - jnp.exp(jnp.sum(lambda_q2[l] * lambda_k2[l])) + lam_init).reshape(1)
        w = dict(w_in=w_in[l], w_proj_attn=w_proj_attn[l], w_proj_ssm=w_proj_ssm[l], w_out=w_out[l],
                 ln1_g=ln1_g[l], ln1_b=ln1_b[l], w_down=w_down[l], ln2_g=ln2_g[l], ln2_b=ln2_b[l])
        sp = _s5_params(ssm_a_re[l], ssm_a_im[l], ssm_log_dt[l], ssm_b_re[l], ssm_b_im[l],
                        ssm_c_re[l], ssm_c_im[l])

        def attend_p(q, kt, v):
            return _prompt_attention(q, kt, v, rel_bias, lam, subln_g[l], batch=bp, seq=seq,
                                     t=ATTN_BLOCK, out_scale=out_scale)

        def ssm_p(u):
            o, sr, si = _s5_prompt(u, sp, ssm_d[l], w_glu[l], batch=bp, seq=seq, tc=S5_POSITIONS)
            last = lambda st: st[:, SUBLANES - bp:, :].transpose(1, 0, 2)
            return o, last(sr), last(si)

        def up_p(hb):
            act, cg, cv = _up_prompt(hb, w_up[l], conv_w[l], conv_b[l], batch=bp, seq=seq)
            return act, jnp.concatenate([cg, cv], axis=-1)

        ck = cache_k[l].transpose(0, 2, 3, 4, 1).reshape(n_pool, width, page)
        cv_ = cache_v[l].reshape(n_pool, page * N_HEADS, V_DIM)

        def attend_s(q, k, v):
            return _sample_attention(q.astype(F32), k, v, ck, cv_, page_table, rel_bias, lam,
                                     subln_g[l], pages=DECODE_PAGES, out_scale=out_scale)

        def ssm_s(u):
            x0r = state_ssm_re[l].reshape(bs, N_SLABS, SLAB_STATE).transpose(1, 0, 2)
            x0i = state_ssm_im[l].reshape(bs, N_SLABS, SLAB_STATE).transpose(1, 0, 2)
            o, sr, si = _s5_sample(u, x0r, x0i, sp, ssm_d[l], w_glu[l])
            return o, sr.transpose(1, 0, 2), si.transpose(1, 0, 2)

        def up_s(hb):
            act, cg, cv = _up_sample(hb, w_up[l], conv_w[l], conv_b[l],
                                     state_conv[l].transpose(1, 0, 2))
            return act, jnp.concatenate([cg, cv], axis=-1).transpose(1, 0, 2)

        (hp, kt_p, v_p, re_p, im_p, c_p), (hs, k_s, v_s, re_s, im_s, c_s) = _layer(
            hp, hs, w, batch=bp, seq=seq, alpha=alpha, attend_p=attend_p, attend_s=attend_s,
            ssm_p=ssm_p, ssm_s=ssm_s, up_p=up_p, up_s=up_s)
        k_p = kt_p.reshape(bp, N_HEADS, 2, HEAD_DIM, seq).transpose(0, 4, 1, 2, 3)

        for name, val in (("kp", k_p.reshape(bp, seq, N_HEADS, 2, HEAD_DIM)),
                          ("vp", v_p.reshape(bp, seq, N_HEADS, V_DIM)),
                          ("rep", re_p.reshape(bp, n_groups, STATE_DIM)),
                          ("imp", im_p.reshape(bp, n_groups, STATE_DIM)),
                          ("cp", c_p),
                          ("ks", k_s.reshape(bs, 1, N_HEADS, 2, HEAD_DIM)),
                          ("vs", v_s.reshape(bs, 1, N_HEADS, V_DIM)),
                          ("res", re_s.reshape(bs, n_groups, STATE_DIM)),
                          ("ims", im_s.reshape(bs, n_groups, STATE_DIM)),
                          ("cs", c_s)):
            outs.setdefault(name, []).append(val)

    st = {k: jnp.stack(v, axis=0) for k, v in outs.items()}
    return (hp.reshape(bp, seq, d_model), hs.reshape(bs, 1, d_model), st["kp"], st["vp"], st["rep"],
            st["imp"], st["cp"], st["ks"], st["vs"], st["res"], st["ims"], st["cs"])
```

```python
import functools
import math

import jax
import jax.numpy as jnp
from jax import lax
from jax.experimental import pallas as pl
from jax.experimental.pallas import tpu as pltpu

F32 = jnp.float32
BF16 = jnp.bfloat16

N_HEADS = 8
HEAD_DIM = 64
V_DIM = 2 * HEAD_DIM
SSM_GROUP = 16
STATE_DIM = 64
CONV_W = 3
NUM_BUCKETS = 32
MAX_EXACT = NUM_BUCKETS // 2
MAX_DISTANCE = 128
LN_EPS = 1e-5
NEG_INF = -1e30

VMEM_LIMIT_BYTES = 56 * 1024 * 1024
LANES = 128
SLAB_GROUPS = 8
N_SLABS = 8
SLAB_IN = SLAB_GROUPS * SSM_GROUP
SLAB_STATE = SLAB_GROUPS * STATE_DIM
SUBLANES = 8
MM_ROWS, MM_COLS = 1024, 1024
LN_MM_ROWS = 512
DOWN_ROWS, DOWN_COLS = 512, 512
LN_ROWS = 512
ATTN_BLOCK = 256
ATTN_KEY_BLOCK = 256
ATTN_HEADS_PER_STEP = 4
ATTN_BLOCKS_PER_CALL = 2
S5_POSITIONS = 64
DECODE_PAGES = 16
UP_ROW_CHUNK = 512


def _params(sem):
    return pltpu.CompilerParams(dimension_semantics=sem, vmem_limit_bytes=VMEM_LIMIT_BYTES)


def _mm_body(*refs, n_pairs, n_extra, n_out, epilogue, emit_lhs):
    it = iter(refs)
    take = lambda n: [next(it) for _ in range(n)]
    x_refs, xs_refs, w_refs = take(n_pairs), take(n_pairs), take(n_pairs)
    extra, extra_s = take(n_extra), take(n_extra)
    outs, outs_s = take(n_out), take(n_out)
    lhs_out = take(1) if emit_lhs else []
    wbf = take(n_pairs)

    def apply(lhs_refs, extra_refs, out_refs, keep):
        lhs = [x[...].astype(BF16) for x in lhs_refs]
        for ref in keep:
            ref[...] = lhs[0]
        accs = [jnp.dot(x, s[...], preferred_element_type=F32) for x, s in zip(lhs, wbf)]
        for o, r in zip(out_refs, epilogue(*accs, *[e[...] for e in extra_refs])):
            o[...] = r.astype(o.dtype)

    @pl.when(pl.program_id(1) == 0)
    def _():
        for w, s in zip(w_refs, wbf):
            s[...] = w[...].astype(BF16)
        apply(xs_refs, extra_s, outs_s, [])

    apply(x_refs, extra, outs, lhs_out)


def _matmul(pairs, extras, out_dtypes, epilogue, *, n_cols, tm, tn, name, emit_lhs=False):
    m = pairs[0][0].shape[0]
    ms = pairs[0][1].shape[0]
    grid = (n_cols // tn, m // tm)
    assert not emit_lhs or grid[0] == 1
    w_mode = dict(pipeline_mode=pl.Buffered(1)) if grid[0] == 1 else {}
    specs_x, specs_xs, specs_w, scratch = [], [], [], []
    for x, xs, w, c0 in pairs:
        k = x.shape[1]
        specs_x.append(pl.BlockSpec((tm, k), lambda j, i: (i, 0)))
        specs_xs.append(pl.BlockSpec((ms, k), lambda j, i: (0, 0)))
        specs_w.append(pl.BlockSpec((k, tn), lambda j, i, off=c0 // tn: (0, j + off), **w_mode))
        scratch.append(pltpu.VMEM((k, tn), BF16))
    specs_e, specs_es, args_e, args_es = [], [], [], []
    for e in extras:
        if isinstance(e, tuple):
            specs_e.append(pl.BlockSpec((tm, tn), lambda j, i: (i, j)))
            specs_es.append(pl.BlockSpec((ms, tn), lambda j, i: (0, j)))
            args_e.append(e[0])
            args_es.append(e[1])
        else:
            specs_e.append(pl.BlockSpec((1, tn), lambda j, i: (0, j)))
            specs_es.append(pl.BlockSpec((1, tn), lambda j, i: (0, j)))
            args_e.append(e)
            args_es.append(e)
    out_shape = ([jax.ShapeDtypeStruct((m, n_cols), d) for d in out_dtypes]
                 + [jax.ShapeDtypeStruct((ms, n_cols), d) for d in out_dtypes])
    out_specs = ([pl.BlockSpec((tm, tn), lambda j, i: (i, j)) for _ in out_dtypes]
                 + [pl.BlockSpec((ms, tn), lambda j, i: (0, j)) for _ in out_dtypes])
    if emit_lhs:
        k0 = pairs[0][0].shape[1]
        out_shape.append(jax.ShapeDtypeStruct((m, k0), BF16))
        out_specs.append(pl.BlockSpec((tm, k0), lambda j, i: (i, 0)))
    body = functools.partial(_mm_body, n_pairs=len(pairs), n_extra=len(extras),
                             n_out=len(out_dtypes), epilogue=epilogue, emit_lhs=emit_lhs)
    outs = pl.pallas_call(
        body, grid=grid, in_specs=specs_x + specs_xs + specs_w + specs_e + specs_es,
        out_specs=out_specs, out_shape=out_shape, scratch_shapes=scratch,
        compiler_params=_params(("arbitrary", "arbitrary")), name=name)(
            *[p[0] for p in pairs], *[p[1] for p in pairs], *[p[2] for p in pairs],
            *args_e, *args_es)
    n = len(out_dtypes)
    if emit_lhs:
        return outs[:n], outs[n:2 * n], outs[2 * n]
    return outs[:n], outs[n:]


def _proj_t_body(x_ref, xs_ref, w_ref, o_ref, ob_ref, os_ref, wt_sc):
    @pl.when(pl.program_id(0) == 0)
    def _():
        for c in range(w_ref.shape[1] // LANES):
            cols = slice(c * LANES, (c + 1) * LANES)
            wt_sc[cols, :] = w_ref[:, cols].T.astype(BF16)
        os_ref[...] = lax.dot_general(xs_ref[...].astype(BF16), wt_sc[...], (((1,), (1,)), ((), ())),
                                      preferred_element_type=F32)

    kt = lax.dot_general(wt_sc[...], x_ref[...], (((1,), (1,)), ((), ())),
                         preferred_element_type=F32)
    o_ref[...] = kt
    ob_ref[...] = kt.astype(BF16)


def _proj_transposed(x, xs, w, c0, *, n_cols, batch, seq, tm, name):
    k = x.shape[1]
    ms = xs.shape[0]
    per_b = seq // tm
    out_spec = pl.BlockSpec((None, n_cols, tm), lambda i: (i // per_b, 0, i % per_b))
    return pl.pallas_call(
        _proj_t_body, grid=(batch * per_b,),
        in_specs=[pl.BlockSpec((tm, k), lambda i: (i, 0)),
                  pl.BlockSpec((ms, k), lambda i: (0, 0)),
                  pl.BlockSpec((k, n_cols), lambda i, off=c0 // n_cols: (0, off),
                               pipeline_mode=pl.Buffered(1))],
        out_specs=[out_spec, out_spec, pl.BlockSpec((ms, n_cols), lambda i: (0, 0))],
        out_shape=[jax.ShapeDtypeStruct((batch, n_cols, seq), F32),
                   jax.ShapeDtypeStruct((batch, n_cols, seq), BF16),
                   jax.ShapeDtypeStruct((ms, n_cols), F32)],
        scratch_shapes=[pltpu.VMEM((n_cols, k), BF16)],
        compiler_params=_params(("arbitrary",)), name=name)(x, xs, w)


def _ln(x, g, b):
    mu = jnp.mean(x, axis=-1, keepdims=True)
    xc = x - mu
    var = jnp.mean(xc * xc, axis=-1, keepdims=True)
    return xc * lax.rsqrt(var + LN_EPS) * g + b


def _ln_body(x_ref, g_ref, b_ref, o_ref):
    o_ref[...] = _ln(x_ref[...], g_ref[...], b_ref[...])


def _layer_norm(x, g, b, *, tm, name):
    m, d = x.shape
    tm = min(tm, m)
    return pl.pallas_call(
        _ln_body, grid=(m // tm,),
        in_specs=[pl.BlockSpec((tm, d), lambda i: (i, 0)),
                  pl.BlockSpec((1, d), lambda i: (0, 0)),
                  pl.BlockSpec((1, d), lambda i: (0, 0))],
        out_specs=pl.BlockSpec((tm, d), lambda i: (i, 0)),
        out_shape=jax.ShapeDtypeStruct((m, d), F32),
        compiler_params=_params(("arbitrary",)), name=name)(x, g.reshape(1, d), b.reshape(1, d))


def _rel_bucket(n):
    n = jnp.maximum(n, 0)
    nf = jnp.maximum(n, 1).astype(F32)
    large = MAX_EXACT + jnp.floor(jnp.log(nf / MAX_EXACT) / math.log(MAX_DISTANCE / MAX_EXACT)
                                  * (NUM_BUCKETS - MAX_EXACT)).astype(jnp.int32)
    large = jnp.minimum(large, NUM_BUCKETS - 1)
    return jnp.where(n < MAX_EXACT, n, large)


def _bucket_lookup(bucket, table_fn):
    out = jnp.zeros(jnp.broadcast_shapes(bucket.shape, table_fn(0).shape), F32)
    for b in range(NUM_BUCKETS):
        out = out + jnp.where(bucket == b, table_fn(b), 0.0)
    return out


def _bias_tile_body(rb_ref, bucket_ref, o_ref):
    h = pl.program_id(0)
    o_ref[...] = _bucket_lookup(bucket_ref[...], lambda bk: rb_ref[bk, h])


def _bias_tiles(rel_bias, t):
    r = jnp.arange(t, dtype=jnp.int32)
    c = jnp.arange(2 * t, dtype=jnp.int32)
    buckets = _rel_bucket(r[:, None] + t - c[None, :])
    return pl.pallas_call(
        _bias_tile_body, grid=(N_HEADS,),
        in_specs=[pl.BlockSpec(memory_space=pltpu.SMEM), pl.BlockSpec((t, 2 * t), lambda h: (0, 0))],
        out_specs=pl.BlockSpec((None, t, 2 * t), lambda h: (h, 0, 0)),
        out_shape=jax.ShapeDtypeStruct((N_HEADS, t, 2 * t), F32),
        compiler_params=_params(("arbitrary",)), name="bias_tiles")(rel_bias, buckets)


def _attn_body(lam_ref, rb_ref, q_ref, k_ref, v_ref, bias_ref, g_ref, o_ref, *scratch,
               t, first, count, heads, out_scale):
    for sub in range(count):
        i = first + sub
        _attn_block(lam_ref, rb_ref, q_ref, k_ref, v_ref, bias_ref, g_ref, o_ref, scratch, t=t,
                    n_far=max(i - 1, 0), near=min(i + 1, 2) * t, heads=heads, out_scale=out_scale,
                    row0=sub * t)


def _attn_block(lam_ref, rb_ref, q_ref, k_ref, v_ref, bias_ref, g_ref, o_ref, scratch,
                *, t, n_far, near, heads, out_scale, row0):
    rows_q = slice(row0, row0 + t)
    s_bufs, p_bufs, m_bufs, mf_bufs, l_bufs, a_bufs = (scratch[0:2], scratch[2:4], scratch[4:6],
                                                       scratch[6:8], scratch[8:10], scratch[10:12])
    hg = pl.program_id(0)
    far = n_far * t
    kb_w = ATTN_KEY_BLOCK
    n_kb = (far + near) // kb_w
    n_tiles = kb_w // LANES

    def fold(x, op):
        out = x[:, 0:LANES]
        for c in range(1, n_tiles):
            out = op(out, x[:, c * LANES:(c + 1) * LANES])
        return out

    def head_cols(hh):
        return slice(hh * V_DIM, (hh + 1) * V_DIM)

    def score_block(hh, kb):
        par = hh % 2
        q = q_ref[rows_q, head_cols(hh)] * (HEAD_DIM ** -0.5)
        lane = lax.broadcasted_iota(jnp.int32, q.shape, 1)
        zero = jnp.zeros_like(q)
        q2 = jnp.concatenate([jnp.where(lane < HEAD_DIM, q, zero), jnp.where(lane >= HEAD_DIM, q, zero)],
                             axis=0)
        cols = slice(kb * kb_w, (kb + 1) * kb_w)
        s = jnp.dot(q2, k_ref[head_cols(hh), cols], preferred_element_type=F32)
        bias_far = rb_ref[NUM_BUCKETS - 1, hg * heads + hh]
        if kb * kb_w < far:
            top = fold(s, jnp.maximum) + bias_far
        else:
            off = kb * kb_w - far
            lo = 2 * t - near + off
            bias = bias_ref[hh, :, lo:lo + kb_w]
            row = lax.broadcasted_iota(jnp.int32, (t, kb_w), 0)
            col = lax.broadcasted_iota(jnp.int32, (t, kb_w), 1)
            keep = col + (off - (near - t)) <= row
            s = jnp.where(jnp.concatenate([keep, keep], axis=0),
                          s + jnp.concatenate([bias, bias], axis=0), NEG_INF)
            top = fold(s, jnp.maximum)
        s_bufs[par][:, cols] = s
        if kb == 0:
            m_bufs[par][...] = top
        else:
            m_bufs[par][...] = jnp.maximum(m_bufs[par][...], top)
        if kb == n_kb - 1:
            m = jnp.broadcast_to(jnp.max(m_bufs[par][...], axis=1, keepdims=True), m_bufs[par].shape)
            m_bufs[par][...] = m
            mf_bufs[par][...] = m - bias_far

    def exp_block(hh, kb):
        par = hh % 2
        cols = slice(kb * kb_w, (kb + 1) * kb_w)
        m = (mf_bufs if kb * kb_w < far else m_bufs)[par][...]
        p = jnp.exp(s_bufs[par][:, cols] - jnp.concatenate([m] * n_tiles, axis=1))
        if kb == 0:
            l_bufs[par][...] = fold(p, jnp.add)
        else:
            l_bufs[par][...] += fold(p, jnp.add)
        p_bufs[par][:, cols] = p.astype(BF16)

    def value_block(hh, kb):
        par = hh % 2
        rows = slice(kb * kb_w, (kb + 1) * kb_w)
        pv = jnp.dot(p_bufs[par][:, rows], v_ref[rows, head_cols(hh)], preferred_element_type=F32)
        if kb == 0:
            a_bufs[par][...] = pv
        else:
            a_bufs[par][...] += pv
        if kb == n_kb - 1:
            nrm = a_bufs[par][...] / jnp.sum(l_bufs[par][...], axis=1, keepdims=True)
            o = nrm[0:t] - lam_ref[0] * nrm[t:2 * t]
            ms = jnp.mean(o * o, axis=-1, keepdims=True)
            o_ref[rows_q, head_cols(hh)] = (o * lax.rsqrt(ms + LN_EPS) * g_ref[...] * out_scale
                                            ).astype(o_ref.dtype)

    for stage in range(heads + 2):
        for kb in range(n_kb):
            if stage < heads:
                score_block(stage, kb)
            if 0 <= stage - 1 < heads:
                exp_block(stage - 1, kb)
            if 0 <= stage - 2 < heads:
                value_block(stage - 2, kb)


def _prompt_attention(q, k, v, rel_bias, lam, subln_g, *, batch, seq, t, out_scale):
    assert t >= MAX_DISTANCE
    nq = seq // t
    width = N_HEADS * V_DIM
    q3 = q.reshape(batch, seq, width)
    v3 = v.reshape(batch, seq, width)
    bias = _bias_tiles(rel_bias, t)
    smem = pl.BlockSpec(memory_space=pltpu.SMEM)
    qo = q3
    group = ATTN_BLOCKS_PER_CALL
    hp = ATTN_HEADS_PER_STEP
    for first in range(0, nq, group):
        keys = (first + group) * t
        body = functools.partial(_attn_body, t=t, first=first, count=group, heads=hp,
                                 out_scale=out_scale)
        pair = lambda shape, dtype: [pltpu.VMEM(shape, dtype)] * 2
        rows_spec = pl.BlockSpec((None, group * t, hp * V_DIM),
                                 lambda h, b, p=first // group: (b, p, h))
        qo = pl.pallas_call(
            body, grid=(N_HEADS // hp, batch),
            in_specs=[smem, smem, rows_spec,
                      pl.BlockSpec((None, hp * V_DIM, keys), lambda h, b: (b, h, 0)),
                      pl.BlockSpec((None, keys, hp * V_DIM), lambda h, b: (b, 0, h)),
                      pl.BlockSpec((hp, t, 2 * t), lambda h, b: (h, 0, 0)),
                      pl.BlockSpec((1, V_DIM), lambda h, b: (0, 0))],
            out_specs=rows_spec,
            out_shape=jax.ShapeDtypeStruct((batch, seq, width), BF16),
            input_output_aliases={2: 0},
            scratch_shapes=(pair((2 * t, keys), F32) + pair((2 * t, keys), BF16)
                            + pair((2 * t, LANES), F32) + pair((2 * t, LANES), F32)
                            + pair((2 * t, LANES), F32) + pair((2 * t, V_DIM), F32)),
            compiler_params=_params(("arbitrary", "arbitrary")),
            name=f"prompt_attention_q{first}")(lam, rel_bias, qo, k, v3, bias, subln_g.reshape(1, V_DIM))
    return qo.reshape(batch * seq, width)


def _decode_body(pt_ref, lam_ref, q_ref, kn_ref, vn_ref, rbt_ref, bucket_ref, g_ref, *rest,
                 pages, page, n_steps, out_scale):
    k_refs = rest[:pages]
    v_refs = rest[pages:2 * pages]
    o_ref = rest[2 * pages]
    qexp_sc, bias_sc, m_sc, l_sc, acc_sc = rest[2 * pages + 1:]
    step = pl.program_id(1)
    rows = 2 * N_HEADS
    width = N_HEADS * V_DIM

    @pl.when(step == 0)
    def _():
        row = lax.broadcasted_iota(jnp.int32, (rows, width), 0)
        col = lax.broadcasted_iota(jnp.int32, (rows, width), 1)
        own_qk = (col // HEAD_DIM) == (row % N_HEADS) * 2 + row // N_HEADS
        q = (q_ref[...] * (HEAD_DIM ** -0.5)).astype(BF16).astype(F32)
        qexp = jnp.where(own_qk, jnp.broadcast_to(q, (rows, width)), 0.0)
        qexp_sc[...] = qexp.astype(BF16)
        bias_sc[...] = _bucket_lookup(bucket_ref[...], lambda bk: rbt_ref[:, bk:bk + 1])
        kn = kn_ref[...].astype(BF16).astype(F32)
        s_self = jnp.sum(qexp * kn, axis=1, keepdims=True) + rbt_ref[:, 0:1]
        m_sc[...] = s_self
        l_sc[...] = jnp.ones(l_sc.shape, F32)
        vn = vn_ref[...].astype(BF16).astype(F32)
        acc_sc[...] = jnp.broadcast_to(vn[:, None, :], acc_sc.shape)

    qexp = qexp_sc[...]
    bias_far = rbt_ref[:, NUM_BUCKETS - 1:NUM_BUCKETS]
    is_last = step == n_steps - 1
    s_parts = []
    for p in range(pages):
        s = jnp.dot(qexp, k_refs[p][...].astype(BF16), preferred_element_type=F32)
        if p == pages - 1:
            s = s + jnp.where(is_last, bias_sc[...], bias_far)
        else:
            s = s + bias_far
        s_parts.append(s)
    m_old = m_sc[...]
    m_new = m_old
    for s in s_parts:
        m_new = jnp.maximum(m_new, jnp.max(s, axis=1, keepdims=True))
    a = jnp.exp(m_old - m_new)
    l_new = a * l_sc[...]
    probs = []
    for p in range(pages):
        pr = jnp.exp(s_parts[p] - m_new)
        l_new = l_new + jnp.sum(pr, axis=1, keepdims=True)
        probs.append(pr.astype(BF16))
    probs = jnp.concatenate(probs, axis=1)
    m_sc[...] = m_new
    l_sc[...] = l_new
    for h in range(N_HEADS):
        vh = jnp.concatenate([v_refs[p][pl.ds(h, page, stride=N_HEADS), :].astype(BF16)
                              for p in range(pages)], axis=0)
        acc_sc[h] = a * acc_sc[h] + jnp.dot(probs, vh, preferred_element_type=F32)

    @pl.when(is_last)
    def _():
        nrm = acc_sc[...] / l_new[None]
        r = lax.broadcasted_iota(jnp.int32, nrm.shape, 1)
        hh = lax.broadcasted_iota(jnp.int32, nrm.shape, 0)
        coef = jnp.where(r == hh, 1.0, jnp.where(r == hh + N_HEADS, -lam_ref[0], 0.0))
        d = jnp.sum(coef * nrm, axis=1)
        ms = jnp.mean(d * d, axis=-1, keepdims=True)
        o_ref[...] = (d * lax.rsqrt(ms + LN_EPS) * g_ref[...] * out_scale).astype(o_ref.dtype)


def _sample_attention(q, k_new, v_new, cache_k, cache_v, page_table, rel_bias, lam, subln_g,
                      *, pages, out_scale):
    bs, width = q.shape
    page = cache_k.shape[2]
    n_pages = page_table.shape[1]
    n_steps = n_pages // pages
    past = n_pages * page
    kpos = past - page + jnp.arange(page, dtype=jnp.int32)
    bucket_last = _rel_bucket(past - kpos).reshape(1, page)
    rbt = jnp.tile(rel_bias.T, (2, 1))

    def tok_spec():
        return pl.BlockSpec((None, 1, width), lambda b, s, pt: (b, 0, 0))

    def page_spec(shape, p):
        return pl.BlockSpec((None,) + shape, lambda b, s, pt, p=p: (pt[b, s * pages + p], 0, 0))

    full = lambda shape: pl.BlockSpec(shape, lambda b, s, pt: tuple(0 for _ in shape))
    head_spec = pl.BlockSpec((None, N_HEADS, V_DIM), lambda b, s, pt: (b, 0, 0))
    in_specs = ([pl.BlockSpec(memory_space=pltpu.SMEM), tok_spec(), tok_spec(), head_spec,
                 full((2 * N_HEADS, NUM_BUCKETS)), full((1, page)), full((1, V_DIM))]
                + [page_spec((width, page), p) for p in range(pages)]
                + [page_spec((page * N_HEADS, V_DIM), p) for p in range(pages)])
    rows = 2 * N_HEADS
    body = functools.partial(_decode_body, pages=pages, page=page, n_steps=n_steps,
                             out_scale=out_scale)
    out = pl.pallas_call(
        body,
        grid_spec=pltpu.PrefetchScalarGridSpec(
            num_scalar_prefetch=1, grid=(bs, n_steps), in_specs=in_specs,
            out_specs=head_spec,
            scratch_shapes=[pltpu.VMEM((rows, width), BF16), pltpu.VMEM((rows, page), F32),
                            pltpu.VMEM((rows, 1), F32), pltpu.VMEM((rows, 1), F32),
                            pltpu.VMEM((N_HEADS, rows, V_DIM), F32)]),
        out_shape=jax.ShapeDtypeStruct((bs, N_HEADS, V_DIM), BF16),
        compiler_params=_params(("arbitrary", "arbitrary")),
        name="sample_attention")(
            page_table, lam, q.reshape(bs, 1, width), k_new.reshape(bs, 1, width),
            v_new.reshape(bs, N_HEADS, V_DIM), rbt, bucket_last, subln_g.reshape(1, V_DIM),
            *([cache_k] * pages), *([cache_v] * pages))
    return out.reshape(bs, width)


def _s5_params(a_re, a_im, log_dt, b_re, b_im, c_re, c_im):
    dt = jnp.exp(log_dt)[:, None]
    mag = jnp.exp(a_re * dt)
    ang = a_im * dt
    abar_re = mag * jnp.cos(ang)
    abar_im = mag * jnp.sin(ang)
    den = a_re * a_re + a_im * a_im
    f_re = ((abar_re - 1.0) * a_re + abar_im * a_im) / den
    f_im = (abar_im * a_re - (abar_re - 1.0) * a_im) / den
    bb_re = f_re[..., None] * b_re - f_im[..., None] * b_im
    bb_im = f_re[..., None] * b_im + f_im[..., None] * b_re
    eye = jnp.eye(SLAB_GROUPS, dtype=F32)

    def in_slabs(bb):
        tt = bb.reshape(N_SLABS, SLAB_GROUPS, STATE_DIM, SSM_GROUP).transpose(0, 1, 3, 2)
        full = tt[:, :, :, None, :] * eye[None, :, None, :, None]
        return full.reshape(N_SLABS, SLAB_IN, SLAB_STATE).astype(BF16)

    ab_re = abar_re[..., None] * bb_re - abar_im[..., None] * bb_im
    ab_im = abar_re[..., None] * bb_im + abar_im[..., None] * bb_re

    def out_slabs(cc):
        tt = cc.reshape(N_SLABS, SLAB_GROUPS, SSM_GROUP, STATE_DIM).transpose(0, 1, 3, 2)
        full = tt[:, :, :, None, :] * eye[None, :, None, :, None]
        return full.reshape(N_SLABS, SLAB_STATE, SLAB_IN).astype(BF16)

    return (abar_re.reshape(N_SLABS, SLAB_STATE), abar_im.reshape(N_SLABS, SLAB_STATE),
            in_slabs(bb_re), in_slabs(bb_im), out_slabs(c_re), out_slabs(c_im),
            in_slabs(ab_re), in_slabs(ab_im))


def _glu_out(y, u, d_ref, wglu_ref):
    g = jax.nn.gelu(y + d_ref[...] * u)
    gate = jnp.dot(g.astype(BF16), wglu_ref[...].astype(BF16), preferred_element_type=F32)
    return g * jax.nn.sigmoid(gate)


def _s5_prompt_body(u_ref, ar_ref, ai_ref, bbr_ref, bbi_ref, cr_ref, ci_ref, d_ref, wglu_ref,
                    o_ref, sre_ref, sim_ref, xr_sc, xi_sc, pw_sc, cr_sc, ci_sc, perm_sc, *, batch):
    c = pl.program_id(0)
    sub = SUBLANES
    _, tc, width = u_ref.shape
    n_rows = batch * tc
    t = lax.broadcasted_iota(jnp.int32, (sub, SLAB_STATE), 0)
    first = t < batch

    def cmul(pr, pi, qr, qi):
        return pr * qr - pi * qi, pr * qi + pi * qr

    @pl.when(c == 0)
    def _():
        for j in range(N_SLABS):
            a = (jnp.broadcast_to(ar_ref[j:j + 1, :], t.shape), jnp.broadcast_to(ai_ref[j:j + 1, :], t.shape))
            a2 = cmul(*a, *a)
            for part in range(2):
                pw_sc[j, part] = jnp.where(first, a[part], a2[part])
        cr_sc[...] = jnp.zeros(cr_sc.shape, F32)
        ci_sc[...] = jnp.zeros(ci_sc.shape, F32)
        r = lax.broadcasted_iota(jnp.int32, (n_rows, n_rows), 0)
        k = lax.broadcasted_iota(jnp.int32, (n_rows, n_rows), 1)
        perm_sc[0] = jnp.where(k == (r % batch) * tc + r // batch, 1.0, 0.0).astype(BF16)
        perm_sc[1] = jnp.where(r == (k % batch) * tc + k // batch, 1.0, 0.0).astype(BF16)

    def permute(which, x):
        return jnp.dot(perm_sc[which], x, preferred_element_type=F32)

    u_seq = u_ref[...].reshape(n_rows, width)
    hi = u_seq.astype(BF16)
    rest = u_seq - hi.astype(F32)
    mid = rest.astype(BF16)
    lo = (rest - mid.astype(F32)).astype(BF16)
    u_hi = permute(0, hi)
    ub = u_hi.astype(BF16)
    u = u_hi + permute(0, mid) + permute(0, lo)
    row = lax.broadcasted_iota(jnp.int32, u_hi.shape, 0)
    u_before = jnp.where(row % sub >= batch, pltpu.roll(u_hi, batch, 0), 0.0).astype(BF16)
    for j in range(N_SLABS):
        cols = slice(j * SLAB_IN, (j + 1) * SLAB_IN)
        uj = jnp.concatenate([ub[:, cols], u_before[:, cols]], axis=1)
        xr_sc[j] = jnp.dot(uj, bbr_ref[j], preferred_element_type=F32)
        xi_sc[j] = jnp.dot(uj, bbi_ref[j], preferred_element_type=F32)

    for j in range(N_SLABS):
        pr, pi = cr_sc[j], ci_sc[j]
        for v in range(n_rows // sub):
            rows = slice(v * sub, (v + 1) * sub)
            lr = jnp.where(first, pltpu.roll(pr, batch, 0), pr)
            li = jnp.where(first, pltpu.roll(pi, batch, 0), pi)
            er, ei = cmul(pw_sc[j, 0], pw_sc[j, 1], lr, li)
            pr, pi = xr_sc[j, rows, :] + er, xi_sc[j, rows, :] + ei
            xr_sc[j, rows, :] = pr
            xi_sc[j, rows, :] = pi
        cr_sc[j] = pr
        ci_sc[j] = pi
    sre_ref[...] = cr_sc[...]
    sim_ref[...] = ci_sc[...]

    ys = []
    for j in range(N_SLABS):
        ys.append(jnp.dot(xr_sc[j].astype(BF16), cr_ref[j], preferred_element_type=F32)
                  - jnp.dot(xi_sc[j].astype(BF16), ci_ref[j], preferred_element_type=F32))
    out = _glu_out(jnp.concatenate(ys, axis=1), u, d_ref, wglu_ref).astype(BF16)
    o_ref[...] = permute(1, out).astype(o_ref.dtype).reshape(batch, tc, width)


def _s5_prompt(u, sp, d, w_glu, *, batch, seq, tc):
    assert SUBLANES == 2 * batch, "tile = two positions of every sequence"
    ar, ai, bbr, bbi, cr, ci, abr, abi = sp
    bbr = jnp.concatenate([bbr, abr], axis=1)
    bbi = jnp.concatenate([bbi, abi], axis=1)
    width = u.shape[1]
    rows = batch * tc
    const = lambda shape: pl.BlockSpec(shape, lambda c: tuple(0 for _ in shape))
    block = pl.BlockSpec((batch, tc, width), lambda c: (0, c, 0))
    state_spec = const((N_SLABS, SUBLANES, SLAB_STATE))
    state_shape = jax.ShapeDtypeStruct((N_SLABS, SUBLANES, SLAB_STATE), F32)
    out, s_re, s_im = pl.pallas_call(
        functools.partial(_s5_prompt_body, batch=batch), grid=(seq // tc,),
        in_specs=[block, const(ar.shape), const(ai.shape), const(bbr.shape), const(bbi.shape),
                  const(cr.shape), const(ci.shape), const((1, width)), const(w_glu.shape)],
        out_specs=[block, state_spec, state_spec],
        out_shape=[jax.ShapeDtypeStruct((batch, seq, width), BF16), state_shape, state_shape],
        scratch_shapes=[pltpu.VMEM((N_SLABS, rows, SLAB_STATE), F32),
                        pltpu.VMEM((N_SLABS, rows, SLAB_STATE), F32),
                        pltpu.VMEM((N_SLABS, 2, SUBLANES, SLAB_STATE), F32),
                        pltpu.VMEM((N_SLABS, SUBLANES, SLAB_STATE), F32),
                        pltpu.VMEM((N_SLABS, SUBLANES, SLAB_STATE), F32),
                        pltpu.VMEM((2, rows, rows), BF16)],
        compiler_params=_params(("arbitrary",)),
        name="s5_prompt")(u.reshape(batch, seq, width), ar, ai, bbr, bbi, cr, ci,
                          d.reshape(1, width), w_glu)
    return out.reshape(batch * seq, width), s_re, s_im


def _s5_sample_body(u_ref, x0r_ref, x0i_ref, ar_ref, ai_ref, bbr_ref, bbi_ref, cr_ref, ci_ref,
                    d_ref, wglu_ref, o_ref, sre_ref, sim_ref):
    u = u_ref[...]
    ub = u.astype(BF16)
    ys = []
    for j in range(N_SLABS):
        uj = ub[:, j * SLAB_IN:(j + 1) * SLAB_IN]
        ar = ar_ref[j:j + 1, :]
        ai = ai_ref[j:j + 1, :]
        x0r = x0r_ref[j]
        x0i = x0i_ref[j]
        xr = ar * x0r - ai * x0i + jnp.dot(uj, bbr_ref[j], preferred_element_type=F32)
        xi = ar * x0i + ai * x0r + jnp.dot(uj, bbi_ref[j], preferred_element_type=F32)
        sre_ref[j] = xr
        sim_ref[j] = xi
        ys.append(jnp.dot(xr.astype(BF16), cr_ref[j], preferred_element_type=F32)
                  - jnp.dot(xi.astype(BF16), ci_ref[j], preferred_element_type=F32))
    o_ref[...] = _glu_out(jnp.concatenate(ys, axis=1), u, d_ref, wglu_ref).astype(o_ref.dtype)


def _s5_sample(u, x0_re, x0_im, sp, d, w_glu):
    ar, ai, bbr, bbi, cr, ci = sp[:6]
    bs, width = u.shape
    state_shape = jax.ShapeDtypeStruct((N_SLABS, bs, SLAB_STATE), F32)
    return pl.pallas_call(
        _s5_sample_body,
        out_shape=[jax.ShapeDtypeStruct((bs, width), BF16), state_shape, state_shape],
        compiler_params=pltpu.CompilerParams(vmem_limit_bytes=VMEM_LIMIT_BYTES),
        name="s5_sample")(u, x0_re, x0_im, ar, ai, bbr, bbi, cr, ci, d.reshape(1, width), w_glu)


def _gated(c):
    gate = c[:, :LANES]
    return gate * jax.nn.sigmoid(gate) * c[:, LANES:]


def _up_prompt_body(h_ref, wg_ref, wv_ref, cwg_ref, cwv_ref, cbg_ref, cbv_ref,
                    act_ref, cg_ref, cv_ref, w_sc, *, rows):
    w_sc[:, 0:LANES] = wg_ref[...].astype(BF16)
    w_sc[:, LANES:2 * LANES] = wv_ref[...].astype(BF16)
    cw = jnp.concatenate([cwg_ref[...], cwv_ref[...]], axis=1)
    cb = jnp.concatenate([cbg_ref[...], cbv_ref[...]], axis=1)
    seq = h_ref.shape[0]
    sub = SUBLANES
    row8 = lax.broadcasted_iota(jnp.int32, (sub, 2 * LANES), 0)

    prev = jnp.zeros((sub, 2 * LANES), F32)
    for c in range(seq // rows):
        up = jnp.dot(h_ref[c * rows:(c + 1) * rows, :], w_sc[...], preferred_element_type=F32)
        cur = cb + cw[CONV_W - 1:CONV_W] * up
        for back in range(1, CONV_W):
            rolled = pltpu.roll(up, back, 0)
            head = jnp.where(row8 < back, pltpu.roll(prev, back, 0), rolled[0:sub])
            shifted = jnp.concatenate([head, rolled[sub:]], axis=0)
            cur = cur + cw[CONV_W - 1 - back:CONV_W - back] * shifted
        act_ref[c * rows:(c + 1) * rows, :] = _gated(cur).astype(act_ref.dtype)
        prev = up[rows - sub:, :]
    tail = prev[sub - (CONV_W - 1):, :]
    cg_ref[...] = tail[:, :LANES]
    cv_ref[...] = tail[:, LANES:]


def _up_prompt(hb, w_up, conv_w, conv_b, *, batch, seq):
    d_model = hb.shape[1]
    d_ff = w_up.shape[1] // 2
    nb = d_ff // LANES
    cb = conv_b.reshape(1, 2 * d_ff)
    col = lambda off: (lambda b, j: (0, j + off))
    tail_spec = pl.BlockSpec((None, CONV_W - 1, LANES), lambda b, j: (b, 0, j))
    tail_shape = jax.ShapeDtypeStruct((batch, CONV_W - 1, d_ff), F32)
    return pl.pallas_call(
        functools.partial(_up_prompt_body, rows=UP_ROW_CHUNK), grid=(batch, nb),
        scratch_shapes=[pltpu.VMEM((d_model, 2 * LANES), BF16)],
        in_specs=[pl.BlockSpec((seq, d_model), lambda b, j: (b, 0)),
                  pl.BlockSpec((d_model, LANES), col(0)), pl.BlockSpec((d_model, LANES), col(nb)),
                  pl.BlockSpec((CONV_W, LANES), col(0)), pl.BlockSpec((CONV_W, LANES), col(nb)),
                  pl.BlockSpec((1, LANES), col(0)), pl.BlockSpec((1, LANES), col(nb))],
        out_specs=[pl.BlockSpec((seq, LANES), lambda b, j: (b, j)), tail_spec, tail_spec],
        out_shape=[jax.ShapeDtypeStruct((batch * seq, d_ff), BF16), tail_shape, tail_shape],
        compiler_params=_params(("arbitrary", "arbitrary")),
        name="up_prompt")(hb, w_up, w_up, conv_w, conv_w, cb, cb)


def _up_sample_body(h_ref, wg_ref, wva_ref, wvb_ref, cwg_ref, cwva_ref, cwvb_ref, cbg_ref, cbva_ref,
                    cbvb_ref, sg_ref, sva_ref, svb_ref, act_ref, cg_ref, cv_ref):
    w2 = 2 * LANES
    w = jnp.concatenate([wg_ref[...].astype(BF16), wva_ref[...].astype(BF16),
                         wvb_ref[...].astype(BF16)], axis=1)
    up = jnp.dot(h_ref[...], w, preferred_element_type=F32)
    cw = jnp.concatenate([cwg_ref[...], cwva_ref[...], cwvb_ref[...]], axis=1)
    cb = jnp.concatenate([cbg_ref[...], cbva_ref[...], cbvb_ref[...]], axis=1)
    c = cb + cw[CONV_W - 1:CONV_W] * up
    for tap in range(CONV_W - 1):
        st = jnp.concatenate([sg_ref[tap], sva_ref[tap], svb_ref[tap]], axis=1)
        c = c + cw[tap:tap + 1] * st
    gate = c[:, 0:w2]
    act_ref[...] = (gate * jax.nn.sigmoid(gate) * c[:, w2:2 * w2]).astype(act_ref.dtype)
    for tap in range(1, CONV_W - 1):
        cg_ref[tap - 1] = sg_ref[tap]
        cv_ref[tap - 1] = jnp.concatenate([sva_ref[tap], svb_ref[tap]], axis=1)
    cg_ref[CONV_W - 2] = up[:, 0:w2]
    cv_ref[CONV_W - 2] = up[:, w2:2 * w2]


def _up_sample(hb, w_up, conv_w, conv_b, state):
    bs, d_model = hb.shape
    d_ff = w_up.shape[1] // 2
    nb = d_ff // LANES
    n_pairs = (nb + 1) // 2
    last = 2 * nb - 1
    w2 = 2 * LANES
    cb = conv_b.reshape(1, 2 * d_ff)
    gate = lambda shape: pl.BlockSpec(shape, lambda j: (0,) * (len(shape) - 1) + (j,))
    val = lambda shape, k: pl.BlockSpec(
        shape, lambda j: (0,) * (len(shape) - 1) + (jnp.minimum(nb + 2 * j + k, last),))
    operands = []
    specs = [pl.BlockSpec((bs, d_model), lambda j: (0, 0))]
    for arr, lead in ((w_up, (d_model,)), (conv_w, (CONV_W,)), (cb, (1,)), (state, (CONV_W - 1, bs))):
        specs += [gate(lead + (w2,)), val(lead + (LANES,), 0), val(lead + (LANES,), 1)]
        operands += [arr, arr, arr]
    tail_spec = pl.BlockSpec((CONV_W - 1, bs, w2), lambda j: (0, 0, j))
    tail_shape = jax.ShapeDtypeStruct((CONV_W - 1, bs, d_ff), F32)
    return pl.pallas_call(
        _up_sample_body, grid=(n_pairs,), in_specs=specs,
        out_specs=[pl.BlockSpec((bs, w2), lambda j: (0, j)), tail_spec, tail_spec],
        out_shape=[jax.ShapeDtypeStruct((bs, d_ff), BF16), tail_shape, tail_shape],
        compiler_params=_params(("arbitrary",)),
        name="up_sample")(hb, *operands)


def _one(x):
    return (x,)


def _sigmoid_out(acc):
    return (jax.nn.sigmoid(acc),)


def _both(acc):
    return (acc, acc)


def _merge(pa, ps, ga, gs):
    return (ga * pa + gs * ps,)


def _layer(xp, xs, w, *, batch, seq, alpha, attend_p, attend_s, ssm_p, ssm_s, up_p, up_s):
    d_model = xp.shape[1]
    qk_w = N_HEADS * 2 * HEAD_DIM
    v_w = N_HEADS * V_DIM
    ssm_w = d_model // 2
    w_in = w["w_in"]
    g1, b1 = w["ln1_g"].reshape(1, d_model), w["ln1_b"].reshape(1, d_model)
    big = dict(tm=MM_ROWS, tn=MM_COLS)
    c = 0
    (q,), (q_s,), xb = _matmul([(xp, xs, w_in, c)], [], [BF16], _one, n_cols=qk_w, name="proj_q",
                               emit_lhs=True, **big)
    c += qk_w
    kt, ktb, k_s = _proj_transposed(xb, xs, w_in, c, n_cols=qk_w, batch=batch, seq=seq, tm=MM_ROWS,
                                    name="proj_kt")
    c += qk_w
    c_v = c
    c += v_w
    (u,), (u_s,) = _matmul([(xb, xs, w_in, c)], [], [F32], _one, n_cols=ssm_w, name="proj_u", **big)
    c += ssm_w
    wide = dict(tm=MM_ROWS, tn=d_model)
    (ga,), (ga_s,) = _matmul([(xb, xs, w_in, c)], [], [F32], _sigmoid_out, n_cols=d_model, name="gate_a", **wide)
    c += d_model
    (gs,), (gs_s,) = _matmul([(xb, xs, w_in, c)], [], [F32], _sigmoid_out, n_cols=d_model, name="gate_s", **wide)
    (v, vb), (v_s, _) = _matmul([(xb, xs, w_in, c_v)], [], [F32, BF16], _both, n_cols=v_w, name="proj_v", **big)

    attn = attend_p(q, ktb, vb)
    ssm_out, re_p, im_p = ssm_p(u)
    ssm_out_s, re_s, im_s = ssm_s(u_s)
    attn_s = attend_s(q_s, k_s, v_s)

    (merged,), (merged_s,) = _matmul(
        [(attn, attn_s, w["w_proj_attn"], 0), (ssm_out, ssm_out_s, w["w_proj_ssm"], 0)],
        [(ga, ga_s), (gs, gs_s)], [BF16], _merge, n_cols=d_model, tm=LN_MM_ROWS, tn=d_model, name="merge")

    def post_ln1(acc, res, g, b):
        h = _ln(alpha * res + acc, g, b)
        return h, h

    (h, hb), (h_s, hb_s) = _matmul([(merged, merged_s, w["w_out"], 0)], [(xp, xs), g1, b1], [F32, BF16],
                                   post_ln1, n_cols=d_model, tm=LN_MM_ROWS, tn=d_model, name="out_proj_ln1")
    act, conv_p = up_p(hb)
    act_s, conv_s = up_s(hb_s)
    (r2,), (r2_s,) = _matmul([(act, act_s, w["w_down"], 0)], [(h, h_s)], [F32],
                             lambda acc, res: (alpha * res + acc,), n_cols=d_model,
                             tm=DOWN_ROWS, tn=DOWN_COLS, name="down_proj")
    y = _layer_norm(r2, w["ln2_g"], w["ln2_b"], tm=LN_ROWS, name="ln2")
    y_s = _layer_norm(r2_s, w["ln2_g"], w["ln2_b"], tm=LN_ROWS, name="ln2_sample")
    return (y, kt, v, re_p, im_p, conv_p), (y_s, k_s, v_s, re_s, im_s, conv_s)


def kernel(x_prompt, x_sample, cache_k, cache_v, state_ssm_re, state_ssm_im, state_conv, page_table, rel_bias, w_in, lambda_q1, lambda_k1, lambda_q2, lambda_k2, subln_g, ssm_a_re, ssm_a_im, ssm_log_dt, ssm_b_re, ssm_b_im, ssm_c_re, ssm_c_im, ssm_d, w_glu, w_proj_attn, w_proj_ssm, w_out, ln1_g, ln1_b, w_up, conv_w, conv_b, w_down, ln2_g, ln2_b):
    depth = w_in.shape[0]
    assert depth == 1, "single-layer trunk"
    bp, seq, d_model = x_prompt.shape
    bs, dec_seq, _ = x_sample.shape
    assert dec_seq == 1
    n_pool, page = cache_k.shape[1], cache_k.shape[2]
    d_ff = w_down.shape[1]
    n_groups = ssm_a_re.shape[1]
    assert n_groups == N_SLABS * SLAB_GROUPS and d_ff % LANES == 0
    alpha = (2.0 * depth) ** 0.25
    width = N_HEADS * V_DIM

    hp = x_prompt.reshape(bp * seq, d_model)
    hs = x_sample.reshape(bs, d_model)
    outs = {}
    for l in range(depth):
        lam_init = 0.8 - 0.6 * math.exp(-0.3 * l)
        out_scale = 1.0 - lam_init
        lam = (jnp.exp(jnp.sum(lambda_q1[l] * lambda_k1[l]))
               - jnp.exp(jnp.sum(lambda_q2[l] * lambda_k2[l])) + lam_init).reshape(1)
        w = dict(w_in=w_in[l], w_proj_attn=w_proj_attn[l], w_proj_ssm=w_proj_ssm[l], w_out=w_out[l],
                 ln1_g=ln1_g[l], ln1_b=ln1_b[l], w_down=w_down[l], ln2_g=ln2_g[l], ln2_b=ln2_b[l])
        sp = _s5_params(ssm_a_re[l], ssm_a_im[l], ssm_log_dt[l], ssm_b_re[l], ssm_b_im[l],
                        ssm_c_re[l], ssm_c_im[l])

        def attend_p(q, kt, v):
            return _prompt_attention(q, kt, v, rel_bias, lam, subln_g[l], batch=bp, seq=seq,
                                     t=ATTN_BLOCK, out_scale=out_scale)

        def ssm_p(u):
            o, sr, si = _s5_prompt(u, sp, ssm_d[l], w_glu[l], batch=bp, seq=seq, tc=S5_POSITIONS)
            last = lambda st: st[:, SUBLANES - bp:, :].transpose(1, 0, 2)
            return o, last(sr), last(si)

        def up_p(hb):
            act, cg, cv = _up_prompt(hb, w_up[l], conv_w[l], conv_b[l], batch=bp, seq=seq)
            return act, jnp.concatenate([cg, cv], axis=-1)

        ck = cache_k[l].transpose(0, 2, 3, 4, 1).reshape(n_pool, width, page)
        cv_ = cache_v[l].reshape(n_pool, page * N_HEADS, V_DIM)

        def attend_s(q, k, v):
            return _sample_attention(q.astype(F32), k, v, ck, cv_, page_table, rel_bias, lam,
                                     subln_g[l], pages=DECODE_PAGES, out_scale=out_scale)

        def ssm_s(u):
            x0r = state_ssm_re[l].reshape(bs, N_SLABS, SLAB_STATE).transpose(1, 0, 2)
            x0i = state_ssm_im[l].reshape(bs, N_SLABS, SLAB_STATE).transpose(1, 0, 2)
            o, sr, si = _s5_sample(u, x0r, x0i, sp, ssm_d[l], w_glu[l])
            return o, sr.transpose(1, 0, 2), si.transpose(1, 0, 2)

        def up_s(hb):
            act, cg, cv = _up_sample(hb, w_up[l], conv_w[l], conv_b[l],
                                     state_conv[l].transpose(1, 0, 2))
            return act, jnp.concatenate([cg, cv], axis=-1).transpose(1, 0, 2)

        (hp, kt_p, v_p, re_p, im_p, c_p), (hs, k_s, v_s, re_s, im_s, c_s) = _layer(
            hp, hs, w, batch=bp, seq=seq, alpha=alpha, attend_p=attend_p, attend_s=attend_s,
            ssm_p=ssm_p, ssm_s=ssm_s, up_p=up_p, up_s=up_s)
        k_p = kt_p.reshape(bp, N_HEADS, 2, HEAD_DIM, seq).transpose(0, 4, 1, 2, 3)

        for name, val in (("kp", k_p.reshape(bp, seq, N_HEADS, 2, HEAD_DIM)),
                          ("vp", v_p.reshape(bp, seq, N_HEADS, V_DIM)),
                          ("rep", re_p.reshape(bp, n_groups, STATE_DIM)),
                          ("imp", im_p.reshape(bp, n_groups, STATE_DIM)),
                          ("cp", c_p),
                          ("ks", k_s.reshape(bs, 1, N_HEADS, 2, HEAD_DIM)),
                          ("vs", v_s.reshape(bs, 1, N_HEADS, V_DIM)),
                          ("res", re_s.reshape(bs, n_groups, STATE_DIM)),
                          ("ims", im_s.reshape(bs, n_groups, STATE_DIM)),
                          ("cs", c_s)):
            outs.setdefault(name, []).append(val)

    st = {k: jnp.stack(v, axis=0) for k, v in outs.items()}
    return (hp.reshape(bp, seq, d_model), hs.reshape(bs, 1, d_model), st["kp"], st["vp"], st["rep"],
            st["imp"], st["cp"], st["ks"], st["vs"], st["res"], st["ims"], st["cs"])
```

```python
import functools
import math

import jax
import jax.numpy as jnp
from jax import lax
from jax.experimental import pallas as pl
from jax.experimental.pallas import tpu as pltpu

F32 = jnp.float32
BF16 = jnp.bfloat16

N_HEADS = 8
HEAD_DIM = 64
V_DIM = 2 * HEAD_DIM
SSM_GROUP = 16
STATE_DIM = 64
CONV_W = 3
NUM_BUCKETS = 32
MAX_EXACT = NUM_BUCKETS // 2
MAX_DISTANCE = 128
LN_EPS = 1e-5
NEG_INF = -1e30

VMEM_LIMIT_BYTES = 56 * 1024 * 1024
LANES = 128
SLAB_GROUPS = 8
N_SLABS = 8
SLAB_IN = SLAB_GROUPS * SSM_GROUP
SLAB_STATE = SLAB_GROUPS * STATE_DIM
SUBLANES = 8
MM_ROWS, MM_COLS = 1024, 1024
LN_MM_ROWS = 512
DOWN_ROWS, DOWN_COLS = 512, 512
LN_ROWS = 512
ATTN_BLOCK = 256
ATTN_KEY_BLOCK = 256
ATTN_HEADS_PER_STEP = 4
ATTN_BLOCKS_PER_CALL = 2
S5_POSITIONS = 64
DECODE_PAGES = 16
UP_ROW_CHUNK = 512


def _params(sem):
    return pltpu.CompilerParams(dimension_semantics=sem, vmem_limit_bytes=VMEM_LIMIT_BYTES)


def _mm_body(*refs, n_pairs, n_extra, n_out, epilogue, emit_lhs):
    it = iter(refs)
    take = lambda n: [next(it) for _ in range(n)]
    x_refs, xs_refs, w_refs = take(n_pairs), take(n_pairs), take(n_pairs)
    extra, extra_s = take(n_extra), take(n_extra)
    outs, outs_s = take(n_out), take(n_out)
    lhs_out = take(1) if emit_lhs else []
    wbf = take(n_pairs)

    def apply(lhs_refs, extra_refs, out_refs, keep):
        lhs = [x[...].astype(BF16) for x in lhs_refs]
        for ref in keep:
            ref[...] = lhs[0]
        accs = [jnp.dot(x, s[...], preferred_element_type=F32) for x, s in zip(lhs, wbf)]
        for o, r in zip(out_refs, epilogue(*accs, *[e[...] for e in extra_refs])):
            o[...] = r.astype(o.dtype)

    @pl.when(pl.program_id(1) == 0)
    def _():
        for w, s in zip(w_refs, wbf):
            s[...] = w[...].astype(BF16)
        apply(xs_refs, extra_s, outs_s, [])

    apply(x_refs, extra, outs, lhs_out)


def _matmul(pairs, extras, out_dtypes, epilogue, *, n_cols, tm, tn, name, emit_lhs=False):
    m = pairs[0][0].shape[0]
    ms = pairs[0][1].shape[0]
    grid = (n_cols // tn, m // tm)
    assert not emit_lhs or grid[0] == 1
    w_mode = dict(pipeline_mode=pl.Buffered(1)) if grid[0] == 1 else {}
    specs_x, specs_xs, specs_w, scratch = [], [], [], []
    for x, xs, w, c0 in pairs:
        k = x.shape[1]
        specs_x.append(pl.BlockSpec((tm, k), lambda j, i: (i, 0)))
        specs_xs.append(pl.BlockSpec((ms, k), lambda j, i: (0, 0)))
        specs_w.append(pl.BlockSpec((k, tn), lambda j, i, off=c0 // tn: (0, j + off), **w_mode))
        scratch.append(pltpu.VMEM((k, tn), BF16))
    specs_e, specs_es, args_e, args_es = [], [], [], []
    for e in extras:
        if isinstance(e, tuple):
            specs_e.append(pl.BlockSpec((tm, tn), lambda j, i: (i, j)))
            specs_es.append(pl.BlockSpec((ms, tn), lambda j, i: (0, j)))
            args_e.append(e[0])
            args_es.append(e[1])
        else:
            specs_e.append(pl.BlockSpec((1, tn), lambda j, i: (0, j)))
            specs_es.append(pl.BlockSpec((1, tn), lambda j, i: (0, j)))
            args_e.append(e)
            args_es.append(e)
    out_shape = ([jax.ShapeDtypeStruct((m, n_cols), d) for d in out_dtypes]
                 + [jax.ShapeDtypeStruct((ms, n_cols), d) for d in out_dtypes])
    out_specs = ([pl.BlockSpec((tm, tn), lambda j, i: (i, j)) for _ in out_dtypes]
                 + [pl.BlockSpec((ms, tn), lambda j, i: (0, j)) for _ in out_dtypes])
    if emit_lhs:
        k0 = pairs[0][0].shape[1]
        out_shape.append(jax.ShapeDtypeStruct((m, k0), BF16))
        out_specs.append(pl.BlockSpec((tm, k0), lambda j, i: (i, 0)))
    body = functools.partial(_mm_body, n_pairs=len(pairs), n_extra=len(extras),
                             n_out=len(out_dtypes), epilogue=epilogue, emit_lhs=emit_lhs)
    outs = pl.pallas_call(
        body, grid=grid, in_specs=specs_x + specs_xs + specs_w + specs_e + specs_es,
        out_specs=out_specs, out_shape=out_shape, scratch_shapes=scratch,
        compiler_params=_params(("arbitrary", "arbitrary")), name=name)(
            *[p[0] for p in pairs], *[p[1] for p in pairs], *[p[2] for p in pairs],
            *args_e, *args_es)
    n = len(out_dtypes)
    if emit_lhs:
        return outs[:n], outs[n:2 * n], outs[2 * n]
    return outs[:n], outs[n:]


def _proj_t_body(x_ref, xs_ref, w_ref, o_ref, ob_ref, os_ref, wt_sc):
    @pl.when(pl.program_id(0) == 0)
    def _():
        for c in range(w_ref.shape[1] // LANES):
            cols = slice(c * LANES, (c + 1) * LANES)
            wt_sc[cols, :] = w_ref[:, cols].T.astype(BF16)
        os_ref[...] = lax.dot_general(xs_ref[...].astype(BF16), wt_sc[...], (((1,), (1,)), ((), ())),
                                      preferred_element_type=F32)

    kt = lax.dot_general(wt_sc[...], x_ref[...], (((1,), (1,)), ((), ())),
                         preferred_element_type=F32)
    o_ref[...] = kt
    ob_ref[...] = kt.astype(BF16)


def _proj_transposed(x, xs, w, c0, *, n_cols, batch, seq, tm, name):
    k = x.shape[1]
    ms = xs.shape[0]
    per_b = seq // tm
    out_spec = pl.BlockSpec((None, n_cols, tm), lambda i: (i // per_b, 0, i % per_b))
    return pl.pallas_call(
        _proj_t_body, grid=(batch * per_b,),
        in_specs=[pl.BlockSpec((tm, k), lambda i: (i, 0)),
                  pl.BlockSpec((ms, k), lambda i: (0, 0)),
                  pl.BlockSpec((k, n_cols), lambda i, off=c0 // n_cols: (0, off),
                               pipeline_mode=pl.Buffered(1))],
        out_specs=[out_spec, out_spec, pl.BlockSpec((ms, n_cols), lambda i: (0, 0))],
        out_shape=[jax.ShapeDtypeStruct((batch, n_cols, seq), F32),
                   jax.ShapeDtypeStruct((batch, n_cols, seq), BF16),
                   jax.ShapeDtypeStruct((ms, n_cols), F32)],
        scratch_shapes=[pltpu.VMEM((n_cols, k), BF16)],
        compiler_params=_params(("arbitrary",)), name=name)(x, xs, w)


def _ln(x, g, b):
    mu = jnp.mean(x, axis=-1, keepdims=True)
    xc = x - mu
    var = jnp.mean(xc * xc, axis=-1, keepdims=True)
    return xc * lax.rsqrt(var + LN_EPS) * g + b


def _ln_body(x_ref, g_ref, b_ref, o_ref):
    o_ref[...] = _ln(x_ref[...], g_ref[...], b_ref[...])


def _layer_norm(x, g, b, *, tm, name):
    m, d = x.shape
    tm = min(tm, m)
    return pl.pallas_call(
        _ln_body, grid=(m // tm,),
        in_specs=[pl.BlockSpec((tm, d), lambda i: (i, 0)),
                  pl.BlockSpec((1, d), lambda i: (0, 0)),
                  pl.BlockSpec((1, d), lambda i: (0, 0))],
        out_specs=pl.BlockSpec((tm, d), lambda i: (i, 0)),
        out_shape=jax.ShapeDtypeStruct((m, d), F32),
        compiler_params=_params(("arbitrary",)), name=name)(x, g.reshape(1, d), b.reshape(1, d))


def _rel_bucket(n):
    n = jnp.maximum(n, 0)
    nf = jnp.maximum(n, 1).astype(F32)
    large = MAX_EXACT + jnp.floor(jnp.log(nf / MAX_EXACT) / math.log(MAX_DISTANCE / MAX_EXACT)
                                  * (NUM_BUCKETS - MAX_EXACT)).astype(jnp.int32)
    large = jnp.minimum(large, NUM_BUCKETS - 1)
    return jnp.where(n < MAX_EXACT, n, large)


def _bucket_lookup(bucket, table_fn):
    out = jnp.zeros(jnp.broadcast_shapes(bucket.shape, table_fn(0).shape), F32)
    for b in range(NUM_BUCKETS):
        out = out + jnp.where(bucket == b, table_fn(b), 0.0)
    return out


def _bias_tile_body(rb_ref, bucket_ref, o_ref):
    h = pl.program_id(0)
    o_ref[...] = _bucket_lookup(bucket_ref[...], lambda bk: rb_ref[bk, h])


def _bias_tiles(rel_bias, t):
    r = jnp.arange(t, dtype=jnp.int32)
    c = jnp.arange(2 * t, dtype=jnp.int32)
    buckets = _rel_bucket(r[:, None] + t - c[None, :])
    return pl.pallas_call(
        _bias_tile_body, grid=(N_HEADS,),
        in_specs=[pl.BlockSpec(memory_space=pltpu.SMEM), pl.BlockSpec((t, 2 * t), lambda h: (0, 0))],
        out_specs=pl.BlockSpec((None, t, 2 * t), lambda h: (h, 0, 0)),
        out_shape=jax.ShapeDtypeStruct((N_HEADS, t, 2 * t), F32),
        compiler_params=_params(("arbitrary",)), name="bias_tiles")(rel_bias, buckets)


def _attn_body(lam_ref, rb_ref, q_ref, k_ref, v_ref, bias_ref, g_ref, o_ref, *scratch,
               t, first, count, heads, out_scale):
    for sub in range(count):
        i = first + sub
        _attn_block(lam_ref, rb_ref, q_ref, k_ref, v_ref, bias_ref, g_ref, o_ref, scratch, t=t,
                    n_far=max(i - 1, 0), near=min(i + 1, 2) * t, heads=heads, out_scale=out_scale,
                    row0=sub * t)


def _attn_block(lam_ref, rb_ref, q_ref, k_ref, v_ref, bias_ref, g_ref, o_ref, scratch,
                *, t, n_far, near, heads, out_scale, row0):
    rows_q = slice(row0, row0 + t)
    s_bufs, p_bufs, m_bufs, mf_bufs, l_bufs, a_bufs = (scratch[0:2], scratch[2:4], scratch[4:6],
                                                       scratch[6:8], scratch[8:10], scratch[10:12])
    hg = pl.program_id(0)
    far = n_far * t
    kb_w = ATTN_KEY_BLOCK
    n_kb = (far + near) // kb_w
    n_tiles = kb_w // LANES

    def fold(x, op):
        out = x[:, 0:LANES]
        for c in range(1, n_tiles):
            out = op(out, x[:, c * LANES:(c + 1) * LANES])
        return out

    def head_cols(hh):
        return slice(hh * V_DIM, (hh + 1) * V_DIM)

    def score_block(hh, kb):
        par = hh % 2
        q = q_ref[rows_q, head_cols(hh)] * (HEAD_DIM ** -0.5)
        lane = lax.broadcasted_iota(jnp.int32, q.shape, 1)
        zero = jnp.zeros_like(q)
        q2 = jnp.concatenate([jnp.where(lane < HEAD_DIM, q, zero), jnp.where(lane >= HEAD_DIM, q, zero)],
                             axis=0)
        cols = slice(kb * kb_w, (kb + 1) * kb_w)
        s = jnp.dot(q2, k_ref[head_cols(hh), cols], preferred_element_type=F32)
        bias_far = rb_ref[NUM_BUCKETS - 1, hg * heads + hh]
        if kb * kb_w < far:
            top = fold(s, jnp.maximum) + bias_far
        else:
            off = kb * kb_w - far
            lo = 2 * t - near + off
            bias = bias_ref[hh, :, lo:lo + kb_w]
            row = lax.broadcasted_iota(jnp.int32, (t, kb_w), 0)
            col = lax.broadcasted_iota(jnp.int32, (t, kb_w), 1)
            keep = col + (off - (near - t)) <= row
            s = jnp.where(jnp.concatenate([keep, keep], axis=0),
                          s + jnp.concatenate([bias, bias], axis=0), NEG_INF)
            top = fold(s, jnp.maximum)
        s_bufs[par][:, cols] = s
        if kb == 0:
            m_bufs[par][...] = top
        else:
            m_bufs[par][...] = jnp.maximum(m_bufs[par][...], top)
        if kb == n_kb - 1:
            m = jnp.broadcast_to(jnp.max(m_bufs[par][...], axis=1, keepdims=True), m_bufs[par].shape)
            m_bufs[par][...] = m
            mf_bufs[par][...] = m - bias_far

    def exp_block(hh, kb):
        par = hh % 2
        cols = slice(kb * kb_w, (kb + 1) * kb_w)
        m = (mf_bufs if kb * kb_w < far else m_bufs)[par][...]
        p = jnp.exp(s_bufs[par][:, cols] - jnp.concatenate([m] * n_tiles, axis=1))
        if kb == 0:
            l_bufs[par][...] = fold(p, jnp.add)
        else:
            l_bufs[par][...] += fold(p, jnp.add)
        p_bufs[par][:, cols] = p.astype(BF16)

    def value_block(hh, kb):
        par = hh % 2
        rows = slice(kb * kb_w, (kb + 1) * kb_w)
        pv = jnp.dot(p_bufs[par][:, rows], v_ref[rows, head_cols(hh)], preferred_element_type=F32)
        if kb == 0:
            a_bufs[par][...] = pv
        else:
            a_bufs[par][...] += pv
        if kb == n_kb - 1:
            nrm = a_bufs[par][...] / jnp.sum(l_bufs[par][...], axis=1, keepdims=True)
            o = nrm[0:t] - lam_ref[0] * nrm[t:2 * t]
            ms = jnp.mean(o * o, axis=-1, keepdims=True)
            o_ref[rows_q, head_cols(hh)] = (o * lax.rsqrt(ms + LN_EPS) * g_ref[...] * out_scale
                                            ).astype(o_ref.dtype)

    for stage in range(heads + 2):
        for kb in range(n_kb):
            if stage < heads:
                score_block(stage, kb)
            if 0 <= stage - 1 < heads:
                exp_block(stage - 1, kb)
            if 0 <= stage - 2 < heads:
                value_block(stage - 2, kb)


def _prompt_attention(q, k, v, rel_bias, lam, subln_g, *, batch, seq, t, out_scale):
    assert t >= MAX_DISTANCE
    nq = seq // t
    width = N_HEADS * V_DIM
    q3 = q.reshape(batch, seq, width)
    v3 = v.reshape(batch, seq, width)
    bias = _bias_tiles(rel_bias, t)
    smem = pl.BlockSpec(memory_space=pltpu.SMEM)
    qo = q3
    group = ATTN_BLOCKS_PER_CALL
    hp = ATTN_HEADS_PER_STEP
    for first in range(0, nq, group):
        keys = (first + group) * t
        body = functools.partial(_attn_body, t=t, first=first, count=group, heads=hp,
                                 out_scale=out_scale)
        pair = lambda shape, dtype: [pltpu.VMEM(shape, dtype)] * 2
        rows_spec = pl.BlockSpec((None, group * t, hp * V_DIM),
                                 lambda h, b, p=first // group: (b, p, h))
        qo = pl.pallas_call(
            body, grid=(N_HEADS // hp, batch),
            in_specs=[smem, smem, rows_spec,
                      pl.BlockSpec((None, hp * V_DIM, keys), lambda h, b: (b, h, 0)),
                      pl.BlockSpec((None, keys, hp * V_DIM), lambda h, b: (b, 0, h)),
                      pl.BlockSpec((hp, t, 2 * t), lambda h, b: (h, 0, 0)),
                      pl.BlockSpec((1, V_DIM), lambda h, b: (0, 0))],
            out_specs=rows_spec,
            out_shape=jax.ShapeDtypeStruct((batch, seq, width), BF16),
            input_output_aliases={2: 0},
            scratch_shapes=(pair((2 * t, keys), F32) + pair((2 * t, keys), BF16)
                            + pair((2 * t, LANES), F32) + pair((2 * t, LANES), F32)
                            + pair((2 * t, LANES), F32) + pair((2 * t, V_DIM), F32)),
            compiler_params=_params(("arbitrary", "arbitrary")),
            name=f"prompt_attention_q{first}")(lam, rel_bias, qo, k, v3, bias, subln_g.reshape(1, V_DIM))
    return qo.reshape(batch * seq, width)


def _decode_body(pt_ref, lam_ref, q_ref, kn_ref, vn_ref, rbt_ref, bucket_ref, g_ref, *rest,
                 pages, page, n_steps, out_scale):
    k_refs = rest[:pages]
    v_refs = rest[pages:2 * pages]
    o_ref = rest[2 * pages]
    qexp_sc, bias_sc, m_sc, l_sc, acc_sc = rest[2 * pages + 1:]
    step = pl.program_id(1)
    rows = 2 * N_HEADS
    width = N_HEADS * V_DIM

    @pl.when(step == 0)
    def _():
        row = lax.broadcasted_iota(jnp.int32, (rows, width), 0)
        col = lax.broadcasted_iota(jnp.int32, (rows, width), 1)
        own_qk = (col // HEAD_DIM) == (row % N_HEADS) * 2 + row // N_HEADS
        q = (q_ref[...] * (HEAD_DIM ** -0.5)).astype(BF16).astype(F32)
        qexp = jnp.where(own_qk, jnp.broadcast_to(q, (rows, width)), 0.0)
        qexp_sc[...] = qexp.astype(BF16)
        bias_sc[...] = _bucket_lookup(bucket_ref[...], lambda bk: rbt_ref[:, bk:bk + 1])
        kn = kn_ref[...].astype(BF16).astype(F32)
        s_self = jnp.sum(qexp * kn, axis=1, keepdims=True) + rbt_ref[:, 0:1]
        m_sc[...] = s_self
        l_sc[...] = jnp.ones(l_sc.shape, F32)
        vn = vn_ref[...].astype(BF16).astype(F32)
        acc_sc[...] = jnp.broadcast_to(vn[:, None, :], acc_sc.shape)

    qexp = qexp_sc[...]
    bias_far = rbt_ref[:, NUM_BUCKETS - 1:NUM_BUCKETS]
    is_last = step == n_steps - 1
    s_parts = []
    for p in range(pages):
        s = jnp.dot(qexp, k_refs[p][...].astype(BF16), preferred_element_type=F32)
        if p == pages - 1:
            s = s + jnp.where(is_last, bias_sc[...], bias_far)
        else:
            s = s + bias_far
        s_parts.append(s)
    m_old = m_sc[...]
    m_new = m_old
    for s in s_parts:
        m_new = jnp.maximum(m_new, jnp.max(s, axis=1, keepdims=True))
    a = jnp.exp(m_old - m_new)
    l_new = a * l_sc[...]
    probs = []
    for p in range(pages):
        pr = jnp.exp(s_parts[p] - m_new)
        l_new = l_new + jnp.sum(pr, axis=1, keepdims=True)
        probs.append(pr.astype(BF16))
    probs = jnp.concatenate(probs, axis=1)
    m_sc[...] = m_new
    l_sc[...] = l_new
    for h in range(N_HEADS):
        vh = jnp.concatenate([v_refs[p][pl.ds(h, page, stride=N_HEADS), :].astype(BF16)
                              for p in range(pages)], axis=0)
        acc_sc[h] = a * acc_sc[h] + jnp.dot(probs, vh, preferred_element_type=F32)

    @pl.when(is_last)
    def _():
        nrm = acc_sc[...] / l_new[None]
        r = lax.broadcasted_iota(jnp.int32, nrm.shape, 1)
        hh = lax.broadcasted_iota(jnp.int32, nrm.shape, 0)
        coef = jnp.where(r == hh, 1.0, jnp.where(r == hh + N_HEADS, -lam_ref[0], 0.0))
        d = jnp.sum(coef * nrm, axis=1)
        ms = jnp.mean(d * d, axis=-1, keepdims=True)
        o_ref[...] = (d * lax.rsqrt(ms + LN_EPS) * g_ref[...] * out_scale).astype(o_ref.dtype)


def _sample_attention(q, k_new, v_new, cache_k, cache_v, page_table, rel_bias, lam, subln_g,
                      *, pages, out_scale):
    bs, width = q.shape
    page = cache_k.shape[2]
    n_pages = page_table.shape[1]
    n_steps = n_pages // pages
    past = n_pages * page
    kpos = past - page + jnp.arange(page, dtype=jnp.int32)
    bucket_last = _rel_bucket(past - kpos).reshape(1, page)
    rbt = jnp.tile(rel_bias.T, (2, 1))

    def tok_spec():
        return pl.BlockSpec((None, 1, width), lambda b, s, pt: (b, 0, 0))

    def page_spec(shape, p):
        return pl.BlockSpec((None,) + shape, lambda b, s, pt, p=p: (pt[b, s * pages + p], 0, 0))

    full = lambda shape: pl.BlockSpec(shape, lambda b, s, pt: tuple(0 for _ in shape))
    head_spec = pl.BlockSpec((None, N_HEADS, V_DIM), lambda b, s, pt: (b, 0, 0))
    in_specs = ([pl.BlockSpec(memory_space=pltpu.SMEM), tok_spec(), tok_spec(), head_spec,
                 full((2 * N_HEADS, NUM_BUCKETS)), full((1, page)), full((1, V_DIM))]
                + [page_spec((width, page), p) for p in range(pages)]
                + [page_spec((page * N_HEADS, V_DIM), p) for p in range(pages)])
    rows = 2 * N_HEADS
    body = functools.partial(_decode_body, pages=pages, page=page, n_steps=n_steps,
                             out_scale=out_scale)
    out = pl.pallas_call(
        body,
        grid_spec=pltpu.PrefetchScalarGridSpec(
            num_scalar_prefetch=1, grid=(bs, n_steps), in_specs=in_specs,
            out_specs=head_spec,
            scratch_shapes=[pltpu.VMEM((rows, width), BF16), pltpu.VMEM((rows, page), F32),
                            pltpu.VMEM((rows, 1), F32), pltpu.VMEM((rows, 1), F32),
                            pltpu.VMEM((N_HEADS, rows, V_DIM), F32)]),
        out_shape=jax.ShapeDtypeStruct((bs, N_HEADS, V_DIM), BF16),
        compiler_params=_params(("arbitrary", "arbitrary")),
        name="sample_attention")(
            page_table, lam, q.reshape(bs, 1, width), k_new.reshape(bs, 1, width),
            v_new.reshape(bs, N_HEADS, V_DIM), rbt, bucket_last, subln_g.reshape(1, V_DIM),
            *([cache_k] * pages), *([cache_v] * pages))
    return out.reshape(bs, width)


def _s5_params(a_re, a_im, log_dt, b_re, b_im, c_re, c_im):
    dt = jnp.exp(log_dt)[:, None]
    mag = jnp.exp(a_re * dt)
    ang = a_im * dt
    abar_re = mag * jnp.cos(ang)
    abar_im = mag * jnp.sin(ang)
    den = a_re * a_re + a_im * a_im
    f_re = ((abar_re - 1.0) * a_re + abar_im * a_im) / den
    f_im = (abar_im * a_re - (abar_re - 1.0) * a_im) / den
    bb_re = f_re[..., None] * b_re - f_im[..., None] * b_im
    bb_im = f_re[..., None] * b_im + f_im[..., None] * b_re
    eye = jnp.eye(SLAB_GROUPS, dtype=F32)

    def in_slabs(bb):
        tt = bb.reshape(N_SLABS, SLAB_GROUPS, STATE_DIM, SSM_GROUP).transpose(0, 1, 3, 2)
        full = tt[:, :, :, None, :] * eye[None, :, None, :, None]
        return full.reshape(N_SLABS, SLAB_IN, SLAB_STATE).astype(BF16)

    ab_re = abar_re[..., None] * bb_re - abar_im[..., None] * bb_im
    ab_im = abar_re[..., None] * bb_im + abar_im[..., None] * bb_re

    def out_slabs(cc):
        tt = cc.reshape(N_SLABS, SLAB_GROUPS, SSM_GROUP, STATE_DIM).transpose(0, 1, 3, 2)
        full = tt[:, :, :, None, :] * eye[None, :, None, :, None]
        return full.reshape(N_SLABS, SLAB_STATE, SLAB_IN).astype(BF16)

    return (abar_re.reshape(N_SLABS, SLAB_STATE), abar_im.reshape(N_SLABS, SLAB_STATE),
            in_slabs(bb_re), in_slabs(bb_im), out_slabs(c_re), out_slabs(c_im),
            in_slabs(ab_re), in_slabs(ab_im))


def _glu_out(y, u, d_ref, wglu_ref):
    g = jax.nn.gelu(y + d_ref[...] * u)
    gate = jnp.dot(g.astype(BF16), wglu_ref[...].astype(BF16), preferred_element_type=F32)
    return g * jax.nn.sigmoid(gate)


def _s5_prompt_body(x_ref, wu_ref, ar_ref, ai_ref, bbr_ref, bbi_ref, cr_ref, ci_ref, d_ref, wglu_ref,
                    o_ref, sre_ref, sim_ref, xr_sc, xi_sc, pw_sc, cr_sc, ci_sc, perm_sc, wu_sc, *, batch):
    c = pl.program_id(0)
    sub = SUBLANES
    _, tc, d_model = x_ref.shape
    width = wu_ref.shape[1]
    n_rows = batch * tc
    t = lax.broadcasted_iota(jnp.int32, (sub, SLAB_STATE), 0)
    first = t < batch

    def cmul(pr, pi, qr, qi):
        return pr * qr - pi * qi, pr * qi + pi * qr

    @pl.when(c == 0)
    def _():
        for j in range(N_SLABS):
            a = (jnp.broadcast_to(ar_ref[j:j + 1, :], t.shape), jnp.broadcast_to(ai_ref[j:j + 1, :], t.shape))
            a2 = cmul(*a, *a)
            for part in range(2):
                pw_sc[j, part] = jnp.where(first, a[part], a2[part])
        cr_sc[...] = jnp.zeros(cr_sc.shape, F32)
        ci_sc[...] = jnp.zeros(ci_sc.shape, F32)
        r = lax.broadcasted_iota(jnp.int32, (n_rows, n_rows), 0)
        k = lax.broadcasted_iota(jnp.int32, (n_rows, n_rows), 1)
        perm_sc[0] = jnp.where(k == (r % batch) * tc + r // batch, 1.0, 0.0).astype(BF16)
        perm_sc[1] = jnp.where(r == (k % batch) * tc + k // batch, 1.0, 0.0).astype(BF16)
        wu_sc[...] = wu_ref[...].astype(BF16)

    def permute(which, x):
        return jnp.dot(perm_sc[which], x, preferred_element_type=F32)

    u_seq = jnp.dot(x_ref[...].reshape(n_rows, d_model), wu_sc[...], preferred_element_type=F32)
    hi = u_seq.astype(BF16)
    rest = u_seq - hi.astype(F32)
    mid = rest.astype(BF16)
    lo = (rest - mid.astype(F32)).astype(BF16)
    u_hi = permute(0, hi)
    ub = u_hi.astype(BF16)
    u = u_hi + permute(0, mid) + permute(0, lo)
    row = lax.broadcasted_iota(jnp.int32, u_hi.shape, 0)
    u_before = jnp.where(row % sub >= batch, pltpu.roll(u_hi, batch, 0), 0.0).astype(BF16)
    for j in range(N_SLABS):
        cols = slice(j * SLAB_IN, (j + 1) * SLAB_IN)
        uj = jnp.concatenate([ub[:, cols], u_before[:, cols]], axis=1)
        xr_sc[j] = jnp.dot(uj, bbr_ref[j], preferred_element_type=F32)
        xi_sc[j] = jnp.dot(uj, bbi_ref[j], preferred_element_type=F32)

    for j in range(N_SLABS):
        pr, pi = cr_sc[j], ci_sc[j]
        for v in range(n_rows // sub):
            rows = slice(v * sub, (v + 1) * sub)
            lr = jnp.where(first, pltpu.roll(pr, batch, 0), pr)
            li = jnp.where(first, pltpu.roll(pi, batch, 0), pi)
            er, ei = cmul(pw_sc[j, 0], pw_sc[j, 1], lr, li)
            pr, pi = xr_sc[j, rows, :] + er, xi_sc[j, rows, :] + ei
            xr_sc[j, rows, :] = pr
            xi_sc[j, rows, :] = pi
        cr_sc[j] = pr
        ci_sc[j] = pi
    sre_ref[...] = cr_sc[...]
    sim_ref[...] = ci_sc[...]

    ys = []
    for j in range(N_SLABS):
        ys.append(jnp.dot(xr_sc[j].astype(BF16), cr_ref[j], preferred_element_type=F32)
                  - jnp.dot(xi_sc[j].astype(BF16), ci_ref[j], preferred_element_type=F32))
    out = _glu_out(jnp.concatenate(ys, axis=1), u, d_ref, wglu_ref).astype(BF16)
    o_ref[...] = permute(1, out).astype(o_ref.dtype).reshape(batch, tc, width)


def _s5_prompt(xb, w_in, c_u, sp, d, w_glu, *, batch, seq, tc):
    assert SUBLANES == 2 * batch, "tile = two positions of every sequence"
    ar, ai, bbr, bbi, cr, ci, abr, abi = sp
    bbr = jnp.concatenate([bbr, abr], axis=1)
    bbi = jnp.concatenate([bbi, abi], axis=1)
    d_model = xb.shape[1]
    width = w_glu.shape[0]
    rows = batch * tc
    const = lambda shape: pl.BlockSpec(shape, lambda c: tuple(0 for _ in shape),
                                       pipeline_mode=pl.Buffered(1))
    block = lambda w: pl.BlockSpec((batch, tc, w), lambda c: (0, c, 0))
    state_spec = pl.BlockSpec((N_SLABS, SUBLANES, SLAB_STATE), lambda c: (0, 0, 0))
    state_shape = jax.ShapeDtypeStruct((N_SLABS, SUBLANES, SLAB_STATE), F32)
    out, s_re, s_im = pl.pallas_call(
        functools.partial(_s5_prompt_body, batch=batch), grid=(seq // tc,),
        in_specs=[block(d_model),
                  pl.BlockSpec((d_model, width), lambda c, off=c_u // width: (0, off),
                               pipeline_mode=pl.Buffered(1)),
                  const(ar.shape), const(ai.shape), const(bbr.shape), const(bbi.shape),
                  const(cr.shape), const(ci.shape), const((1, width)), const(w_glu.shape)],
        out_specs=[block(width), state_spec, state_spec],
        out_shape=[jax.ShapeDtypeStruct((batch, seq, width), BF16), state_shape, state_shape],
        scratch_shapes=[pltpu.VMEM((N_SLABS, rows, SLAB_STATE), F32),
                        pltpu.VMEM((N_SLABS, rows, SLAB_STATE), F32),
                        pltpu.VMEM((N_SLABS, 2, SUBLANES, SLAB_STATE), F32),
                        pltpu.VMEM((N_SLABS, SUBLANES, SLAB_STATE), F32),
                        pltpu.VMEM((N_SLABS, SUBLANES, SLAB_STATE), F32),
                        pltpu.VMEM((2, rows, rows), BF16),
                        pltpu.VMEM((d_model, width), BF16)],
        compiler_params=_params(("arbitrary",)),
        name="s5_prompt")(xb.reshape(batch, seq, d_model), w_in, ar, ai, bbr, bbi, cr, ci,
                          d.reshape(1, width), w_glu)
    return out.reshape(batch * seq, width), s_re, s_im


def _s5_sample_body(x_ref, wu_ref, x0r_ref, x0i_ref, ar_ref, ai_ref, bbr_ref, bbi_ref, cr_ref, ci_ref,
                    d_ref, wglu_ref, o_ref, sre_ref, sim_ref):
    u = jnp.dot(x_ref[...].astype(BF16), wu_ref[...].astype(BF16), preferred_element_type=F32)
    ub = u.astype(BF16)
    ys = []
    for j in range(N_SLABS):
        uj = ub[:, j * SLAB_IN:(j + 1) * SLAB_IN]
        ar = ar_ref[j:j + 1, :]
        ai = ai_ref[j:j + 1, :]
        x0r = x0r_ref[j]
        x0i = x0i_ref[j]
        xr = ar * x0r - ai * x0i + jnp.dot(uj, bbr_ref[j], preferred_element_type=F32)
        xi = ar * x0i + ai * x0r + jnp.dot(uj, bbi_ref[j], preferred_element_type=F32)
        sre_ref[j] = xr
        sim_ref[j] = xi
        ys.append(jnp.dot(xr.astype(BF16), cr_ref[j], preferred_element_type=F32)
                  - jnp.dot(xi.astype(BF16), ci_ref[j], preferred_element_type=F32))
    o_ref[...] = _glu_out(jnp.concatenate(ys, axis=1), u, d_ref, wglu_ref).astype(o_ref.dtype)


def _s5_sample(xs, w_in, c_u, x0_re, x0_im, sp, d, w_glu):
    ar, ai, bbr, bbi, cr, ci = sp[:6]
    bs, d_model = xs.shape
    width = w_glu.shape[0]
    whole = lambda a: pl.BlockSpec(a.shape, lambda i: tuple(0 for _ in a.shape))
    state_shape = jax.ShapeDtypeStruct((N_SLABS, bs, SLAB_STATE), F32)
    d2 = d.reshape(1, width)
    operands = [x0_re, x0_im, ar, ai, bbr, bbi, cr, ci, d2, w_glu]
    return pl.pallas_call(
        _s5_sample_body, grid=(1,),
        in_specs=[whole(xs), pl.BlockSpec((d_model, width), lambda i, off=c_u // width: (0, off))]
                 + [whole(a) for a in operands],
        out_specs=[pl.BlockSpec((bs, width), lambda i: (0, 0)),
                   pl.BlockSpec(state_shape.shape, lambda i: (0, 0, 0)),
                   pl.BlockSpec(state_shape.shape, lambda i: (0, 0, 0))],
        out_shape=[jax.ShapeDtypeStruct((bs, width), BF16), state_shape, state_shape],
        compiler_params=_params(("arbitrary",)),
        name="s5_sample")(xs, w_in, *operands)


def _gated(c):
    gate = c[:, :LANES]
    return gate * jax.nn.sigmoid(gate) * c[:, LANES:]


def _up_prompt_body(h_ref, wg_ref, wv_ref, cwg_ref, cwv_ref, cbg_ref, cbv_ref,
                    act_ref, cg_ref, cv_ref, w_sc, *, rows):
    w_sc[:, 0:LANES] = wg_ref[...].astype(BF16)
    w_sc[:, LANES:2 * LANES] = wv_ref[...].astype(BF16)
    cw = jnp.concatenate([cwg_ref[...], cwv_ref[...]], axis=1)
    cb = jnp.concatenate([cbg_ref[...], cbv_ref[...]], axis=1)
    seq = h_ref.shape[0]
    sub = SUBLANES
    row8 = lax.broadcasted_iota(jnp.int32, (sub, 2 * LANES), 0)

    prev = jnp.zeros((sub, 2 * LANES), F32)
    for c in range(seq // rows):
        up = jnp.dot(h_ref[c * rows:(c + 1) * rows, :], w_sc[...], preferred_element_type=F32)
        cur = cb + cw[CONV_W - 1:CONV_W] * up
        for back in range(1, CONV_W):
            rolled = pltpu.roll(up, back, 0)
            head = jnp.where(row8 < back, pltpu.roll(prev, back, 0), rolled[0:sub])
            shifted = jnp.concatenate([head, rolled[sub:]], axis=0)
            cur = cur + cw[CONV_W - 1 - back:CONV_W - back] * shifted
        act_ref[c * rows:(c + 1) * rows, :] = _gated(cur).astype(act_ref.dtype)
        prev = up[rows - sub:, :]
    tail = prev[sub - (CONV_W - 1):, :]
    cg_ref[...] = tail[:, :LANES]
    cv_ref[...] = tail[:, LANES:]


def _up_prompt(hb, w_up, conv_w, conv_b, *, batch, seq):
    d_model = hb.shape[1]
    d_ff = w_up.shape[1] // 2
    nb = d_ff // LANES
    cb = conv_b.reshape(1, 2 * d_ff)
    col = lambda off: (lambda b, j: (0, j + off))
    tail_spec = pl.BlockSpec((None, CONV_W - 1, LANES), lambda b, j: (b, 0, j))
    tail_shape = jax.ShapeDtypeStruct((batch, CONV_W - 1, d_ff), F32)
    return pl.pallas_call(
        functools.partial(_up_prompt_body, rows=UP_ROW_CHUNK), grid=(batch, nb),
        scratch_shapes=[pltpu.VMEM((d_model, 2 * LANES), BF16)],
        in_specs=[pl.BlockSpec((seq, d_model), lambda b, j: (b, 0)),
                  pl.BlockSpec((d_model, LANES), col(0)), pl.BlockSpec((d_model, LANES), col(nb)),
                  pl.BlockSpec((CONV_W, LANES), col(0)), pl.BlockSpec((CONV_W, LANES), col(nb)),
                  pl.BlockSpec((1, LANES), col(0)), pl.BlockSpec((1, LANES), col(nb))],
        out_specs=[pl.BlockSpec((seq, LANES), lambda b, j: (b, j)), tail_spec, tail_spec],
        out_shape=[jax.ShapeDtypeStruct((batch * seq, d_ff), BF16), tail_shape, tail_shape],
        compiler_params=_params(("arbitrary", "arbitrary")),
        name="up_prompt")(hb, w_up, w_up, conv_w, conv_w, cb, cb)


def _up_sample_body(h_ref, wg_ref, wva_ref, wvb_ref, cwg_ref, cwva_ref, cwvb_ref, cbg_ref, cbva_ref,
                    cbvb_ref, sg_ref, sva_ref, svb_ref, act_ref, cg_ref, cv_ref):
    w2 = 2 * LANES
    w = jnp.concatenate([wg_ref[...].astype(BF16), wva_ref[...].astype(BF16),
                         wvb_ref[...].astype(BF16)], axis=1)
    up = jnp.dot(h_ref[...], w, preferred_element_type=F32)
    cw = jnp.concatenate([cwg_ref[...], cwva_ref[...], cwvb_ref[...]], axis=1)
    cb = jnp.concatenate([cbg_ref[...], cbva_ref[...], cbvb_ref[...]], axis=1)
    c = cb + cw[CONV_W - 1:CONV_W] * up
    for tap in range(CONV_W - 1):
        st = jnp.concatenate([sg_ref[tap], sva_ref[tap], svb_ref[tap]], axis=1)
        c = c + cw[tap:tap + 1] * st
    gate = c[:, 0:w2]
    act_ref[...] = (gate * jax.nn.sigmoid(gate) * c[:, w2:2 * w2]).astype(act_ref.dtype)
    for tap in range(1, CONV_W - 1):
        cg_ref[tap - 1] = sg_ref[tap]
        cv_ref[tap - 1] = jnp.concatenate([sva_ref[tap], svb_ref[tap]], axis=1)
    cg_ref[CONV_W - 2] = up[:, 0:w2]
    cv_ref[CONV_W - 2] = up[:, w2:2 * w2]


def _up_sample(hb, w_up, conv_w, conv_b, state):
    bs, d_model = hb.shape
    d_ff = w_up.shape[1] // 2
    nb = d_ff // LANES
    n_pairs = (nb + 1) // 2
    last = 2 * nb - 1
    w2 = 2 * LANES
    cb = conv_b.reshape(1, 2 * d_ff)
    gate = lambda shape: pl.BlockSpec(shape, lambda j: (0,) * (len(shape) - 1) + (j,))
    val = lambda shape, k: pl.BlockSpec(
        shape, lambda j: (0,) * (len(shape) - 1) + (jnp.minimum(nb + 2 * j + k, last),))
    operands = []
    specs = [pl.BlockSpec((bs, d_model), lambda j: (0, 0))]
    for arr, lead in ((w_up, (d_model,)), (conv_w, (CONV_W,)), (cb, (1,)), (state, (CONV_W - 1, bs))):
        specs += [gate(lead + (w2,)), val(lead + (LANES,), 0), val(lead + (LANES,), 1)]
        operands += [arr, arr, arr]
    tail_spec = pl.BlockSpec((CONV_W - 1, bs, w2), lambda j: (0, 0, j))
    tail_shape = jax.ShapeDtypeStruct((CONV_W - 1, bs, d_ff), F32)
    return pl.pallas_call(
        _up_sample_body, grid=(n_pairs,), in_specs=specs,
        out_specs=[pl.BlockSpec((bs, w2), lambda j: (0, j)), tail_spec, tail_spec],
        out_shape=[jax.ShapeDtypeStruct((bs, d_ff), BF16), tail_shape, tail_shape],
        compiler_params=_params(("arbitrary",)),
        name="up_sample")(hb, *operands)


def _one(x):
    return (x,)


def _sigmoid_out(acc):
    return (jax.nn.sigmoid(acc),)


def _both(acc):
    return (acc, acc)


def _merge(pa, ps, ga, gs):
    return (ga * pa + gs * ps,)


def _layer(xp, xs, w, *, batch, seq, alpha, attend_p, attend_s, ssm_p, ssm_s, up_p, up_s):
    d_model = xp.shape[1]
    qk_w = N_HEADS * 2 * HEAD_DIM
    v_w = N_HEADS * V_DIM
    ssm_w = d_model // 2
    w_in = w["w_in"]
    g1, b1 = w["ln1_g"].reshape(1, d_model), w["ln1_b"].reshape(1, d_model)
    big = dict(tm=MM_ROWS, tn=MM_COLS)
    c = 0
    (q,), (q_s,), xb = _matmul([(xp, xs, w_in, c)], [], [BF16], _one, n_cols=qk_w, name="proj_q",
                               emit_lhs=True, **big)
    c += qk_w
    kt, ktb, k_s = _proj_transposed(xb, xs, w_in, c, n_cols=qk_w, batch=batch, seq=seq, tm=MM_ROWS,
                                    name="proj_kt")
    c += qk_w
    c_v = c
    c += v_w
    c_u = c
    c += ssm_w
    wide = dict(tm=MM_ROWS, tn=d_model)
    (ga,), (ga_s,) = _matmul([(xb, xs, w_in, c)], [], [F32], _sigmoid_out, n_cols=d_model, name="gate_a", **wide)
    c += d_model
    (gs,), (gs_s,) = _matmul([(xb, xs, w_in, c)], [], [F32], _sigmoid_out, n_cols=d_model, name="gate_s", **wide)
    (v, vb), (v_s, _) = _matmul([(xb, xs, w_in, c_v)], [], [F32, BF16], _both, n_cols=v_w, name="proj_v", **big)

    attn = attend_p(q, ktb, vb)
    ssm_out, re_p, im_p = ssm_p(xb, w_in, c_u)
    ssm_out_s, re_s, im_s = ssm_s(xs, w_in, c_u)
    attn_s = attend_s(q_s, k_s, v_s)

    (merged,), (merged_s,) = _matmul(
        [(attn, attn_s, w["w_proj_attn"], 0), (ssm_out, ssm_out_s, w["w_proj_ssm"], 0)],
        [(ga, ga_s), (gs, gs_s)], [BF16], _merge, n_cols=d_model, tm=LN_MM_ROWS, tn=d_model, name="merge")

    def post_ln1(acc, res, g, b):
        h = _ln(alpha * res + acc, g, b)
        return h, h

    (h, hb), (h_s, hb_s) = _matmul([(merged, merged_s, w["w_out"], 0)], [(xp, xs), g1, b1], [F32, BF16],
                                   post_ln1, n_cols=d_model, tm=LN_MM_ROWS, tn=d_model, name="out_proj_ln1")
    act, conv_p = up_p(hb)
    act_s, conv_s = up_s(hb_s)
    (r2,), (r2_s,) = _matmul([(act, act_s, w["w_down"], 0)], [(h, h_s)], [F32],
                             lambda acc, res: (alpha * res + acc,), n_cols=d_model,
                             tm=DOWN_ROWS, tn=DOWN_COLS, name="down_proj")
    y = _layer_norm(r2, w["ln2_g"], w["ln2_b"], tm=LN_ROWS, name="ln2")
    y_s = _layer_norm(r2_s, w["ln2_g"], w["ln2_b"], tm=LN_ROWS, name="ln2_sample")
    return (y, kt, v, re_p, im_p, conv_p), (y_s, k_s, v_s, re_s, im_s, conv_s)


def kernel(x_prompt, x_sample, cache_k, cache_v, state_ssm_re, state_ssm_im, state_conv, page_table, rel_bias, w_in, lambda_q1, lambda_k1, lambda_q2, lambda_k2, subln_g, ssm_a_re, ssm_a_im, ssm_log_dt, ssm_b_re, ssm_b_im, ssm_c_re, ssm_c_im, ssm_d, w_glu, w_proj_attn, w_proj_ssm, w_out, ln1_g, ln1_b, w_up, conv_w, conv_b, w_down, ln2_g, ln2_b):
    depth = w_in.shape[0]
    assert depth == 1, "single-layer trunk"
    bp, seq, d_model = x_prompt.shape
    bs, dec_seq, _ = x_sample.shape
    assert dec_seq == 1
    n_pool, page = cache_k.shape[1], cache_k.shape[2]
    d_ff = w_down.shape[1]
    n_groups = ssm_a_re.shape[1]
    assert n_groups == N_SLABS * SLAB_GROUPS and d_ff % LANES == 0
    alpha = (2.0 * depth) ** 0.25
    width = N_HEADS * V_DIM

    hp = x_prompt.reshape(bp * seq, d_model)
    hs = x_sample.reshape(bs, d_model)
    outs = {}
    for l in range(depth):
        lam_init = 0.8 - 0.6 * math.exp(-0.3 * l)
        out_scale = 1.0 - lam_init
        lam = (jnp.exp(jnp.sum(lambda_q1[l] * lambda_k1[l]))
               - jnp.exp(jnp.sum(lambda_q2[l] * lambda_k2[l])) + lam_init).reshape(1)
        w = dict(w_in=w_in[l], w_proj_attn=w_proj_attn[l], w_proj_ssm=w_proj_ssm[l], w_out=w_out[l],
                 ln1_g=ln1_g[l], ln1_b=ln1_b[l], w_down=w_down[l], ln2_g=ln2_g[l], ln2_b=ln2_b[l])
        sp = _s5_params(ssm_a_re[l], ssm_a_im[l], ssm_log_dt[l], ssm_b_re[l], ssm_b_im[l],
                        ssm_c_re[l], ssm_c_im[l])

        def attend_p(q, kt, v):
            return _prompt_attention(q, kt, v, rel_bias, lam, subln_g[l], batch=bp, seq=seq,
                                     t=ATTN_BLOCK, out_scale=out_scale)

        def ssm_p(xb, w_in_l, c_u):
            o, sr, si = _s5_prompt(xb, w_in_l, c_u, sp, ssm_d[l], w_glu[l], batch=bp, seq=seq,
                                   tc=S5_POSITIONS)
            last = lambda st: st[:, SUBLANES - bp:, :].transpose(1, 0, 2)
            return o, last(sr), last(si)

        def up_p(hb):
            act, cg, cv = _up_prompt(hb, w_up[l], conv_w[l], conv_b[l], batch=bp, seq=seq)
            return act, jnp.concatenate([cg, cv], axis=-1)

        ck = cache_k[l].transpose(0, 2, 3, 4, 1).reshape(n_pool, width, page)
        cv_ = cache_v[l].reshape(n_pool, page * N_HEADS, V_DIM)

        def attend_s(q, k, v):
            return _sample_attention(q.astype(F32), k, v, ck, cv_, page_table, rel_bias, lam,
                                     subln_g[l], pages=DECODE_PAGES, out_scale=out_scale)

        def ssm_s(xs, w_in_l, c_u):
            x0r = state_ssm_re[l].reshape(bs, N_SLABS, SLAB_STATE).transpose(1, 0, 2)
            x0i = state_ssm_im[l].reshape(bs, N_SLABS, SLAB_STATE).transpose(1, 0, 2)
            o, sr, si = _s5_sample(xs, w_in_l, c_u, x0r, x0i, sp, ssm_d[l], w_glu[l])
            return o, sr.transpose(1, 0, 2), si.transpose(1, 0, 2)

        def up_s(hb):
            act, cg, cv = _up_sample(hb, w_up[l], conv_w[l], conv_b[l],
                                     state_conv[l].transpose(1, 0, 2))
            return act, jnp.concatenate([cg, cv], axis=-1).transpose(1, 0, 2)

        (hp, kt_p, v_p, re_p, im_p, c_p), (hs, k_s, v_s, re_s, im_s, c_s) = _layer(
            hp, hs, w, batch=bp, seq=seq, alpha=alpha, attend_p=attend_p, attend_s=attend_s,
            ssm_p=ssm_p, ssm_s=ssm_s, up_p=up_p, up_s=up_s)
        k_p = kt_p.reshape(bp, N_HEADS, 2, HEAD_DIM, seq).transpose(0, 4, 1, 2, 3)

        for name, val in (("kp", k_p.reshape(bp, seq, N_HEADS, 2, HEAD_DIM)),
                          ("vp", v_p.reshape(bp, seq, N_HEADS, V_DIM)),
                          ("rep", re_p.reshape(bp, n_groups, STATE_DIM)),
                          ("imp", im_p.reshape(bp, n_groups, STATE_DIM)),
                          ("cp", c_p),
                          ("ks", k_s.reshape(bs, 1, N_HEADS, 2, HEAD_DIM)),
                          ("vs", v_s.reshape(bs, 1, N_HEADS, V_DIM)),
                          ("res", re_s.reshape(bs, n_groups, STATE_DIM)),
                          ("ims", im_s.reshape(bs, n_groups, STATE_DIM)),
                          ("cs", c_s)):
            outs.setdefault(name, []).append(val)

    st = {k: jnp.stack(v, axis=0) for k, v in outs.items()}
    return (hp.reshape(bp, seq, d_model), hs.reshape(bs, 1, d_model), st["kp"], st["vp"], st["rep"],
            st["imp"], st["cp"], st["ks"], st["vs"], st["res"], st["ims"], st["cs"])
```

```python
import functools
import math

import jax
import jax.numpy as jnp
from jax import lax
from jax.experimental import pallas as pl
from jax.experimental.pallas import tpu as pltpu

F32 = jnp.float32
BF16 = jnp.bfloat16

N_HEADS = 8
HEAD_DIM = 64
V_DIM = 2 * HEAD_DIM
SSM_GROUP = 16
STATE_DIM = 64
CONV_W = 3
NUM_BUCKETS = 32
MAX_EXACT = NUM_BUCKETS // 2
MAX_DISTANCE = 128
LN_EPS = 1e-5
NEG_INF = -1e30

VMEM_LIMIT_BYTES = 56 * 1024 * 1024
LANES = 128
SLAB_GROUPS = 8
N_SLABS = 8
SLAB_IN = SLAB_GROUPS * SSM_GROUP
SLAB_STATE = SLAB_GROUPS * STATE_DIM
SUBLANES = 8
MM_ROWS, MM_COLS = 1024, 1024
LN_MM_ROWS = 512
DOWN_ROWS, DOWN_COLS = 512, 512
LN_ROWS = 1024
ATTN_BLOCK = 256
ATTN_KEY_BLOCK = 256
ATTN_HEADS_PER_STEP = 4
ATTN_BLOCKS_PER_CALL = 2
S5_POSITIONS = 64
DECODE_PAGES = 16
UP_ROW_CHUNK = 512


def _params(sem):
    return pltpu.CompilerParams(dimension_semantics=sem, vmem_limit_bytes=VMEM_LIMIT_BYTES)


def _mm_body(*refs, n_pairs, n_extra, n_out, epilogue, emit_lhs):
    it = iter(refs)
    take = lambda n: [next(it) for _ in range(n)]
    x_refs, xs_refs, w_refs = take(n_pairs), take(n_pairs), take(n_pairs)
    extra, extra_s = take(n_extra), take(n_extra)
    outs, outs_s = take(n_out), take(n_out)
    lhs_out = take(1) if emit_lhs else []
    wbf = take(n_pairs)

    def apply(lhs_refs, extra_refs, out_refs, keep):
        lhs = [x[...].astype(BF16) for x in lhs_refs]
        for ref in keep:
            ref[...] = lhs[0]
        accs = [jnp.dot(x, s[...], preferred_element_type=F32) for x, s in zip(lhs, wbf)]
        for o, r in zip(out_refs, epilogue(*accs, *[e[...] for e in extra_refs])):
            o[...] = r.astype(o.dtype)

    @pl.when(pl.program_id(1) == 0)
    def _():
        for w, s in zip(w_refs, wbf):
            s[...] = w[...].astype(BF16)
        apply(xs_refs, extra_s, outs_s, [])

    apply(x_refs, extra, outs, lhs_out)


def _matmul(pairs, extras, out_dtypes, epilogue, *, n_cols, tm, tn, name, emit_lhs=False):
    m = pairs[0][0].shape[0]
    ms = pairs[0][1].shape[0]
    grid = (n_cols // tn, m // tm)
    assert not emit_lhs or grid[0] == 1
    w_mode = dict(pipeline_mode=pl.Buffered(1)) if grid[0] == 1 else {}
    specs_x, specs_xs, specs_w, scratch = [], [], [], []
    for x, xs, w, c0 in pairs:
        k = x.shape[1]
        specs_x.append(pl.BlockSpec((tm, k), lambda j, i: (i, 0)))
        specs_xs.append(pl.BlockSpec((ms, k), lambda j, i: (0, 0)))
        specs_w.append(pl.BlockSpec((k, tn), lambda j, i, off=c0 // tn: (0, j + off), **w_mode))
        scratch.append(pltpu.VMEM((k, tn), BF16))
    specs_e, specs_es, args_e, args_es = [], [], [], []
    for e in extras:
        if isinstance(e, tuple):
            specs_e.append(pl.BlockSpec((tm, tn), lambda j, i: (i, j)))
            specs_es.append(pl.BlockSpec((ms, tn), lambda j, i: (0, j)))
            args_e.append(e[0])
            args_es.append(e[1])
        else:
            specs_e.append(pl.BlockSpec((1, tn), lambda j, i: (0, j)))
            specs_es.append(pl.BlockSpec((1, tn), lambda j, i: (0, j)))
            args_e.append(e)
            args_es.append(e)
    out_shape = ([jax.ShapeDtypeStruct((m, n_cols), d) for d in out_dtypes]
                 + [jax.ShapeDtypeStruct((ms, n_cols), d) for d in out_dtypes])
    out_specs = ([pl.BlockSpec((tm, tn), lambda j, i: (i, j)) for _ in out_dtypes]
                 + [pl.BlockSpec((ms, tn), lambda j, i: (0, j)) for _ in out_dtypes])
    if emit_lhs:
        k0 = pairs[0][0].shape[1]
        out_shape.append(jax.ShapeDtypeStruct((m, k0), BF16))
        out_specs.append(pl.BlockSpec((tm, k0), lambda j, i: (i, 0)))
    body = functools.partial(_mm_body, n_pairs=len(pairs), n_extra=len(extras),
                             n_out=len(out_dtypes), epilogue=epilogue, emit_lhs=emit_lhs)
    outs = pl.pallas_call(
        body, grid=grid, in_specs=specs_x + specs_xs + specs_w + specs_e + specs_es,
        out_specs=out_specs, out_shape=out_shape, scratch_shapes=scratch,
        compiler_params=_params(("arbitrary", "arbitrary")), name=name)(
            *[p[0] for p in pairs], *[p[1] for p in pairs], *[p[2] for p in pairs],
            *args_e, *args_es)
    n = len(out_dtypes)
    if emit_lhs:
        return outs[:n], outs[n:2 * n], outs[2 * n]
    return outs[:n], outs[n:]


def _proj_t_body(x_ref, xs_ref, w_ref, o_ref, ob_ref, os_ref, wt_sc):
    @pl.when(pl.program_id(0) == 0)
    def _():
        for c in range(w_ref.shape[1] // LANES):
            cols = slice(c * LANES, (c + 1) * LANES)
            wt_sc[cols, :] = w_ref[:, cols].T.astype(BF16)
        os_ref[...] = lax.dot_general(xs_ref[...].astype(BF16), wt_sc[...], (((1,), (1,)), ((), ())),
                                      preferred_element_type=F32)

    kt = lax.dot_general(wt_sc[...], x_ref[...], (((1,), (1,)), ((), ())),
                         preferred_element_type=F32)
    o_ref[...] = kt
    ob_ref[...] = kt.astype(BF16)


def _proj_transposed(x, xs, w, c0, *, n_cols, batch, seq, tm, name):
    k = x.shape[1]
    ms = xs.shape[0]
    per_b = seq // tm
    out_spec = pl.BlockSpec((None, n_cols, tm), lambda i: (i // per_b, 0, i % per_b))
    return pl.pallas_call(
        _proj_t_body, grid=(batch * per_b,),
        in_specs=[pl.BlockSpec((tm, k), lambda i: (i, 0)),
                  pl.BlockSpec((ms, k), lambda i: (0, 0)),
                  pl.BlockSpec((k, n_cols), lambda i, off=c0 // n_cols: (0, off),
                               pipeline_mode=pl.Buffered(1))],
        out_specs=[out_spec, out_spec, pl.BlockSpec((ms, n_cols), lambda i: (0, 0))],
        out_shape=[jax.ShapeDtypeStruct((batch, n_cols, seq), F32),
                   jax.ShapeDtypeStruct((batch, n_cols, seq), BF16),
                   jax.ShapeDtypeStruct((ms, n_cols), F32)],
        scratch_shapes=[pltpu.VMEM((n_cols, k), BF16)],
        compiler_params=_params(("arbitrary",)), name=name)(x, xs, w)


def _ln(x, g, b):
    mu = jnp.mean(x, axis=-1, keepdims=True)
    xc = x - mu
    var = jnp.mean(xc * xc, axis=-1, keepdims=True)
    return xc * lax.rsqrt(var + LN_EPS) * g + b


def _ln_body(x_ref, g_ref, b_ref, o_ref):
    o_ref[...] = _ln(x_ref[...], g_ref[...], b_ref[...])


def _layer_norm(x, g, b, *, tm, name):
    m, d = x.shape
    tm = min(tm, m)
    return pl.pallas_call(
        _ln_body, grid=(m // tm,),
        in_specs=[pl.BlockSpec((tm, d), lambda i: (i, 0)),
                  pl.BlockSpec((1, d), lambda i: (0, 0)),
                  pl.BlockSpec((1, d), lambda i: (0, 0))],
        out_specs=pl.BlockSpec((tm, d), lambda i: (i, 0)),
        out_shape=jax.ShapeDtypeStruct((m, d), F32),
        compiler_params=_params(("arbitrary",)), name=name)(x, g.reshape(1, d), b.reshape(1, d))


def _rel_bucket(n):
    n = jnp.maximum(n, 0)
    nf = jnp.maximum(n, 1).astype(F32)
    large = MAX_EXACT + jnp.floor(jnp.log(nf / MAX_EXACT) / math.log(MAX_DISTANCE / MAX_EXACT)
                                  * (NUM_BUCKETS - MAX_EXACT)).astype(jnp.int32)
    large = jnp.minimum(large, NUM_BUCKETS - 1)
    return jnp.where(n < MAX_EXACT, n, large)


def _bucket_lookup(bucket, table_fn):
    out = jnp.zeros(jnp.broadcast_shapes(bucket.shape, table_fn(0).shape), F32)
    for b in range(NUM_BUCKETS):
        out = out + jnp.where(bucket == b, table_fn(b), 0.0)
    return out


def _bias_tile_body(rb_ref, bucket_ref, o_ref):
    h = pl.program_id(0)
    o_ref[...] = _bucket_lookup(bucket_ref[...], lambda bk: rb_ref[bk, h])


def _bias_tiles(rel_bias, t):
    r = jnp.arange(t, dtype=jnp.int32)
    c = jnp.arange(2 * t, dtype=jnp.int32)
    buckets = _rel_bucket(r[:, None] + t - c[None, :])
    return pl.pallas_call(
        _bias_tile_body, grid=(N_HEADS,),
        in_specs=[pl.BlockSpec(memory_space=pltpu.SMEM), pl.BlockSpec((t, 2 * t), lambda h: (0, 0))],
        out_specs=pl.BlockSpec((None, t, 2 * t), lambda h: (h, 0, 0)),
        out_shape=jax.ShapeDtypeStruct((N_HEADS, t, 2 * t), F32),
        compiler_params=_params(("arbitrary",)), name="bias_tiles")(rel_bias, buckets)


def _attn_body(lam_ref, rb_ref, q_ref, k_ref, v_ref, bias_ref, g_ref, o_ref, *scratch,
               t, first, count, heads, out_scale):
    for sub in range(count):
        i = first + sub
        _attn_block(lam_ref, rb_ref, q_ref, k_ref, v_ref, bias_ref, g_ref, o_ref, scratch, t=t,
                    n_far=max(i - 1, 0), near=min(i + 1, 2) * t, heads=heads, out_scale=out_scale,
                    row0=sub * t)


def _attn_block(lam_ref, rb_ref, q_ref, k_ref, v_ref, bias_ref, g_ref, o_ref, scratch,
                *, t, n_far, near, heads, out_scale, row0):
    rows_q = slice(row0, row0 + t)
    s_bufs, p_bufs, m_bufs, mf_bufs, l_bufs, a_bufs = (scratch[0:2], scratch[2:4], scratch[4:6],
                                                       scratch[6:8], scratch[8:10], scratch[10:12])
    hg = pl.program_id(0)
    far = n_far * t
    kb_w = ATTN_KEY_BLOCK
    n_kb = (far + near) // kb_w
    n_tiles = kb_w // LANES

    def fold(x, op):
        out = x[:, 0:LANES]
        for c in range(1, n_tiles):
            out = op(out, x[:, c * LANES:(c + 1) * LANES])
        return out

    def head_cols(hh):
        return slice(hh * V_DIM, (hh + 1) * V_DIM)

    def score_block(hh, kb):
        par = hh % 2
        q = q_ref[rows_q, head_cols(hh)] * (HEAD_DIM ** -0.5)
        lane = lax.broadcasted_iota(jnp.int32, q.shape, 1)
        zero = jnp.zeros_like(q)
        q2 = jnp.concatenate([jnp.where(lane < HEAD_DIM, q, zero), jnp.where(lane >= HEAD_DIM, q, zero)],
                             axis=0)
        cols = slice(kb * kb_w, (kb + 1) * kb_w)
        s = jnp.dot(q2, k_ref[head_cols(hh), cols], preferred_element_type=F32)
        bias_far = rb_ref[NUM_BUCKETS - 1, hg * heads + hh]
        if kb * kb_w < far:
            top = fold(s, jnp.maximum) + bias_far
        else:
            off = kb * kb_w - far
            lo = 2 * t - near + off
            bias = bias_ref[hh, :, lo:lo + kb_w]
            row = lax.broadcasted_iota(jnp.int32, (t, kb_w), 0)
            col = lax.broadcasted_iota(jnp.int32, (t, kb_w), 1)
            keep = col + (off - (near - t)) <= row
            s = jnp.where(jnp.concatenate([keep, keep], axis=0),
                          s + jnp.concatenate([bias, bias], axis=0), NEG_INF)
            top = fold(s, jnp.maximum)
        s_bufs[par][:, cols] = s
        if kb == 0:
            m_bufs[par][...] = top
        else:
            m_bufs[par][...] = jnp.maximum(m_bufs[par][...], top)
        if kb == n_kb - 1:
            m = jnp.broadcast_to(jnp.max(m_bufs[par][...], axis=1, keepdims=True), m_bufs[par].shape)
            m_bufs[par][...] = m
            mf_bufs[par][...] = m - bias_far

    def exp_block(hh, kb):
        par = hh % 2
        cols = slice(kb * kb_w, (kb + 1) * kb_w)
        m = (mf_bufs if kb * kb_w < far else m_bufs)[par][...]
        p = jnp.exp(s_bufs[par][:, cols] - jnp.concatenate([m] * n_tiles, axis=1))
        if kb == 0:
            l_bufs[par][...] = fold(p, jnp.add)
        else:
            l_bufs[par][...] += fold(p, jnp.add)
        p_bufs[par][:, cols] = p.astype(BF16)

    def value_block(hh, kb):
        par = hh % 2
        rows = slice(kb * kb_w, (kb + 1) * kb_w)
        pv = jnp.dot(p_bufs[par][:, rows], v_ref[rows, head_cols(hh)], preferred_element_type=F32)
        if kb == 0:
            a_bufs[par][...] = pv
        else:
            a_bufs[par][...] += pv
        if kb == n_kb - 1:
            nrm = a_bufs[par][...] / jnp.sum(l_bufs[par][...], axis=1, keepdims=True)
            o = nrm[0:t] - lam_ref[0] * nrm[t:2 * t]
            ms = jnp.mean(o * o, axis=-1, keepdims=True)
            o_ref[rows_q, head_cols(hh)] = (o * lax.rsqrt(ms + LN_EPS) * g_ref[...] * out_scale
                                            ).astype(o_ref.dtype)

    for stage in range(heads + 2):
        for kb in range(n_kb):
            if stage < heads:
                score_block(stage, kb)
            if 0 <= stage - 1 < heads:
                exp_block(stage - 1, kb)
            if 0 <= stage - 2 < heads:
                value_block(stage - 2, kb)


def _prompt_attention(q, k, v, rel_bias, lam, subln_g, *, batch, seq, t, out_scale):
    assert t >= MAX_DISTANCE
    nq = seq // t
    width = N_HEADS * V_DIM
    q3 = q.reshape(batch, seq, width)
    v3 = v.reshape(batch, seq, width)
    bias = _bias_tiles(rel_bias, t)
    smem = pl.BlockSpec(memory_space=pltpu.SMEM)
    qo = q3
    group = ATTN_BLOCKS_PER_CALL
    hp = ATTN_HEADS_PER_STEP
    for first in range(0, nq, group):
        keys = (first + group) * t
        body = functools.partial(_attn_body, t=t, first=first, count=group, heads=hp,
                                 out_scale=out_scale)
        pair = lambda shape, dtype: [pltpu.VMEM(shape, dtype)] * 2
        rows_spec = pl.BlockSpec((None, group * t, hp * V_DIM),
                                 lambda h, b, p=first // group: (b, p, h))
        qo = pl.pallas_call(
            body, grid=(N_HEADS // hp, batch),
            in_specs=[smem, smem, rows_spec,
                      pl.BlockSpec((None, hp * V_DIM, keys), lambda h, b: (b, h, 0)),
                      pl.BlockSpec((None, keys, hp * V_DIM), lambda h, b: (b, 0, h)),
                      pl.BlockSpec((hp, t, 2 * t), lambda h, b: (h, 0, 0)),
                      pl.BlockSpec((1, V_DIM), lambda h, b: (0, 0))],
            out_specs=rows_spec,
            out_shape=jax.ShapeDtypeStruct((batch, seq, width), BF16),
            input_output_aliases={2: 0},
            scratch_shapes=(pair((2 * t, keys), F32) + pair((2 * t, keys), BF16)
                            + pair((2 * t, LANES), F32) + pair((2 * t, LANES), F32)
                            + pair((2 * t, LANES), F32) + pair((2 * t, V_DIM), F32)),
            compiler_params=_params(("arbitrary", "arbitrary")),
            name=f"prompt_attention_q{first}")(lam, rel_bias, qo, k, v3, bias, subln_g.reshape(1, V_DIM))
    return qo.reshape(batch * seq, width)


def _decode_body(pt_ref, lam_ref, q_ref, kn_ref, vn_ref, rbt_ref, bucket_ref, g_ref, *rest,
                 pages, page, n_steps, out_scale):
    k_refs = rest[:pages]
    v_refs = rest[pages:2 * pages]
    o_ref = rest[2 * pages]
    qexp_sc, bias_sc, m_sc, l_sc, acc_sc = rest[2 * pages + 1:]
    step = pl.program_id(1)
    rows = 2 * N_HEADS
    width = N_HEADS * V_DIM

    @pl.when(step == 0)
    def _():
        row = lax.broadcasted_iota(jnp.int32, (rows, width), 0)
        col = lax.broadcasted_iota(jnp.int32, (rows, width), 1)
        own_qk = (col // HEAD_DIM) == (row % N_HEADS) * 2 + row // N_HEADS
        q = (q_ref[...] * (HEAD_DIM ** -0.5)).astype(BF16).astype(F32)
        qexp = jnp.where(own_qk, jnp.broadcast_to(q, (rows, width)), 0.0)
        qexp_sc[...] = qexp.astype(BF16)
        bias_sc[...] = _bucket_lookup(bucket_ref[...], lambda bk: rbt_ref[:, bk:bk + 1])
        kn = kn_ref[...].astype(BF16).astype(F32)
        s_self = jnp.sum(qexp * kn, axis=1, keepdims=True) + rbt_ref[:, 0:1]
        m_sc[...] = s_self
        l_sc[...] = jnp.ones(l_sc.shape, F32)
        vn = vn_ref[...].astype(BF16).astype(F32)
        acc_sc[...] = jnp.broadcast_to(vn[:, None, :], acc_sc.shape)

    qexp = qexp_sc[...]
    bias_far = rbt_ref[:, NUM_BUCKETS - 1:NUM_BUCKETS]
    is_last = step == n_steps - 1
    s_parts = []
    for p in range(pages):
        s = jnp.dot(qexp, k_refs[p][...].astype(BF16), preferred_element_type=F32)
        if p == pages - 1:
            s = s + jnp.where(is_last, bias_sc[...], bias_far)
        else:
            s = s + bias_far
        s_parts.append(s)
    m_old = m_sc[...]
    m_new = m_old
    for s in s_parts:
        m_new = jnp.maximum(m_new, jnp.max(s, axis=1, keepdims=True))
    a = jnp.exp(m_old - m_new)
    l_new = a * l_sc[...]
    probs = []
    for p in range(pages):
        pr = jnp.exp(s_parts[p] - m_new)
        l_new = l_new + jnp.sum(pr, axis=1, keepdims=True)
        probs.append(pr.astype(BF16))
    probs = jnp.concatenate(probs, axis=1)
    m_sc[...] = m_new
    l_sc[...] = l_new
    for h in range(N_HEADS):
        vh = jnp.concatenate([v_refs[p][pl.ds(h, page, stride=N_HEADS), :].astype(BF16)
                              for p in range(pages)], axis=0)
        acc_sc[h] = a * acc_sc[h] + jnp.dot(probs, vh, preferred_element_type=F32)

    @pl.when(is_last)
    def _():
        nrm = acc_sc[...] / l_new[None]
        r = lax.broadcasted_iota(jnp.int32, nrm.shape, 1)
        hh = lax.broadcasted_iota(jnp.int32, nrm.shape, 0)
        coef = jnp.where(r == hh, 1.0, jnp.where(r == hh + N_HEADS, -lam_ref[0], 0.0))
        d = jnp.sum(coef * nrm, axis=1)
        ms = jnp.mean(d * d, axis=-1, keepdims=True)
        o_ref[...] = (d * lax.rsqrt(ms + LN_EPS) * g_ref[...] * out_scale).astype(o_ref.dtype)


def _sample_attention(q, k_new, v_new, cache_k, cache_v, page_table, rel_bias, lam, subln_g,
                      *, pages, out_scale):
    bs, width = q.shape
    page = cache_k.shape[2]
    n_pages = page_table.shape[1]
    n_steps = n_pages // pages
    past = n_pages * page
    kpos = past - page + jnp.arange(page, dtype=jnp.int32)
    bucket_last = _rel_bucket(past - kpos).reshape(1, page)
    rbt = jnp.tile(rel_bias.T, (2, 1))

    def tok_spec():
        return pl.BlockSpec((None, 1, width), lambda b, s, pt: (b, 0, 0))

    def page_spec(shape, p):
        return pl.BlockSpec((None,) + shape, lambda b, s, pt, p=p: (pt[b, s * pages + p], 0, 0))

    full = lambda shape: pl.BlockSpec(shape, lambda b, s, pt: tuple(0 for _ in shape))
    head_spec = pl.BlockSpec((None, N_HEADS, V_DIM), lambda b, s, pt: (b, 0, 0))
    in_specs = ([pl.BlockSpec(memory_space=pltpu.SMEM), tok_spec(), tok_spec(), head_spec,
                 full((2 * N_HEADS, NUM_BUCKETS)), full((1, page)), full((1, V_DIM))]
                + [page_spec((width, page), p) for p in range(pages)]
                + [page_spec((page * N_HEADS, V_DIM), p) for p in range(pages)])
    rows = 2 * N_HEADS
    body = functools.partial(_decode_body, pages=pages, page=page, n_steps=n_steps,
                             out_scale=out_scale)
    out = pl.pallas_call(
        body,
        grid_spec=pltpu.PrefetchScalarGridSpec(
            num_scalar_prefetch=1, grid=(bs, n_steps), in_specs=in_specs,
            out_specs=head_spec,
            scratch_shapes=[pltpu.VMEM((rows, width), BF16), pltpu.VMEM((rows, page), F32),
                            pltpu.VMEM((rows, 1), F32), pltpu.VMEM((rows, 1), F32),
                            pltpu.VMEM((N_HEADS, rows, V_DIM), F32)]),
        out_shape=jax.ShapeDtypeStruct((bs, N_HEADS, V_DIM), BF16),
        compiler_params=_params(("arbitrary", "arbitrary")),
        name="sample_attention")(
            page_table, lam, q.reshape(bs, 1, width), k_new.reshape(bs, 1, width),
            v_new.reshape(bs, N_HEADS, V_DIM), rbt, bucket_last, subln_g.reshape(1, V_DIM),
            *([cache_k] * pages), *([cache_v] * pages))
    return out.reshape(bs, width)


def _s5_params(a_re, a_im, log_dt, b_re, b_im, c_re, c_im):
    dt = jnp.exp(log_dt)[:, None]
    mag = jnp.exp(a_re * dt)
    ang = a_im * dt
    abar_re = mag * jnp.cos(ang)
    abar_im = mag * jnp.sin(ang)
    den = a_re * a_re + a_im * a_im
    f_re = ((abar_re - 1.0) * a_re + abar_im * a_im) / den
    f_im = (abar_im * a_re - (abar_re - 1.0) * a_im) / den
    bb_re = f_re[..., None] * b_re - f_im[..., None] * b_im
    bb_im = f_re[..., None] * b_im + f_im[..., None] * b_re
    eye = jnp.eye(SLAB_GROUPS, dtype=F32)

    def in_slabs(bb):
        tt = bb.reshape(N_SLABS, SLAB_GROUPS, STATE_DIM, SSM_GROUP).transpose(0, 1, 3, 2)
        full = tt[:, :, :, None, :] * eye[None, :, None, :, None]
        return full.reshape(N_SLABS, SLAB_IN, SLAB_STATE).astype(BF16)

    ab_re = abar_re[..., None] * bb_re - abar_im[..., None] * bb_im
    ab_im = abar_re[..., None] * bb_im + abar_im[..., None] * bb_re

    def out_slabs(cc):
        tt = cc.reshape(N_SLABS, SLAB_GROUPS, SSM_GROUP, STATE_DIM).transpose(0, 1, 3, 2)
        full = tt[:, :, :, None, :] * eye[None, :, None, :, None]
        return full.reshape(N_SLABS, SLAB_STATE, SLAB_IN).astype(BF16)

    return (abar_re.reshape(N_SLABS, SLAB_STATE), abar_im.reshape(N_SLABS, SLAB_STATE),
            in_slabs(bb_re), in_slabs(bb_im), out_slabs(c_re), out_slabs(c_im),
            in_slabs(ab_re), in_slabs(ab_im))


def _glu_out(y, u, d_ref, wglu_ref):
    g = jax.nn.gelu(y + d_ref[...] * u)
    gate = jnp.dot(g.astype(BF16), wglu_ref[...].astype(BF16), preferred_element_type=F32)
    return g * jax.nn.sigmoid(gate)


def _s5_prompt_body(u_ref, ar_ref, ai_ref, bbr_ref, bbi_ref, cr_ref, ci_ref, d_ref, wglu_ref,
                    o_ref, sre_ref, sim_ref, xr_sc, xi_sc, pw_sc, cr_sc, ci_sc, perm_sc, *, batch):
    c = pl.program_id(0)
    sub = SUBLANES
    _, tc, width = u_ref.shape
    n_rows = batch * tc
    t = lax.broadcasted_iota(jnp.int32, (sub, SLAB_STATE), 0)
    first = t < batch

    def cmul(pr, pi, qr, qi):
        return pr * qr - pi * qi, pr * qi + pi * qr

    @pl.when(c == 0)
    def _():
        for j in range(N_SLABS):
            a = (jnp.broadcast_to(ar_ref[j:j + 1, :], t.shape), jnp.broadcast_to(ai_ref[j:j + 1, :], t.shape))
            a2 = cmul(*a, *a)
            for part in range(2):
                pw_sc[j, part] = jnp.where(first, a[part], a2[part])
        cr_sc[...] = jnp.zeros(cr_sc.shape, F32)
        ci_sc[...] = jnp.zeros(ci_sc.shape, F32)
        r = lax.broadcasted_iota(jnp.int32, (n_rows, n_rows), 0)
        k = lax.broadcasted_iota(jnp.int32, (n_rows, n_rows), 1)
        perm_sc[0] = jnp.where(k == (r % batch) * tc + r // batch, 1.0, 0.0).astype(BF16)
        perm_sc[1] = jnp.where(r == (k % batch) * tc + k // batch, 1.0, 0.0).astype(BF16)

    def permute(which, x):
        return jnp.dot(perm_sc[which], x, preferred_element_type=F32)

    u_seq = u_ref[...].reshape(n_rows, width)
    hi = u_seq.astype(BF16)
    rest = u_seq - hi.astype(F32)
    mid = rest.astype(BF16)
    lo = (rest - mid.astype(F32)).astype(BF16)
    u_hi = permute(0, hi)
    ub = u_hi.astype(BF16)
    u = u_hi + permute(0, mid) + permute(0, lo)
    row = lax.broadcasted_iota(jnp.int32, u_hi.shape, 0)
    u_before = jnp.where(row % sub >= batch, pltpu.roll(u_hi, batch, 0), 0.0).astype(BF16)
    for j in range(N_SLABS):
        cols = slice(j * SLAB_IN, (j + 1) * SLAB_IN)
        uj = jnp.concatenate([ub[:, cols], u_before[:, cols]], axis=1)
        xr_sc[j] = jnp.dot(uj, bbr_ref[j], preferred_element_type=F32)
        xi_sc[j] = jnp.dot(uj, bbi_ref[j], preferred_element_type=F32)

    for j in range(N_SLABS):
        pr, pi = cr_sc[j], ci_sc[j]
        for v in range(n_rows // sub):
            rows = slice(v * sub, (v + 1) * sub)
            lr = jnp.where(first, pltpu.roll(pr, batch, 0), pr)
            li = jnp.where(first, pltpu.roll(pi, batch, 0), pi)
            er, ei = cmul(pw_sc[j, 0], pw_sc[j, 1], lr, li)
            pr, pi = xr_sc[j, rows, :] + er, xi_sc[j, rows, :] + ei
            xr_sc[j, rows, :] = pr
            xi_sc[j, rows, :] = pi
        cr_sc[j] = pr
        ci_sc[j] = pi
    sre_ref[...] = cr_sc[...]
    sim_ref[...] = ci_sc[...]

    ys = []
    for j in range(N_SLABS):
        ys.append(jnp.dot(xr_sc[j].astype(BF16), cr_ref[j], preferred_element_type=F32)
                  - jnp.dot(xi_sc[j].astype(BF16), ci_ref[j], preferred_element_type=F32))
    out = _glu_out(jnp.concatenate(ys, axis=1), u, d_ref, wglu_ref).astype(BF16)
    o_ref[...] = permute(1, out).astype(o_ref.dtype).reshape(batch, tc, width)


def _s5_prompt(u, sp, d, w_glu, *, batch, seq, tc):
    assert SUBLANES == 2 * batch, "tile = two positions of every sequence"
    ar, ai, bbr, bbi, cr, ci, abr, abi = sp
    bbr = jnp.concatenate([bbr, abr], axis=1)
    bbi = jnp.concatenate([bbi, abi], axis=1)
    width = u.shape[1]
    rows = batch * tc
    const = lambda shape: pl.BlockSpec(shape, lambda c: tuple(0 for _ in shape))
    block = pl.BlockSpec((batch, tc, width), lambda c: (0, c, 0))
    state_spec = const((N_SLABS, SUBLANES, SLAB_STATE))
    state_shape = jax.ShapeDtypeStruct((N_SLABS, SUBLANES, SLAB_STATE), F32)
    out, s_re, s_im = pl.pallas_call(
        functools.partial(_s5_prompt_body, batch=batch), grid=(seq // tc,),
        in_specs=[block, const(ar.shape), const(ai.shape), const(bbr.shape), const(bbi.shape),
                  const(cr.shape), const(ci.shape), const((1, width)), const(w_glu.shape)],
        out_specs=[block, state_spec, state_spec],
        out_shape=[jax.ShapeDtypeStruct((batch, seq, width), BF16), state_shape, state_shape],
        scratch_shapes=[pltpu.VMEM((N_SLABS, rows, SLAB_STATE), F32),
                        pltpu.VMEM((N_SLABS, rows, SLAB_STATE), F32),
                        pltpu.VMEM((N_SLABS, 2, SUBLANES, SLAB_STATE), F32),
                        pltpu.VMEM((N_SLABS, SUBLANES, SLAB_STATE), F32),
                        pltpu.VMEM((N_SLABS, SUBLANES, SLAB_STATE), F32),
                        pltpu.VMEM((2, rows, rows), BF16)],
        compiler_params=_params(("arbitrary",)),
        name="s5_prompt")(u.reshape(batch, seq, width), ar, ai, bbr, bbi, cr, ci,
                          d.reshape(1, width), w_glu)
    return out.reshape(batch * seq, width), s_re, s_im


def _s5_sample_body(u_ref, x0r_ref, x0i_ref, ar_ref, ai_ref, bbr_ref, bbi_ref, cr_ref, ci_ref,
                    d_ref, wglu_ref, o_ref, sre_ref, sim_ref):
    u = u_ref[...]
    ub = u.astype(BF16)
    ys = []
    for j in range(N_SLABS):
        uj = ub[:, j * SLAB_IN:(j + 1) * SLAB_IN]
        ar = ar_ref[j:j + 1, :]
        ai = ai_ref[j:j + 1, :]
        x0r = x0r_ref[j]
        x0i = x0i_ref[j]
        xr = ar * x0r - ai * x0i + jnp.dot(uj, bbr_ref[j], preferred_element_type=F32)
        xi = ar * x0i + ai * x0r + jnp.dot(uj, bbi_ref[j], preferred_element_type=F32)
        sre_ref[j] = xr
        sim_ref[j] = xi
        ys.append(jnp.dot(xr.astype(BF16), cr_ref[j], preferred_element_type=F32)
                  - jnp.dot(xi.astype(BF16), ci_ref[j], preferred_element_type=F32))
    o_ref[...] = _glu_out(jnp.concatenate(ys, axis=1), u, d_ref, wglu_ref).astype(o_ref.dtype)


def _s5_sample(u, x0_re, x0_im, sp, d, w_glu):
    ar, ai, bbr, bbi, cr, ci = sp[:6]
    bs, width = u.shape
    state_shape = jax.ShapeDtypeStruct((N_SLABS, bs, SLAB_STATE), F32)
    return pl.pallas_call(
        _s5_sample_body,
        out_shape=[jax.ShapeDtypeStruct((bs, width), BF16), state_shape, state_shape],
        compiler_params=pltpu.CompilerParams(vmem_limit_bytes=VMEM_LIMIT_BYTES),
        name="s5_sample")(u, x0_re, x0_im, ar, ai, bbr, bbi, cr, ci, d.reshape(1, width), w_glu)


def _gated(c):
    gate = c[:, :LANES]
    return gate * jax.nn.sigmoid(gate) * c[:, LANES:]


def _up_prompt_body(h_ref, wg_ref, wv_ref, cwg_ref, cwv_ref, cbg_ref, cbv_ref,
                    act_ref, cg_ref, cv_ref, w_sc, *, rows):
    w_sc[:, 0:LANES] = wg_ref[...].astype(BF16)
    w_sc[:, LANES:2 * LANES] = wv_ref[...].astype(BF16)
    cw = jnp.concatenate([cwg_ref[...], cwv_ref[...]], axis=1)
    cb = jnp.concatenate([cbg_ref[...], cbv_ref[...]], axis=1)
    seq = h_ref.shape[0]
    sub = SUBLANES
    row8 = lax.broadcasted_iota(jnp.int32, (sub, 2 * LANES), 0)

    prev = jnp.zeros((sub, 2 * LANES), F32)
    for c in range(seq // rows):
        up = jnp.dot(h_ref[c * rows:(c + 1) * rows, :], w_sc[...], preferred_element_type=F32)
        cur = cb + cw[CONV_W - 1:CONV_W] * up
        for back in range(1, CONV_W):
            rolled = pltpu.roll(up, back, 0)
            head = jnp.where(row8 < back, pltpu.roll(prev, back, 0), rolled[0:sub])
            shifted = jnp.concatenate([head, rolled[sub:]], axis=0)
            cur = cur + cw[CONV_W - 1 - back:CONV_W - back] * shifted
        act_ref[c * rows:(c + 1) * rows, :] = _gated(cur).astype(act_ref.dtype)
        prev = up[rows - sub:, :]
    tail = prev[sub - (CONV_W - 1):, :]
    cg_ref[...] = tail[:, :LANES]
    cv_ref[...] = tail[:, LANES:]


def _up_prompt(hb, w_up, conv_w, conv_b, *, batch, seq):
    d_model = hb.shape[1]
    d_ff = w_up.shape[1] // 2
    nb = d_ff // LANES
    cb = conv_b.reshape(1, 2 * d_ff)
    col = lambda off: (lambda b, j: (0, j + off))
    tail_spec = pl.BlockSpec((None, CONV_W - 1, LANES), lambda b, j: (b, 0, j))
    tail_shape = jax.ShapeDtypeStruct((batch, CONV_W - 1, d_ff), F32)
    return pl.pallas_call(
        functools.partial(_up_prompt_body, rows=UP_ROW_CHUNK), grid=(batch, nb),
        scratch_shapes=[pltpu.VMEM((d_model, 2 * LANES), BF16)],
        in_specs=[pl.BlockSpec((seq, d_model), lambda b, j: (b, 0)),
                  pl.BlockSpec((d_model, LANES), col(0)), pl.BlockSpec((d_model, LANES), col(nb)),
                  pl.BlockSpec((CONV_W, LANES), col(0)), pl.BlockSpec((CONV_W, LANES), col(nb)),
                  pl.BlockSpec((1, LANES), col(0)), pl.BlockSpec((1, LANES), col(nb))],
        out_specs=[pl.BlockSpec((seq, LANES), lambda b, j: (b, j)), tail_spec, tail_spec],
        out_shape=[jax.ShapeDtypeStruct((batch * seq, d_ff), BF16), tail_shape, tail_shape],
        compiler_params=_params(("arbitrary", "arbitrary")),
        name="up_prompt")(hb, w_up, w_up, conv_w, conv_w, cb, cb)


def _up_sample_body(h_ref, wg_ref, wva_ref, wvb_ref, cwg_ref, cwva_ref, cwvb_ref, cbg_ref, cbva_ref,
                    cbvb_ref, sg_ref, sva_ref, svb_ref, act_ref, cg_ref, cv_ref):
    w2 = 2 * LANES
    w = jnp.concatenate([wg_ref[...].astype(BF16), wva_ref[...].astype(BF16),
                         wvb_ref[...].astype(BF16)], axis=1)
    up = jnp.dot(h_ref[...], w, preferred_element_type=F32)
    cw = jnp.concatenate([cwg_ref[...], cwva_ref[...], cwvb_ref[...]], axis=1)
    cb = jnp.concatenate([cbg_ref[...], cbva_ref[...], cbvb_ref[...]], axis=1)
    c = cb + cw[CONV_W - 1:CONV_W] * up
    for tap in range(CONV_W - 1):
        st = jnp.concatenate([sg_ref[tap], sva_ref[tap], svb_ref[tap]], axis=1)
        c = c + cw[tap:tap + 1] * st
    gate = c[:, 0:w2]
    act_ref[...] = (gate * jax.nn.sigmoid(gate) * c[:, w2:2 * w2]).astype(act_ref.dtype)
    for tap in range(1, CONV_W - 1):
        cg_ref[tap - 1] = sg_ref[tap]
        cv_ref[tap - 1] = jnp.concatenate([sva_ref[tap], svb_ref[tap]], axis=1)
    cg_ref[CONV_W - 2] = up[:, 0:w2]
    cv_ref[CONV_W - 2] = up[:, w2:2 * w2]


def _up_sample(hb, w_up, conv_w, conv_b, state):
    bs, d_model = hb.shape
    d_ff = w_up.shape[1] // 2
    nb = d_ff // LANES
    n_pairs = (nb + 1) // 2
    last = 2 * nb - 1
    w2 = 2 * LANES
    cb = conv_b.reshape(1, 2 * d_ff)
    gate = lambda shape: pl.BlockSpec(shape, lambda j: (0,) * (len(shape) - 1) + (j,))
    val = lambda shape, k: pl.BlockSpec(
        shape, lambda j: (0,) * (len(shape) - 1) + (jnp.minimum(nb + 2 * j + k, last),))
    operands = []
    specs = [pl.BlockSpec((bs, d_model), lambda j: (0, 0))]
    for arr, lead in ((w_up, (d_model,)), (conv_w, (CONV_W,)), (cb, (1,)), (state, (CONV_W - 1, bs))):
        specs += [gate(lead + (w2,)), val(lead + (LANES,), 0), val(lead + (LANES,), 1)]
        operands += [arr, arr, arr]
    tail_spec = pl.BlockSpec((CONV_W - 1, bs, w2), lambda j: (0, 0, j))
    tail_shape = jax.ShapeDtypeStruct((CONV_W - 1, bs, d_ff), F32)
    return pl.pallas_call(
        _up_sample_body, grid=(n_pairs,), in_specs=specs,
        out_specs=[pl.BlockSpec((bs, w2), lambda j: (0, j)), tail_spec, tail_spec],
        out_shape=[jax.ShapeDtypeStruct((bs, d_ff), BF16), tail_shape, tail_shape],
        compiler_params=_params(("arbitrary",)),
        name="up_sample")(hb, *operands)


def _one(x):
    return (x,)


def _sigmoid_out(acc):
    return (jax.nn.sigmoid(acc),)


def _both(acc):
    return (acc, acc)


def _merge(pa, ps, ga, gs):
    return (ga * pa + gs * ps,)


def _layer(xp, xs, w, *, batch, seq, alpha, attend_p, attend_s, ssm_p, ssm_s, up_p, up_s):
    d_model = xp.shape[1]
    qk_w = N_HEADS * 2 * HEAD_DIM
    v_w = N_HEADS * V_DIM
    ssm_w = d_model // 2
    w_in = w["w_in"]
    g1, b1 = w["ln1_g"].reshape(1, d_model), w["ln1_b"].reshape(1, d_model)
    big = dict(tm=MM_ROWS, tn=MM_COLS)
    c = 0
    (q,), (q_s,), xb = _matmul([(xp, xs, w_in, c)], [], [BF16], _one, n_cols=qk_w, name="proj_q",
                               emit_lhs=True, **big)
    c += qk_w
    kt, ktb, k_s = _proj_transposed(xb, xs, w_in, c, n_cols=qk_w, batch=batch, seq=seq, tm=MM_ROWS,
                                    name="proj_kt")
    c += qk_w
    c_v = c
    c += v_w
    (u,), (u_s,) = _matmul([(xb, xs, w_in, c)], [], [F32], _one, n_cols=ssm_w, name="proj_u", **big)
    c += ssm_w
    wide = dict(tm=MM_ROWS, tn=d_model)
    (ga,), (ga_s,) = _matmul([(xb, xs, w_in, c)], [], [F32], _sigmoid_out, n_cols=d_model, name="gate_a", **wide)
    c += d_model
    (gs,), (gs_s,) = _matmul([(xb, xs, w_in, c)], [], [F32], _sigmoid_out, n_cols=d_model, name="gate_s", **wide)
    (v, vb), (v_s, _) = _matmul([(xb, xs, w_in, c_v)], [], [F32, BF16], _both, n_cols=v_w, name="proj_v", **big)

    attn = attend_p(q, ktb, vb)
    ssm_out, re_p, im_p = ssm_p(u)
    ssm_out_s, re_s, im_s = ssm_s(u_s)
    attn_s = attend_s(q_s, k_s, v_s)

    (merged,), (merged_s,) = _matmul(
        [(attn, attn_s, w["w_proj_attn"], 0), (ssm_out, ssm_out_s, w["w_proj_ssm"], 0)],
        [(ga, ga_s), (gs, gs_s)], [BF16], _merge, n_cols=d_model, tm=LN_MM_ROWS, tn=d_model, name="merge")

    def post_ln1(acc, res, g, b):
        h = _ln(alpha * res + acc, g, b)
        return h, h

    (h, hb), (h_s, hb_s) = _matmul([(merged, merged_s, w["w_out"], 0)], [(xp, xs), g1, b1], [F32, BF16],
                                   post_ln1, n_cols=d_model, tm=LN_MM_ROWS, tn=d_model, name="out_proj_ln1")
    act, conv_p = up_p(hb)
    act_s, conv_s = up_s(hb_s)
    (r2,), (r2_s,) = _matmul([(act, act_s, w["w_down"], 0)], [(h, h_s)], [F32],
                             lambda acc, res: (alpha * res + acc,), n_cols=d_model,
                             tm=DOWN_ROWS, tn=DOWN_COLS, name="down_proj")
    y = _layer_norm(r2, w["ln2_g"], w["ln2_b"], tm=LN_ROWS, name="ln2")
    y_s = _layer_norm(r2_s, w["ln2_g"], w["ln2_b"], tm=LN_ROWS, name="ln2_sample")
    return (y, kt, v, re_p, im_p, conv_p), (y_s, k_s, v_s, re_s, im_s, conv_s)


def kernel(x_prompt, x_sample, cache_k, cache_v, state_ssm_re, state_ssm_im, state_conv, page_table, rel_bias, w_in, lambda_q1, lambda_k1, lambda_q2, lambda_k2, subln_g, ssm_a_re, ssm_a_im, ssm_log_dt, ssm_b_re, ssm_b_im, ssm_c_re, ssm_c_im, ssm_d, w_glu, w_proj_attn, w_proj_ssm, w_out, ln1_g, ln1_b, w_up, conv_w, conv_b, w_down, ln2_g, ln2_b):
    depth = w_in.shape[0]
    assert depth == 1, "single-layer trunk"
    bp, seq, d_model = x_prompt.shape
    bs, dec_seq, _ = x_sample.shape
    assert dec_seq == 1
    n_pool, page = cache_k.shape[1], cache_k.shape[2]
    d_ff = w_down.shape[1]
    n_groups = ssm_a_re.shape[1]
    assert n_groups == N_SLABS * SLAB_GROUPS and d_ff % LANES == 0
    alpha = (2.0 * depth) ** 0.25
    width = N_HEADS * V_DIM

    hp = x_prompt.reshape(bp * seq, d_model)
    hs = x_sample.reshape(bs, d_model)
    outs = {}
    for l in range(depth):
        lam_init = 0.8 - 0.6 * math.exp(-0.3 * l)
        out_scale = 1.0 - lam_init
        lam = (jnp.exp(jnp.sum(lambda_q1[l] * lambda_k1[l]))
               - jnp.exp(jnp.sum(lambda_q2[l] * lambda_k2[l])) + lam_init).reshape(1)
        w = dict(w_in=w_in[l], w_proj_attn=w_proj_attn[l], w_proj_ssm=w_proj_ssm[l], w_out=w_out[l],
                 ln1_g=ln1_g[l], ln1_b=ln1_b[l], w_down=w_down[l], ln2_g=ln2_g[l], ln2_b=ln2_b[l])
        sp = _s5_params(ssm_a_re[l], ssm_a_im[l], ssm_log_dt[l], ssm_b_re[l], ssm_b_im[l],
                        ssm_c_re[l], ssm_c_im[l])

        def attend_p(q, kt, v):
            return _prompt_attention(q, kt, v, rel_bias, lam, subln_g[l], batch=bp, seq=seq,
                                     t=ATTN_BLOCK, out_scale=out_scale)

        def ssm_p(u):
            o, sr, si = _s5_prompt(u, sp, ssm_d[l], w_glu[l], batch=bp, seq=seq, tc=S5_POSITIONS)
            last = lambda st: st[:, SUBLANES - bp:, :].transpose(1, 0, 2)
            return o, last(sr), last(si)

        def up_p(hb):
            act, cg, cv = _up_prompt(hb, w_up[l], conv_w[l], conv_b[l], batch=bp, seq=seq)
            return act, jnp.concatenate([cg, cv], axis=-1)

        ck = cache_k[l].transpose(0, 2, 3, 4, 1).reshape(n_pool, width, page)
        cv_ = cache_v[l].reshape(n_pool, page * N_HEADS, V_DIM)

        def attend_s(q, k, v):
            return _sample_attention(q.astype(F32), k, v, ck, cv_, page_table, rel_bias, lam,
                                     subln_g[l], pages=DECODE_PAGES, out_scale=out_scale)

        def ssm_s(u):
            x0r = state_ssm_re[l].reshape(bs, N_SLABS, SLAB_STATE).transpose(1, 0, 2)
            x0i = state_ssm_im[l].reshape(bs, N_SLABS, SLAB_STATE).transpose(1, 0, 2)
            o, sr, si = _s5_sample(u, x0r, x0i, sp, ssm_d[l], w_glu[l])
            return o, sr.transpose(1, 0, 2), si.transpose(1, 0, 2)

        def up_s(hb):
            act, cg, cv = _up_sample(hb, w_up[l], conv_w[l], conv_b[l],
                                     state_conv[l].transpose(1, 0, 2))
            return act, jnp.concatenate([cg, cv], axis=-1).transpose(1, 0, 2)

        (hp, kt_p, v_p, re_p, im_p, c_p), (hs, k_s, v_s, re_s, im_s, c_s) = _layer(
            hp, hs, w, batch=bp, seq=seq, alpha=alpha, attend_p=attend_p, attend_s=attend_s,
            ssm_p=ssm_p, ssm_s=ssm_s, up_p=up_p, up_s=up_s)
        k_p = kt_p.reshape(bp, N_HEADS, 2, HEAD_DIM, seq).transpose(0, 4, 1, 2, 3)

        for name, val in (("kp", k_p.reshape(bp, seq, N_HEADS, 2, HEAD_DIM)),
                          ("vp", v_p.reshape(bp, seq, N_HEADS, V_DIM)),
                          ("rep", re_p.reshape(bp, n_groups, STATE_DIM)),
                          ("imp", im_p.reshape(bp, n_groups, STATE_DIM)),
                          ("cp", c_p),
                          ("ks", k_s.reshape(bs, 1, N_HEADS, 2, HEAD_DIM)),
                          ("vs", v_s.reshape(bs, 1, N_HEADS, V_DIM)),
                          ("res", re_s.reshape(bs, n_groups, STATE_DIM)),
                          ("ims", im_s.reshape(bs, n_groups, STATE_DIM)),
                          ("cs", c_s)):
            outs.setdefault(name, []).append(val)

    st = {k: jnp.stack(v, axis=0) for k, v in outs.items()}
    return (hp.reshape(bp, seq, d_model), hs.reshape(bs, 1, d_model), st["kp"], st["vp"], st["rep"],
            st["imp"], st["cp"], st["ks"], st["vs"], st["res"], st["ims"], st["cs"])
```

```python
import functools
import math

import jax
import jax.numpy as jnp
from jax import lax
from jax.experimental import pallas as pl
from jax.experimental.pallas import tpu as pltpu

F32 = jnp.float32
BF16 = jnp.bfloat16

N_HEADS = 8
HEAD_DIM = 64
V_DIM = 2 * HEAD_DIM
SSM_GROUP = 16
STATE_DIM = 64
CONV_W = 3
NUM_BUCKETS = 32
MAX_EXACT = NUM_BUCKETS // 2
MAX_DISTANCE = 128
LN_EPS = 1e-5
NEG_INF = -1e30

VMEM_LIMIT_BYTES = 56 * 1024 * 1024
LANES = 128
SLAB_GROUPS = 8
N_SLABS = 8
SLAB_IN = SLAB_GROUPS * SSM_GROUP
SLAB_STATE = SLAB_GROUPS * STATE_DIM
SUBLANES = 8
MM_ROWS, MM_COLS = 1024, 1024
LN_MM_ROWS = 512
DOWN_ROWS, DOWN_COLS = 512, 512
LN_ROWS = 512
ATTN_BLOCK = 256
ATTN_KEY_BLOCK = 256
ATTN_HEADS_PER_STEP = 4
ATTN_BLOCKS_PER_CALL = 2
S5_POSITIONS = 64
DECODE_PAGES = 16
UP_ROW_CHUNK = 512


def _params(sem):
    return pltpu.CompilerParams(dimension_semantics=sem, vmem_limit_bytes=VMEM_LIMIT_BYTES)


def _mm_body(*refs, n_pairs, n_extra, n_out, epilogue, emit_lhs):
    it = iter(refs)
    take = lambda n: [next(it) for _ in range(n)]
    x_refs, xs_refs, w_refs = take(n_pairs), take(n_pairs), take(n_pairs)
    extra, extra_s = take(n_extra), take(n_extra)
    outs, outs_s = take(n_out), take(n_out)
    lhs_out = take(1) if emit_lhs else []
    wbf = take(n_pairs)

    def apply(lhs_refs, extra_refs, out_refs, keep):
        lhs = [x[...].astype(BF16) for x in lhs_refs]
        for ref in keep:
            ref[...] = lhs[0]
        accs = [jnp.dot(x, s[...], preferred_element_type=F32) for x, s in zip(lhs, wbf)]
        for o, r in zip(out_refs, epilogue(*accs, *[e[...] for e in extra_refs])):
            o[...] = r.astype(o.dtype)

    @pl.when(pl.program_id(1) == 0)
    def _():
        for w, s in zip(w_refs, wbf):
            s[...] = w[...].astype(BF16)
        apply(xs_refs, extra_s, outs_s, [])

    apply(x_refs, extra, outs, lhs_out)


def _matmul(pairs, extras, out_dtypes, epilogue, *, n_cols, tm, tn, name, emit_lhs=False):
    m = pairs[0][0].shape[0]
    ms = pairs[0][1].shape[0]
    grid = (n_cols // tn, m // tm)
    assert not emit_lhs or grid[0] == 1
    w_mode = dict(pipeline_mode=pl.Buffered(1)) if grid[0] == 1 else {}
    specs_x, specs_xs, specs_w, scratch = [], [], [], []
    for x, xs, w, c0 in pairs:
        k = x.shape[1]
        specs_x.append(pl.BlockSpec((tm, k), lambda j, i: (i, 0)))
        specs_xs.append(pl.BlockSpec((ms, k), lambda j, i: (0, 0)))
        specs_w.append(pl.BlockSpec((k, tn), lambda j, i, off=c0 // tn: (0, j + off), **w_mode))
        scratch.append(pltpu.VMEM((k, tn), BF16))
    specs_e, specs_es, args_e, args_es = [], [], [], []
    for e in extras:
        if isinstance(e, tuple):
            specs_e.append(pl.BlockSpec((tm, tn), lambda j, i: (i, j)))
            specs_es.append(pl.BlockSpec((ms, tn), lambda j, i: (0, j)))
            args_e.append(e[0])
            args_es.append(e[1])
        else:
            specs_e.append(pl.BlockSpec((1, tn), lambda j, i: (0, j)))
            specs_es.append(pl.BlockSpec((1, tn), lambda j, i: (0, j)))
            args_e.append(e)
            args_es.append(e)
    out_shape = ([jax.ShapeDtypeStruct((m, n_cols), d) for d in out_dtypes]
                 + [jax.ShapeDtypeStruct((ms, n_cols), d) for d in out_dtypes])
    out_specs = ([pl.BlockSpec((tm, tn), lambda j, i: (i, j)) for _ in out_dtypes]
                 + [pl.BlockSpec((ms, tn), lambda j, i: (0, j)) for _ in out_dtypes])
    if emit_lhs:
        k0 = pairs[0][0].shape[1]
        out_shape.append(jax.ShapeDtypeStruct((m, k0), BF16))
        out_specs.append(pl.BlockSpec((tm, k0), lambda j, i: (i, 0)))
    body = functools.partial(_mm_body, n_pairs=len(pairs), n_extra=len(extras),
                             n_out=len(out_dtypes), epilogue=epilogue, emit_lhs=emit_lhs)
    outs = pl.pallas_call(
        body, grid=grid, in_specs=specs_x + specs_xs + specs_w + specs_e + specs_es,
        out_specs=out_specs, out_shape=out_shape, scratch_shapes=scratch,
        compiler_params=_params(("arbitrary", "arbitrary")), name=name)(
            *[p[0] for p in pairs], *[p[1] for p in pairs], *[p[2] for p in pairs],
            *args_e, *args_es)
    n = len(out_dtypes)
    if emit_lhs:
        return outs[:n], outs[n:2 * n], outs[2 * n]
    return outs[:n], outs[n:]


def _proj_t_body(x_ref, xs_ref, w_ref, o_ref, ob_ref, os_ref, wt_sc):
    @pl.when(pl.program_id(0) == 0)
    def _():
        for c in range(w_ref.shape[1] // LANES):
            cols = slice(c * LANES, (c + 1) * LANES)
            wt_sc[cols, :] = w_ref[:, cols].T.astype(BF16)
        os_ref[...] = lax.dot_general(xs_ref[...].astype(BF16), wt_sc[...], (((1,), (1,)), ((), ())),
                                      preferred_element_type=F32)

    kt = lax.dot_general(wt_sc[...], x_ref[...], (((1,), (1,)), ((), ())),
                         preferred_element_type=F32)
    o_ref[...] = kt
    ob_ref[...] = kt.astype(BF16)


def _proj_transposed(x, xs, w, c0, *, n_cols, batch, seq, tm, name):
    k = x.shape[1]
    ms = xs.shape[0]
    per_b = seq // tm
    out_spec = pl.BlockSpec((None, n_cols, tm), lambda i: (i // per_b, 0, i % per_b))
    return pl.pallas_call(
        _proj_t_body, grid=(batch * per_b,),
        in_specs=[pl.BlockSpec((tm, k), lambda i: (i, 0)),
                  pl.BlockSpec((ms, k), lambda i: (0, 0)),
                  pl.BlockSpec((k, n_cols), lambda i, off=c0 // n_cols: (0, off),
                               pipeline_mode=pl.Buffered(1))],
        out_specs=[out_spec, out_spec, pl.BlockSpec((ms, n_cols), lambda i: (0, 0))],
        out_shape=[jax.ShapeDtypeStruct((batch, n_cols, seq), F32),
                   jax.ShapeDtypeStruct((batch, n_cols, seq), BF16),
                   jax.ShapeDtypeStruct((ms, n_cols), F32)],
        scratch_shapes=[pltpu.VMEM((n_cols, k), BF16)],
        compiler_params=_params(("arbitrary",)), name=name)(x, xs, w)


def _ln(x, g, b):
    mu = jnp.mean(x, axis=-1, keepdims=True)
    xc = x - mu
    var = jnp.mean(xc * xc, axis=-1, keepdims=True)
    return xc * lax.rsqrt(var + LN_EPS) * g + b


def _ln_body(x_ref, g_ref, b_ref, o_ref):
    o_ref[...] = _ln(x_ref[...], g_ref[...], b_ref[...])


def _layer_norm(x, g, b, *, tm, name):
    m, d = x.shape
    tm = min(tm, m)
    return pl.pallas_call(
        _ln_body, grid=(m // tm,),
        in_specs=[pl.BlockSpec((tm, d), lambda i: (i, 0)),
                  pl.BlockSpec((1, d), lambda i: (0, 0)),
                  pl.BlockSpec((1, d), lambda i: (0, 0))],
        out_specs=pl.BlockSpec((tm, d), lambda i: (i, 0)),
        out_shape=jax.ShapeDtypeStruct((m, d), F32),
        compiler_params=_params(("arbitrary",)), name=name)(x, g.reshape(1, d), b.reshape(1, d))


def _rel_bucket(n):
    n = jnp.maximum(n, 0)
    nf = jnp.maximum(n, 1).astype(F32)
    large = MAX_EXACT + jnp.floor(jnp.log(nf / MAX_EXACT) / math.log(MAX_DISTANCE / MAX_EXACT)
                                  * (NUM_BUCKETS - MAX_EXACT)).astype(jnp.int32)
    large = jnp.minimum(large, NUM_BUCKETS - 1)
    return jnp.where(n < MAX_EXACT, n, large)


def _bucket_lookup(bucket, table_fn):
    out = jnp.zeros(jnp.broadcast_shapes(bucket.shape, table_fn(0).shape), F32)
    for b in range(NUM_BUCKETS):
        out = out + jnp.where(bucket == b, table_fn(b), 0.0)
    return out


def _bias_tile_body(rb_ref, bucket_ref, o_ref):
    h = pl.program_id(0)
    t, t2 = o_ref.shape
    line = _bucket_lookup(bucket_ref[...], lambda bk: rb_ref[bk, h])
    full = pltpu.roll(jnp.broadcast_to(line, (t, line.shape[1])), 0, 1, stride=1, stride_axis=0)
    o_ref[...] = full[:, :t2]


def _bias_tiles(rel_bias, t):
    x = jnp.arange(4 * t, dtype=jnp.int32)
    buckets = _rel_bucket(jnp.where(x < 2 * t, t - x, 5 * t - x))[None, :]
    return pl.pallas_call(
        _bias_tile_body, grid=(N_HEADS,),
        in_specs=[pl.BlockSpec(memory_space=pltpu.SMEM), pl.BlockSpec((1, 4 * t), lambda h: (0, 0))],
        out_specs=pl.BlockSpec((None, t, 2 * t), lambda h: (h, 0, 0)),
        out_shape=jax.ShapeDtypeStruct((N_HEADS, t, 2 * t), F32),
        compiler_params=_params(("arbitrary",)), name="bias_tiles")(rel_bias, buckets)


def _attn_body(lam_ref, rb_ref, q_ref, k_ref, v_ref, bias_ref, g_ref, o_ref, *scratch,
               t, first, count, heads, out_scale):
    for sub in range(count):
        i = first + sub
        _attn_block(lam_ref, rb_ref, q_ref, k_ref, v_ref, bias_ref, g_ref, o_ref, scratch, t=t,
                    n_far=max(i - 1, 0), near=min(i + 1, 2) * t, heads=heads, out_scale=out_scale,
                    row0=sub * t)


def _attn_block(lam_ref, rb_ref, q_ref, k_ref, v_ref, bias_ref, g_ref, o_ref, scratch,
                *, t, n_far, near, heads, out_scale, row0):
    rows_q = slice(row0, row0 + t)
    s_bufs, p_bufs, m_bufs, mf_bufs, l_bufs, a_bufs = (scratch[0:2], scratch[2:4], scratch[4:6],
                                                       scratch[6:8], scratch[8:10], scratch[10:12])
    hg = pl.program_id(0)
    far = n_far * t
    kb_w = ATTN_KEY_BLOCK
    n_kb = (far + near) // kb_w
    n_tiles = kb_w // LANES

    def fold(x, op):
        out = x[:, 0:LANES]
        for c in range(1, n_tiles):
            out = op(out, x[:, c * LANES:(c + 1) * LANES])
        return out

    def head_cols(hh):
        return slice(hh * V_DIM, (hh + 1) * V_DIM)

    def score_block(hh, kb):
        par = hh % 2
        q = q_ref[rows_q, head_cols(hh)] * (HEAD_DIM ** -0.5)
        lane = lax.broadcasted_iota(jnp.int32, q.shape, 1)
        zero = jnp.zeros_like(q)
        q2 = jnp.concatenate([jnp.where(lane < HEAD_DIM, q, zero), jnp.where(lane >= HEAD_DIM, q, zero)],
                             axis=0)
        cols = slice(kb * kb_w, (kb + 1) * kb_w)
        s = jnp.dot(q2, k_ref[head_cols(hh), cols], preferred_element_type=F32)
        bias_far = rb_ref[NUM_BUCKETS - 1, hg * heads + hh]
        if kb * kb_w < far:
            top = fold(s, jnp.maximum) + bias_far
        else:
            off = kb * kb_w - far
            lo = 2 * t - near + off
            bias = bias_ref[hh, :, lo:lo + kb_w]
            row = lax.broadcasted_iota(jnp.int32, (t, kb_w), 0)
            col = lax.broadcasted_iota(jnp.int32, (t, kb_w), 1)
            keep = col + (off - (near - t)) <= row
            s = jnp.where(jnp.concatenate([keep, keep], axis=0),
                          s + jnp.concatenate([bias, bias], axis=0), NEG_INF)
            top = fold(s, jnp.maximum)
        s_bufs[par][:, cols] = s
        if kb == 0:
            m_bufs[par][...] = top
        else:
            m_bufs[par][...] = jnp.maximum(m_bufs[par][...], top)
        if kb == n_kb - 1:
            m = jnp.broadcast_to(jnp.max(m_bufs[par][...], axis=1, keepdims=True), m_bufs[par].shape)
            m_bufs[par][...] = m
            mf_bufs[par][...] = m - bias_far

    def exp_block(hh, kb):
        par = hh % 2
        cols = slice(kb * kb_w, (kb + 1) * kb_w)
        m = (mf_bufs if kb * kb_w < far else m_bufs)[par][...]
        p = jnp.exp(s_bufs[par][:, cols] - jnp.concatenate([m] * n_tiles, axis=1))
        if kb == 0:
            l_bufs[par][...] = fold(p, jnp.add)
        else:
            l_bufs[par][...] += fold(p, jnp.add)
        p_bufs[par][:, cols] = p.astype(BF16)

    def value_block(hh, kb):
        par = hh % 2
        rows = slice(kb * kb_w, (kb + 1) * kb_w)
        pv = jnp.dot(p_bufs[par][:, rows], v_ref[rows, head_cols(hh)], preferred_element_type=F32)
        if kb == 0:
            a_bufs[par][...] = pv
        else:
            a_bufs[par][...] += pv
        if kb == n_kb - 1:
            nrm = a_bufs[par][...] / jnp.sum(l_bufs[par][...], axis=1, keepdims=True)
            o = nrm[0:t] - lam_ref[0] * nrm[t:2 * t]
            ms = jnp.mean(o * o, axis=-1, keepdims=True)
            o_ref[rows_q, head_cols(hh)] = (o * lax.rsqrt(ms + LN_EPS) * g_ref[...] * out_scale
                                            ).astype(o_ref.dtype)

    for stage in range(heads + 2):
        for kb in range(n_kb):
            if stage < heads:
                score_block(stage, kb)
            if 0 <= stage - 1 < heads:
                exp_block(stage - 1, kb)
            if 0 <= stage - 2 < heads:
                value_block(stage - 2, kb)


def _prompt_attention(q, k, v, rel_bias, lam, subln_g, *, batch, seq, t, out_scale):
    assert t >= MAX_DISTANCE
    nq = seq // t
    width = N_HEADS * V_DIM
    q3 = q.reshape(batch, seq, width)
    v3 = v.reshape(batch, seq, width)
    bias = _bias_tiles(rel_bias, t)
    smem = pl.BlockSpec(memory_space=pltpu.SMEM)
    qo = q3
    group = ATTN_BLOCKS_PER_CALL
    hp = ATTN_HEADS_PER_STEP
    for first in range(0, nq, group):
        keys = (first + group) * t
        body = functools.partial(_attn_body, t=t, first=first, count=group, heads=hp,
                                 out_scale=out_scale)
        pair = lambda shape, dtype: [pltpu.VMEM(shape, dtype)] * 2
        rows_spec = pl.BlockSpec((None, group * t, hp * V_DIM),
                                 lambda h, b, p=first // group: (b, p, h))
        qo = pl.pallas_call(
            body, grid=(N_HEADS // hp, batch),
            in_specs=[smem, smem, rows_spec,
                      pl.BlockSpec((None, hp * V_DIM, keys), lambda h, b: (b, h, 0)),
                      pl.BlockSpec((None, keys, hp * V_DIM), lambda h, b: (b, 0, h)),
                      pl.BlockSpec((hp, t, 2 * t), lambda h, b: (h, 0, 0)),
                      pl.BlockSpec((1, V_DIM), lambda h, b: (0, 0))],
            out_specs=rows_spec,
            out_shape=jax.ShapeDtypeStruct((batch, seq, width), BF16),
            input_output_aliases={2: 0},
            scratch_shapes=(pair((2 * t, keys), F32) + pair((2 * t, keys), BF16)
                            + pair((2 * t, LANES), F32) + pair((2 * t, LANES), F32)
                            + pair((2 * t, LANES), F32) + pair((2 * t, V_DIM), F32)),
            compiler_params=_params(("arbitrary", "arbitrary")),
            name=f"prompt_attention_q{first}")(lam, rel_bias, qo, k, v3, bias, subln_g.reshape(1, V_DIM))
    return qo.reshape(batch * seq, width)


def _decode_body(pt_ref, lam_ref, q_ref, kn_ref, vn_ref, rbt_ref, bucket_ref, g_ref, *rest,
                 pages, page, n_steps, out_scale):
    k_refs = rest[:pages]
    v_refs = rest[pages:2 * pages]
    o_ref = rest[2 * pages]
    qexp_sc, bias_sc, m_sc, l_sc, acc_sc = rest[2 * pages + 1:]
    step = pl.program_id(1)
    rows = 2 * N_HEADS
    width = N_HEADS * V_DIM

    @pl.when(step == 0)
    def _():
        row = lax.broadcasted_iota(jnp.int32, (rows, width), 0)
        col = lax.broadcasted_iota(jnp.int32, (rows, width), 1)
        own_qk = (col // HEAD_DIM) == (row % N_HEADS) * 2 + row // N_HEADS
        q = (q_ref[...] * (HEAD_DIM ** -0.5)).astype(BF16).astype(F32)
        qexp = jnp.where(own_qk, jnp.broadcast_to(q, (rows, width)), 0.0)
        qexp_sc[...] = qexp.astype(BF16)
        bias_sc[...] = _bucket_lookup(bucket_ref[...], lambda bk: rbt_ref[:, bk:bk + 1])
        kn = kn_ref[...].astype(BF16).astype(F32)
        s_self = jnp.sum(qexp * kn, axis=1, keepdims=True) + rbt_ref[:, 0:1]
        m_sc[...] = s_self
        l_sc[...] = jnp.ones(l_sc.shape, F32)
        vn = vn_ref[...].astype(BF16).astype(F32)
        acc_sc[...] = jnp.broadcast_to(vn[:, None, :], acc_sc.shape)

    qexp = qexp_sc[...]
    bias_far = rbt_ref[:, NUM_BUCKETS - 1:NUM_BUCKETS]
    is_last = step == n_steps - 1
    s_parts = []
    for p in range(pages):
        s = jnp.dot(qexp, k_refs[p][...].astype(BF16), preferred_element_type=F32)
        if p == pages - 1:
            s = s + jnp.where(is_last, bias_sc[...], bias_far)
        else:
            s = s + bias_far
        s_parts.append(s)
    m_old = m_sc[...]
    m_new = m_old
    for s in s_parts:
        m_new = jnp.maximum(m_new, jnp.max(s, axis=1, keepdims=True))
    a = jnp.exp(m_old - m_new)
    l_new = a * l_sc[...]
    probs = []
    for p in range(pages):
        pr = jnp.exp(s_parts[p] - m_new)
        l_new = l_new + jnp.sum(pr, axis=1, keepdims=True)
        probs.append(pr.astype(BF16))
    probs = jnp.concatenate(probs, axis=1)
    m_sc[...] = m_new
    l_sc[...] = l_new
    for h in range(N_HEADS):
        vh = jnp.concatenate([v_refs[p][pl.ds(h, page, stride=N_HEADS), :].astype(BF16)
                              for p in range(pages)], axis=0)
        acc_sc[h] = a * acc_sc[h] + jnp.dot(probs, vh, preferred_element_type=F32)

    @pl.when(is_last)
    def _():
        nrm = acc_sc[...] / l_new[None]
        r = lax.broadcasted_iota(jnp.int32, nrm.shape, 1)
        hh = lax.broadcasted_iota(jnp.int32, nrm.shape, 0)
        coef = jnp.where(r == hh, 1.0, jnp.where(r == hh + N_HEADS, -lam_ref[0], 0.0))
        d = jnp.sum(coef * nrm, axis=1)
        ms = jnp.mean(d * d, axis=-1, keepdims=True)
        o_ref[...] = (d * lax.rsqrt(ms + LN_EPS) * g_ref[...] * out_scale).astype(o_ref.dtype)


def _sample_attention(q, k_new, v_new, cache_k, cache_v, page_table, rel_bias, lam, subln_g,
                      *, pages, out_scale):
    bs, width = q.shape
    page = cache_k.shape[2]
    n_pages = page_table.shape[1]
    n_steps = n_pages // pages
    past = n_pages * page
    kpos = past - page + jnp.arange(page, dtype=jnp.int32)
    bucket_last = _rel_bucket(past - kpos).reshape(1, page)
    rbt = jnp.tile(rel_bias.T, (2, 1))

    def tok_spec():
        return pl.BlockSpec((None, 1, width), lambda b, s, pt: (b, 0, 0))

    def page_spec(shape, p):
        return pl.BlockSpec((None,) + shape, lambda b, s, pt, p=p: (pt[b, s * pages + p], 0, 0))

    full = lambda shape: pl.BlockSpec(shape, lambda b, s, pt: tuple(0 for _ in shape))
    head_spec = pl.BlockSpec((None, N_HEADS, V_DIM), lambda b, s, pt: (b, 0, 0))
    in_specs = ([pl.BlockSpec(memory_space=pltpu.SMEM), tok_spec(), tok_spec(), head_spec,
                 full((2 * N_HEADS, NUM_BUCKETS)), full((1, page)), full((1, V_DIM))]
                + [page_spec((width, page), p) for p in range(pages)]
                + [page_spec((page * N_HEADS, V_DIM), p) for p in range(pages)])
    rows = 2 * N_HEADS
    body = functools.partial(_decode_body, pages=pages, page=page, n_steps=n_steps,
                             out_scale=out_scale)
    out = pl.pallas_call(
        body,
        grid_spec=pltpu.PrefetchScalarGridSpec(
            num_scalar_prefetch=1, grid=(bs, n_steps), in_specs=in_specs,
            out_specs=head_spec,
            scratch_shapes=[pltpu.VMEM((rows, width), BF16), pltpu.VMEM((rows, page), F32),
                            pltpu.VMEM((rows, 1), F32), pltpu.VMEM((rows, 1), F32),
                            pltpu.VMEM((N_HEADS, rows, V_DIM), F32)]),
        out_shape=jax.ShapeDtypeStruct((bs, N_HEADS, V_DIM), BF16),
        compiler_params=_params(("arbitrary", "arbitrary")),
        name="sample_attention")(
            page_table, lam, q.reshape(bs, 1, width), k_new.reshape(bs, 1, width),
            v_new.reshape(bs, N_HEADS, V_DIM), rbt, bucket_last, subln_g.reshape(1, V_DIM),
            *([cache_k] * pages), *([cache_v] * pages))
    return out.reshape(bs, width)


def _s5_params(a_re, a_im, log_dt, b_re, b_im, c_re, c_im):
    dt = jnp.exp(log_dt)[:, None]
    mag = jnp.exp(a_re * dt)
    ang = a_im * dt
    abar_re = mag * jnp.cos(ang)
    abar_im = mag * jnp.sin(ang)
    den = a_re * a_re + a_im * a_im
    f_re = ((abar_re - 1.0) * a_re + abar_im * a_im) / den
    f_im = (abar_im * a_re - (abar_re - 1.0) * a_im) / den
    bb_re = f_re[..., None] * b_re - f_im[..., None] * b_im
    bb_im = f_re[..., None] * b_im + f_im[..., None] * b_re
    eye = jnp.eye(SLAB_GROUPS, dtype=F32)

    def in_slabs(bb):
        tt = bb.reshape(N_SLABS, SLAB_GROUPS, STATE_DIM, SSM_GROUP).transpose(0, 1, 3, 2)
        full = tt[:, :, :, None, :] * eye[None, :, None, :, None]
        return full.reshape(N_SLABS, SLAB_IN, SLAB_STATE).astype(BF16)

    ab_re = abar_re[..., None] * bb_re - abar_im[..., None] * bb_im
    ab_im = abar_re[..., None] * bb_im + abar_im[..., None] * bb_re

    def out_slabs(cc):
        tt = cc.reshape(N_SLABS, SLAB_GROUPS, SSM_GROUP, STATE_DIM).transpose(0, 1, 3, 2)
        full = tt[:, :, :, None, :] * eye[None, :, None, :, None]
        return full.reshape(N_SLABS, SLAB_STATE, SLAB_IN).astype(BF16)

    return (abar_re.reshape(N_SLABS, SLAB_STATE), abar_im.reshape(N_SLABS, SLAB_STATE),
            in_slabs(bb_re), in_slabs(bb_im), out_slabs(c_re), out_slabs(c_im),
            in_slabs(ab_re), in_slabs(ab_im))


def _glu_out(y, u, d_ref, wglu_ref):
    g = jax.nn.gelu(y + d_ref[...] * u)
    gate = jnp.dot(g.astype(BF16), wglu_ref[...].astype(BF16), preferred_element_type=F32)
    return g * jax.nn.sigmoid(gate)


def _s5_prompt_body(u_ref, ar_ref, ai_ref, bbr_ref, bbi_ref, cr_ref, ci_ref, d_ref, wglu_ref,
                    o_ref, sre_ref, sim_ref, xr_sc, xi_sc, pw_sc, cr_sc, ci_sc, perm_sc, *, batch):
    c = pl.program_id(0)
    sub = SUBLANES
    _, tc, width = u_ref.shape
    n_rows = batch * tc
    t = lax.broadcasted_iota(jnp.int32, (sub, SLAB_STATE), 0)
    first = t < batch

    def cmul(pr, pi, qr, qi):
        return pr * qr - pi * qi, pr * qi + pi * qr

    @pl.when(c == 0)
    def _():
        for j in range(N_SLABS):
            a = (jnp.broadcast_to(ar_ref[j:j + 1, :], t.shape), jnp.broadcast_to(ai_ref[j:j + 1, :], t.shape))
            a2 = cmul(*a, *a)
            for part in range(2):
                pw_sc[j, part] = jnp.where(first, a[part], a2[part])
        cr_sc[...] = jnp.zeros(cr_sc.shape, F32)
        ci_sc[...] = jnp.zeros(ci_sc.shape, F32)
        r = lax.broadcasted_iota(jnp.int32, (n_rows, n_rows), 0)
        k = lax.broadcasted_iota(jnp.int32, (n_rows, n_rows), 1)
        perm_sc[0] = jnp.where(k == (r % batch) * tc + r // batch, 1.0, 0.0).astype(BF16)
        perm_sc[1] = jnp.where(r == (k % batch) * tc + k // batch, 1.0, 0.0).astype(BF16)

    def permute(which, x):
        return jnp.dot(perm_sc[which], x, preferred_element_type=F32)

    u_seq = u_ref[...].reshape(n_rows, width)
    hi = u_seq.astype(BF16)
    rest = u_seq - hi.astype(F32)
    mid = rest.astype(BF16)
    lo = (rest - mid.astype(F32)).astype(BF16)
    u_hi = permute(0, hi)
    ub = u_hi.astype(BF16)
    u = u_hi + permute(0, mid) + permute(0, lo)
    row = lax.broadcasted_iota(jnp.int32, u_hi.shape, 0)
    u_before = jnp.where(row % sub >= batch, pltpu.roll(u_hi, batch, 0), 0.0).astype(BF16)
    for j in range(N_SLABS):
        cols = slice(j * SLAB_IN, (j + 1) * SLAB_IN)
        uj = jnp.concatenate([ub[:, cols], u_before[:, cols]], axis=1)
        xr_sc[j] = jnp.dot(uj, bbr_ref[j], preferred_element_type=F32)
        xi_sc[j] = jnp.dot(uj, bbi_ref[j], preferred_element_type=F32)

    for j in range(N_SLABS):
        pr, pi = cr_sc[j], ci_sc[j]
        for v in range(n_rows // sub):
            rows = slice(v * sub, (v + 1) * sub)
            lr = jnp.where(first, pltpu.roll(pr, batch, 0), pr)
            li = jnp.where(first, pltpu.roll(pi, batch, 0), pi)
            er, ei = cmul(pw_sc[j, 0], pw_sc[j, 1], lr, li)
            pr, pi = xr_sc[j, rows, :] + er, xi_sc[j, rows, :] + ei
            xr_sc[j, rows, :] = pr
            xi_sc[j, rows, :] = pi
        cr_sc[j] = pr
        ci_sc[j] = pi
    sre_ref[...] = cr_sc[...]
    sim_ref[...] = ci_sc[...]

    ys = []
    for j in range(N_SLABS):
        ys.append(jnp.dot(xr_sc[j].astype(BF16), cr_ref[j], preferred_element_type=F32)
                  - jnp.dot(xi_sc[j].astype(BF16), ci_ref[j], preferred_element_type=F32))
    out = _glu_out(jnp.concatenate(ys, axis=1), u, d_ref, wglu_ref).astype(BF16)
    o_ref[...] = permute(1, out).astype(o_ref.dtype).reshape(batch, tc, width)


def _s5_prompt(u, sp, d, w_glu, *, batch, seq, tc):
    assert SUBLANES == 2 * batch, "tile = two positions of every sequence"
    ar, ai, bbr, bbi, cr, ci, abr, abi = sp
    bbr = jnp.concatenate([bbr, abr], axis=1)
    bbi = jnp.concatenate([bbi, abi], axis=1)
    width = u.shape[1]
    rows = batch * tc
    const = lambda shape: pl.BlockSpec(shape, lambda c: tuple(0 for _ in shape))
    block = pl.BlockSpec((batch, tc, width), lambda c: (0, c, 0))
    state_spec = const((N_SLABS, SUBLANES, SLAB_STATE))
    state_shape = jax.ShapeDtypeStruct((N_SLABS, SUBLANES, SLAB_STATE), F32)
    out, s_re, s_im = pl.pallas_call(
        functools.partial(_s5_prompt_body, batch=batch), grid=(seq // tc,),
        in_specs=[block, const(ar.shape), const(ai.shape), const(bbr.shape), const(bbi.shape),
                  const(cr.shape), const(ci.shape), const((1, width)), const(w_glu.shape)],
        out_specs=[block, state_spec, state_spec],
        out_shape=[jax.ShapeDtypeStruct((batch, seq, width), BF16), state_shape, state_shape],
        scratch_shapes=[pltpu.VMEM((N_SLABS, rows, SLAB_STATE), F32),
                        pltpu.VMEM((N_SLABS, rows, SLAB_STATE), F32),
                        pltpu.VMEM((N_SLABS, 2, SUBLANES, SLAB_STATE), F32),
                        pltpu.VMEM((N_SLABS, SUBLANES, SLAB_STATE), F32),
                        pltpu.VMEM((N_SLABS, SUBLANES, SLAB_STATE), F32),
                        pltpu.VMEM((2, rows, rows), BF16)],
        compiler_params=_params(("arbitrary",)),
        name="s5_prompt")(u.reshape(batch, seq, width), ar, ai, bbr, bbi, cr, ci,
                          d.reshape(1, width), w_glu)
    return out.reshape(batch * seq, width), s_re, s_im


def _s5_sample_body(u_ref, x0r_ref, x0i_ref, ar_ref, ai_ref, bbr_ref, bbi_ref, cr_ref, ci_ref,
                    d_ref, wglu_ref, o_ref, sre_ref, sim_ref):
    u = u_ref[...]
    ub = u.astype(BF16)
    ys = []
    for j in range(N_SLABS):
        uj = ub[:, j * SLAB_IN:(j + 1) * SLAB_IN]
        ar = ar_ref[j:j + 1, :]
        ai = ai_ref[j:j + 1, :]
        x0r = x0r_ref[j]
        x0i = x0i_ref[j]
        xr = ar * x0r - ai * x0i + jnp.dot(uj, bbr_ref[j], preferred_element_type=F32)
        xi = ar * x0i + ai * x0r + jnp.dot(uj, bbi_ref[j], preferred_element_type=F32)
        sre_ref[j] = xr
        sim_ref[j] = xi
        ys.append(jnp.dot(xr.astype(BF16), cr_ref[j], preferred_element_type=F32)
                  - jnp.dot(xi.astype(BF16), ci_ref[j], preferred_element_type=F32))
    o_ref[...] = _glu_out(jnp.concatenate(ys, axis=1), u, d_ref, wglu_ref).astype(o_ref.dtype)


def _s5_sample(u, x0_re, x0_im, sp, d, w_glu):
    ar, ai, bbr, bbi, cr, ci = sp[:6]
    bs, width = u.shape
    state_shape = jax.ShapeDtypeStruct((N_SLABS, bs, SLAB_STATE), F32)
    return pl.pallas_call(
        _s5_sample_body,
        out_shape=[jax.ShapeDtypeStruct((bs, width), BF16), state_shape, state_shape],
        compiler_params=pltpu.CompilerParams(vmem_limit_bytes=VMEM_LIMIT_BYTES),
        name="s5_sample")(u, x0_re, x0_im, ar, ai, bbr, bbi, cr, ci, d.reshape(1, width), w_glu)


def _gated(c):
    gate = c[:, :LANES]
    return gate * jax.nn.sigmoid(gate) * c[:, LANES:]


def _up_prompt_body(h_ref, wg_ref, wv_ref, cwg_ref, cwv_ref, cbg_ref, cbv_ref,
                    act_ref, cg_ref, cv_ref, w_sc, *, rows):
    w_sc[:, 0:LANES] = wg_ref[...].astype(BF16)
    w_sc[:, LANES:2 * LANES] = wv_ref[...].astype(BF16)
    cw = jnp.concatenate([cwg_ref[...], cwv_ref[...]], axis=1)
    cb = jnp.concatenate([cbg_ref[...], cbv_ref[...]], axis=1)
    seq = h_ref.shape[0]
    sub = SUBLANES
    row8 = lax.broadcasted_iota(jnp.int32, (sub, 2 * LANES), 0)

    prev = jnp.zeros((sub, 2 * LANES), F32)
    for c in range(seq // rows):
        up = jnp.dot(h_ref[c * rows:(c + 1) * rows, :], w_sc[...], preferred_element_type=F32)
        cur = cb + cw[CONV_W - 1:CONV_W] * up
        for back in range(1, CONV_W):
            rolled = pltpu.roll(up, back, 0)
            head = jnp.where(row8 < back, pltpu.roll(prev, back, 0), rolled[0:sub])
            shifted = jnp.concatenate([head, rolled[sub:]], axis=0)
            cur = cur + cw[CONV_W - 1 - back:CONV_W - back] * shifted
        act_ref[c * rows:(c + 1) * rows, :] = _gated(cur).astype(act_ref.dtype)
        prev = up[rows - sub:, :]
    tail = prev[sub - (CONV_W - 1):, :]
    cg_ref[...] = tail[:, :LANES]
    cv_ref[...] = tail[:, LANES:]


def _up_prompt(hb, w_up, conv_w, conv_b, *, batch, seq):
    d_model = hb.shape[1]
    d_ff = w_up.shape[1] // 2
    nb = d_ff // LANES
    cb = conv_b.reshape(1, 2 * d_ff)
    col = lambda off: (lambda b, j: (0, j + off))
    tail_spec = pl.BlockSpec((None, CONV_W - 1, LANES), lambda b, j: (b, 0, j))
    tail_shape = jax.ShapeDtypeStruct((batch, CONV_W - 1, d_ff), F32)
    return pl.pallas_call(
        functools.partial(_up_prompt_body, rows=UP_ROW_CHUNK), grid=(batch, nb),
        scratch_shapes=[pltpu.VMEM((d_model, 2 * LANES), BF16)],
        in_specs=[pl.BlockSpec((seq, d_model), lambda b, j: (b, 0)),
                  pl.BlockSpec((d_model, LANES), col(0)), pl.BlockSpec((d_model, LANES), col(nb)),
                  pl.BlockSpec((CONV_W, LANES), col(0)), pl.BlockSpec((CONV_W, LANES), col(nb)),
                  pl.BlockSpec((1, LANES), col(0)), pl.BlockSpec((1, LANES), col(nb))],
        out_specs=[pl.BlockSpec((seq, LANES), lambda b, j: (b, j)), tail_spec, tail_spec],
        out_shape=[jax.ShapeDtypeStruct((batch * seq, d_ff), BF16), tail_shape, tail_shape],
        compiler_params=_params(("arbitrary", "arbitrary")),
        name="up_prompt")(hb, w_up, w_up, conv_w, conv_w, cb, cb)


def _up_sample_body(h_ref, wg_ref, wva_ref, wvb_ref, cwg_ref, cwva_ref, cwvb_ref, cbg_ref, cbva_ref,
                    cbvb_ref, sg_ref, sva_ref, svb_ref, act_ref, cg_ref, cv_ref):
    w2 = 2 * LANES
    w = jnp.concatenate([wg_ref[...].astype(BF16), wva_ref[...].astype(BF16),
                         wvb_ref[...].astype(BF16)], axis=1)
    up = jnp.dot(h_ref[...], w, preferred_element_type=F32)
    cw = jnp.concatenate([cwg_ref[...], cwva_ref[...], cwvb_ref[...]], axis=1)
    cb = jnp.concatenate([cbg_ref[...], cbva_ref[...], cbvb_ref[...]], axis=1)
    c = cb + cw[CONV_W - 1:CONV_W] * up
    for tap in range(CONV_W - 1):
        st = jnp.concatenate([sg_ref[tap], sva_ref[tap], svb_ref[tap]], axis=1)
        c = c + cw[tap:tap + 1] * st
    gate = c[:, 0:w2]
    act_ref[...] = (gate * jax.nn.sigmoid(gate) * c[:, w2:2 * w2]).astype(act_ref.dtype)
    for tap in range(1, CONV_W - 1):
        cg_ref[tap - 1] = sg_ref[tap]
        cv_ref[tap - 1] = jnp.concatenate([sva_ref[tap], svb_ref[tap]], axis=1)
    cg_ref[CONV_W - 2] = up[:, 0:w2]
    cv_ref[CONV_W - 2] = up[:, w2:2 * w2]


def _up_sample(hb, w_up, conv_w, conv_b, state):
    bs, d_model = hb.shape
    d_ff = w_up.shape[1] // 2
    nb = d_ff // LANES
    n_pairs = (nb + 1) // 2
    last = 2 * nb - 1
    w2 = 2 * LANES
    cb = conv_b.reshape(1, 2 * d_ff)
    gate = lambda shape: pl.BlockSpec(shape, lambda j: (0,) * (len(shape) - 1) + (j,))
    val = lambda shape, k: pl.BlockSpec(
        shape, lambda j: (0,) * (len(shape) - 1) + (jnp.minimum(nb + 2 * j + k, last),))
    operands = []
    specs = [pl.BlockSpec((bs, d_model), lambda j: (0, 0))]
    for arr, lead in ((w_up, (d_model,)), (conv_w, (CONV_W,)), (cb, (1,)), (state, (CONV_W - 1, bs))):
        specs += [gate(lead + (w2,)), val(lead + (LANES,), 0), val(lead + (LANES,), 1)]
        operands += [arr, arr, arr]
    tail_spec = pl.BlockSpec((CONV_W - 1, bs, w2), lambda j: (0, 0, j))
    tail_shape = jax.ShapeDtypeStruct((CONV_W - 1, bs, d_ff), F32)
    return pl.pallas_call(
        _up_sample_body, grid=(n_pairs,), in_specs=specs,
        out_specs=[pl.BlockSpec((bs, w2), lambda j: (0, j)), tail_spec, tail_spec],
        out_shape=[jax.ShapeDtypeStruct((bs, d_ff), BF16), tail_shape, tail_shape],
        compiler_params=_params(("arbitrary",)),
        name="up_sample")(hb, *operands)


def _one(x):
    return (x,)


def _sigmoid_out(acc):
    return (jax.nn.sigmoid(acc),)


def _both(acc):
    return (acc, acc)


def _merge(pa, ps, ga, gs):
    return (ga * pa + gs * ps,)


def _layer(xp, xs, w, *, batch, seq, alpha, attend_p, attend_s, ssm_p, ssm_s, up_p, up_s):
    d_model = xp.shape[1]
    qk_w = N_HEADS * 2 * HEAD_DIM
    v_w = N_HEADS * V_DIM
    ssm_w = d_model // 2
    w_in = w["w_in"]
    g1, b1 = w["ln1_g"].reshape(1, d_model), w["ln1_b"].reshape(1, d_model)
    big = dict(tm=MM_ROWS, tn=MM_COLS)
    c = 0
    (q,), (q_s,), xb = _matmul([(xp, xs, w_in, c)], [], [BF16], _one, n_cols=qk_w, name="proj_q",
                               emit_lhs=True, **big)
    c += qk_w
    kt, ktb, k_s = _proj_transposed(xb, xs, w_in, c, n_cols=qk_w, batch=batch, seq=seq, tm=MM_ROWS,
                                    name="proj_kt")
    c += qk_w
    c_v = c
    c += v_w
    (u,), (u_s,) = _matmul([(xb, xs, w_in, c)], [], [F32], _one, n_cols=ssm_w, name="proj_u", **big)
    c += ssm_w
    wide = dict(tm=MM_ROWS, tn=d_model)
    (ga,), (ga_s,) = _matmul([(xb, xs, w_in, c)], [], [F32], _sigmoid_out, n_cols=d_model, name="gate_a", **wide)
    c += d_model
    (gs,), (gs_s,) = _matmul([(xb, xs, w_in, c)], [], [F32], _sigmoid_out, n_cols=d_model, name="gate_s", **wide)
    (v, vb), (v_s, _) = _matmul([(xb, xs, w_in, c_v)], [], [F32, BF16], _both, n_cols=v_w, name="proj_v", **big)

    attn = attend_p(q, ktb, vb)
    ssm_out, re_p, im_p = ssm_p(u)
    ssm_out_s, re_s, im_s = ssm_s(u_s)
    attn_s = attend_s(q_s, k_s, v_s)

    (merged,), (merged_s,) = _matmul(
        [(attn, attn_s, w["w_proj_attn"], 0), (ssm_out, ssm_out_s, w["w_proj_ssm"], 0)],
        [(ga, ga_s), (gs, gs_s)], [BF16], _merge, n_cols=d_model, tm=LN_MM_ROWS, tn=d_model, name="merge")

    def post_ln1(acc, res, g, b):
        h = _ln(alpha * res + acc, g, b)
        return h, h

    (h, hb), (h_s, hb_s) = _matmul([(merged, merged_s, w["w_out"], 0)], [(xp, xs), g1, b1], [F32, BF16],
                                   post_ln1, n_cols=d_model, tm=LN_MM_ROWS, tn=d_model, name="out_proj_ln1")
    act, conv_p = up_p(hb)
    act_s, conv_s = up_s(hb_s)
    (r2,), (r2_s,) = _matmul([(act, act_s, w["w_down"], 0)], [(h, h_s)], [F32],
                             lambda acc, res: (alpha * res + acc,), n_cols=d_model,
                             tm=DOWN_ROWS, tn=DOWN_COLS, name="down_proj")
    y = _layer_norm(r2, w["ln2_g"], w["ln2_b"], tm=LN_ROWS, name="ln2")
    y_s = _layer_norm(r2_s, w["ln2_g"], w["ln2_b"], tm=LN_ROWS, name="ln2_sample")
    return (y, kt, v, re_p, im_p, conv_p), (y_s, k_s, v_s, re_s, im_s, conv_s)


def kernel(x_prompt, x_sample, cache_k, cache_v, state_ssm_re, state_ssm_im, state_conv, page_table, rel_bias, w_in, lambda_q1, lambda_k1, lambda_q2, lambda_k2, subln_g, ssm_a_re, ssm_a_im, ssm_log_dt, ssm_b_re, ssm_b_im, ssm_c_re, ssm_c_im, ssm_d, w_glu, w_proj_attn, w_proj_ssm, w_out, ln1_g, ln1_b, w_up, conv_w, conv_b, w_down, ln2_g, ln2_b):
    depth = w_in.shape[0]
    assert depth == 1, "single-layer trunk"
    bp, seq, d_model = x_prompt.shape
    bs, dec_seq, _ = x_sample.shape
    assert dec_seq == 1
    n_pool, page = cache_k.shape[1], cache_k.shape[2]
    d_ff = w_down.shape[1]
    n_groups = ssm_a_re.shape[1]
    assert n_groups == N_SLABS * SLAB_GROUPS and d_ff % LANES == 0
    alpha = (2.0 * depth) ** 0.25
    width = N_HEADS * V_DIM

    hp = x_prompt.reshape(bp * seq, d_model)
    hs = x_sample.reshape(bs, d_model)
    outs = {}
    for l in range(depth):
        lam_init = 0.8 - 0.6 * math.exp(-0.3 * l)
        out_scale = 1.0 - lam_init
        lam = (jnp.exp(jnp.sum(lambda_q1[l] * lambda_k1[l]))
               - jnp.exp(jnp.sum(lambda_q2[l] * lambda_k2[l])) + lam_init).reshape(1)
        w = dict(w_in=w_in[l], w_proj_attn=w_proj_attn[l], w_proj_ssm=w_proj_ssm[l], w_out=w_out[l],
                 ln1_g=ln1_g[l], ln1_b=ln1_b[l], w_down=w_down[l], ln2_g=ln2_g[l], ln2_b=ln2_b[l])
        sp = _s5_params(ssm_a_re[l], ssm_a_im[l], ssm_log_dt[l], ssm_b_re[l], ssm_b_im[l],
                        ssm_c_re[l], ssm_c_im[l])

        def attend_p(q, kt, v):
            return _prompt_attention(q, kt, v, rel_bias, lam, subln_g[l], batch=bp, seq=seq,
                                     t=ATTN_BLOCK, out_scale=out_scale)

        def ssm_p(u):
            o, sr, si = _s5_prompt(u, sp, ssm_d[l], w_glu[l], batch=bp, seq=seq, tc=S5_POSITIONS)
            last = lambda st: st[:, SUBLANES - bp:, :].transpose(1, 0, 2)
            return o, last(sr), last(si)

        def up_p(hb):
            act, cg, cv = _up_prompt(hb, w_up[l], conv_w[l], conv_b[l], batch=bp, seq=seq)
            return act, jnp.concatenate([cg, cv], axis=-1)

        ck = cache_k[l].transpose(0, 2, 3, 4, 1).reshape(n_pool, width, page)
        cv_ = cache_v[l].reshape(n_pool, page * N_HEADS, V_DIM)

        def attend_s(q, k, v):
            return _sample_attention(q.astype(F32), k, v, ck, cv_, page_table, rel_bias, lam,
                                     subln_g[l], pages=DECODE_PAGES, out_scale=out_scale)

        def ssm_s(u):
            x0r = state_ssm_re[l].reshape(bs, N_SLABS, SLAB_STATE).transpose(1, 0, 2)
            x0i = state_ssm_im[l].reshape(bs, N_SLABS, SLAB_STATE).transpose(1, 0, 2)
            o, sr, si = _s5_sample(u, x0r, x0i, sp, ssm_d[l], w_glu[l])
            return o, sr.transpose(1, 0, 2), si.transpose(1, 0, 2)

        def up_s(hb):
            act, cg, cv = _up_sample(hb, w_up[l], conv_w[l], conv_b[l],
                                     state_conv[l].transpose(1, 0, 2))
            return act, jnp.concatenate([cg, cv], axis=-1).transpose(1, 0, 2)

        (hp, kt_p, v_p, re_p, im_p, c_p), (hs, k_s, v_s, re_s, im_s, c_s) = _layer(
            hp, hs, w, batch=bp, seq=seq, alpha=alpha, attend_p=attend_p, attend_s=attend_s,
            ssm_p=ssm_p, ssm_s=ssm_s, up_p=up_p, up_s=up_s)
        k_p = kt_p.reshape(bp, N_HEADS, 2, HEAD_DIM, seq).transpose(0, 4, 1, 2, 3)

        for name, val in (("kp", k_p.reshape(bp, seq, N_HEADS, 2, HEAD_DIM)),
                          ("vp", v_p.reshape(bp, seq, N_HEADS, V_DIM)),
                          ("rep", re_p.reshape(bp, n_groups, STATE_DIM)),
                          ("imp", im_p.reshape(bp, n_groups, STATE_DIM)),
                          ("cp", c_p),
                          ("ks", k_s.reshape(bs, 1, N_HEADS, 2, HEAD_DIM)),
                          ("vs", v_s.reshape(bs, 1, N_HEADS, V_DIM)),
                          ("res", re_s.reshape(bs, n_groups, STATE_DIM)),
                          ("ims", im_s.reshape(bs, n_groups, STATE_DIM)),
                          ("cs", c_s)):
            outs.setdefault(name, []).append(val)

    st = {k: jnp.stack(v, axis=0) for k, v in outs.items()}
    return (hp.reshape(bp, seq, d_model), hs.reshape(bs, 1, d_model), st["kp"], st["vp"], st["rep"],
            st["imp"], st["cp"], st["ks"], st["vs"], st["res"], st["ims"], st["cs"])
```
